```python
import jax
import jax.numpy as jnp
from jax import lax
import numpy as np

D_MODEL = 1024
BATCH = 32
SEQ = 256
DEPTH = 2
DEC_BATCH = 2
DEC_SEQ = 2048
PAST_LEN = 512

GRID_W = 64
EPS = 1e-6
CONV_W = 512
CONV_K = 3
GLA_HEADS = 4
GLA_DK = 128
GLA_DV = 128
GLA_QK_W = GLA_HEADS * GLA_DK
GLA_W = GLA_HEADS * GLA_DV
GLA_RANK = 16
GLA_TAU = 16.0
GLA_CHUNK = 64
ATT_HEADS = 8
ATT_KV_HEADS = 2
ATT_GROUP = ATT_HEADS // ATT_KV_HEADS
HEAD_DIM = 64
ATT_W = ATT_HEADS * HEAD_DIM
KV_W = ATT_KV_HEADS * HEAD_DIM
WINDOW = 128
BLOCK = 128
ROPE_THETA = 10000.0
N_BRANCH = 3
BRANCH_W = 512
N_EXPERT_GROUPS = 4
EXPERTS_PER_GROUP = 4
N_EXPERTS = N_EXPERT_GROUPS * EXPERTS_PER_GROUP
TOP_K = 2
D_EXPERT = 256
SPLITS = (CONV_W, CONV_W, CONV_W, GLA_QK_W, GLA_QK_W, GLA_W, GLA_W, GLA_RANK, GLA_RANK, ATT_W, KV_W, KV_W, D_MODEL, D_MODEL, D_MODEL)
N_IN = sum(SPLITS)
NEG_INF = -1e30

kernel_name = "hybrid_diffusion_prefix_step"


def rms_norm(x, g):
    xf = x.astype(jnp.float32)
    y = xf * lax.rsqrt(jnp.mean(xf * xf, axis=-1, keepdims=True) + EPS)
    return (y * g.astype(jnp.float32)).astype(x.dtype)


def split_cols(p):
    out, start = [], 0
    for n in SPLITS:
        out.append(p[..., start:start + n])
        start += n
    return out


def modulation(cond, w_ada, b_ada):
    m = jax.nn.silu(cond) @ w_ada + b_ada
    return [t[:, None, :] for t in jnp.split(m, 6, axis=-1)]


def axial_rope(x):
    L = x.shape[1]
    rows = L // GRID_W
    row = jnp.repeat(jnp.arange(rows, dtype=jnp.float32), GRID_W)
    col = jnp.tile(jnp.arange(GRID_W, dtype=jnp.float32), rows)
    axis_dim = HEAD_DIM // 2
    n_freq = axis_dim // 2
    inv = ROPE_THETA ** (-jnp.arange(n_freq, dtype=jnp.float32) / n_freq)
    xf = x.astype(jnp.float32)
    outs = []
    for i, pos in enumerate((row, col)):
        ang = pos[:, None] * inv[None, :]
        cos = jnp.cos(ang)[None, :, None, :]
        sin = jnp.sin(ang)[None, :, None, :]
        xa = xf[..., i * axis_dim:(i + 1) * axis_dim]
        x1, x2 = xa[..., :n_freq], xa[..., n_freq:]
        outs += [x1 * cos - x2 * sin, x1 * sin + x2 * cos]
    return jnp.concatenate(outs, axis=-1).astype(x.dtype)


def short_conv_mixer(b_gate, c_gate, h, conv_w):
    u = c_gate * h
    L = u.shape[1]
    pad = CONV_K // 2
    up = jnp.pad(u, ((0, 0), (pad, pad), (0, 0)))
    y = sum(up[:, j:j + L] * conv_w[j] for j in range(CONV_K))
    return b_gate * y


def gla_chunk_scan(q, k, v, log_a, s0):
    B, L, H, DK = q.shape
    DV = v.shape[-1]
    C = GLA_CHUNK
    N = L // C
    f32 = jnp.float32

    def chunks(t):
        return t.astype(f32).reshape(B, N, C, H, t.shape[-1]).transpose(1, 0, 3, 2, 4)

    qc, kc, vc, ac = chunks(q), chunks(k), chunks(v), chunks(log_a)
    b = jnp.cumsum(ac, axis=3)
    b_last = b[:, :, :, -1:, :]
    q_in = qc * jnp.exp(b)
    k_in = kc * jnp.exp(-b)
    k_end = kc * jnp.exp(b_last - b)
    causal = jnp.tril(jnp.ones((C, C), dtype=bool))
    att = jnp.where(causal, jnp.einsum("nbhcd,nbhsd->nbhcs", q_in, k_in), 0.0)
    o_intra = jnp.einsum("nbhcs,nbhsv->nbhcv", att, vc)
    ds = jnp.einsum("nbhsd,nbhsv->nbhdv", k_end, vc)
    decay = jnp.exp(b_last[:, :, :, 0, :])

    def step(s, inp):
        dec, d = inp
        return dec[..., None] * s + d, s

    s_fin, s_prev = lax.scan(step, s0.astype(f32), (decay, ds))
    o_inter = jnp.einsum("nbhcd,nbhdv->nbhcv", q_in, s_prev)
    o = (o_intra + o_inter).transpose(1, 0, 3, 2, 4).reshape(B, L, H, DV)
    return o, s_fin


def gla_mixer(q, k, v, r, lr_f, lr_b, w_gate, b_gate, norm_g, s0_f, s0_b):
    B, L, _ = q.shape
    f32 = jnp.float32
    qh = q.reshape(B, L, GLA_HEADS, GLA_DK) * (GLA_DK ** -0.5)
    kh = k.reshape(B, L, GLA_HEADS, GLA_DK)
    vh = v.reshape(B, L, GLA_HEADS, GLA_DV)

    def log_decay(lr, d):
        z = (lr @ w_gate[d]).astype(f32) + b_gate[d].astype(f32)
        return (jax.nn.log_sigmoid(z) / GLA_TAU).reshape(B, L, GLA_HEADS, GLA_DK)

    def rev(t):
        return t[:, ::-1]

    o_f, s_f = gla_chunk_scan(qh, kh, vh, log_decay(lr_f, 0), s0_f)
    o_b, s_b = gla_chunk_scan(rev(qh), rev(kh), rev(vh), rev(log_decay(lr_b, 1)), s0_b)
    o = o_f + rev(o_b)
    o = o * lax.rsqrt(jnp.mean(o * o, axis=-1, keepdims=True) + EPS)
    y = o.reshape(B, L, GLA_W) * norm_g.astype(f32) * jax.nn.silu(r.astype(f32))
    return y.astype(q.dtype), s_f.astype(q.dtype), s_b.astype(q.dtype)


def sink_softmax(s, sink):
    B, HKV, G, R, _ = s.shape
    sk = jnp.broadcast_to(sink.astype(jnp.float32).reshape(1, HKV, G, 1, 1), (B, HKV, G, R, 1))
    return jax.nn.softmax(jnp.concatenate([s, sk], axis=-1), axis=-1)[..., :-1]


def context_attention(q, k, v, sink):
    B, P = q.shape[:2]
    n_blk = P // BLOCK
    scale = HEAD_DIM ** -0.5
    qb = q.reshape(B, n_blk, BLOCK, ATT_KV_HEADS, ATT_GROUP, HEAD_DIM).transpose(1, 0, 2, 3, 4, 5)

    def one_block(qn):
        s = jnp.einsum("brkgd,bpkd->bkgrp", qn, k).astype(jnp.float32) * scale
        p = sink_softmax(s, sink).astype(v.dtype)
        return jnp.einsum("bkgrp,bpkd->brkgd", p, v)

    o = lax.map(one_block, qb)
    return o.transpose(1, 0, 2, 3, 4, 5).reshape(B, P, ATT_W)


def latent_window_attention(q, k, v, k_ctx, v_ctx, sink):
    B, L = q.shape[:2]
    n_blk = L // BLOCK
    span = BLOCK + 2 * WINDOW
    scale = HEAD_DIM ** -0.5
    P = k_ctx.shape[1]
    pad = ((0, 0), (WINDOW, WINDOW), (0, 0), (0, 0))
    kp, vp = jnp.pad(k, pad), jnp.pad(v, pad)
    qb = q.reshape(B, n_blk, BLOCK, ATT_KV_HEADS, ATT_GROUP, HEAD_DIM).transpose(1, 0, 2, 3, 4, 5)
    offs = jnp.arange(BLOCK)[:, None] + WINDOW - jnp.arange(span)[None, :]
    band = jnp.abs(offs) <= WINDOW

    def one_block(args):
        n, qn = args
        start = n * BLOCK
        kn = lax.dynamic_slice_in_dim(kp, start, span, axis=1)
        vn = lax.dynamic_slice_in_dim(vp, start, span, axis=1)
        kpos = start - WINDOW + jnp.arange(span)
        valid = band & ((kpos >= 0) & (kpos < L))[None, :]
        s_ctx = jnp.einsum("brkgd,bpkd->bkgrp", qn, k_ctx).astype(jnp.float32) * scale
        s_loc = jnp.einsum("brkgd,bckd->bkgrc", qn, kn).astype(jnp.float32) * scale
        s_loc = jnp.where(valid, s_loc, NEG_INF)
        p = sink_softmax(jnp.concatenate([s_ctx, s_loc], axis=-1), sink).astype(v.dtype)
        return (jnp.einsum("bkgrp,bpkd->brkgd", p[..., :P], v_ctx)
                + jnp.einsum("bkgrc,bckd->brkgd", p[..., P:], vn))

    o = lax.map(one_block, (jnp.arange(n_blk), qb))
    return o.transpose(1, 0, 2, 3, 4, 5).reshape(B, L, ATT_W)


def hier_moe(h, w_rg, b_rg, w_re, b_re, w_eg, w_eu, w_ed):
    B, L, D = h.shape
    t = h.reshape(B * L, D)
    f32 = jnp.float32
    lg = (t @ w_rg).astype(f32) + b_rg.astype(f32)
    grp = jnp.argmax(lg, axis=-1)
    p_grp = jnp.take_along_axis(jax.nn.softmax(lg, axis=-1), grp[:, None], axis=-1)
    le = ((t @ w_re).astype(f32) + b_re.astype(f32)).reshape(-1, N_EXPERT_GROUPS, EXPERTS_PER_GROUP)
    le_sel = jnp.take_along_axis(le, grp[:, None, None], axis=1)[:, 0]
    top_w, top_i = lax.top_k(jax.nn.softmax(le_sel, axis=-1), TOP_K)
    top_w = top_w / jnp.sum(top_w, axis=-1, keepdims=True)
    expert_id = grp[:, None] * EXPERTS_PER_GROUP + top_i
    gate = jnp.sum(jax.nn.one_hot(expert_id, N_EXPERTS, dtype=f32) * (p_grp * top_w)[..., None], axis=1)
    hid = jax.nn.silu(jnp.einsum("td,edf->tef", t, w_eg)) * jnp.einsum("td,edf->tef", t, w_eu)
    y = jnp.einsum("tef,efd->td", hid * gate[..., None].astype(hid.dtype), w_ed)
    return y.reshape(B, L, D)


def trunk_layer(x, cond, lw, ctx):
    (w_ada, b_ada, norm1_g, norm2_g, w_in, conv_w, gla_w_gate, gla_b_gate, gla_norm_g,
     attn_sink, w_branch, w_out, w_rg, b_rg, w_re, b_re, w_eg, w_eu, w_ed) = lw
    B, L, _ = x.shape
    sh1, sc1, gt1, sh2, sc2, gt2 = modulation(cond, w_ada, b_ada)
    h = rms_norm(x, norm1_g) * (1 + sc1) + sh1
    (a_b, a_c, a_h, g_q, g_k, g_v, g_r, g_lf, g_lb,
     t_q, t_k, t_v, m_a, m_b, m_c) = split_cols(h @ w_in)
    y_a = short_conv_mixer(a_b, a_c, a_h, conv_w)
    q = t_q.reshape(B, L, ATT_HEADS, HEAD_DIM)
    k = t_k.reshape(B, L, ATT_KV_HEADS, HEAD_DIM)
    v = t_v.reshape(B, L, ATT_KV_HEADS, HEAD_DIM)
    if ctx is None:
        s0 = jnp.zeros((B, GLA_HEADS, GLA_DK, GLA_DV), jnp.float32)
        y_b, s_f, s_b = gla_mixer(g_q, g_k, g_v, g_r, g_lf, g_lb, gla_w_gate, gla_b_gate, gla_norm_g, s0, s0)
        y_c = context_attention(q, k, v, attn_sink)
        side = (jnp.stack([s_f, s_b], axis=1), k, v)
    else:
        st, k_ctx, v_ctx = ctx
        y_b, _, _ = gla_mixer(g_q, g_k, g_v, g_r, g_lf, g_lb, gla_w_gate, gla_b_gate, gla_norm_g,
                              st[:, 0], st[:, 1])
        y_c = latent_window_attention(axial_rope(q), axial_rope(k), v, k_ctx, v_ctx, attn_sink)
        side = None
    z = (jax.nn.sigmoid(m_a) * (y_a @ w_branch[0])
         + jax.nn.sigmoid(m_b) * (y_b @ w_branch[1])
         + jax.nn.sigmoid(m_c) * (y_c @ w_branch[2]))
    x = x + gt1 * (z @ w_out)
    h2 = rms_norm(x, norm2_g) * (1 + sc2) + sh2
    x = x + gt2 * hier_moe(h2, w_rg, b_rg, w_re, b_re, w_eg, w_eu, w_ed)
    return x, side


def setup_inputs(seed: int = 0) -> dict:
    key = jax.random.key(seed)
    ks = jax.random.split(key, 27)
    D = D_MODEL

    def nrm(i, shape, scale):
        return jax.random.normal(ks[i], shape, jnp.float32) * scale

    return {
        "x_prompt": nrm(0, (BATCH, SEQ, D), 1.0),
        "x_sample": nrm(1, (DEC_BATCH, DEC_SEQ, D), 1.0),
        "state_gla": nrm(2, (DEC_BATCH, DEPTH, 2, GLA_HEADS, GLA_DK, GLA_DV), 0.5),
        "cache_k": nrm(3, (DEC_BATCH, DEPTH, PAST_LEN, ATT_KV_HEADS, HEAD_DIM), 1.0),
        "cache_v": nrm(4, (DEC_BATCH, DEPTH, PAST_LEN, ATT_KV_HEADS, HEAD_DIM), 1.0),
        "c": nrm(5, (DEC_BATCH, D), 1.0),
        "c_ctx": nrm(6, (D,), 1.0),
        "w_ada": nrm(7, (DEPTH, D, 6 * D), 0.5 * D ** -0.5),
        "b_ada": nrm(8, (DEPTH, 6 * D), 0.02),
        "norm1_g": 1.0 + nrm(9, (DEPTH, D), 0.02),
        "norm2_g": 1.0 + nrm(10, (DEPTH, D), 0.02),
        "w_in": nrm(11, (DEPTH, D, N_IN), D ** -0.5),
        "conv_w": nrm(12, (DEPTH, CONV_K, CONV_W), CONV_K ** -0.5),
        "gla_w_gate": nrm(13, (DEPTH, 2, GLA_RANK, GLA_QK_W), GLA_RANK ** -0.5),
        "gla_b_gate": 1.0 + nrm(14, (DEPTH, 2, GLA_QK_W), 0.5),
        "gla_norm_g": 1.0 + nrm(15, (DEPTH, GLA_W), 0.02),
        "attn_sink": nrm(16, (DEPTH, ATT_HEADS), 0.5),
        "w_branch": nrm(17, (DEPTH, N_BRANCH, BRANCH_W, D), BRANCH_W ** -0.5),
        "w_out": nrm(18, (DEPTH, D, D), D ** -0.5),
        "w_route_group": nrm(19, (DEPTH, D, N_EXPERT_GROUPS), D ** -0.5),
        "b_route_group": nrm(20, (DEPTH, N_EXPERT_GROUPS), 0.01),
        "w_route_expert": nrm(21, (DEPTH, D, N_EXPERTS), D ** -0.5),
        "b_route_expert": nrm(22, (DEPTH, N_EXPERTS), 0.01),
        "w_exp_gate": nrm(23, (DEPTH, N_EXPERTS, D, D_EXPERT), D ** -0.5),
        "w_exp_up": nrm(24, (DEPTH, N_EXPERTS, D, D_EXPERT), D ** -0.5),
        "w_exp_down": nrm(25, (DEPTH, N_EXPERTS, D_EXPERT, D), D_EXPERT ** -0.5),
        "final_norm_g": 1.0 + nrm(26, (D,), 0.02),
    }


def reference(x_prompt, x_sample, state_gla, cache_k, cache_v, c, c_ctx, w_ada, b_ada, norm1_g, norm2_g,
              w_in, conv_w, gla_w_gate, gla_b_gate, gla_norm_g, attn_sink, w_branch, w_out,
              w_route_group, b_route_group, w_route_expert, b_route_expert,
              w_exp_gate, w_exp_up, w_exp_down, final_norm_g):
    def layer_w(l):
        return (w_ada[l], b_ada[l], norm1_g[l], norm2_g[l], w_in[l], conv_w[l], gla_w_gate[l], gla_b_gate[l],
                gla_norm_g[l], attn_sink[l], w_branch[l], w_out[l], w_route_group[l], b_route_group[l],
                w_route_expert[l], b_route_expert[l], w_exp_gate[l], w_exp_up[l], w_exp_down[l])

    xp = x_prompt
    cond_ctx = c_ctx[None, :]
    gla_states, ctx_keys, ctx_vals = [], [], []
    for l in range(DEPTH):
        xp, (s_l, k_l, v_l) = trunk_layer(xp, cond_ctx, layer_w(l), None)
        gla_states.append(s_l)
        ctx_keys.append(k_l)
        ctx_vals.append(v_l)
    y_prompt = rms_norm(xp, final_norm_g)

    xs = x_sample
    for l in range(DEPTH):
        xs, _ = trunk_layer(xs, c, layer_w(l), (state_gla[:, l], cache_k[:, l], cache_v[:, l]))
    y_sample = rms_norm(xs, final_norm_g)

    new_state_gla = jnp.stack(gla_states, axis=1)
    new_cache_k = jnp.stack(ctx_keys, axis=1)
    new_cache_v = jnp.stack(ctx_vals, axis=1)
    return (y_prompt, y_sample, new_state_gla, new_cache_k, new_cache_v)
```

```python
import functools

import jax
import jax.numpy as jnp
import numpy as np
from jax import lax
from jax.experimental import pallas as pl
from jax.experimental.pallas import tpu as pltpu

F32 = jnp.float32
BF16 = jnp.bfloat16

D_MODEL = 1024
BATCH = 32
SEQ = 256
DEPTH = 2
DEC_BATCH = 2
DEC_SEQ = 2048
PAST_LEN = 512
GRID_W = 64
EPS = 1e-6
CONV_W = 512
CONV_K = 3
GLA_HEADS = 4
GLA_DK = 128
GLA_DV = 128
GLA_RANK = 16
GLA_TAU = 16.0
GLA_CHUNK = 64
ATT_HEADS = 8
ATT_KV_HEADS = 2
ATT_GROUP = ATT_HEADS // ATT_KV_HEADS
HEAD_DIM = 64
WINDOW = 128
BLOCK = 128
ROPE_THETA = 10000.0
N_EXPERT_GROUPS = 4
EXPERTS_PER_GROUP = 4
N_EXPERTS = 16
D_EXPERT = 256
NEG_INF = -1e30

T_CTX = BATCH * SEQ
T_LAT = DEC_BATCH * DEC_SEQ
T_ALL = T_CTX + T_LAT
N_COND = 8

C_GATE = 0
C_CONV = 3072
C_GLA = 4608
C_TQ = 6656
C_TKV = 7168
C_LR = 7424
N_PROJ = 7680
LANES = 128
ROUTE_E0 = N_EXPERT_GROUPS

VMEM_LIMIT = 56 * 1024 * 1024


def _cparams(*sem):
    return pltpu.CompilerParams(dimension_semantics=sem, vmem_limit_bytes=VMEM_LIMIT)


def _cond_row(i, tm):
    n_ctx = T_CTX // tm
    per = DEC_SEQ // tm
    return jnp.where(i < n_ctx, 0, 1 + (i - n_ctx) // per)


def _mod_kernel(c_ref, w_ref, b_ref, o_ref):
    c = c_ref[...]
    s = (c * jax.nn.sigmoid(c)).astype(BF16)
    o_ref[...] = jnp.dot(s, w_ref[...].astype(BF16), preferred_element_type=F32) + b_ref[...]


def _modulation(cond, w_ada, b_ada):
    tn = 1536
    return pl.pallas_call(
        _mod_kernel,
        grid=(DEPTH, 6 * D_MODEL // tn),
        in_specs=[
            pl.BlockSpec((N_COND, D_MODEL), lambda l, j: (0, 0)),
            pl.BlockSpec((None, D_MODEL, tn), lambda l, j: (l, 0, j)),
            pl.BlockSpec((None, 1, tn), lambda l, j: (l, 0, j)),
        ],
        out_specs=pl.BlockSpec((None, N_COND, tn), lambda l, j: (l, 0, j)),
        out_shape=jax.ShapeDtypeStruct((DEPTH, N_COND, 6 * D_MODEL), F32),
        compiler_params=_cparams("parallel", "parallel"),
        name="modulation",
    )(cond, w_ada, b_ada.reshape(DEPTH, 1, 6 * D_MODEL))


def _rms(x):
    return x * lax.rsqrt(jnp.mean(x * x, axis=-1, keepdims=True) + EPS)


def _in_proj_kernel(x_ref, mod_ref, g_ref, w_ref, o_ref, h_ref):
    @pl.when(pl.program_id(1) == 0)
    def _():
        y = _rms(x_ref[...]) * g_ref[...]
        h_ref[...] = (y * (1.0 + mod_ref[1:2, :]) + mod_ref[0:1, :]).astype(BF16)

    o_ref[...] = jnp.dot(h_ref[...], w_ref[...], preferred_element_type=F32)


def _in_proj(x, mod_l, g, w):
    tm, tn = 1024, 1280
    return pl.pallas_call(
        _in_proj_kernel,
        grid=(T_ALL // tm, N_PROJ // tn),
        in_specs=[
            pl.BlockSpec((tm, D_MODEL), lambda i, j: (i, 0)),
            pl.BlockSpec((None, 6, D_MODEL), lambda i, j: (_cond_row(i, tm), 0, 0)),
            pl.BlockSpec((1, D_MODEL), lambda i, j: (0, 0)),
            pl.BlockSpec((D_MODEL, tn), lambda i, j: (0, j)),
        ],
        out_specs=pl.BlockSpec((tm, tn), lambda i, j: (i, j)),
        out_shape=jax.ShapeDtypeStruct((T_ALL, N_PROJ), F32),
        scratch_shapes=[pltpu.VMEM((tm, D_MODEL), BF16)],
        compiler_params=_cparams("parallel", "arbitrary"),
        name="in_proj",
    )(x, mod_l, g, w)


def _log_sigmoid(z):
    return jnp.minimum(z, 0.0) - jnp.log1p(jnp.exp(-jnp.abs(z)))


def _dot_t(a, b):
    return lax.dot_general(a, b, (((1,), (1,)), ((), ())), preferred_element_type=F32)


def _dot_ta(a, b):
    return lax.dot_general(a, b, (((0,), (0,)), ((), ())), preferred_element_type=F32)


def _gla_kernel(*refs, seq_len, has_state, emit_state):
    q_ref, k_ref, v_ref, r_ref, lr_ref, wg_ref, bg_ref, ng_ref = refs[:8]
    pos = 8
    if has_state:
        s0_ref = refs[pos]
        pos += 1
    y_ref = refs[pos]
    pos += 1
    if emit_state:
        s_out_ref = refs[pos]
        pos += 1
    o_acc = refs[pos]

    C = GLA_CHUNK
    n_chunks = seq_len // C
    ri = lax.broadcasted_iota(jnp.int32, (C, C), 0)
    ci = lax.broadcasted_iota(jnp.int32, (C, C), 1)
    hi = lax.Precision.HIGHEST

    for d in range(2):
        keep = (ci <= ri) if d == 0 else (ci >= ri)
        tri = keep.astype(F32)
        wg = wg_ref[d]
        bg = bg_ref[d]

        def chunk(n, st, d=d, keep=keep, tri=tri, wg=wg, bg=bg):
            c = n if d == 0 else n_chunks - 1 - n
            rows = pl.ds(pl.multiple_of(c * C, C), C)
            z = jnp.dot(lr_ref[rows, :], wg, precision=hi, preferred_element_type=F32) + bg
            la = _log_sigmoid(z) / GLA_TAU
            b = jnp.dot(tri, la, precision=hi, preferred_element_type=F32)
            b_last = b[C - 1:C, :] if d == 0 else b[0:1, :]
            q = q_ref[rows, :] * (GLA_DK ** -0.5)
            k = k_ref[rows, :]
            vb = v_ref[rows, :].astype(BF16)
            q_in = (q * jnp.exp(b)).astype(BF16)
            k_in = (k * jnp.exp(-b)).astype(BF16)
            k_end = (k * jnp.exp(b_last - b)).astype(BF16)
            att = jnp.where(keep, _dot_t(q_in, k_in), 0.0)
            o = jnp.dot(att.astype(BF16), vb, preferred_element_type=F32)
            o = o + _dot_t(q_in, st.astype(BF16))
            if d == 0:
                o_acc[rows, :] = o
            else:
                o_acc[rows, :] += o
            return st * jnp.exp(b_last) + _dot_ta(vb, k_end)

        st0 = s0_ref[d].T if has_state else jnp.zeros((GLA_DV, GLA_DK), F32)
        st = lax.fori_loop(0, n_chunks, chunk, st0)
        if emit_state:
            s_out_ref[d] = st.T

    o = o_acc[...]
    o = o * lax.rsqrt(jnp.mean(o * o, axis=-1, keepdims=True) + EPS)
    r = r_ref[...]
    y_ref[...] = (o * ng_ref[...] * (r * jax.nn.sigmoid(r))).astype(y_ref.dtype)


def _gla(p, row0, n_seq, seq_len, wg_pad, bg, ng, s0):
    has_state = s0 is not None
    emit_state = not has_state
    rb0 = row0 // seq_len
    cb = C_GLA // LANES

    def col(piece):
        return pl.BlockSpec((seq_len, LANES), lambda s, h: (rb0 + s, cb + 4 * piece + h))

    in_specs = [col(0), col(1), col(2), col(3),
                pl.BlockSpec((seq_len, LANES), lambda s, h: (rb0 + s, C_LR // LANES)),
                pl.BlockSpec((2, LANES, LANES), lambda s, h: (0, 0, h)),
                pl.BlockSpec((2, 1, LANES), lambda s, h: (0, 0, h)),
                pl.BlockSpec((1, LANES), lambda s, h: (0, h))]
    args = [p, p, p, p, p, wg_pad, bg, ng]
    if has_state:
        in_specs.append(pl.BlockSpec((None, 2, None, GLA_DK, GLA_DV), lambda s, h: (s, 0, h, 0, 0)))
        args.append(s0)
    out_specs = [pl.BlockSpec((seq_len, LANES), lambda s, h: (s, h))]
    out_shape = [jax.ShapeDtypeStruct((n_seq * seq_len, GLA_HEADS * GLA_DV), BF16)]
    if emit_state:
        out_specs.append(pl.BlockSpec((None, 2, None, GLA_DK, GLA_DV), lambda s, h: (s, 0, h, 0, 0)))
        out_shape.append(jax.ShapeDtypeStruct((n_seq, 2, GLA_HEADS, GLA_DK, GLA_DV), F32))
    out = pl.pallas_call(
        functools.partial(_gla_kernel, seq_len=seq_len, has_state=has_state, emit_state=emit_state),
        grid=(n_seq, GLA_HEADS),
        in_specs=in_specs,
        out_specs=out_specs,
        out_shape=out_shape,
        scratch_shapes=[pltpu.VMEM((seq_len, GLA_DV), F32)],
        compiler_params=_cparams("parallel", "parallel"),
        name="gla_lat" if has_state else "gla_ctx",
    )(*args)
    return out if emit_state else (out[0], None)


def _softmax_sink(s_parts, sink):
    m = sink
    for s in s_parts:
        m = jnp.maximum(m, jnp.max(s, axis=-1, keepdims=True))
    es = [jnp.exp(s - m) for s in s_parts]
    den = jnp.exp(sink - m)
    for e in es:
        den = den + jnp.sum(e, axis=-1, keepdims=True)
    return [e / den for e in es]


def _attn_ctx_kernel(sink_ref, q_ref, kv_ref, o_ref):
    scale = HEAD_DIM ** -0.5
    outs = []
    for kv in range(ATT_KV_HEADS):
        k = kv_ref[:, kv * HEAD_DIM:(kv + 1) * HEAD_DIM].astype(BF16)
        v = kv_ref[:, LANES + kv * HEAD_DIM:LANES + (kv + 1) * HEAD_DIM].astype(BF16)
        for g in range(ATT_GROUP):
            h = kv * ATT_GROUP + g
            q = q_ref[:, h * HEAD_DIM:(h + 1) * HEAD_DIM].astype(BF16)
            s = _dot_t(q, k) * scale
            sink = jnp.full((1, 1), sink_ref[h], F32)
            (p,) = _softmax_sink([s], sink)
            outs.append(jnp.dot(p.astype(BF16), v, preferred_element_type=F32))
    o_ref[...] = jnp.concatenate(outs, axis=-1).astype(o_ref.dtype)


def _attn_ctx(p, sink):
    return pl.pallas_call(
        _attn_ctx_kernel,
        grid=(BATCH,),
        in_specs=[
            pl.BlockSpec(memory_space=pltpu.SMEM),
            pl.BlockSpec((SEQ, ATT_HEADS * HEAD_DIM), lambda s: (s, C_TQ // 512)),
            pl.BlockSpec((SEQ, 2 * LANES), lambda s: (s, C_TKV // 256)),
        ],
        out_specs=pl.BlockSpec((SEQ, ATT_HEADS * HEAD_DIM), lambda s: (s, 0)),
        out_shape=jax.ShapeDtypeStruct((T_CTX, ATT_HEADS * HEAD_DIM), BF16),
        compiler_params=_cparams("parallel"),
        name="attn_ctx",
    )(sink, p, p)


def _rope(x, cos, sin_signed):
    lane = lax.broadcasted_iota(jnp.int32, x.shape, 1)
    partner = jnp.where(lane % 32 < 16, pltpu.roll(x, LANES - 16, 1), pltpu.roll(x, 16, 1))
    return x * cos + partner * sin_signed


def _attn_lat_kernel(sink_ref, q_ref, kvp_ref, kvc_ref, kvn_ref, kctx_ref, vctx_ref, cos_ref, sin_ref, o_ref):
    scale = HEAD_DIM ** -0.5
    n = pl.program_id(1)
    n_blk = DEC_SEQ // BLOCK
    start = n * BLOCK

    def table(ref, blk):
        blk = jnp.clip(blk, 0, n_blk - 1)
        return ref[pl.ds(pl.multiple_of(blk * BLOCK, BLOCK), BLOCK), :]

    k_loc, v_loc = [], []
    for off, ref in ((-1, kvp_ref), (0, kvc_ref), (1, kvn_ref)):
        kr = _rope(ref[:, :LANES], table(cos_ref, n + off), table(sin_ref, n + off))
        k_loc.append(kr)
        v_loc.append(ref[:, LANES:])
    k_loc = jnp.concatenate(k_loc, axis=0)
    v_loc = jnp.concatenate(v_loc, axis=0)
    k_ctx = kctx_ref[...]
    v_ctx = vctx_ref[...]

    def head_bf16(a, kv):
        return a[:, kv * HEAD_DIM:(kv + 1) * HEAD_DIM].astype(BF16)

    k_loc = [head_bf16(k_loc, kv) for kv in range(ATT_KV_HEADS)]
    v_loc = [head_bf16(v_loc, kv) for kv in range(ATT_KV_HEADS)]
    k_ctx = [head_bf16(k_ctx, kv) for kv in range(ATT_KV_HEADS)]
    v_ctx = [head_bf16(v_ctx, kv) for kv in range(ATT_KV_HEADS)]

    span = 3 * BLOCK
    qpos = start + lax.broadcasted_iota(jnp.int32, (BLOCK, span), 0)
    kpos = start - WINDOW + lax.broadcasted_iota(jnp.int32, (BLOCK, span), 1)
    valid = (jnp.abs(qpos - kpos) <= WINDOW) & (kpos >= 0) & (kpos < DEC_SEQ)

    cos_q = table(cos_ref, n)
    sin_q = table(sin_ref, n)
    outs = []
    for pair in range(ATT_HEADS // 2):
        qr = _rope(q_ref[:, pair * LANES:(pair + 1) * LANES], cos_q, sin_q)
        for sub in range(2):
            h = 2 * pair + sub
            kv = h // ATT_GROUP
            q = head_bf16(qr, sub)
            s_ctx = _dot_t(q, k_ctx[kv]) * scale
            s_loc = jnp.where(valid, _dot_t(q, k_loc[kv]) * scale, NEG_INF)
            sink = jnp.full((1, 1), sink_ref[h], F32)
            p_ctx, p_loc = _softmax_sink([s_ctx, s_loc], sink)
            o = (jnp.dot(p_ctx.astype(BF16), v_ctx[kv], preferred_element_type=F32)
                 + jnp.dot(p_loc.astype(BF16), v_loc[kv], preferred_element_type=F32))
            outs.append(o)
    o_ref[...] = jnp.concatenate(outs, axis=-1).astype(o_ref.dtype)


def _attn_lat(p, sink, k_ctx, v_ctx, cos_t, sin_t):
    n_blk = DEC_SEQ // BLOCK
    rb0 = T_CTX // BLOCK

    def kv_spec(off):
        return pl.BlockSpec(
            (BLOCK, 2 * LANES),
            lambda b, n: (rb0 + b * n_blk + jnp.clip(n + off, 0, n_blk - 1), C_TKV // 256))

    return pl.pallas_call(
        _attn_lat_kernel,
        grid=(DEC_BATCH, n_blk),
        in_specs=[
            pl.BlockSpec(memory_space=pltpu.SMEM),
            pl.BlockSpec((BLOCK, ATT_HEADS * HEAD_DIM), lambda b, n: (rb0 + b * n_blk + n, C_TQ // 512)),
            kv_spec(-1), kv_spec(0), kv_spec(1),
            pl.BlockSpec((None, PAST_LEN, LANES), lambda b, n: (b, 0, 0)),
            pl.BlockSpec((None, PAST_LEN, LANES), lambda b, n: (b, 0, 0)),
            pl.BlockSpec((DEC_SEQ, LANES), lambda b, n: (0, 0)),
            pl.BlockSpec((DEC_SEQ, LANES), lambda b, n: (0, 0)),
        ],
        out_specs=pl.BlockSpec((BLOCK, ATT_HEADS * HEAD_DIM), lambda b, n: (b * n_blk + n, 0)),
        out_shape=jax.ShapeDtypeStruct((T_LAT, ATT_HEADS * HEAD_DIM), BF16),
        compiler_params=_cparams("parallel", "parallel"),
        name="attn_lat",
    )(sink, p, p, p, p, k_ctx, v_ctx, cos_t, sin_t)


def _rope_tables():
    pos = np.arange(DEC_SEQ)
    n_freq = HEAD_DIM // 4
    inv = jnp.asarray(ROPE_THETA, F32) ** (-jnp.arange(n_freq, dtype=F32) / n_freq)
    row = jnp.asarray(pos // GRID_W, F32)
    colp = jnp.asarray(pos % GRID_W, F32)
    ang_r = row[:, None] * inv[None, :]
    ang_c = colp[:, None] * inv[None, :]
    cos = jnp.concatenate([jnp.cos(ang_r)] * 2 + [jnp.cos(ang_c)] * 2, axis=-1)
    sin = jnp.concatenate([-jnp.sin(ang_r), jnp.sin(ang_r), -jnp.sin(ang_c), jnp.sin(ang_c)], axis=-1)
    return jnp.tile(cos, (1, 2)), jnp.tile(sin, (1, 2))


def _split_dot(a, w_hi, w_lo):
    a_hi = a.astype(BF16)
    a_lo = (a - a_hi.astype(F32)).astype(BF16)
    return (jnp.dot(a_hi, w_hi, preferred_element_type=F32)
            + jnp.dot(a_lo, w_hi, preferred_element_type=F32)
            + jnp.dot(a_hi, w_lo, preferred_element_type=F32))


def _route(logits):
    lane_i = lax.broadcasted_iota(jnp.int32, logits.shape, 1)
    lane = lane_i.astype(F32)
    big = jnp.float32(1 << 20)
    is_g = lane_i < N_EXPERT_GROUPS
    lg = jnp.where(is_g, logits, -jnp.inf)
    m_g = jnp.max(lg, axis=-1, keepdims=True)
    grp = jnp.min(jnp.where(lg == m_g, lane, big), axis=-1, keepdims=True)
    z_g = jnp.sum(jnp.where(is_g, jnp.exp(lg - m_g), 0.0), axis=-1, keepdims=True)
    p_grp = 1.0 / z_g

    e_idx = lane_i - ROUTE_E0
    e_grp = (e_idx >> 2).astype(F32)
    sel = (e_idx >= 0) & (e_idx < N_EXPERTS) & (e_grp == grp)
    le = jnp.where(sel, logits, -jnp.inf)
    m_e = jnp.max(le, axis=-1, keepdims=True)
    ex = jnp.where(sel, jnp.exp(le - m_e), 0.0)
    pe = ex / jnp.sum(ex, axis=-1, keepdims=True)
    pe = jnp.where(sel, pe, -1.0)
    v1 = jnp.max(pe, axis=-1, keepdims=True)
    i1 = jnp.min(jnp.where(pe == v1, lane, big), axis=-1, keepdims=True)
    pe2 = jnp.where(lane == i1, -1.0, pe)
    v2 = jnp.max(pe2, axis=-1, keepdims=True)
    i2 = jnp.min(jnp.where(pe2 == v2, lane, big), axis=-1, keepdims=True)
    tot = v1 + v2
    return (jnp.where(lane == i1, p_grp * (v1 / tot), 0.0)
            + jnp.where(lane == i2, p_grp * (v2 / tot), 0.0))


def _merge_kernel(gate_ref, conv_ref, cprev_ref, cnext_ref, yb_ref, yc_ref, x_ref, mod_ref, cw_ref,
                  wb_ref, wo_ref, g2_ref, wr_hi_ref, wr_lo_ref, br_ref,
                  x1_ref, h2_ref, rg_ref, *, tm):
    i = pl.program_id(0)
    a_b = conv_ref[:, 0:CONV_W]
    u = conv_ref[:, CONV_W:2 * CONV_W] * conv_ref[:, 2 * CONV_W:3 * CONV_W]
    u_before = cprev_ref[7:8, CONV_W:2 * CONV_W] * cprev_ref[7:8, 2 * CONV_W:3 * CONV_W]
    u_after = cnext_ref[0:1, CONV_W:2 * CONV_W] * cnext_ref[0:1, 2 * CONV_W:3 * CONV_W]
    r = lax.broadcasted_iota(jnp.int32, (tm, 1), 0)
    g_row = i * tm + r
    seq_mask = jnp.where(g_row < T_CTX, SEQ - 1, DEC_SEQ - 1)
    first = (g_row & seq_mask) == 0
    last = ((g_row + 1) & seq_mask) == 0
    u_prev = jnp.where(r == 0, u_before, pltpu.roll(u, 1, 0))
    u_next = jnp.where(r == tm - 1, u_after, pltpu.roll(u, tm - 1, 0))
    u_prev = jnp.where(first, 0.0, u_prev)
    u_next = jnp.where(last, 0.0, u_next)
    y_a = a_b * (u_prev * cw_ref[0:1, :] + u * cw_ref[1:2, :] + u_next * cw_ref[2:3, :])

    def branch(j, y):
        m = gate_ref[:, j * D_MODEL:(j + 1) * D_MODEL]
        return jax.nn.sigmoid(m) * jnp.dot(y, wb_ref[j], preferred_element_type=F32)

    z = branch(0, y_a.astype(BF16)) + branch(1, yb_ref[...]) + branch(2, yc_ref[...])
    x1 = x_ref[...] + mod_ref[2:3, :] * jnp.dot(z.astype(BF16), wo_ref[...], preferred_element_type=F32)
    x1_ref[...] = x1
    h2 = _rms(x1) * g2_ref[...] * (1.0 + mod_ref[4:5, :]) + mod_ref[3:4, :]
    h2_ref[...] = h2.astype(BF16)
    logits = _split_dot(h2, wr_hi_ref[...], wr_lo_ref[...]) + br_ref[...]
    rg_ref[...] = _route(logits)


def _merge(p, y_b, y_c, x, mod_l, conv_w, wb, wo, g2, wr_hi, wr_lo, br):
    tm = 256
    n_tiles = T_ALL // tm
    hb = tm // 8
    conv_blk = C_CONV // (3 * CONV_W)
    const = lambda shape: pl.BlockSpec(shape, lambda i: (0,) * len(shape))
    return pl.pallas_call(
        functools.partial(_merge_kernel, tm=tm),
        grid=(n_tiles,),
        in_specs=[
            pl.BlockSpec((tm, 3 * D_MODEL), lambda i: (i, 0)),
            pl.BlockSpec((tm, 3 * CONV_W), lambda i: (i, conv_blk)),
            pl.BlockSpec((8, 3 * CONV_W), lambda i: (jnp.maximum(i * hb - 1, 0), conv_blk)),
            pl.BlockSpec((8, 3 * CONV_W), lambda i: (jnp.minimum((i + 1) * hb, n_tiles * hb - 1), conv_blk)),
            pl.BlockSpec((tm, 512), lambda i: (i, 0)),
            pl.BlockSpec((tm, 512), lambda i: (i, 0)),
            pl.BlockSpec((tm, D_MODEL), lambda i: (i, 0)),
            pl.BlockSpec((None, 6, D_MODEL), lambda i: (_cond_row(i, tm), 0, 0)),
            const((CONV_K, CONV_W)),
            const((3, 512, D_MODEL)),
            const((D_MODEL, D_MODEL)),
            const((1, D_MODEL)),
            const((D_MODEL, LANES)),
            const((D_MODEL, LANES)),
            const((1, LANES)),
        ],
        out_specs=[
            pl.BlockSpec((tm, D_MODEL), lambda i: (i, 0)),
            pl.BlockSpec((tm, D_MODEL), lambda i: (i, 0)),
            pl.BlockSpec((tm, LANES), lambda i: (i, 0)),
        ],
        out_shape=[
            jax.ShapeDtypeStruct((T_ALL, D_MODEL), F32),
            jax.ShapeDtypeStruct((T_ALL, D_MODEL), BF16),
            jax.ShapeDtypeStruct((T_ALL, LANES), F32),
        ],
        compiler_params=_cparams("parallel"),
        name="merge",
    )(p, p, p, p, y_b, y_c, x, mod_l, conv_w, wb, wo, g2, wr_hi, wr_lo, br)


def _moe_kernel(h_ref, rg_ref, x1_ref, mod_ref, wgu_ref, wd_ref, gf_ref, *out_and_scratch, final):
    if final:
        x2_ref, y_ref, acc_ref = out_and_scratch
    else:
        x2_ref, acc_ref = out_and_scratch
    e = pl.program_id(1)

    @pl.when(e == 0)
    def _():
        acc_ref[...] = jnp.zeros_like(acc_ref)

    gu = jnp.dot(h_ref[...], wgu_ref[...], preferred_element_type=F32)
    g = gu[:, :D_EXPERT]
    u = gu[:, D_EXPERT:]
    rg = rg_ref[...]
    lane = lax.broadcasted_iota(jnp.int32, rg.shape, 1)
    w = jnp.sum(jnp.where(lane == e + ROUTE_E0, rg, 0.0), axis=-1, keepdims=True)
    hid = (g * jax.nn.sigmoid(g)) * u * w
    acc_ref[...] += jnp.dot(hid.astype(BF16), wd_ref[...], preferred_element_type=F32)

    @pl.when(e == N_EXPERTS - 1)
    def _():
        x2 = x1_ref[...] + mod_ref[5:6, :] * acc_ref[...]
        x2_ref[...] = x2
        if final:
            y_ref[...] = _rms(x2) * gf_ref[...]


def _moe(h2, rg, x1, mod_l, wgu, wd, gf, final):
    tm = 1024
    out_specs = [pl.BlockSpec((tm, D_MODEL), lambda i, e: (i, 0))]
    out_shape = [jax.ShapeDtypeStruct((T_ALL, D_MODEL), F32)]
    if final:
        out_specs = out_specs * 2
        out_shape = out_shape * 2
    return pl.pallas_call(
        functools.partial(_moe_kernel, final=final),
        grid=(T_ALL // tm, N_EXPERTS),
        in_specs=[
            pl.BlockSpec((tm, D_MODEL), lambda i, e: (i, 0)),
            pl.BlockSpec((tm, LANES), lambda i, e: (i, 0)),
            pl.BlockSpec((tm, D_MODEL), lambda i, e: (i, 0)),
            pl.BlockSpec((None, 6, D_MODEL), lambda i, e: (_cond_row(i, tm), 0, 0)),
            pl.BlockSpec((None, D_MODEL, 2 * D_EXPERT), lambda i, e: (e, 0, 0)),
            pl.BlockSpec((None, D_EXPERT, D_MODEL), lambda i, e: (e, 0, 0)),
            pl.BlockSpec((1, D_MODEL), lambda i, e: (0, 0)),
        ],
        out_specs=out_specs,
        out_shape=out_shape,
        scratch_shapes=[pltpu.VMEM((tm, D_MODEL), F32)],
        compiler_params=_cparams("parallel", "arbitrary"),
        name="moe",
    )(h2, rg, x1, mod_l, wgu, wd, gf)


def _split_cols(w):
    sizes = (CONV_W, CONV_W, CONV_W, 512, 512, 512, 512, GLA_RANK, GLA_RANK, 512, 128, 128,
             D_MODEL, D_MODEL, D_MODEL)
    out, start = [], 0
    for n in sizes:
        out.append(w[..., start:start + n])
        start += n
    return out


def _prep_w_in(w):
    (a_b, a_c, a_h, g_q, g_k, g_v, g_r, g_lf, g_lb, t_q, t_k, t_v, m_a, m_b, m_c) = _split_cols(w)
    pad = jnp.zeros((D_MODEL, N_PROJ - C_LR - 2 * GLA_RANK), w.dtype)
    return jnp.concatenate([m_a, m_b, m_c, a_b, a_c, a_h, g_q, g_k, g_v, g_r, t_q, t_k, t_v, g_lf, g_lb, pad],
                           axis=-1).astype(BF16)


def _prep_gla_gate(w_gate):
    out = jnp.zeros((2, LANES, GLA_HEADS * GLA_DK), F32)
    out = out.at[0, 0:GLA_RANK].set(w_gate[0])
    out = out.at[1, GLA_RANK:2 * GLA_RANK].set(w_gate[1])
    return out


def _prep_router(w_rg, b_rg, w_re, b_re):
    w = jnp.zeros((D_MODEL, LANES), F32)
    w = w.at[:, :N_EXPERT_GROUPS].set(w_rg).at[:, ROUTE_E0:ROUTE_E0 + N_EXPERTS].set(w_re)
    b = jnp.zeros((1, LANES), F32)
    b = b.at[0, :N_EXPERT_GROUPS].set(b_rg).at[0, ROUTE_E0:ROUTE_E0 + N_EXPERTS].set(b_re)
    w_hi = w.astype(BF16)
    w_lo = (w - w_hi.astype(F32)).astype(BF16)
    return w_hi, w_lo, b


def kernel(x_prompt, x_sample, state_gla, cache_k, cache_v, c, c_ctx, w_ada, b_ada, norm1_g, norm2_g, w_in,
           conv_w, gla_w_gate, gla_b_gate, gla_norm_g, attn_sink, w_branch, w_out, w_route_group,
           b_route_group, w_route_expert, b_route_expert, w_exp_gate, w_exp_up, w_exp_down, final_norm_g):
    x = jnp.concatenate([x_prompt.reshape(T_CTX, D_MODEL), x_sample.reshape(T_LAT, D_MODEL)], axis=0)
    cond = jnp.zeros((N_COND, D_MODEL), F32).at[0].set(c_ctx).at[1:1 + DEC_BATCH].set(c)
    mod = _modulation(cond, w_ada, b_ada).reshape(DEPTH, N_COND, 6, D_MODEL)
    cos_t, sin_t = _rope_tables()
    gf = final_norm_g.reshape(1, D_MODEL)

    states, keys, vals = [], [], []
    y = None
    for l in range(DEPTH):
        p = _in_proj(x, mod[l], norm1_g[l].reshape(1, D_MODEL), _prep_w_in(w_in[l]))
        wg_pad = _prep_gla_gate(gla_w_gate[l])
        bg = gla_b_gate[l].reshape(2, 1, GLA_HEADS * GLA_DK)
        ng = gla_norm_g[l].reshape(1, GLA_HEADS * GLA_DV)
        yb_ctx, s_ctx = _gla(p, 0, BATCH, SEQ, wg_pad, bg, ng, None)
        yb_lat, _ = _gla(p, T_CTX, DEC_BATCH, DEC_SEQ, wg_pad, bg, ng, state_gla[:, l])
        yc_ctx = _attn_ctx(p, attn_sink[l])
        yc_lat = _attn_lat(p, attn_sink[l],
                           cache_k[:, l].reshape(DEC_BATCH, PAST_LEN, LANES),
                           cache_v[:, l].reshape(DEC_BATCH, PAST_LEN, LANES), cos_t, sin_t)
        y_b = jnp.concatenate([yb_ctx, yb_lat], axis=0)
        y_c = jnp.concatenate([yc_ctx, yc_lat], axis=0)
        wr_hi, wr_lo, br = _prep_router(w_route_group[l], b_route_group[l], w_route_expert[l], b_route_expert[l])
        x1, h2, rg = _merge(p, y_b, y_c, x, mod[l], conv_w[l], w_branch[l].astype(BF16), w_out[l].astype(BF16),
                            norm2_g[l].reshape(1, D_MODEL), wr_hi, wr_lo, br)
        wgu = jnp.concatenate([w_exp_gate[l], w_exp_up[l]], axis=-1).astype(BF16)
        final = l == DEPTH - 1
        out = _moe(h2, rg, x1, mod[l], wgu, w_exp_down[l].astype(BF16), gf, final)
        if final:
            x, y = out
        else:
            (x,) = out
        states.append(s_ctx)
        keys.append(p[:T_CTX, C_TKV:C_TKV + LANES].reshape(BATCH, SEQ, ATT_KV_HEADS, HEAD_DIM))
        vals.append(p[:T_CTX, C_TKV + LANES:C_TKV + 2 * LANES].reshape(BATCH, SEQ, ATT_KV_HEADS, HEAD_DIM))

    y_prompt = y[:T_CTX].reshape(BATCH, SEQ, D_MODEL)
    y_sample = y[T_CTX:].reshape(DEC_BATCH, DEC_SEQ, D_MODEL)
    return (y_prompt, y_sample, jnp.stack(states, axis=1), jnp.stack(keys, axis=1), jnp.stack(vals, axis=1))
```

```python
import functools

import jax
import jax.numpy as jnp
import numpy as np
from jax import lax
from jax.experimental import pallas as pl
from jax.experimental.pallas import tpu as pltpu

F32 = jnp.float32
BF16 = jnp.bfloat16

D_MODEL = 1024
BATCH = 32
SEQ = 256
DEPTH = 2
DEC_BATCH = 2
DEC_SEQ = 2048
PAST_LEN = 512
GRID_W = 64
EPS = 1e-6
CONV_W = 512
CONV_K = 3
GLA_HEADS = 4
GLA_DK = 128
GLA_DV = 128
GLA_RANK = 16
GLA_TAU = 16.0
GLA_CHUNK = 64
ATT_HEADS = 8
ATT_KV_HEADS = 2
ATT_GROUP = ATT_HEADS // ATT_KV_HEADS
HEAD_DIM = 64
WINDOW = 128
BLOCK = 128
ROPE_THETA = 10000.0
N_EXPERT_GROUPS = 4
EXPERTS_PER_GROUP = 4
N_EXPERTS = 16
D_EXPERT = 256
NEG_INF = -1e30

T_CTX = BATCH * SEQ
T_LAT = DEC_BATCH * DEC_SEQ
T_ALL = T_CTX + T_LAT
N_COND = 8

C_GATE = 0
C_CONV = 3072
C_GLA = 4608
C_TQ = 6656
C_TKV = 7168
C_LR = 7424
N_PROJ = 7680
LANES = 128
ROUTE_E0 = N_EXPERT_GROUPS

VMEM_LIMIT = 56 * 1024 * 1024


def _cparams(*sem):
    return pltpu.CompilerParams(dimension_semantics=sem, vmem_limit_bytes=VMEM_LIMIT)


def _cond_row(i, tm):
    n_ctx = T_CTX // tm
    per = DEC_SEQ // tm
    return jnp.where(i < n_ctx, 0, 1 + (i - n_ctx) // per)


def _mod_kernel(c_ref, w_ref, b_ref, o_ref):
    c = c_ref[...]
    s = (c * jax.nn.sigmoid(c)).astype(BF16)
    o_ref[...] = jnp.dot(s, w_ref[...].astype(BF16), preferred_element_type=F32) + b_ref[...]


def _modulation(cond, w_ada, b_ada):
    tn = 1536
    return pl.pallas_call(
        _mod_kernel,
        grid=(DEPTH, 6 * D_MODEL // tn),
        in_specs=[
            pl.BlockSpec((N_COND, D_MODEL), lambda l, j: (0, 0)),
            pl.BlockSpec((None, D_MODEL, tn), lambda l, j: (l, 0, j)),
            pl.BlockSpec((None, 1, tn), lambda l, j: (l, 0, j)),
        ],
        out_specs=pl.BlockSpec((None, N_COND, tn), lambda l, j: (l, 0, j)),
        out_shape=jax.ShapeDtypeStruct((DEPTH, N_COND, 6 * D_MODEL), F32),
        compiler_params=_cparams("parallel", "parallel"),
        name="modulation",
    )(cond, w_ada, b_ada.reshape(DEPTH, 1, 6 * D_MODEL))


def _rms(x):
    return x * lax.rsqrt(jnp.mean(x * x, axis=-1, keepdims=True) + EPS)


def _in_proj_kernel(x_ref, mod_ref, g_ref, w_ref, o_ref, h_ref):
    @pl.when(pl.program_id(1) == 0)
    def _():
        y = _rms(x_ref[...]) * g_ref[...]
        h_ref[...] = (y * (1.0 + mod_ref[1:2, :]) + mod_ref[0:1, :]).astype(BF16)

    o_ref[...] = jnp.dot(h_ref[...], w_ref[...], preferred_element_type=F32)


def _in_proj(x, mod_l, g, w):
    tm, tn = 1024, 1280
    return pl.pallas_call(
        _in_proj_kernel,
        grid=(T_ALL // tm, N_PROJ // tn),
        in_specs=[
            pl.BlockSpec((tm, D_MODEL), lambda i, j: (i, 0)),
            pl.BlockSpec((None, 6, D_MODEL), lambda i, j: (_cond_row(i, tm), 0, 0)),
            pl.BlockSpec((1, D_MODEL), lambda i, j: (0, 0)),
            pl.BlockSpec((D_MODEL, tn), lambda i, j: (0, j)),
        ],
        out_specs=pl.BlockSpec((tm, tn), lambda i, j: (i, j)),
        out_shape=jax.ShapeDtypeStruct((T_ALL, N_PROJ), F32),
        scratch_shapes=[pltpu.VMEM((tm, D_MODEL), BF16)],
        compiler_params=_cparams("parallel", "arbitrary"),
        name="in_proj",
    )(x, mod_l, g, w)


def _log_sigmoid(z):
    return jnp.minimum(z, 0.0) - jnp.log1p(jnp.exp(-jnp.abs(z)))


def _dot_t(a, b):
    return lax.dot_general(a, b, (((1,), (1,)), ((), ())), preferred_element_type=F32)


def _dot_ta(a, b):
    return lax.dot_general(a, b, (((0,), (0,)), ((), ())), preferred_element_type=F32)


GLA_GROUP = 256


def _mm(a, b):
    return jnp.dot(a, b, preferred_element_type=F32)


def _split3(x):
    hi = x.astype(BF16)
    r1 = x - hi.astype(F32)
    mid = r1.astype(BF16)
    lo = (r1 - mid.astype(F32)).astype(BF16)
    return hi, mid, lo


def _gla_keep(d):
    ri = lax.broadcasted_iota(jnp.int32, (GLA_GROUP, GLA_GROUP), 0)
    ci = lax.broadcasted_iota(jnp.int32, (GLA_GROUP, GLA_GROUP), 1)
    if d == 0:
        return (ci <= ri) & (ci >= (ri & ~(GLA_CHUNK - 1)))
    return (ci >= ri) & (ci <= (ri | (GLA_CHUNK - 1)))


def _gla_group(q, k, v, lr, wg_hi, wg_lo, bg, states, d, keep):
    C = GLA_CHUNK
    nc = GLA_GROUP // C
    width = q.shape[-1]
    lr_hi = lr.astype(BF16)
    lr_lo = (lr - lr_hi.astype(F32)).astype(BF16)
    z = _mm(lr_hi, wg_hi) + _mm(lr_lo, wg_hi) + _mm(lr_hi, wg_lo) + bg
    la = _log_sigmoid(z) / GLA_TAU
    tri = jnp.where(keep, 1.0, 0.0).astype(BF16)
    la_hi, la_mid, la_lo = _split3(la)
    b = _mm(tri, la_hi) + _mm(tri, la_mid) + _mm(tri, la_lo)
    edge = C - 1 if d == 0 else 0
    b_last = [b[c * C + edge:c * C + edge + 1, :] for c in range(nc)]
    bl = jnp.concatenate([jnp.broadcast_to(t, (C, width)) for t in b_last], axis=0)
    q_in = (q * (GLA_DK ** -0.5) * jnp.exp(b)).astype(BF16)
    k_in = (k * jnp.exp(-b)).astype(BF16)
    k_end = (k * jnp.exp(bl - b)).astype(BF16)
    vb = v.astype(BF16)
    dec = [jnp.exp(t) for t in b_last]
    order = range(nc) if d == 0 else range(nc - 1, -1, -1)
    outs, new_states = [], []
    for h, st in enumerate(states):
        cs = slice(h * GLA_DK, (h + 1) * GLA_DK)
        att = jnp.where(keep, _dot_t(q_in[:, cs], k_in[:, cs]), 0.0)
        o = _mm(att.astype(BF16), vb[:, cs])
        pieces = [None] * nc
        for c in order:
            rs = slice(c * C, (c + 1) * C)
            pieces[c] = o[rs] + _dot_t(q_in[rs, cs], st.astype(BF16))
            st = st * dec[c][:, cs] + _dot_ta(vb[rs, cs], k_end[rs, cs])
        outs.append(jnp.concatenate(pieces, axis=0))
        new_states.append(st)
    return outs, new_states


def _gla_finish(o, r, ng):
    o = o * lax.rsqrt(jnp.mean(o * o, axis=-1, keepdims=True) + EPS)
    return o * ng * (r * jax.nn.sigmoid(r))


def _gla_ctx_kernel(q_ref, k_ref, v_ref, r_ref, lr_ref, wgh_ref, wgl_ref, bg_ref, ng_ref, y_ref, s_ref):
    q, k, v, lr = q_ref[...], k_ref[...], v_ref[...], lr_ref[...]
    zero = jnp.zeros((GLA_DV, GLA_DK), F32)
    o_dir = []
    for d in range(2):
        outs, sts = _gla_group(q, k, v, lr, wgh_ref[d], wgl_ref[d], bg_ref[d], [zero] * GLA_HEADS, d,
                               _gla_keep(d))
        o_dir.append(outs)
        for h in range(GLA_HEADS):
            s_ref[d, h] = sts[h].T
    for h in range(GLA_HEADS):
        cs = slice(h * GLA_DV, (h + 1) * GLA_DV)
        y = _gla_finish(o_dir[0][h] + o_dir[1][h], r_ref[:, cs], ng_ref[:, cs])
        y_ref[:, cs] = y.astype(y_ref.dtype)


def _gla_lat_kernel(q_ref, k_ref, v_ref, r_ref, lr_ref, wgh_ref, wgl_ref, bg_ref, ng_ref, s0_ref, y_ref,
                    of_ref, ob_ref):
    n_groups = DEC_SEQ // GLA_GROUP
    keep_f, keep_b = _gla_keep(0), _gla_keep(1)

    def body(g, carry):
        st_f, st_b = carry
        rf = pl.ds(pl.multiple_of(g * GLA_GROUP, GLA_GROUP), GLA_GROUP)
        rb = pl.ds(pl.multiple_of((n_groups - 1 - g) * GLA_GROUP, GLA_GROUP), GLA_GROUP)
        (o,), (st_f,) = _gla_group(q_ref[rf, :], k_ref[rf, :], v_ref[rf, :], lr_ref[rf, :],
                                   wgh_ref[0], wgl_ref[0], bg_ref[0], [st_f], 0, keep_f)
        of_ref[rf, :] = o
        (o,), (st_b,) = _gla_group(q_ref[rb, :], k_ref[rb, :], v_ref[rb, :], lr_ref[rb, :],
                                   wgh_ref[1], wgl_ref[1], bg_ref[1], [st_b], 1, keep_b)
        ob_ref[rb, :] = o
        return st_f, st_b

    lax.fori_loop(0, n_groups, body, (s0_ref[0].T, s0_ref[1].T))
    y_ref[...] = _gla_finish(of_ref[...] + ob_ref[...], r_ref[...], ng_ref[...]).astype(y_ref.dtype)


def _gla_ctx(p, wg_hi, wg_lo, bg, ng):
    width = GLA_HEADS * GLA_DK
    cb = C_GLA // width
    const = lambda shape: pl.BlockSpec(shape, lambda s: (0,) * len(shape))
    return pl.pallas_call(
        _gla_ctx_kernel,
        grid=(BATCH,),
        in_specs=[pl.BlockSpec((SEQ, width), lambda s, j=j: (s, cb + j)) for j in range(4)] + [
            pl.BlockSpec((SEQ, LANES), lambda s: (s, C_LR // LANES)),
            const((2, LANES, width)), const((2, LANES, width)), const((2, 1, width)), const((1, width))],
        out_specs=[pl.BlockSpec((SEQ, width), lambda s: (s, 0)),
                   pl.BlockSpec((None, 2, GLA_HEADS, GLA_DK, GLA_DV), lambda s: (s, 0, 0, 0, 0))],
        out_shape=[jax.ShapeDtypeStruct((T_CTX, width), BF16),
                   jax.ShapeDtypeStruct((BATCH, 2, GLA_HEADS, GLA_DK, GLA_DV), F32)],
        compiler_params=_cparams("parallel"),
        name="gla_ctx",
    )(p, p, p, p, p, wg_hi, wg_lo, bg, ng)


def _gla_lat(p, wg_hi, wg_lo, bg, ng, s0):
    rb0 = T_CTX // DEC_SEQ
    cb = C_GLA // LANES

    def col(piece):
        return pl.BlockSpec((DEC_SEQ, LANES), lambda s, h: (rb0 + s, cb + GLA_HEADS * piece + h))

    return pl.pallas_call(
        _gla_lat_kernel,
        grid=(DEC_BATCH, GLA_HEADS),
        in_specs=[col(0), col(1), col(2), col(3),
                  pl.BlockSpec((DEC_SEQ, LANES), lambda s, h: (rb0 + s, C_LR // LANES)),
                  pl.BlockSpec((2, LANES, LANES), lambda s, h: (0, 0, h)),
                  pl.BlockSpec((2, LANES, LANES), lambda s, h: (0, 0, h)),
                  pl.BlockSpec((2, 1, LANES), lambda s, h: (0, 0, h)),
                  pl.BlockSpec((1, LANES), lambda s, h: (0, h)),
                  pl.BlockSpec((None, 2, None, GLA_DK, GLA_DV), lambda s, h: (s, 0, h, 0, 0))],
        out_specs=pl.BlockSpec((DEC_SEQ, LANES), lambda s, h: (s, h)),
        out_shape=jax.ShapeDtypeStruct((T_LAT, GLA_HEADS * GLA_DV), BF16),
        scratch_shapes=[pltpu.VMEM((DEC_SEQ, GLA_DV), F32), pltpu.VMEM((DEC_SEQ, GLA_DV), F32)],
        compiler_params=_cparams("parallel", "parallel"),
        name="gla_lat",
    )(p, p, p, p, p, wg_hi, wg_lo, bg, ng, s0)


def _softmax_sink(s_parts, sink):
    m = sink
    for s in s_parts:
        m = jnp.maximum(m, jnp.max(s, axis=-1, keepdims=True))
    es = [jnp.exp(s - m) for s in s_parts]
    den = jnp.exp(sink - m)
    for e in es:
        den = den + jnp.sum(e, axis=-1, keepdims=True)
    return [e / den for e in es]


def _attn_ctx_kernel(sink_ref, q_ref, kv_ref, o_ref):
    scale = HEAD_DIM ** -0.5
    outs = []
    for kv in range(ATT_KV_HEADS):
        k = kv_ref[:, kv * HEAD_DIM:(kv + 1) * HEAD_DIM].astype(BF16)
        v = kv_ref[:, LANES + kv * HEAD_DIM:LANES + (kv + 1) * HEAD_DIM].astype(BF16)
        for g in range(ATT_GROUP):
            h = kv * ATT_GROUP + g
            q = q_ref[:, h * HEAD_DIM:(h + 1) * HEAD_DIM].astype(BF16)
            s = _dot_t(q, k) * scale
            sink = jnp.full((1, 1), sink_ref[h], F32)
            (p,) = _softmax_sink([s], sink)
            outs.append(jnp.dot(p.astype(BF16), v, preferred_element_type=F32))
    o_ref[...] = jnp.concatenate(outs, axis=-1).astype(o_ref.dtype)


def _attn_ctx(p, sink):
    return pl.pallas_call(
        _attn_ctx_kernel,
        grid=(BATCH,),
        in_specs=[
            pl.BlockSpec(memory_space=pltpu.SMEM),
            pl.BlockSpec((SEQ, ATT_HEADS * HEAD_DIM), lambda s: (s, C_TQ // 512)),
            pl.BlockSpec((SEQ, 2 * LANES), lambda s: (s, C_TKV // 256)),
        ],
        out_specs=pl.BlockSpec((SEQ, ATT_HEADS * HEAD_DIM), lambda s: (s, 0)),
        out_shape=jax.ShapeDtypeStruct((T_CTX, ATT_HEADS * HEAD_DIM), BF16),
        compiler_params=_cparams("parallel"),
        name="attn_ctx",
    )(sink, p, p)


def _rope(x, cos, sin_signed):
    lane = lax.broadcasted_iota(jnp.int32, x.shape, 1)
    partner = jnp.where(lane % 32 < 16, pltpu.roll(x, LANES - 16, 1), pltpu.roll(x, 16, 1))
    return x * cos + partner * sin_signed


def _attn_lat_kernel(sink_ref, q_ref, kvp_ref, kvc_ref, kvn_ref, kctx_ref, vctx_ref, cos_ref, sin_ref, o_ref):
    scale = HEAD_DIM ** -0.5
    n = pl.program_id(1)
    n_blk = DEC_SEQ // BLOCK
    start = n * BLOCK

    def table(ref, blk):
        blk = jnp.clip(blk, 0, n_blk - 1)
        return ref[pl.ds(pl.multiple_of(blk * BLOCK, BLOCK), BLOCK), :]

    k_loc, v_loc = [], []
    for off, ref in ((-1, kvp_ref), (0, kvc_ref), (1, kvn_ref)):
        kr = _rope(ref[:, :LANES], table(cos_ref, n + off), table(sin_ref, n + off))
        k_loc.append(kr)
        v_loc.append(ref[:, LANES:])
    k_loc = jnp.concatenate(k_loc, axis=0)
    v_loc = jnp.concatenate(v_loc, axis=0)
    k_ctx = kctx_ref[...]
    v_ctx = vctx_ref[...]

    def head_bf16(a, kv):
        return a[:, kv * HEAD_DIM:(kv + 1) * HEAD_DIM].astype(BF16)

    k_loc = [head_bf16(k_loc, kv) for kv in range(ATT_KV_HEADS)]
    v_loc = [head_bf16(v_loc, kv) for kv in range(ATT_KV_HEADS)]
    k_ctx = [head_bf16(k_ctx, kv) for kv in range(ATT_KV_HEADS)]
    v_ctx = [head_bf16(v_ctx, kv) for kv in range(ATT_KV_HEADS)]

    span = 3 * BLOCK
    qpos = start + lax.broadcasted_iota(jnp.int32, (BLOCK, span), 0)
    kpos = start - WINDOW + lax.broadcasted_iota(jnp.int32, (BLOCK, span), 1)
    valid = (jnp.abs(qpos - kpos) <= WINDOW) & (kpos >= 0) & (kpos < DEC_SEQ)

    cos_q = table(cos_ref, n)
    sin_q = table(sin_ref, n)
    outs = []
    for pair in range(ATT_HEADS // 2):
        qr = _rope(q_ref[:, pair * LANES:(pair + 1) * LANES], cos_q, sin_q)
        for sub in range(2):
            h = 2 * pair + sub
            kv = h // ATT_GROUP
            q = head_bf16(qr, sub)
            s_ctx = _dot_t(q, k_ctx[kv]) * scale
            s_loc = jnp.where(valid, _dot_t(q, k_loc[kv]) * scale, NEG_INF)
            sink = jnp.full((1, 1), sink_ref[h], F32)
            p_ctx, p_loc = _softmax_sink([s_ctx, s_loc], sink)
            o = (jnp.dot(p_ctx.astype(BF16), v_ctx[kv], preferred_element_type=F32)
                 + jnp.dot(p_loc.astype(BF16), v_loc[kv], preferred_element_type=F32))
            outs.append(o)
    o_ref[...] = jnp.concatenate(outs, axis=-1).astype(o_ref.dtype)


def _attn_lat(p, sink, k_ctx, v_ctx, cos_t, sin_t):
    n_blk = DEC_SEQ // BLOCK
    rb0 = T_CTX // BLOCK

    def kv_spec(off):
        return pl.BlockSpec(
            (BLOCK, 2 * LANES),
            lambda b, n: (rb0 + b * n_blk + jnp.clip(n + off, 0, n_blk - 1), C_TKV // 256))

    return pl.pallas_call(
        _attn_lat_kernel,
        grid=(DEC_BATCH, n_blk),
        in_specs=[
            pl.BlockSpec(memory_space=pltpu.SMEM),
            pl.BlockSpec((BLOCK, ATT_HEADS * HEAD_DIM), lambda b, n: (rb0 + b * n_blk + n, C_TQ // 512)),
            kv_spec(-1), kv_spec(0), kv_spec(1),
            pl.BlockSpec((None, PAST_LEN, LANES), lambda b, n: (b, 0, 0)),
            pl.BlockSpec((None, PAST_LEN, LANES), lambda b, n: (b, 0, 0)),
            pl.BlockSpec((DEC_SEQ, LANES), lambda b, n: (0, 0)),
            pl.BlockSpec((DEC_SEQ, LANES), lambda b, n: (0, 0)),
        ],
        out_specs=pl.BlockSpec((BLOCK, ATT_HEADS * HEAD_DIM), lambda b, n: (b * n_blk + n, 0)),
        out_shape=jax.ShapeDtypeStruct((T_LAT, ATT_HEADS * HEAD_DIM), BF16),
        compiler_params=_cparams("parallel", "parallel"),
        name="attn_lat",
    )(sink, p, p, p, p, k_ctx, v_ctx, cos_t, sin_t)


def _rope_tables():
    pos = np.arange(DEC_SEQ)
    n_freq = HEAD_DIM // 4
    inv = jnp.asarray(ROPE_THETA, F32) ** (-jnp.arange(n_freq, dtype=F32) / n_freq)
    row = jnp.asarray(pos // GRID_W, F32)
    colp = jnp.asarray(pos % GRID_W, F32)
    ang_r = row[:, None] * inv[None, :]
    ang_c = colp[:, None] * inv[None, :]
    cos = jnp.concatenate([jnp.cos(ang_r)] * 2 + [jnp.cos(ang_c)] * 2, axis=-1)
    sin = jnp.concatenate([-jnp.sin(ang_r), jnp.sin(ang_r), -jnp.sin(ang_c), jnp.sin(ang_c)], axis=-1)
    return jnp.tile(cos, (1, 2)), jnp.tile(sin, (1, 2))


def _split_dot(a, w_hi, w_lo):
    a_hi = a.astype(BF16)
    a_lo = (a - a_hi.astype(F32)).astype(BF16)
    return (jnp.dot(a_hi, w_hi, preferred_element_type=F32)
            + jnp.dot(a_lo, w_hi, preferred_element_type=F32)
            + jnp.dot(a_hi, w_lo, preferred_element_type=F32))


def _route(logits):
    lane_i = lax.broadcasted_iota(jnp.int32, logits.shape, 1)
    lane = lane_i.astype(F32)
    big = jnp.float32(1 << 20)
    is_g = lane_i < N_EXPERT_GROUPS
    lg = jnp.where(is_g, logits, -jnp.inf)
    m_g = jnp.max(lg, axis=-1, keepdims=True)
    grp = jnp.min(jnp.where(lg == m_g, lane, big), axis=-1, keepdims=True)
    z_g = jnp.sum(jnp.where(is_g, jnp.exp(lg - m_g), 0.0), axis=-1, keepdims=True)
    p_grp = 1.0 / z_g

    e_idx = lane_i - ROUTE_E0
    e_grp = (e_idx >> 2).astype(F32)
    sel = (e_idx >= 0) & (e_idx < N_EXPERTS) & (e_grp == grp)
    le = jnp.where(sel, logits, -jnp.inf)
    m_e = jnp.max(le, axis=-1, keepdims=True)
    ex = jnp.where(sel, jnp.exp(le - m_e), 0.0)
    pe = ex / jnp.sum(ex, axis=-1, keepdims=True)
    pe = jnp.where(sel, pe, -1.0)
    v1 = jnp.max(pe, axis=-1, keepdims=True)
    i1 = jnp.min(jnp.where(pe == v1, lane, big), axis=-1, keepdims=True)
    pe2 = jnp.where(lane == i1, -1.0, pe)
    v2 = jnp.max(pe2, axis=-1, keepdims=True)
    i2 = jnp.min(jnp.where(pe2 == v2, lane, big), axis=-1, keepdims=True)
    tot = v1 + v2
    return (jnp.where(lane == i1, p_grp * (v1 / tot), 0.0)
            + jnp.where(lane == i2, p_grp * (v2 / tot), 0.0))


def _merge_kernel(gate_ref, conv_ref, cprev_ref, cnext_ref, yb_ref, yc_ref, x_ref, mod_ref, cw_ref,
                  wb_ref, wo_ref, g2_ref, wr_hi_ref, wr_lo_ref, br_ref,
                  x1_ref, h2_ref, rg_ref, *, tm):
    i = pl.program_id(0)
    a_b = conv_ref[:, 0:CONV_W]
    u = conv_ref[:, CONV_W:2 * CONV_W] * conv_ref[:, 2 * CONV_W:3 * CONV_W]
    u_before = cprev_ref[7:8, CONV_W:2 * CONV_W] * cprev_ref[7:8, 2 * CONV_W:3 * CONV_W]
    u_after = cnext_ref[0:1, CONV_W:2 * CONV_W] * cnext_ref[0:1, 2 * CONV_W:3 * CONV_W]
    r = lax.broadcasted_iota(jnp.int32, (tm, 1), 0)
    g_row = i * tm + r
    seq_mask = jnp.where(g_row < T_CTX, SEQ - 1, DEC_SEQ - 1)
    first = (g_row & seq_mask) == 0
    last = ((g_row + 1) & seq_mask) == 0
    u_prev = jnp.where(r == 0, u_before, pltpu.roll(u, 1, 0))
    u_next = jnp.where(r == tm - 1, u_after, pltpu.roll(u, tm - 1, 0))
    u_prev = jnp.where(first, 0.0, u_prev)
    u_next = jnp.where(last, 0.0, u_next)
    y_a = a_b * (u_prev * cw_ref[0:1, :] + u * cw_ref[1:2, :] + u_next * cw_ref[2:3, :])

    def branch(j, y):
        m = gate_ref[:, j * D_MODEL:(j + 1) * D_MODEL]
        return jax.nn.sigmoid(m) * jnp.dot(y, wb_ref[j], preferred_element_type=F32)

    z = branch(0, y_a.astype(BF16)) + branch(1, yb_ref[...]) + branch(2, yc_ref[...])
    x1 = x_ref[...] + mod_ref[2:3, :] * jnp.dot(z.astype(BF16), wo_ref[...], preferred_element_type=F32)
    x1_ref[...] = x1
    h2 = _rms(x1) * g2_ref[...] * (1.0 + mod_ref[4:5, :]) + mod_ref[3:4, :]
    h2_ref[...] = h2.astype(BF16)
    logits = _split_dot(h2, wr_hi_ref[...], wr_lo_ref[...]) + br_ref[...]
    rg_ref[...] = _route(logits)


def _merge(p, y_b, y_c, x, mod_l, conv_w, wb, wo, g2, wr_hi, wr_lo, br):
    tm = 256
    n_tiles = T_ALL // tm
    hb = tm // 8
    conv_blk = C_CONV // (3 * CONV_W)
    const = lambda shape: pl.BlockSpec(shape, lambda i: (0,) * len(shape))
    return pl.pallas_call(
        functools.partial(_merge_kernel, tm=tm),
        grid=(n_tiles,),
        in_specs=[
            pl.BlockSpec((tm, 3 * D_MODEL), lambda i: (i, 0)),
            pl.BlockSpec((tm, 3 * CONV_W), lambda i: (i, conv_blk)),
            pl.BlockSpec((8, 3 * CONV_W), lambda i: (jnp.maximum(i * hb - 1, 0), conv_blk)),
            pl.BlockSpec((8, 3 * CONV_W), lambda i: (jnp.minimum((i + 1) * hb, n_tiles * hb - 1), conv_blk)),
            pl.BlockSpec((tm, 512), lambda i: (i, 0)),
            pl.BlockSpec((tm, 512), lambda i: (i, 0)),
            pl.BlockSpec((tm, D_MODEL), lambda i: (i, 0)),
            pl.BlockSpec((None, 6, D_MODEL), lambda i: (_cond_row(i, tm), 0, 0)),
            const((CONV_K, CONV_W)),
            const((3, 512, D_MODEL)),
            const((D_MODEL, D_MODEL)),
            const((1, D_MODEL)),
            const((D_MODEL, LANES)),
            const((D_MODEL, LANES)),
            const((1, LANES)),
        ],
        out_specs=[
            pl.BlockSpec((tm, D_MODEL), lambda i: (i, 0)),
            pl.BlockSpec((tm, D_MODEL), lambda i: (i, 0)),
            pl.BlockSpec((tm, LANES), lambda i: (i, 0)),
        ],
        out_shape=[
            jax.ShapeDtypeStruct((T_ALL, D_MODEL), F32),
            jax.ShapeDtypeStruct((T_ALL, D_MODEL), BF16),
            jax.ShapeDtypeStruct((T_ALL, LANES), F32),
        ],
        compiler_params=_cparams("parallel"),
        name="merge",
    )(p, p, p, p, y_b, y_c, x, mod_l, conv_w, wb, wo, g2, wr_hi, wr_lo, br)


def _moe_kernel(h_ref, rg_ref, x1_ref, mod_ref, wgu_ref, wd_ref, gf_ref, *out_and_scratch, final):
    if final:
        x2_ref, y_ref, acc_ref = out_and_scratch
    else:
        x2_ref, acc_ref = out_and_scratch
    e = pl.program_id(1)

    @pl.when(e == 0)
    def _():
        acc_ref[...] = jnp.zeros_like(acc_ref)

    gu = jnp.dot(h_ref[...], wgu_ref[...], preferred_element_type=F32)
    g = gu[:, :D_EXPERT]
    u = gu[:, D_EXPERT:]
    rg = rg_ref[...]
    lane = lax.broadcasted_iota(jnp.int32, rg.shape, 1)
    w = jnp.sum(jnp.where(lane == e + ROUTE_E0, rg, 0.0), axis=-1, keepdims=True)
    hid = (g * jax.nn.sigmoid(g)) * u * w
    acc_ref[...] += jnp.dot(hid.astype(BF16), wd_ref[...], preferred_element_type=F32)

    @pl.when(e == N_EXPERTS - 1)
    def _():
        x2 = x1_ref[...] + mod_ref[5:6, :] * acc_ref[...]
        x2_ref[...] = x2
        if final:
            y_ref[...] = _rms(x2) * gf_ref[...]


def _moe(h2, rg, x1, mod_l, wgu, wd, gf, final):
    tm = 1024
    out_specs = [pl.BlockSpec((tm, D_MODEL), lambda i, e: (i, 0))]
    out_shape = [jax.ShapeDtypeStruct((T_ALL, D_MODEL), F32)]
    if final:
        out_specs = out_specs * 2
        out_shape = out_shape * 2
    return pl.pallas_call(
        functools.partial(_moe_kernel, final=final),
        grid=(T_ALL // tm, N_EXPERTS),
        in_specs=[
            pl.BlockSpec((tm, D_MODEL), lambda i, e: (i, 0)),
            pl.BlockSpec((tm, LANES), lambda i, e: (i, 0)),
            pl.BlockSpec((tm, D_MODEL), lambda i, e: (i, 0)),
            pl.BlockSpec((None, 6, D_MODEL), lambda i, e: (_cond_row(i, tm), 0, 0)),
            pl.BlockSpec((None, D_MODEL, 2 * D_EXPERT), lambda i, e: (e, 0, 0)),
            pl.BlockSpec((None, D_EXPERT, D_MODEL), lambda i, e: (e, 0, 0)),
            pl.BlockSpec((1, D_MODEL), lambda i, e: (0, 0)),
        ],
        out_specs=out_specs,
        out_shape=out_shape,
        scratch_shapes=[pltpu.VMEM((tm, D_MODEL), F32)],
        compiler_params=_cparams("parallel", "arbitrary"),
        name="moe",
    )(h2, rg, x1, mod_l, wgu, wd, gf)


def _split_cols(w):
    sizes = (CONV_W, CONV_W, CONV_W, 512, 512, 512, 512, GLA_RANK, GLA_RANK, 512, 128, 128,
             D_MODEL, D_MODEL, D_MODEL)
    out, start = [], 0
    for n in sizes:
        out.append(w[..., start:start + n])
        start += n
    return out


def _prep_w_in(w):
    (a_b, a_c, a_h, g_q, g_k, g_v, g_r, g_lf, g_lb, t_q, t_k, t_v, m_a, m_b, m_c) = _split_cols(w)
    pad = jnp.zeros((D_MODEL, N_PROJ - C_LR - 2 * GLA_RANK), w.dtype)
    return jnp.concatenate([m_a, m_b, m_c, a_b, a_c, a_h, g_q, g_k, g_v, g_r, t_q, t_k, t_v, g_lf, g_lb, pad],
                           axis=-1).astype(BF16)


def _prep_gla_gate(w_gate):
    out = jnp.zeros((2, LANES, GLA_HEADS * GLA_DK), F32)
    out = out.at[0, 0:GLA_RANK].set(w_gate[0])
    out = out.at[1, GLA_RANK:2 * GLA_RANK].set(w_gate[1])
    return out


def _prep_router(w_rg, b_rg, w_re, b_re):
    w = jnp.zeros((D_MODEL, LANES), F32)
    w = w.at[:, :N_EXPERT_GROUPS].set(w_rg).at[:, ROUTE_E0:ROUTE_E0 + N_EXPERTS].set(w_re)
    b = jnp.zeros((1, LANES), F32)
    b = b.at[0, :N_EXPERT_GROUPS].set(b_rg).at[0, ROUTE_E0:ROUTE_E0 + N_EXPERTS].set(b_re)
    w_hi = w.astype(BF16)
    w_lo = (w - w_hi.astype(F32)).astype(BF16)
    return w_hi, w_lo, b


def kernel(x_prompt, x_sample, state_gla, cache_k, cache_v, c, c_ctx, w_ada, b_ada, norm1_g, norm2_g, w_in,
           conv_w, gla_w_gate, gla_b_gate, gla_norm_g, attn_sink, w_branch, w_out, w_route_group,
           b_route_group, w_route_expert, b_route_expert, w_exp_gate, w_exp_up, w_exp_down, final_norm_g):
    x = jnp.concatenate([x_prompt.reshape(T_CTX, D_MODEL), x_sample.reshape(T_LAT, D_MODEL)], axis=0)
    cond = jnp.zeros((N_COND, D_MODEL), F32).at[0].set(c_ctx).at[1:1 + DEC_BATCH].set(c)
    mod = _modulation(cond, w_ada, b_ada).reshape(DEPTH, N_COND, 6, D_MODEL)
    cos_t, sin_t = _rope_tables()
    gf = final_norm_g.reshape(1, D_MODEL)

    states, keys, vals = [], [], []
    y = None
    for l in range(DEPTH):
        p = _in_proj(x, mod[l], norm1_g[l].reshape(1, D_MODEL), _prep_w_in(w_in[l]))
        wg_pad = _prep_gla_gate(gla_w_gate[l])
        wg_hi = wg_pad.astype(BF16)
        wg_lo = (wg_pad - wg_hi.astype(F32)).astype(BF16)
        bg = gla_b_gate[l].reshape(2, 1, GLA_HEADS * GLA_DK)
        ng = gla_norm_g[l].reshape(1, GLA_HEADS * GLA_DV)
        yb_ctx, s_ctx = _gla_ctx(p, wg_hi, wg_lo, bg, ng)
        yb_lat = _gla_lat(p, wg_hi, wg_lo, bg, ng, state_gla[:, l])
        yc_ctx = _attn_ctx(p, attn_sink[l])
        yc_lat = _attn_lat(p, attn_sink[l],
                           cache_k[:, l].reshape(DEC_BATCH, PAST_LEN, LANES),
                           cache_v[:, l].reshape(DEC_BATCH, PAST_LEN, LANES), cos_t, sin_t)
        y_b = jnp.concatenate([yb_ctx, yb_lat], axis=0)
        y_c = jnp.concatenate([yc_ctx, yc_lat], axis=0)
        wr_hi, wr_lo, br = _prep_router(w_route_group[l], b_route_group[l], w_route_expert[l], b_route_expert[l])
        x1, h2, rg = _merge(p, y_b, y_c, x, mod[l], conv_w[l], w_branch[l].astype(BF16), w_out[l].astype(BF16),
                            norm2_g[l].reshape(1, D_MODEL), wr_hi, wr_lo, br)
        wgu = jnp.concatenate([w_exp_gate[l], w_exp_up[l]], axis=-1).astype(BF16)
        final = l == DEPTH - 1
        out = _moe(h2, rg, x1, mod[l], wgu, w_exp_down[l].astype(BF16), gf, final)
        if final:
            x, y = out
        else:
            (x,) = out
        states.append(s_ctx)
        keys.append(p[:T_CTX, C_TKV:C_TKV + LANES].reshape(BATCH, SEQ, ATT_KV_HEADS, HEAD_DIM))
        vals.append(p[:T_CTX, C_TKV + LANES:C_TKV + 2 * LANES].reshape(BATCH, SEQ, ATT_KV_HEADS, HEAD_DIM))

    y_prompt = y[:T_CTX].reshape(BATCH, SEQ, D_MODEL)
    y_sample = y[T_CTX:].reshape(DEC_BATCH, DEC_SEQ, D_MODEL)
    return (y_prompt, y_sample, jnp.stack(states, axis=1), jnp.stack(keys, axis=1), jnp.stack(vals, axis=1))
```

```python
import functools

import jax
import jax.numpy as jnp
import numpy as np
from jax import lax
from jax.experimental import pallas as pl
from jax.experimental.pallas import tpu as pltpu

F32 = jnp.float32
BF16 = jnp.bfloat16

D_MODEL = 1024
BATCH = 32
SEQ = 256
DEPTH = 2
DEC_BATCH = 2
DEC_SEQ = 2048
PAST_LEN = 512
GRID_W = 64
EPS = 1e-6
CONV_W = 512
CONV_K = 3
GLA_HEADS = 4
GLA_DK = 128
GLA_DV = 128
GLA_W = GLA_HEADS * GLA_DV
GLA_RANK = 16
GLA_TAU = 16.0
GLA_CHUNK = 64
ATT_HEADS = 8
ATT_KV_HEADS = 2
ATT_GROUP = ATT_HEADS // ATT_KV_HEADS
HEAD_DIM = 64
ATT_W = ATT_HEADS * HEAD_DIM
WINDOW = 128
BLOCK = 128
ROPE_THETA = 10000.0
N_EXPERT_GROUPS = 4
EXPERTS_PER_GROUP = 4
N_EXPERTS = 16
D_EXPERT = 256
NEG_INF = -1e30

T_CTX = BATCH * SEQ
T_LAT = DEC_BATCH * DEC_SEQ
T_ALL = T_CTX + T_LAT
N_COND = 8
LANES = 128

W_IN_LR = 3584
W_IN_TQ = 3616
W_IN_GATE = 4384
N_IN = 7456
PA_W = 3584
PA_GLA = 1536
PS_W = 896
PS_KV = 512
PS_LR = 768
PG_W = 3 * D_MODEL
ROUTE_E0 = N_EXPERT_GROUPS

TM = 1024
TM_MERGE = 256
VMEM_LIMIT = 56 * 1024 * 1024


def _cparams(*sem):
    return pltpu.CompilerParams(dimension_semantics=sem, vmem_limit_bytes=VMEM_LIMIT)


def _cond_row(i, tm):
    n_ctx = T_CTX // tm
    per = DEC_SEQ // tm
    return jnp.where(i < n_ctx, 0, 1 + (i - n_ctx) // per)


def _mm(a, b):
    return jnp.dot(a, b, preferred_element_type=F32)


def _dot_t(a, b):
    return lax.dot_general(a, b, (((1,), (1,)), ((), ())), preferred_element_type=F32)


def _dot_ta(a, b):
    return lax.dot_general(a, b, (((0,), (0,)), ((), ())), preferred_element_type=F32)


def _rms(x):
    return x * lax.rsqrt(jnp.mean(x * x, axis=-1, keepdims=True) + EPS)


def _mod_norm(x, g, mod, shift_row):
    return _rms(x) * g * (1.0 + mod[shift_row + 1:shift_row + 2, :]) + mod[shift_row:shift_row + 1, :]


def _mod_kernel(c_ref, w_ref, b_ref, o_ref):
    c = c_ref[...]
    s = (c * jax.nn.sigmoid(c)).astype(BF16)
    o_ref[...] = _mm(s, w_ref[...].astype(BF16)) + b_ref[...]


def _modulation(cond, w_ada, b_ada):
    tn = 1536
    return pl.pallas_call(
        _mod_kernel,
        grid=(DEPTH, 6 * D_MODEL // tn),
        in_specs=[
            pl.BlockSpec((N_COND, D_MODEL), lambda l, j: (0, 0)),
            pl.BlockSpec((None, D_MODEL, tn), lambda l, j: (l, 0, j)),
            pl.BlockSpec((None, 1, tn), lambda l, j: (l, 0, j)),
        ],
        out_specs=pl.BlockSpec((None, N_COND, tn), lambda l, j: (l, 0, j)),
        out_shape=jax.ShapeDtypeStruct((DEPTH, N_COND, 6 * D_MODEL), F32),
        compiler_params=_cparams("parallel", "parallel"),
        name="modulation",
    )(cond, w_ada, b_ada.reshape(DEPTH, 1, 6 * D_MODEL))


def _prenorm_kernel(xp_ref, xs_ref, mod_ref, g_ref, x_ref, h_ref):
    x = jnp.where(pl.program_id(0) < T_CTX // TM, xp_ref[...], xs_ref[...])
    x_ref[...] = x
    h_ref[...] = _mod_norm(x, g_ref[...], mod_ref[...], 0).astype(BF16)


def _prenorm(xp, xs, mod_l, g):
    n_ctx = T_CTX // TM
    return pl.pallas_call(
        _prenorm_kernel,
        grid=(T_ALL // TM,),
        in_specs=[
            pl.BlockSpec((TM, D_MODEL), lambda i: (jnp.minimum(i, n_ctx - 1), 0)),
            pl.BlockSpec((TM, D_MODEL), lambda i: (jnp.maximum(i - n_ctx, 0), 0)),
            pl.BlockSpec((None, 6, D_MODEL), lambda i: (_cond_row(i, TM), 0, 0)),
            pl.BlockSpec((1, D_MODEL), lambda i: (0, 0)),
        ],
        out_specs=[pl.BlockSpec((TM, D_MODEL), lambda i: (i, 0))] * 2,
        out_shape=[jax.ShapeDtypeStruct((T_ALL, D_MODEL), F32), jax.ShapeDtypeStruct((T_ALL, D_MODEL), BF16)],
        compiler_params=_cparams("parallel"),
        name="prenorm",
    )(xp, xs, mod_l, g)


def _proj_kernel(h_ref, w_ref, o_ref, wb_ref):
    @pl.when(pl.program_id(1) == 0)
    def _():
        wb_ref[...] = w_ref[...].astype(BF16)

    o_ref[...] = _mm(h_ref[...], wb_ref[...]).astype(o_ref.dtype)


def _proj(h, w, layer, n_out, tn, out_dtype, name):
    return pl.pallas_call(
        _proj_kernel,
        grid=(n_out // tn, T_ALL // TM),
        in_specs=[
            pl.BlockSpec((TM, D_MODEL), lambda j, i: (i, 0)),
            pl.BlockSpec((None, D_MODEL, tn), lambda j, i: (layer, 0, j)),
        ],
        out_specs=pl.BlockSpec((TM, tn), lambda j, i: (i, j)),
        out_shape=jax.ShapeDtypeStruct((T_ALL, n_out), out_dtype),
        scratch_shapes=[pltpu.VMEM((D_MODEL, tn), BF16)],
        compiler_params=_cparams("parallel", "arbitrary"),
        name=name,
    )(h, w)


GLA_GROUP = 256


def _log_sigmoid(z):
    return jnp.minimum(z, 0.0) - jnp.log1p(jnp.exp(-jnp.abs(z)))


def _split3(x):
    hi = x.astype(BF16)
    r1 = x - hi.astype(F32)
    mid = r1.astype(BF16)
    lo = (r1 - mid.astype(F32)).astype(BF16)
    return hi, mid, lo


def _gla_keep(d):
    ri = lax.broadcasted_iota(jnp.int32, (GLA_GROUP, GLA_GROUP), 0)
    ci = lax.broadcasted_iota(jnp.int32, (GLA_GROUP, GLA_GROUP), 1)
    if d == 0:
        return (ci <= ri) & (ci >= (ri & ~(GLA_CHUNK - 1)))
    return (ci >= ri) & (ci <= (ri | (GLA_CHUNK - 1)))


def _gla_group(q, k, v, lr, wg_hi, wg_lo, bg, states, d, keep):
    C = GLA_CHUNK
    nc = GLA_GROUP // C
    width = q.shape[-1]
    lr_hi = lr.astype(BF16)
    lr_lo = (lr - lr_hi.astype(F32)).astype(BF16)
    z = _mm(lr_hi, wg_hi) + _mm(lr_lo, wg_hi) + _mm(lr_hi, wg_lo) + bg
    la = _log_sigmoid(z) / GLA_TAU
    tri = jnp.where(keep, 1.0, 0.0).astype(BF16)
    la_hi, la_mid, la_lo = _split3(la)
    b = _mm(tri, la_hi) + _mm(tri, la_mid) + _mm(tri, la_lo)
    edge = C - 1 if d == 0 else 0
    b_last = [b[c * C + edge:c * C + edge + 1, :] for c in range(nc)]
    bl = jnp.concatenate([jnp.broadcast_to(t, (C, width)) for t in b_last], axis=0)
    q_in = (q * (GLA_DK ** -0.5) * jnp.exp(b)).astype(BF16)
    k_in = (k * jnp.exp(-b)).astype(BF16)
    k_end = (k * jnp.exp(bl - b)).astype(BF16)
    vb = v.astype(BF16)
    dec = [jnp.exp(t) for t in b_last]
    order = range(nc) if d == 0 else range(nc - 1, -1, -1)
    outs, new_states = [], []
    for h, st in enumerate(states):
        cs = slice(h * GLA_DK, (h + 1) * GLA_DK)
        att = jnp.where(keep, _dot_t(q_in[:, cs], k_in[:, cs]), 0.0)
        o = _mm(att.astype(BF16), vb[:, cs])
        pieces = [None] * nc
        for c in order:
            rs = slice(c * C, (c + 1) * C)
            pieces[c] = o[rs] + _dot_t(q_in[rs, cs], st.astype(BF16))
            st = st * dec[c][:, cs] + _dot_ta(vb[rs, cs], k_end[rs, cs])
        outs.append(jnp.concatenate(pieces, axis=0))
        new_states.append(st)
    return outs, new_states


def _gla_finish(o, r, ng):
    o = o * lax.rsqrt(jnp.mean(o * o, axis=-1, keepdims=True) + EPS)
    return o * ng * (r * jax.nn.sigmoid(r))


def _gla_ctx_kernel(q_ref, k_ref, v_ref, r_ref, lr_ref, wgh_ref, wgl_ref, bg_ref, ng_ref, *rest):
    y_ref, s_ref = rest[-2:]
    q, k, v = q_ref[...].astype(F32), k_ref[...].astype(F32), v_ref[...].astype(F32)
    lr = lr_ref[...]
    zero = jnp.zeros((GLA_DV, GLA_DK), F32)
    o_dir = []
    for d in range(2):
        outs, sts = _gla_group(q, k, v, lr, wgh_ref[d], wgl_ref[d], bg_ref[d], [zero] * GLA_HEADS, d,
                               _gla_keep(d))
        o_dir.append(outs)
        for h in range(GLA_HEADS):
            s_ref[d, h] = sts[h].T
    for h in range(GLA_HEADS):
        cs = slice(h * GLA_DV, (h + 1) * GLA_DV)
        y = _gla_finish(o_dir[0][h] + o_dir[1][h], r_ref[:, cs].astype(F32), ng_ref[:, cs])
        y_ref[:, cs] = y.astype(y_ref.dtype)


def _gla_lat_kernel(q_ref, k_ref, v_ref, r_ref, lr_ref, wgh_ref, wgl_ref, bg_ref, ng_ref, s0_ref, yin_ref,
                    y_ref, of_ref, ob_ref):
    del yin_ref
    n_groups = DEC_SEQ // GLA_GROUP
    keep_f, keep_b = _gla_keep(0), _gla_keep(1)

    def load(rows):
        return (q_ref[rows, :].astype(F32), k_ref[rows, :].astype(F32), v_ref[rows, :].astype(F32),
                lr_ref[rows, :])

    def body(g, carry):
        st_f, st_b = carry
        rf = pl.ds(pl.multiple_of(g * GLA_GROUP, GLA_GROUP), GLA_GROUP)
        rb = pl.ds(pl.multiple_of((n_groups - 1 - g) * GLA_GROUP, GLA_GROUP), GLA_GROUP)
        (o,), (st_f,) = _gla_group(*load(rf), wgh_ref[0], wgl_ref[0], bg_ref[0], [st_f], 0, keep_f)
        of_ref[rf, :] = o
        (o,), (st_b,) = _gla_group(*load(rb), wgh_ref[1], wgl_ref[1], bg_ref[1], [st_b], 1, keep_b)
        ob_ref[rb, :] = o
        return st_f, st_b

    lax.fori_loop(0, n_groups, body, (s0_ref[0].T, s0_ref[1].T))
    y = _gla_finish(of_ref[...] + ob_ref[...], r_ref[...].astype(F32), ng_ref[...])
    y_ref[...] = y.astype(y_ref.dtype)


def _gla_ctx(pa, ps, wg_hi, wg_lo, bg, ng, states_prev, layer):
    cb = PA_GLA // GLA_W
    const = lambda shape: pl.BlockSpec(shape, lambda s: (0,) * len(shape))
    in_specs = [pl.BlockSpec((SEQ, GLA_W), lambda s, j=j: (s, cb + j)) for j in range(4)] + [
        pl.BlockSpec((SEQ, LANES), lambda s: (s, PS_LR // LANES)),
        const((2, LANES, GLA_W)), const((2, LANES, GLA_W)), const((2, 1, GLA_W)), const((1, GLA_W))]
    args = [pa, pa, pa, pa, ps, wg_hi, wg_lo, bg, ng]
    aliases = {}
    if states_prev is not None:
        in_specs.append(pl.BlockSpec(memory_space=pl.ANY))
        args.append(states_prev)
        aliases = {len(args) - 1: 1}
    return pl.pallas_call(
        _gla_ctx_kernel,
        grid=(BATCH,),
        in_specs=in_specs,
        out_specs=[pl.BlockSpec((SEQ, GLA_W), lambda s: (s, 0)),
                   pl.BlockSpec((None, None, 2, GLA_HEADS, GLA_DK, GLA_DV), lambda s: (s, layer, 0, 0, 0, 0))],
        out_shape=[jax.ShapeDtypeStruct((T_ALL, GLA_W), BF16),
                   jax.ShapeDtypeStruct((BATCH, DEPTH, 2, GLA_HEADS, GLA_DK, GLA_DV), F32)],
        input_output_aliases=aliases,
        compiler_params=_cparams("parallel"),
        name="gla_ctx",
    )(*args)


def _gla_lat(pa, ps, wg_hi, wg_lo, bg, ng, s0, y):
    rb0 = T_CTX // DEC_SEQ
    cb = PA_GLA // LANES

    def col(piece):
        return pl.BlockSpec((DEC_SEQ, LANES), lambda s, h: (rb0 + s, cb + GLA_HEADS * piece + h))

    return pl.pallas_call(
        _gla_lat_kernel,
        grid=(DEC_BATCH, GLA_HEADS),
        in_specs=[col(0), col(1), col(2), col(3),
                  pl.BlockSpec((DEC_SEQ, LANES), lambda s, h: (rb0 + s, PS_LR // LANES)),
                  pl.BlockSpec((2, LANES, LANES), lambda s, h: (0, 0, h)),
                  pl.BlockSpec((2, LANES, LANES), lambda s, h: (0, 0, h)),
                  pl.BlockSpec((2, 1, LANES), lambda s, h: (0, 0, h)),
                  pl.BlockSpec((1, LANES), lambda s, h: (0, h)),
                  pl.BlockSpec((None, 2, None, GLA_DK, GLA_DV), lambda s, h: (s, 0, h, 0, 0)),
                  pl.BlockSpec(memory_space=pl.ANY)],
        out_specs=pl.BlockSpec((DEC_SEQ, LANES), lambda s, h: (rb0 + s, h)),
        out_shape=jax.ShapeDtypeStruct((T_ALL, GLA_W), BF16),
        scratch_shapes=[pltpu.VMEM((DEC_SEQ, GLA_DV), F32), pltpu.VMEM((DEC_SEQ, GLA_DV), F32)],
        input_output_aliases={10: 0},
        compiler_params=_cparams("parallel", "parallel"),
        name="gla_lat",
    )(pa, pa, pa, pa, ps, wg_hi, wg_lo, bg, ng, s0, y)


def _attend(q_heads, sinks, k_parts, v_parts, masks):
    rows = q_heads[0].shape[0]
    q = jnp.concatenate(q_heads, axis=0)
    sink = jnp.concatenate([jnp.full((rows, 1), s, F32) for s in sinks], axis=0)
    scores = []
    for k, mask in zip(k_parts, masks):
        s = _dot_t(q, k) * (HEAD_DIM ** -0.5)
        if mask is not None:
            s = jnp.where(mask, s, NEG_INF)
        scores.append(s)
    m = sink
    for s in scores:
        m = jnp.maximum(m, jnp.max(s, axis=-1, keepdims=True))
    den = jnp.exp(sink - m)
    o = None
    for s, v in zip(scores, v_parts):
        e = jnp.exp(s - m)
        den = den + jnp.sum(e, axis=-1, keepdims=True)
        pv = _mm(e.astype(BF16), v)
        o = pv if o is None else o + pv
    o = o / den
    return [o[g * rows:(g + 1) * rows] for g in range(len(q_heads))]


def _head(a, j):
    return a[:, j * HEAD_DIM:(j + 1) * HEAD_DIM]


def _attn_ctx_kernel(sink_ref, q_ref, kv_ref, *rest):
    o_ref = rest[-1]
    q = q_ref[...]
    kv_all = kv_ref[...]
    outs = []
    for kv in range(ATT_KV_HEADS):
        k = _head(kv_all, kv).astype(BF16)
        v = _head(kv_all, ATT_KV_HEADS + kv).astype(BF16)
        heads = range(kv * ATT_GROUP, (kv + 1) * ATT_GROUP)
        outs += _attend([_head(q, h).astype(BF16) for h in heads], [sink_ref[h] for h in heads],
                        [k], [v], [None])
    o_ref[...] = jnp.concatenate(outs, axis=-1).astype(o_ref.dtype)


def _attn_ctx(ps, sink):
    return pl.pallas_call(
        _attn_ctx_kernel,
        grid=(BATCH,),
        in_specs=[
            pl.BlockSpec(memory_space=pltpu.SMEM),
            pl.BlockSpec((SEQ, ATT_W), lambda s: (s, 0)),
            pl.BlockSpec((SEQ, 2 * LANES), lambda s: (s, PS_KV // (2 * LANES))),
        ],
        out_specs=pl.BlockSpec((SEQ, ATT_W), lambda s: (s, 0)),
        out_shape=jax.ShapeDtypeStruct((T_ALL, ATT_W), BF16),
        compiler_params=_cparams("parallel"),
        name="attn_ctx",
    )(sink, ps, ps)


def _rope(x, cos, sin_signed):
    lane = lax.broadcasted_iota(jnp.int32, x.shape, 1)
    partner = jnp.where((lane & 31) < 16, pltpu.roll(x, LANES - 16, 1), pltpu.roll(x, 16, 1))
    return x * cos + partner * sin_signed


def _attn_lat_kernel(sink_ref, q_ref, kvp_ref, kvc_ref, kvn_ref, kctx_ref, vctx_ref, cos_ref, sin_ref, yin_ref,
                     o_ref):
    del yin_ref
    n = pl.program_id(1)
    n_blk = DEC_SEQ // BLOCK
    start = n * BLOCK

    def table(ref, blk):
        blk = jnp.clip(blk, 0, n_blk - 1)
        return ref[pl.ds(pl.multiple_of(blk * BLOCK, BLOCK), BLOCK), :]

    k_loc, v_loc = [], []
    for off, ref in ((-1, kvp_ref), (0, kvc_ref), (1, kvn_ref)):
        k_loc.append(_rope(ref[:, :LANES], table(cos_ref, n + off), table(sin_ref, n + off)))
        v_loc.append(ref[:, LANES:])
    k_loc = jnp.concatenate(k_loc, axis=0)
    v_loc = jnp.concatenate(v_loc, axis=0)
    k_ctx = kctx_ref[...]
    v_ctx = vctx_ref[...]

    span = 3 * BLOCK
    stacked = (ATT_GROUP * BLOCK, span)
    qpos = start + (lax.broadcasted_iota(jnp.int32, stacked, 0) & (BLOCK - 1))
    kpos = start - WINDOW + lax.broadcasted_iota(jnp.int32, stacked, 1)
    valid = (jnp.abs(qpos - kpos) <= WINDOW) & (kpos >= 0) & (kpos < DEC_SEQ)

    cos_q = table(cos_ref, n)
    sin_q = table(sin_ref, n)
    q_heads = []
    for pair in range(ATT_HEADS // 2):
        qr = _rope(q_ref[:, pair * LANES:(pair + 1) * LANES], cos_q, sin_q)
        q_heads += [_head(qr, 0).astype(BF16), _head(qr, 1).astype(BF16)]
    outs = []
    for kv in range(ATT_KV_HEADS):
        heads = range(kv * ATT_GROUP, (kv + 1) * ATT_GROUP)
        outs += _attend([q_heads[h] for h in heads], [sink_ref[h] for h in heads],
                        [_head(k_ctx, kv).astype(BF16), _head(k_loc, kv).astype(BF16)],
                        [_head(v_ctx, kv).astype(BF16), _head(v_loc, kv).astype(BF16)],
                        [None, valid])
    o_ref[...] = jnp.concatenate(outs, axis=-1).astype(o_ref.dtype)


def _attn_lat(ps, sink, k_ctx, v_ctx, cos_t, sin_t, y):
    n_blk = DEC_SEQ // BLOCK
    rb0 = T_CTX // BLOCK

    def kv_spec(off):
        return pl.BlockSpec(
            (BLOCK, 2 * LANES),
            lambda b, n: (rb0 + b * n_blk + jnp.clip(n + off, 0, n_blk - 1), PS_KV // (2 * LANES)))

    return pl.pallas_call(
        _attn_lat_kernel,
        grid=(DEC_BATCH, n_blk),
        in_specs=[
            pl.BlockSpec(memory_space=pltpu.SMEM),
            pl.BlockSpec((BLOCK, ATT_W), lambda b, n: (rb0 + b * n_blk + n, 0)),
            kv_spec(-1), kv_spec(0), kv_spec(1),
            pl.BlockSpec((None, PAST_LEN, LANES), lambda b, n: (b, 0, 0)),
            pl.BlockSpec((None, PAST_LEN, LANES), lambda b, n: (b, 0, 0)),
            pl.BlockSpec((DEC_SEQ, LANES), lambda b, n: (0, 0)),
            pl.BlockSpec((DEC_SEQ, LANES), lambda b, n: (0, 0)),
            pl.BlockSpec(memory_space=pl.ANY),
        ],
        out_specs=pl.BlockSpec((BLOCK, ATT_W), lambda b, n: (rb0 + b * n_blk + n, 0)),
        out_shape=jax.ShapeDtypeStruct((T_ALL, ATT_W), BF16),
        input_output_aliases={9: 0},
        compiler_params=_cparams("parallel", "parallel"),
        name="attn_lat",
    )(sink, ps, ps, ps, ps, k_ctx, v_ctx, cos_t, sin_t, y)


def _rope_tables():
    pos = np.arange(DEC_SEQ)
    n_freq = HEAD_DIM // 4
    inv = jnp.asarray(ROPE_THETA, F32) ** (-jnp.arange(n_freq, dtype=F32) / n_freq)
    row = jnp.asarray(pos // GRID_W, F32)
    colp = jnp.asarray(pos % GRID_W, F32)
    ang_r = row[:, None] * inv[None, :]
    ang_c = colp[:, None] * inv[None, :]
    cos = jnp.concatenate([jnp.cos(ang_r)] * 2 + [jnp.cos(ang_c)] * 2, axis=-1)
    sin = jnp.concatenate([-jnp.sin(ang_r), jnp.sin(ang_r), -jnp.sin(ang_c), jnp.sin(ang_c)], axis=-1)
    return jnp.tile(cos, (1, 2)), jnp.tile(sin, (1, 2))


def _split_dot(a, w_hi, w_lo):
    a_hi = a.astype(BF16)
    a_lo = (a - a_hi.astype(F32)).astype(BF16)
    return _mm(a_hi, w_hi) + _mm(a_lo, w_hi) + _mm(a_hi, w_lo)


def _route(logits):
    lane_i = lax.broadcasted_iota(jnp.int32, logits.shape, 1)
    lane = lane_i.astype(F32)
    big = jnp.float32(1 << 20)
    is_g = lane_i < N_EXPERT_GROUPS
    lg = jnp.where(is_g, logits, -jnp.inf)
    m_g = jnp.max(lg, axis=-1, keepdims=True)
    grp = jnp.min(jnp.where(lg == m_g, lane, big), axis=-1, keepdims=True)
    z_g = jnp.sum(jnp.where(is_g, jnp.exp(lg - m_g), 0.0), axis=-1, keepdims=True)
    p_grp = 1.0 / z_g

    e_idx = lane_i - ROUTE_E0
    e_grp = (e_idx >> 2).astype(F32)
    sel = (e_idx >= 0) & (e_idx < N_EXPERTS) & (e_grp == grp)
    le = jnp.where(sel, logits, -jnp.inf)
    m_e = jnp.max(le, axis=-1, keepdims=True)
    ex = jnp.where(sel, jnp.exp(le - m_e), 0.0)
    pe = ex / jnp.sum(ex, axis=-1, keepdims=True)
    pe = jnp.where(sel, pe, -1.0)
    v1 = jnp.max(pe, axis=-1, keepdims=True)
    i1 = jnp.min(jnp.where(pe == v1, lane, big), axis=-1, keepdims=True)
    pe2 = jnp.where(lane == i1, -1.0, pe)
    v2 = jnp.max(pe2, axis=-1, keepdims=True)
    i2 = jnp.min(jnp.where(pe2 == v2, lane, big), axis=-1, keepdims=True)
    tot = v1 + v2
    return (jnp.where(lane == i1, p_grp * (v1 / tot), 0.0)
            + jnp.where(lane == i2, p_grp * (v2 / tot), 0.0))


HALO = 16


def _merge_kernel(gate_ref, conv_ref, cprev_ref, cnext_ref, yb_ref, yc_ref, x_ref, mod_ref, cw_ref,
                  wb_ref, wo_ref, g2_ref, wr_hi_ref, wr_lo_ref, br_ref,
                  x1_ref, h2_ref, rg_ref, wbb_ref, wob_ref):
    tm = TM_MERGE
    i = pl.program_id(0)

    @pl.when(i == 0)
    def _():
        wbb_ref[...] = wb_ref[...].astype(BF16)
        wob_ref[...] = wo_ref[...].astype(BF16)

    def gated(ref):
        return ref[:, CONV_W:2 * CONV_W].astype(F32) * ref[:, 2 * CONV_W:3 * CONV_W].astype(F32)

    a_b = conv_ref[:, 0:CONV_W].astype(F32)
    u = gated(conv_ref)
    u_before = gated(cprev_ref)[HALO - 1:HALO]
    u_after = gated(cnext_ref)[0:1]
    r = lax.broadcasted_iota(jnp.int32, (tm, 1), 0)
    g_row = i * tm + r
    seq_mask = jnp.where(g_row < T_CTX, SEQ - 1, DEC_SEQ - 1)
    first = (g_row & seq_mask) == 0
    last = ((g_row + 1) & seq_mask) == 0
    u_prev = jnp.where(r == 0, u_before, pltpu.roll(u, 1, 0))
    u_next = jnp.where(r == tm - 1, u_after, pltpu.roll(u, tm - 1, 0))
    u_prev = jnp.where(first, 0.0, u_prev)
    u_next = jnp.where(last, 0.0, u_next)
    y_a = a_b * (u_prev * cw_ref[0:1, :] + u * cw_ref[1:2, :] + u_next * cw_ref[2:3, :])

    def branch(j, y):
        m = gate_ref[:, j * D_MODEL:(j + 1) * D_MODEL].astype(F32)
        return jax.nn.sigmoid(m) * _mm(y, wbb_ref[j])

    z = branch(0, y_a.astype(BF16)) + branch(1, yb_ref[...]) + branch(2, yc_ref[...])
    x1 = x_ref[...] + mod_ref[2:3, :] * _mm(z.astype(BF16), wob_ref[...])
    x1_ref[...] = x1
    h2 = _mod_norm(x1, g2_ref[...], mod_ref[...], 3)
    h2_ref[...] = h2.astype(BF16)
    logits = _split_dot(h2, wr_hi_ref[...], wr_lo_ref[...]) + br_ref[...]
    rg_ref[...] = _route(logits)


def _merge(pg, pa, y_b, y_c, x, mod_l, conv_w, layer, wb, wo, g2, wr_hi, wr_lo, br):
    tm = TM_MERGE
    n_tiles = T_ALL // tm
    hb = tm // HALO
    const = lambda shape: pl.BlockSpec(shape, lambda i: (0,) * len(shape))
    return pl.pallas_call(
        _merge_kernel,
        grid=(n_tiles,),
        in_specs=[
            pl.BlockSpec((tm, PG_W), lambda i: (i, 0)),
            pl.BlockSpec((tm, 3 * CONV_W), lambda i: (i, 0)),
            pl.BlockSpec((HALO, 3 * CONV_W), lambda i: (jnp.maximum(i * hb - 1, 0), 0)),
            pl.BlockSpec((HALO, 3 * CONV_W), lambda i: (jnp.minimum((i + 1) * hb, n_tiles * hb - 1), 0)),
            pl.BlockSpec((tm, GLA_W), lambda i: (i, 0)),
            pl.BlockSpec((tm, ATT_W), lambda i: (i, 0)),
            pl.BlockSpec((tm, D_MODEL), lambda i: (i, 0)),
            pl.BlockSpec((None, 6, D_MODEL), lambda i: (_cond_row(i, tm), 0, 0)),
            const((CONV_K, CONV_W)),
            pl.BlockSpec((None, 3, 512, D_MODEL), lambda i: (layer, 0, 0, 0)),
            pl.BlockSpec((None, D_MODEL, D_MODEL), lambda i: (layer, 0, 0)),
            const((1, D_MODEL)),
            const((D_MODEL, LANES)),
            const((D_MODEL, LANES)),
            const((1, LANES)),
        ],
        out_specs=[
            pl.BlockSpec((tm, D_MODEL), lambda i: (i, 0)),
            pl.BlockSpec((tm, D_MODEL), lambda i: (i, 0)),
            pl.BlockSpec((tm, LANES), lambda i: (i, 0)),
        ],
        out_shape=[
            jax.ShapeDtypeStruct((T_ALL, D_MODEL), F32),
            jax.ShapeDtypeStruct((T_ALL, D_MODEL), BF16),
            jax.ShapeDtypeStruct((T_ALL, LANES), F32),
        ],
        scratch_shapes=[pltpu.VMEM((3, 512, D_MODEL), BF16), pltpu.VMEM((D_MODEL, D_MODEL), BF16)],
        compiler_params=_cparams("arbitrary"),
        name="merge",
    )(pg, pa, pa, pa, y_b, y_c, x, mod_l, conv_w, wb, wo, g2, wr_hi, wr_lo, br)


def _moe_kernel(h_ref, rg_ref, x1_ref, mod_ref, wg_ref, wu_ref, wd_ref, g_ref, *rest, final):
    if final:
        y_ref, acc_ref = rest
    else:
        modn_ref, x2_ref, hn_ref, acc_ref = rest
    e = pl.program_id(1)

    @pl.when(e == 0)
    def _():
        acc_ref[...] = jnp.zeros_like(acc_ref)

    h = h_ref[...]
    g = _mm(h, wg_ref[...].astype(BF16))
    u = _mm(h, wu_ref[...].astype(BF16))
    rg = rg_ref[...]
    lane = lax.broadcasted_iota(jnp.int32, rg.shape, 1)
    w = jnp.sum(jnp.where(lane == e + ROUTE_E0, rg, 0.0), axis=-1, keepdims=True)
    hid = (g * jax.nn.sigmoid(g)) * u * w
    acc_ref[...] += _mm(hid.astype(BF16), wd_ref[...].astype(BF16))

    @pl.when(e == N_EXPERTS - 1)
    def _():
        x2 = x1_ref[...] + mod_ref[5:6, :] * acc_ref[...]
        if final:
            y_ref[...] = _rms(x2) * g_ref[...]
        else:
            x2_ref[...] = x2
            hn_ref[...] = _mod_norm(x2, g_ref[...], modn_ref[...], 0).astype(BF16)


def _moe(h2, rg, x1, mod_l, layer, w_eg, w_eu, w_ed, g, mod_next, tile0, n_tiles):
    final = mod_next is None
    row = lambda i, e: (tile0 + i, 0)
    mod_spec = lambda: pl.BlockSpec((None, 6, D_MODEL), lambda i, e: (_cond_row(tile0 + i, TM), 0, 0))
    in_specs = [
        pl.BlockSpec((TM, D_MODEL), row),
        pl.BlockSpec((TM, LANES), row),
        pl.BlockSpec((TM, D_MODEL), row),
        mod_spec(),
        pl.BlockSpec((None, None, D_MODEL, D_EXPERT), lambda i, e: (layer, e, 0, 0)),
        pl.BlockSpec((None, None, D_MODEL, D_EXPERT), lambda i, e: (layer, e, 0, 0)),
        pl.BlockSpec((None, None, D_EXPERT, D_MODEL), lambda i, e: (layer, e, 0, 0)),
        pl.BlockSpec((1, D_MODEL), lambda i, e: (0, 0)),
    ]
    args = [h2, rg, x1, mod_l, w_eg, w_eu, w_ed, g]
    if final:
        out_specs = pl.BlockSpec((TM, D_MODEL), lambda i, e: (i, 0))
        out_shape = jax.ShapeDtypeStruct((n_tiles * TM, D_MODEL), F32)
    else:
        in_specs.append(mod_spec())
        args.append(mod_next)
        out_specs = [pl.BlockSpec((TM, D_MODEL), row)] * 2
        out_shape = [jax.ShapeDtypeStruct((T_ALL, D_MODEL), F32), jax.ShapeDtypeStruct((T_ALL, D_MODEL), BF16)]
    return pl.pallas_call(
        functools.partial(_moe_kernel, final=final),
        grid=(n_tiles, N_EXPERTS),
        in_specs=in_specs,
        out_specs=out_specs,
        out_shape=out_shape,
        scratch_shapes=[pltpu.VMEM((TM, D_MODEL), F32)],
        compiler_params=_cparams("parallel", "arbitrary"),
        name="moe_final" if final else "moe",
    )(*args)


def _cache_kernel(*refs):
    ps_refs, (k_ref, v_ref) = refs[:DEPTH], refs[DEPTH:]
    n_seq = TM // SEQ
    for l, ps_ref in enumerate(ps_refs):
        k_ref[:, l] = ps_ref[:, :LANES].reshape(n_seq, SEQ, LANES)
        v_ref[:, l] = ps_ref[:, LANES:].reshape(n_seq, SEQ, LANES)


def _cache(ps_layers):
    n_seq = TM // SEQ
    out = pl.BlockSpec((n_seq, DEPTH, SEQ, LANES), lambda i: (i, 0, 0, 0))
    return pl.pallas_call(
        _cache_kernel,
        grid=(T_CTX // TM,),
        in_specs=[pl.BlockSpec((TM, 2 * LANES), lambda i: (i, PS_KV // (2 * LANES)))] * DEPTH,
        out_specs=[out, out],
        out_shape=[jax.ShapeDtypeStruct((BATCH, DEPTH, SEQ, LANES), F32)] * 2,
        compiler_params=_cparams("parallel"),
        name="cache",
    )(*ps_layers)


def _prep_gla_gate(w_gate):
    w = jnp.zeros((2, LANES, GLA_W), F32)
    w = w.at[0, 0:GLA_RANK].set(w_gate[0]).at[1, GLA_RANK:2 * GLA_RANK].set(w_gate[1])
    hi = w.astype(BF16)
    return hi, (w - hi.astype(F32)).astype(BF16)


def _prep_router(w_rg, b_rg, w_re, b_re):
    w = jnp.zeros((D_MODEL, LANES), F32)
    w = w.at[:, :N_EXPERT_GROUPS].set(w_rg).at[:, ROUTE_E0:ROUTE_E0 + N_EXPERTS].set(w_re)
    b = jnp.zeros((1, LANES), F32)
    b = b.at[0, :N_EXPERT_GROUPS].set(b_rg).at[0, ROUTE_E0:ROUTE_E0 + N_EXPERTS].set(b_re)
    w_hi = w.astype(BF16)
    w_lo = (w - w_hi.astype(F32)).astype(BF16)
    return w_hi, w_lo, b


def kernel(x_prompt, x_sample, state_gla, cache_k, cache_v, c, c_ctx, w_ada, b_ada, norm1_g, norm2_g, w_in,
           conv_w, gla_w_gate, gla_b_gate, gla_norm_g, attn_sink, w_branch, w_out, w_route_group,
           b_route_group, w_route_expert, b_route_expert, w_exp_gate, w_exp_up, w_exp_down, final_norm_g):
    cond = jnp.zeros((N_COND, D_MODEL), F32).at[0].set(c_ctx).at[1:1 + DEC_BATCH].set(c)
    mod = _modulation(cond, w_ada, b_ada).reshape(DEPTH, N_COND, 6, D_MODEL)
    cos_t, sin_t = _rope_tables()
    row = lambda v: v.reshape(1, -1)

    x, h = _prenorm(x_prompt.reshape(T_CTX, D_MODEL), x_sample.reshape(T_LAT, D_MODEL), mod[0], row(norm1_g[0]))
    states = None
    ps_layers = []
    y_prompt = y_sample = None
    w_s = jnp.concatenate([w_in[:, :, W_IN_TQ:W_IN_GATE], w_in[:, :, W_IN_LR:W_IN_TQ],
                           jnp.zeros((DEPTH, D_MODEL, PS_W - (W_IN_GATE - W_IN_LR)), F32)], axis=-1)
    w_g = w_in[:, :, W_IN_GATE:]
    for l in range(DEPTH):
        pa = _proj(h, w_in, l, PA_W, PA_W // 2, BF16, "proj_a")
        ps = _proj(h, w_s, l, PS_W, PS_W, F32, "proj_s")
        pg = _proj(h, w_g, l, PG_W, PG_W // 2, BF16, "proj_g")
        ps_layers.append(ps)

        wg_hi, wg_lo = _prep_gla_gate(gla_w_gate[l])
        bg = gla_b_gate[l].reshape(2, 1, GLA_W)
        ng = row(gla_norm_g[l])
        y_b, states = _gla_ctx(pa, ps, wg_hi, wg_lo, bg, ng, states, l)
        y_b = _gla_lat(pa, ps, wg_hi, wg_lo, bg, ng, state_gla[:, l], y_b)
        y_c = _attn_ctx(ps, attn_sink[l])
        y_c = _attn_lat(ps, attn_sink[l], cache_k[:, l].reshape(DEC_BATCH, PAST_LEN, LANES),
                        cache_v[:, l].reshape(DEC_BATCH, PAST_LEN, LANES), cos_t, sin_t, y_c)

        wr_hi, wr_lo, br = _prep_router(w_route_group[l], b_route_group[l], w_route_expert[l], b_route_expert[l])
        x1, h2, rg = _merge(pg, pa, y_b, y_c, x, mod[l], conv_w[l], l, w_branch, w_out, row(norm2_g[l]),
                            wr_hi, wr_lo, br)
        experts = (l, w_exp_gate, w_exp_up, w_exp_down)
        if l + 1 < DEPTH:
            x, h = _moe(h2, rg, x1, mod[l], *experts, row(norm1_g[l + 1]), mod[l + 1], 0, T_ALL // TM)
        else:
            gf = row(final_norm_g)
            y_prompt = _moe(h2, rg, x1, mod[l], *experts, gf, None, 0, T_CTX // TM)
            y_sample = _moe(h2, rg, x1, mod[l], *experts, gf, None, T_CTX // TM, T_LAT // TM)

    new_k, new_v = _cache(ps_layers)
    kv_shape = (BATCH, DEPTH, SEQ, ATT_KV_HEADS, HEAD_DIM)
    return (y_prompt.reshape(BATCH, SEQ, D_MODEL), y_sample.reshape(DEC_BATCH, DEC_SEQ, D_MODEL), states,
            new_k.reshape(kv_shape), new_v.reshape(kv_shape))
```

```python
import functools

import jax
import jax.numpy as jnp
import numpy as np
from jax import lax
from jax.experimental import pallas as pl
from jax.experimental.pallas import tpu as pltpu

F32 = jnp.float32
BF16 = jnp.bfloat16

D_MODEL = 1024
BATCH = 32
SEQ = 256
DEPTH = 2
DEC_BATCH = 2
DEC_SEQ = 2048
PAST_LEN = 512
GRID_W = 64
EPS = 1e-6
CONV_W = 512
CONV_K = 3
GLA_HEADS = 4
GLA_DK = 128
GLA_DV = 128
GLA_W = GLA_HEADS * GLA_DV
GLA_RANK = 16
GLA_TAU = 16.0
GLA_CHUNK = 64
ATT_HEADS = 8
ATT_KV_HEADS = 2
ATT_GROUP = ATT_HEADS // ATT_KV_HEADS
HEAD_DIM = 64
ATT_W = ATT_HEADS * HEAD_DIM
WINDOW = 128
BLOCK = 128
ROPE_THETA = 10000.0
N_EXPERT_GROUPS = 4
EXPERTS_PER_GROUP = 4
N_EXPERTS = 16
D_EXPERT = 256
NEG_INF = -1e30

T_CTX = BATCH * SEQ
T_LAT = DEC_BATCH * DEC_SEQ
T_ALL = T_CTX + T_LAT
N_COND = 8
LANES = 128

W_IN_LR = 3584
W_IN_TQ = 3616
W_IN_GATE = 4384
N_IN = 7456
PA_W = 3584
PA_GLA = 1536
PS_W = 896
PS_KV = 512
PS_LR = 768
PG_W = 3 * D_MODEL
ROUTE_E0 = N_EXPERT_GROUPS

TM = 1024
TM_MERGE = 256
VMEM_LIMIT = 56 * 1024 * 1024


def _cparams(*sem):
    return pltpu.CompilerParams(dimension_semantics=sem, vmem_limit_bytes=VMEM_LIMIT)


def _cond_row(i, tm):
    n_ctx = T_CTX // tm
    per = DEC_SEQ // tm
    return jnp.where(i < n_ctx, 0, 1 + (i - n_ctx) // per)


def _mm(a, b):
    return jnp.dot(a, b, preferred_element_type=F32)


def _dot_t(a, b):
    return lax.dot_general(a, b, (((1,), (1,)), ((), ())), preferred_element_type=F32)


def _dot_ta(a, b):
    return lax.dot_general(a, b, (((0,), (0,)), ((), ())), preferred_element_type=F32)


def _rms(x):
    return x * lax.rsqrt(jnp.mean(x * x, axis=-1, keepdims=True) + EPS)


def _mod_norm(x, g, mod, shift_row):
    return _rms(x) * g * (1.0 + mod[shift_row + 1:shift_row + 2, :]) + mod[shift_row:shift_row + 1, :]


def _mod_kernel(c_ref, w_ref, b_ref, o_ref):
    c = c_ref[...]
    s = (c * jax.nn.sigmoid(c)).astype(BF16)
    o_ref[...] = _mm(s, w_ref[...].astype(BF16)) + b_ref[...]


def _modulation(cond, w_ada, b_ada):
    tn = 1536
    return pl.pallas_call(
        _mod_kernel,
        grid=(DEPTH, 6 * D_MODEL // tn),
        in_specs=[
            pl.BlockSpec((N_COND, D_MODEL), lambda l, j: (0, 0)),
            pl.BlockSpec((None, D_MODEL, tn), lambda l, j: (l, 0, j)),
            pl.BlockSpec((None, 1, tn), lambda l, j: (l, 0, j)),
        ],
        out_specs=pl.BlockSpec((None, N_COND, tn), lambda l, j: (l, 0, j)),
        out_shape=jax.ShapeDtypeStruct((DEPTH, N_COND, 6 * D_MODEL), F32),
        compiler_params=_cparams("parallel", "parallel"),
        name="modulation",
    )(cond, w_ada, b_ada.reshape(DEPTH, 1, 6 * D_MODEL))


def _prenorm_kernel(xp_ref, xs_ref, mod_ref, g_ref, x_ref, h_ref):
    x = jnp.where(pl.program_id(0) < T_CTX // TM, xp_ref[...], xs_ref[...])
    x_ref[...] = x
    h_ref[...] = _mod_norm(x, g_ref[...], mod_ref[...], 0).astype(BF16)


def _prenorm(xp, xs, mod_l, g):
    n_ctx = T_CTX // TM
    return pl.pallas_call(
        _prenorm_kernel,
        grid=(T_ALL // TM,),
        in_specs=[
            pl.BlockSpec((TM, D_MODEL), lambda i: (jnp.minimum(i, n_ctx - 1), 0)),
            pl.BlockSpec((TM, D_MODEL), lambda i: (jnp.maximum(i - n_ctx, 0), 0)),
            pl.BlockSpec((None, 6, D_MODEL), lambda i: (_cond_row(i, TM), 0, 0)),
            pl.BlockSpec((1, D_MODEL), lambda i: (0, 0)),
        ],
        out_specs=[pl.BlockSpec((TM, D_MODEL), lambda i: (i, 0))] * 2,
        out_shape=[jax.ShapeDtypeStruct((T_ALL, D_MODEL), F32), jax.ShapeDtypeStruct((T_ALL, D_MODEL), BF16)],
        compiler_params=_cparams("parallel"),
        name="prenorm",
    )(xp, xs, mod_l, g)


def _proj_kernel(h_ref, *refs):
    w_refs, (o_ref, wb_ref) = refs[:-2], refs[-2:]

    @pl.when(pl.program_id(1) == 0)
    def _():
        col = 0
        for w_ref in w_refs:
            n = w_ref.shape[0]
            wb_ref[:, col:col + n] = w_ref[...].T.astype(BF16)
            col += n

    o_ref[...] = _mm(h_ref[...], wb_ref[...]).astype(o_ref.dtype)


def _proj(h, w_t, layer, pieces, n_tiles, out_dtype, name):
    tn = sum(n for _, n in pieces)

    def w_spec(c0, n):
        return pl.BlockSpec((pl.Squeezed(), pl.Element(n), pl.Element(D_MODEL)),
                            lambda j, i: (layer, pl.multiple_of(c0 + j * tn, 8), 0))

    return pl.pallas_call(
        _proj_kernel,
        grid=(n_tiles, T_ALL // TM),
        in_specs=[pl.BlockSpec((TM, D_MODEL), lambda j, i: (i, 0))] + [w_spec(c0, n) for c0, n in pieces],
        out_specs=pl.BlockSpec((TM, tn), lambda j, i: (i, j)),
        out_shape=jax.ShapeDtypeStruct((T_ALL, n_tiles * tn), out_dtype),
        scratch_shapes=[pltpu.VMEM((D_MODEL, tn), BF16)],
        compiler_params=_cparams("parallel", "arbitrary"),
        name=name,
    )(h, *([w_t] * len(pieces)))


GLA_GROUP = 256


def _log_sigmoid(z):
    return jnp.minimum(z, 0.0) - jnp.log1p(jnp.exp(-jnp.abs(z)))


def _split3(x):
    hi = x.astype(BF16)
    r1 = x - hi.astype(F32)
    mid = r1.astype(BF16)
    lo = (r1 - mid.astype(F32)).astype(BF16)
    return hi, mid, lo


def _gla_keep(d):
    ri = lax.broadcasted_iota(jnp.int32, (GLA_GROUP, GLA_GROUP), 0)
    ci = lax.broadcasted_iota(jnp.int32, (GLA_GROUP, GLA_GROUP), 1)
    if d == 0:
        return (ci <= ri) & (ci >= (ri & ~(GLA_CHUNK - 1)))
    return (ci >= ri) & (ci <= (ri | (GLA_CHUNK - 1)))


def _gla_group(q, k, v, lr, wg_hi, wg_lo, bg, states, d, keep):
    C = GLA_CHUNK
    nc = GLA_GROUP // C
    width = q.shape[-1]
    lr_hi = lr.astype(BF16)
    lr_lo = (lr - lr_hi.astype(F32)).astype(BF16)
    z = _mm(lr_hi, wg_hi) + _mm(lr_lo, wg_hi) + _mm(lr_hi, wg_lo) + bg
    la = _log_sigmoid(z) / GLA_TAU
    tri = jnp.where(keep, 1.0, 0.0).astype(BF16)
    la_hi, la_mid, la_lo = _split3(la)
    b = _mm(tri, la_hi) + _mm(tri, la_mid) + _mm(tri, la_lo)
    edge = C - 1 if d == 0 else 0
    b_last = [b[c * C + edge:c * C + edge + 1, :] for c in range(nc)]
    bl = jnp.concatenate([jnp.broadcast_to(t, (C, width)) for t in b_last], axis=0)
    q_in = (q * (GLA_DK ** -0.5) * jnp.exp(b)).astype(BF16)
    k_in = (k * jnp.exp(-b)).astype(BF16)
    k_end = (k * jnp.exp(bl - b)).astype(BF16)
    vb = v.astype(BF16)
    dec = [jnp.exp(t) for t in b_last]
    order = range(nc) if d == 0 else range(nc - 1, -1, -1)
    outs, new_states = [], []
    for h, st in enumerate(states):
        cs = slice(h * GLA_DK, (h + 1) * GLA_DK)
        att = jnp.where(keep, _dot_t(q_in[:, cs], k_in[:, cs]), 0.0)
        o = _mm(att.astype(BF16), vb[:, cs])
        pieces = [None] * nc
        for c in order:
            rs = slice(c * C, (c + 1) * C)
            pieces[c] = o[rs] + _dot_t(q_in[rs, cs], st.astype(BF16))
            st = st * dec[c][:, cs] + _dot_ta(vb[rs, cs], k_end[rs, cs])
        outs.append(jnp.concatenate(pieces, axis=0))
        new_states.append(st)
    return outs, new_states


def _gla_finish(o, r, ng):
    o = o * lax.rsqrt(jnp.mean(o * o, axis=-1, keepdims=True) + EPS)
    return o * ng * (r * jax.nn.sigmoid(r))


def _gla_ctx_kernel(q_ref, k_ref, v_ref, r_ref, lr_ref, wgh_ref, wgl_ref, bg_ref, ng_ref, *rest):
    y_ref, s_ref = rest[-2:]
    q, k, v = q_ref[...].astype(F32), k_ref[...].astype(F32), v_ref[...].astype(F32)
    lr = lr_ref[...]
    zero = jnp.zeros((GLA_DV, GLA_DK), F32)
    o_dir = []
    for d in range(2):
        outs, sts = _gla_group(q, k, v, lr, wgh_ref[d], wgl_ref[d], bg_ref[d], [zero] * GLA_HEADS, d,
                               _gla_keep(d))
        o_dir.append(outs)
        for h in range(GLA_HEADS):
            s_ref[d, h] = sts[h].T
    for h in range(GLA_HEADS):
        cs = slice(h * GLA_DV, (h + 1) * GLA_DV)
        y = _gla_finish(o_dir[0][h] + o_dir[1][h], r_ref[:, cs].astype(F32), ng_ref[:, cs])
        y_ref[:, cs] = y.astype(y_ref.dtype)


def _gla_lat_kernel(q_ref, k_ref, v_ref, r_ref, lr_ref, wgh_ref, wgl_ref, bg_ref, ng_ref, s0_ref, yin_ref,
                    y_ref, of_ref, ob_ref):
    del yin_ref
    n_groups = DEC_SEQ // GLA_GROUP
    keep_f, keep_b = _gla_keep(0), _gla_keep(1)

    def load(rows):
        return (q_ref[rows, :].astype(F32), k_ref[rows, :].astype(F32), v_ref[rows, :].astype(F32),
                lr_ref[rows, :])

    def body(g, carry):
        st_f, st_b = carry
        rf = pl.ds(pl.multiple_of(g * GLA_GROUP, GLA_GROUP), GLA_GROUP)
        rb = pl.ds(pl.multiple_of((n_groups - 1 - g) * GLA_GROUP, GLA_GROUP), GLA_GROUP)
        (o,), (st_f,) = _gla_group(*load(rf), wgh_ref[0], wgl_ref[0], bg_ref[0], [st_f], 0, keep_f)
        of_ref[rf, :] = o
        (o,), (st_b,) = _gla_group(*load(rb), wgh_ref[1], wgl_ref[1], bg_ref[1], [st_b], 1, keep_b)
        ob_ref[rb, :] = o
        return st_f, st_b

    lax.fori_loop(0, n_groups, body, (s0_ref[0].T, s0_ref[1].T))
    y = _gla_finish(of_ref[...] + ob_ref[...], r_ref[...].astype(F32), ng_ref[...])
    y_ref[...] = y.astype(y_ref.dtype)


def _gla_ctx(pa, ps, wg_hi, wg_lo, bg, ng, states_prev, layer):
    cb = PA_GLA // GLA_W
    const = lambda shape: pl.BlockSpec(shape, lambda s: (0,) * len(shape))
    in_specs = [pl.BlockSpec((SEQ, GLA_W), lambda s, j=j: (s, cb + j)) for j in range(4)] + [
        pl.BlockSpec((SEQ, LANES), lambda s: (s, PS_LR // LANES)),
        const((2, LANES, GLA_W)), const((2, LANES, GLA_W)), const((2, 1, GLA_W)), const((1, GLA_W))]
    args = [pa, pa, pa, pa, ps, wg_hi, wg_lo, bg, ng]
    aliases = {}
    if states_prev is not None:
        in_specs.append(pl.BlockSpec(memory_space=pl.ANY))
        args.append(states_prev)
        aliases = {len(args) - 1: 1}
    return pl.pallas_call(
        _gla_ctx_kernel,
        grid=(BATCH,),
        in_specs=in_specs,
        out_specs=[pl.BlockSpec((SEQ, GLA_W), lambda s: (s, 0)),
                   pl.BlockSpec((None, None, 2, GLA_HEADS, GLA_DK, GLA_DV), lambda s: (s, layer, 0, 0, 0, 0))],
        out_shape=[jax.ShapeDtypeStruct((T_ALL, GLA_W), BF16),
                   jax.ShapeDtypeStruct((BATCH, DEPTH, 2, GLA_HEADS, GLA_DK, GLA_DV), F32)],
        input_output_aliases=aliases,
        compiler_params=_cparams("parallel"),
        name="gla_ctx",
    )(*args)


def _gla_lat(pa, ps, wg_hi, wg_lo, bg, ng, s0, y):
    rb0 = T_CTX // DEC_SEQ
    cb = PA_GLA // LANES

    def col(piece):
        return pl.BlockSpec((DEC_SEQ, LANES), lambda s, h: (rb0 + s, cb + GLA_HEADS * piece + h))

    return pl.pallas_call(
        _gla_lat_kernel,
        grid=(DEC_BATCH, GLA_HEADS),
        in_specs=[col(0), col(1), col(2), col(3),
                  pl.BlockSpec((DEC_SEQ, LANES), lambda s, h: (rb0 + s, PS_LR // LANES)),
                  pl.BlockSpec((2, LANES, LANES), lambda s, h: (0, 0, h)),
                  pl.BlockSpec((2, LANES, LANES), lambda s, h: (0, 0, h)),
                  pl.BlockSpec((2, 1, LANES), lambda s, h: (0, 0, h)),
                  pl.BlockSpec((1, LANES), lambda s, h: (0, h)),
                  pl.BlockSpec((None, 2, None, GLA_DK, GLA_DV), lambda s, h: (s, 0, h, 0, 0)),
                  pl.BlockSpec(memory_space=pl.ANY)],
        out_specs=pl.BlockSpec((DEC_SEQ, LANES), lambda s, h: (rb0 + s, h)),
        out_shape=jax.ShapeDtypeStruct((T_ALL, GLA_W), BF16),
        scratch_shapes=[pltpu.VMEM((DEC_SEQ, GLA_DV), F32), pltpu.VMEM((DEC_SEQ, GLA_DV), F32)],
        input_output_aliases={10: 0},
        compiler_params=_cparams("parallel", "parallel"),
        name="gla_lat",
    )(pa, pa, pa, pa, ps, wg_hi, wg_lo, bg, ng, s0, y)


def _attend(q, sink, k_parts, v_parts, masks):
    scores = []
    for k, mask in zip(k_parts, masks):
        s = _dot_t(q, k) * (HEAD_DIM ** -0.5)
        if mask is not None:
            s = jnp.where(mask, s, NEG_INF)
        scores.append(s)
    m = jnp.broadcast_to(sink, (q.shape[0], 1)).astype(F32)
    for s in scores:
        m = jnp.maximum(m, jnp.max(s, axis=-1, keepdims=True))
    den = jnp.exp(sink - m)
    o = None
    for s, v in zip(scores, v_parts):
        e = jnp.exp(s - m)
        den = den + jnp.sum(e, axis=-1, keepdims=True)
        pv = _mm(e.astype(BF16), v)
        o = pv if o is None else o + pv
    return o / den


def _head(a, j):
    return a[:, j * HEAD_DIM:(j + 1) * HEAD_DIM]


ATT_ROWS_CTX = 128


def _attn_ctx_kernel(sink_ref, q_ref, kv_ref, *rest):
    o_ref = rest[-1]
    q = q_ref[...]
    kv_all = kv_ref[...]
    outs = []
    for h in range(ATT_HEADS):
        kv = h // ATT_GROUP
        k = _head(kv_all, kv).astype(BF16)
        v = _head(kv_all, ATT_KV_HEADS + kv).astype(BF16)
        qh = _head(q, h).astype(BF16)
        blocks = [_attend(qh[r:r + ATT_ROWS_CTX], sink_ref[h], [k], [v], [None])
                  for r in range(0, SEQ, ATT_ROWS_CTX)]
        outs.append(jnp.concatenate(blocks, axis=0))
    o_ref[...] = jnp.concatenate(outs, axis=-1).astype(o_ref.dtype)


def _attn_ctx(ps, sink):
    return pl.pallas_call(
        _attn_ctx_kernel,
        grid=(BATCH,),
        in_specs=[
            pl.BlockSpec(memory_space=pltpu.SMEM),
            pl.BlockSpec((SEQ, ATT_W), lambda s: (s, 0)),
            pl.BlockSpec((SEQ, 2 * LANES), lambda s: (s, PS_KV // (2 * LANES))),
        ],
        out_specs=pl.BlockSpec((SEQ, ATT_W), lambda s: (s, 0)),
        out_shape=jax.ShapeDtypeStruct((T_ALL, ATT_W), BF16),
        compiler_params=_cparams("parallel"),
        name="attn_ctx",
    )(sink, ps, ps)


def _rope(x, cos, sin_signed):
    lane = lax.broadcasted_iota(jnp.int32, x.shape, 1)
    partner = jnp.where((lane & 31) < 16, pltpu.roll(x, LANES - 16, 1), pltpu.roll(x, 16, 1))
    return x * cos + partner * sin_signed


def _attn_lat_kernel(sink_ref, q_ref, kvp_ref, kvc_ref, kvn_ref, kctx_ref, vctx_ref, cos_ref, sin_ref, yin_ref,
                     o_ref):
    del yin_ref
    n = pl.program_id(1)
    n_blk = DEC_SEQ // BLOCK
    start = n * BLOCK

    def table(ref, blk):
        blk = jnp.clip(blk, 0, n_blk - 1)
        return ref[pl.ds(pl.multiple_of(blk * BLOCK, BLOCK), BLOCK), :]

    k_loc, v_loc = [], []
    for off, ref in ((-1, kvp_ref), (0, kvc_ref), (1, kvn_ref)):
        k_loc.append(_rope(ref[:, :LANES], table(cos_ref, n + off), table(sin_ref, n + off)))
        v_loc.append(ref[:, LANES:])
    k_loc = jnp.concatenate(k_loc, axis=0)
    v_loc = jnp.concatenate(v_loc, axis=0)
    k_ctx = kctx_ref[...]
    v_ctx = vctx_ref[...]

    span = 3 * BLOCK
    stacked = (ATT_GROUP * BLOCK, span)
    qpos = start + (lax.broadcasted_iota(jnp.int32, stacked, 0) & (BLOCK - 1))
    kpos = start - WINDOW + lax.broadcasted_iota(jnp.int32, stacked, 1)
    valid = (jnp.abs(qpos - kpos) <= WINDOW) & (kpos >= 0) & (kpos < DEC_SEQ)

    cos_q = table(cos_ref, n)
    sin_q = table(sin_ref, n)
    q_heads = []
    for pair in range(ATT_HEADS // 2):
        qr = _rope(q_ref[:, pair * LANES:(pair + 1) * LANES], cos_q, sin_q)
        q_heads += [_head(qr, 0).astype(BF16), _head(qr, 1).astype(BF16)]
    outs = []
    for kv in range(ATT_KV_HEADS):
        heads = range(kv * ATT_GROUP, (kv + 1) * ATT_GROUP)
        q = jnp.concatenate([q_heads[h] for h in heads], axis=0)
        sink = jnp.concatenate([jnp.full((BLOCK, 1), sink_ref[h], F32) for h in heads], axis=0)
        o = _attend(q, sink, [_head(k_ctx, kv).astype(BF16), _head(k_loc, kv).astype(BF16)],
                    [_head(v_ctx, kv).astype(BF16), _head(v_loc, kv).astype(BF16)], [None, valid])
        outs += [o[g * BLOCK:(g + 1) * BLOCK] for g in range(ATT_GROUP)]
    o_ref[...] = jnp.concatenate(outs, axis=-1).astype(o_ref.dtype)


def _attn_lat(ps, sink, k_ctx, v_ctx, cos_t, sin_t, y):
    n_blk = DEC_SEQ // BLOCK
    rb0 = T_CTX // BLOCK

    def kv_spec(off):
        return pl.BlockSpec(
            (BLOCK, 2 * LANES),
            lambda b, n: (rb0 + b * n_blk + jnp.clip(n + off, 0, n_blk - 1), PS_KV // (2 * LANES)))

    return pl.pallas_call(
        _attn_lat_kernel,
        grid=(DEC_BATCH, n_blk),
        in_specs=[
            pl.BlockSpec(memory_space=pltpu.SMEM),
            pl.BlockSpec((BLOCK, ATT_W), lambda b, n: (rb0 + b * n_blk + n, 0)),
            kv_spec(-1), kv_spec(0), kv_spec(1),
            pl.BlockSpec((None, PAST_LEN, LANES), lambda b, n: (b, 0, 0)),
            pl.BlockSpec((None, PAST_LEN, LANES), lambda b, n: (b, 0, 0)),
            pl.BlockSpec((DEC_SEQ, LANES), lambda b, n: (0, 0)),
            pl.BlockSpec((DEC_SEQ, LANES), lambda b, n: (0, 0)),
            pl.BlockSpec(memory_space=pl.ANY),
        ],
        out_specs=pl.BlockSpec((BLOCK, ATT_W), lambda b, n: (rb0 + b * n_blk + n, 0)),
        out_shape=jax.ShapeDtypeStruct((T_ALL, ATT_W), BF16),
        input_output_aliases={9: 0},
        compiler_params=_cparams("parallel", "parallel"),
        name="attn_lat",
    )(sink, ps, ps, ps, ps, k_ctx, v_ctx, cos_t, sin_t, y)


def _rope_tables():
    pos = np.arange(DEC_SEQ)
    n_freq = HEAD_DIM // 4
    inv = jnp.asarray(ROPE_THETA, F32) ** (-jnp.arange(n_freq, dtype=F32) / n_freq)
    row = jnp.asarray(pos // GRID_W, F32)
    colp = jnp.asarray(pos % GRID_W, F32)
    ang_r = row[:, None] * inv[None, :]
    ang_c = colp[:, None] * inv[None, :]
    cos = jnp.concatenate([jnp.cos(ang_r)] * 2 + [jnp.cos(ang_c)] * 2, axis=-1)
    sin = jnp.concatenate([-jnp.sin(ang_r), jnp.sin(ang_r), -jnp.sin(ang_c), jnp.sin(ang_c)], axis=-1)
    return jnp.tile(cos, (1, 2)), jnp.tile(sin, (1, 2))


def _split_dot(a, w_hi, w_lo):
    a_hi = a.astype(BF16)
    a_lo = (a - a_hi.astype(F32)).astype(BF16)
    return _mm(a_hi, w_hi) + _mm(a_lo, w_hi) + _mm(a_hi, w_lo)


def _route(logits):
    lane_i = lax.broadcasted_iota(jnp.int32, logits.shape, 1)
    lane = lane_i.astype(F32)
    big = jnp.float32(1 << 20)
    is_g = lane_i < N_EXPERT_GROUPS
    lg = jnp.where(is_g, logits, -jnp.inf)
    m_g = jnp.max(lg, axis=-1, keepdims=True)
    grp = jnp.min(jnp.where(lg == m_g, lane, big), axis=-1, keepdims=True)
    z_g = jnp.sum(jnp.where(is_g, jnp.exp(lg - m_g), 0.0), axis=-1, keepdims=True)
    p_grp = 1.0 / z_g

    e_idx = lane_i - ROUTE_E0
    e_grp = (e_idx >> 2).astype(F32)
    sel = (e_idx >= 0) & (e_idx < N_EXPERTS) & (e_grp == grp)
    le = jnp.where(sel, logits, -jnp.inf)
    m_e = jnp.max(le, axis=-1, keepdims=True)
    ex = jnp.where(sel, jnp.exp(le - m_e), 0.0)
    pe = ex / jnp.sum(ex, axis=-1, keepdims=True)
    pe = jnp.where(sel, pe, -1.0)
    v1 = jnp.max(pe, axis=-1, keepdims=True)
    i1 = jnp.min(jnp.where(pe == v1, lane, big), axis=-1, keepdims=True)
    pe2 = jnp.where(lane == i1, -1.0, pe)
    v2 = jnp.max(pe2, axis=-1, keepdims=True)
    i2 = jnp.min(jnp.where(pe2 == v2, lane, big), axis=-1, keepdims=True)
    tot = v1 + v2
    return (jnp.where(lane == i1, p_grp * (v1 / tot), 0.0)
            + jnp.where(lane == i2, p_grp * (v2 / tot), 0.0))


HALO = 16


def _merge_kernel(gate_ref, conv_ref, cprev_ref, cnext_ref, yb_ref, yc_ref, x_ref, mod_ref, cw_ref,
                  wb_ref, wo_ref, g2_ref, wr_hi_ref, wr_lo_ref, br_ref,
                  x1_ref, h2_ref, rg_ref, wbb_ref, wob_ref):
    tm = TM_MERGE
    i = pl.program_id(0)

    @pl.when(i == 0)
    def _():
        wbb_ref[...] = wb_ref[...].astype(BF16)
        wob_ref[...] = wo_ref[...].astype(BF16)

    def gated(ref):
        return ref[:, CONV_W:2 * CONV_W].astype(F32) * ref[:, 2 * CONV_W:3 * CONV_W].astype(F32)

    a_b = conv_ref[:, 0:CONV_W].astype(F32)
    u = gated(conv_ref)
    u_before = gated(cprev_ref)[HALO - 1:HALO]
    u_after = gated(cnext_ref)[0:1]
    r = lax.broadcasted_iota(jnp.int32, (tm, 1), 0)
    g_row = i * tm + r
    seq_mask = jnp.where(g_row < T_CTX, SEQ - 1, DEC_SEQ - 1)
    first = (g_row & seq_mask) == 0
    last = ((g_row + 1) & seq_mask) == 0
    u_prev = jnp.where(r == 0, u_before, pltpu.roll(u, 1, 0))
    u_next = jnp.where(r == tm - 1, u_after, pltpu.roll(u, tm - 1, 0))
    u_prev = jnp.where(first, 0.0, u_prev)
    u_next = jnp.where(last, 0.0, u_next)
    y_a = a_b * (u_prev * cw_ref[0:1, :] + u * cw_ref[1:2, :] + u_next * cw_ref[2:3, :])

    def branch(j, y):
        m = gate_ref[:, j * D_MODEL:(j + 1) * D_MODEL].astype(F32)
        return jax.nn.sigmoid(m) * _mm(y, wbb_ref[j])

    z = branch(0, y_a.astype(BF16)) + branch(1, yb_ref[...]) + branch(2, yc_ref[...])
    x1 = x_ref[...] + mod_ref[2:3, :] * _mm(z.astype(BF16), wob_ref[...])
    x1_ref[...] = x1
    h2 = _mod_norm(x1, g2_ref[...], mod_ref[...], 3)
    h2_ref[...] = h2.astype(BF16)
    logits = _split_dot(h2, wr_hi_ref[...], wr_lo_ref[...]) + br_ref[...]
    rg_ref[...] = _route(logits)


def _merge(pg, pa, y_b, y_c, x, mod_l, conv_w, layer, wb, wo, g2, wr_hi, wr_lo, br):
    tm = TM_MERGE
    n_tiles = T_ALL // tm
    hb = tm // HALO
    const = lambda shape: pl.BlockSpec(shape, lambda i: (0,) * len(shape))
    return pl.pallas_call(
        _merge_kernel,
        grid=(n_tiles,),
        in_specs=[
            pl.BlockSpec((tm, PG_W), lambda i: (i, 0)),
            pl.BlockSpec((tm, 3 * CONV_W), lambda i: (i, 0)),
            pl.BlockSpec((HALO, 3 * CONV_W), lambda i: (jnp.maximum(i * hb - 1, 0), 0)),
            pl.BlockSpec((HALO, 3 * CONV_W), lambda i: (jnp.minimum((i + 1) * hb, n_tiles * hb - 1), 0)),
            pl.BlockSpec((tm, GLA_W), lambda i: (i, 0)),
            pl.BlockSpec((tm, ATT_W), lambda i: (i, 0)),
            pl.BlockSpec((tm, D_MODEL), lambda i: (i, 0)),
            pl.BlockSpec((None, 6, D_MODEL), lambda i: (_cond_row(i, tm), 0, 0)),
            const((CONV_K, CONV_W)),
            pl.BlockSpec((None, 3, 512, D_MODEL), lambda i: (layer, 0, 0, 0)),
            pl.BlockSpec((None, D_MODEL, D_MODEL), lambda i: (layer, 0, 0)),
            const((1, D_MODEL)),
            const((D_MODEL, LANES)),
            const((D_MODEL, LANES)),
            const((1, LANES)),
        ],
        out_specs=[
            pl.BlockSpec((tm, D_MODEL), lambda i: (i, 0)),
            pl.BlockSpec((tm, D_MODEL), lambda i: (i, 0)),
            pl.BlockSpec((tm, LANES), lambda i: (i, 0)),
        ],
        out_shape=[
            jax.ShapeDtypeStruct((T_ALL, D_MODEL), F32),
            jax.ShapeDtypeStruct((T_ALL, D_MODEL), BF16),
            jax.ShapeDtypeStruct((T_ALL, LANES), F32),
        ],
        scratch_shapes=[pltpu.VMEM((3, 512, D_MODEL), BF16), pltpu.VMEM((D_MODEL, D_MODEL), BF16)],
        compiler_params=_cparams("arbitrary"),
        name="merge",
    )(pg, pa, pa, pa, y_b, y_c, x, mod_l, conv_w, wb, wo, g2, wr_hi, wr_lo, br)


def _moe_kernel(h_ref, rg_ref, x1_ref, mod_ref, wg_ref, wu_ref, wd_ref, g_ref, *rest, final):
    if final:
        y_ref, acc_ref = rest
    else:
        modn_ref, x2_ref, hn_ref, acc_ref = rest
    e = pl.program_id(1)

    @pl.when(e == 0)
    def _():
        acc_ref[...] = jnp.zeros_like(acc_ref)

    h = h_ref[...]
    g = _mm(h, wg_ref[...].astype(BF16))
    u = _mm(h, wu_ref[...].astype(BF16))
    rg = rg_ref[...]
    lane = lax.broadcasted_iota(jnp.int32, rg.shape, 1)
    w = jnp.sum(jnp.where(lane == e + ROUTE_E0, rg, 0.0), axis=-1, keepdims=True)
    hid = (g * jax.nn.sigmoid(g)) * u * w
    acc_ref[...] += _mm(hid.astype(BF16), wd_ref[...].astype(BF16))

    @pl.when(e == N_EXPERTS - 1)
    def _():
        x2 = x1_ref[...] + mod_ref[5:6, :] * acc_ref[...]
        if final:
            y_ref[...] = _rms(x2) * g_ref[...]
        else:
            x2_ref[...] = x2
            hn_ref[...] = _mod_norm(x2, g_ref[...], modn_ref[...], 0).astype(BF16)


def _moe(h2, rg, x1, mod_l, layer, w_eg, w_eu, w_ed, g, mod_next, tile0, n_tiles):
    final = mod_next is None
    row = lambda i, e: (tile0 + i, 0)
    mod_spec = lambda: pl.BlockSpec((None, 6, D_MODEL), lambda i, e: (_cond_row(tile0 + i, TM), 0, 0))
    in_specs = [
        pl.BlockSpec((TM, D_MODEL), row),
        pl.BlockSpec((TM, LANES), row),
        pl.BlockSpec((TM, D_MODEL), row),
        mod_spec(),
        pl.BlockSpec((None, None, D_MODEL, D_EXPERT), lambda i, e: (layer, e, 0, 0)),
        pl.BlockSpec((None, None, D_MODEL, D_EXPERT), lambda i, e: (layer, e, 0, 0)),
        pl.BlockSpec((None, None, D_EXPERT, D_MODEL), lambda i, e: (layer, e, 0, 0)),
        pl.BlockSpec((1, D_MODEL), lambda i, e: (0, 0)),
    ]
    args = [h2, rg, x1, mod_l, w_eg, w_eu, w_ed, g]
    if final:
        out_specs = pl.BlockSpec((TM, D_MODEL), lambda i, e: (i, 0))
        out_shape = jax.ShapeDtypeStruct((n_tiles * TM, D_MODEL), F32)
    else:
        in_specs.append(mod_spec())
        args.append(mod_next)
        out_specs = [pl.BlockSpec((TM, D_MODEL), row)] * 2
        out_shape = [jax.ShapeDtypeStruct((T_ALL, D_MODEL), F32), jax.ShapeDtypeStruct((T_ALL, D_MODEL), BF16)]
    return pl.pallas_call(
        functools.partial(_moe_kernel, final=final),
        grid=(n_tiles, N_EXPERTS),
        in_specs=in_specs,
        out_specs=out_specs,
        out_shape=out_shape,
        scratch_shapes=[pltpu.VMEM((TM, D_MODEL), F32)],
        compiler_params=_cparams("parallel", "arbitrary"),
        name="moe_final" if final else "moe",
    )(*args)


def _cache_kernel(*refs):
    ps_refs, (k_ref, v_ref) = refs[:DEPTH], refs[DEPTH:]
    n_seq = TM // SEQ
    for l, ps_ref in enumerate(ps_refs):
        k_ref[:, l] = ps_ref[:, :LANES].reshape(n_seq, SEQ, LANES)
        v_ref[:, l] = ps_ref[:, LANES:].reshape(n_seq, SEQ, LANES)


def _cache(ps_layers):
    n_seq = TM // SEQ
    out = pl.BlockSpec((n_seq, DEPTH, SEQ, LANES), lambda i: (i, 0, 0, 0))
    return pl.pallas_call(
        _cache_kernel,
        grid=(T_CTX // TM,),
        in_specs=[pl.BlockSpec((TM, 2 * LANES), lambda i: (i, PS_KV // (2 * LANES)))] * DEPTH,
        out_specs=[out, out],
        out_shape=[jax.ShapeDtypeStruct((BATCH, DEPTH, SEQ, LANES), F32)] * 2,
        compiler_params=_cparams("parallel"),
        name="cache",
    )(*ps_layers)


def _prep_gla_gate(w_gate):
    w = jnp.zeros((2, LANES, GLA_W), F32)
    w = w.at[0, 0:GLA_RANK].set(w_gate[0]).at[1, GLA_RANK:2 * GLA_RANK].set(w_gate[1])
    hi = w.astype(BF16)
    return hi, (w - hi.astype(F32)).astype(BF16)


def _prep_router(w_rg, b_rg, w_re, b_re):
    w = jnp.zeros((D_MODEL, LANES), F32)
    w = w.at[:, :N_EXPERT_GROUPS].set(w_rg).at[:, ROUTE_E0:ROUTE_E0 + N_EXPERTS].set(w_re)
    b = jnp.zeros((1, LANES), F32)
    b = b.at[0, :N_EXPERT_GROUPS].set(b_rg).at[0, ROUTE_E0:ROUTE_E0 + N_EXPERTS].set(b_re)
    w_hi = w.astype(BF16)
    w_lo = (w - w_hi.astype(F32)).astype(BF16)
    return w_hi, w_lo, b


def kernel(x_prompt, x_sample, state_gla, cache_k, cache_v, c, c_ctx, w_ada, b_ada, norm1_g, norm2_g, w_in,
           conv_w, gla_w_gate, gla_b_gate, gla_norm_g, attn_sink, w_branch, w_out, w_route_group,
           b_route_group, w_route_expert, b_route_expert, w_exp_gate, w_exp_up, w_exp_down, final_norm_g):
    cond = jnp.zeros((N_COND, D_MODEL), F32).at[0].set(c_ctx).at[1:1 + DEC_BATCH].set(c)
    mod = _modulation(cond, w_ada, b_ada).reshape(DEPTH, N_COND, 6, D_MODEL)
    cos_t, sin_t = _rope_tables()
    row = lambda v: v.reshape(1, -1)

    x, h = _prenorm(x_prompt.reshape(T_CTX, D_MODEL), x_sample.reshape(T_LAT, D_MODEL), mod[0], row(norm1_g[0]))
    states = None
    ps_layers = []
    y_prompt = y_sample = None
    w_t = jnp.swapaxes(w_in, 1, 2)
    for l in range(DEPTH):
        pa = _proj(h, w_t, l, [(0, PA_W // 2)], 2, BF16, "proj_a")
        ps = _proj(h, w_t, l, [(W_IN_TQ, W_IN_GATE - W_IN_TQ), (W_IN_LR, LANES)], 1, F32, "proj_s")
        pg = _proj(h, w_t, l, [(W_IN_GATE, PG_W // 2)], 2, BF16, "proj_g")
        ps_layers.append(ps)

        wg_hi, wg_lo = _prep_gla_gate(gla_w_gate[l])
        bg = gla_b_gate[l].reshape(2, 1, GLA_W)
        ng = row(gla_norm_g[l])
        y_b, states = _gla_ctx(pa, ps, wg_hi, wg_lo, bg, ng, states, l)
        y_b = _gla_lat(pa, ps, wg_hi, wg_lo, bg, ng, state_gla[:, l], y_b)
        y_c = _attn_ctx(ps, attn_sink[l])
        y_c = _attn_lat(ps, attn_sink[l], cache_k[:, l].reshape(DEC_BATCH, PAST_LEN, LANES),
                        cache_v[:, l].reshape(DEC_BATCH, PAST_LEN, LANES), cos_t, sin_t, y_c)

        wr_hi, wr_lo, br = _prep_router(w_route_group[l], b_route_group[l], w_route_expert[l], b_route_expert[l])
        x1, h2, rg = _merge(pg, pa, y_b, y_c, x, mod[l], conv_w[l], l, w_branch, w_out, row(norm2_g[l]),
                            wr_hi, wr_lo, br)
        experts = (l, w_exp_gate, w_exp_up, w_exp_down)
        if l + 1 < DEPTH:
            x, h = _moe(h2, rg, x1, mod[l], *experts, row(norm1_g[l + 1]), mod[l + 1], 0, T_ALL // TM)
        else:
            gf = row(final_norm_g)
            y_prompt = _moe(h2, rg, x1, mod[l], *experts, gf, None, 0, T_CTX // TM)
            y_sample = _moe(h2, rg, x1, mod[l], *experts, gf, None, T_CTX // TM, T_LAT // TM)

    new_k, new_v = _cache(ps_layers)
    kv_shape = (BATCH, DEPTH, SEQ, ATT_KV_HEADS, HEAD_DIM)
    return (y_prompt.reshape(BATCH, SEQ, D_MODEL), y_sample.reshape(DEC_BATCH, DEC_SEQ, D_MODEL), states,
            new_k.reshape(kv_shape), new_v.reshape(kv_shape))
```

```python
import functools

import jax
import jax.numpy as jnp
import numpy as np
from jax import lax
from jax.experimental import pallas as pl
from jax.experimental.pallas import tpu as pltpu

F32 = jnp.float32
BF16 = jnp.bfloat16

D_MODEL = 1024
BATCH = 32
SEQ = 256
DEPTH = 2
DEC_BATCH = 2
DEC_SEQ = 2048
PAST_LEN = 512
GRID_W = 64
EPS = 1e-6
CONV_W = 512
CONV_K = 3
GLA_HEADS = 4
GLA_DK = 128
GLA_DV = 128
GLA_W = GLA_HEADS * GLA_DV
GLA_RANK = 16
GLA_TAU = 16.0
GLA_CHUNK = 64
ATT_HEADS = 8
ATT_KV_HEADS = 2
ATT_GROUP = ATT_HEADS // ATT_KV_HEADS
HEAD_DIM = 64
ATT_W = ATT_HEADS * HEAD_DIM
WINDOW = 128
BLOCK = 128
ROPE_THETA = 10000.0
N_EXPERT_GROUPS = 4
EXPERTS_PER_GROUP = 4
N_EXPERTS = 16
D_EXPERT = 256
NEG_INF = -1e30

T_CTX = BATCH * SEQ
T_LAT = DEC_BATCH * DEC_SEQ
T_ALL = T_CTX + T_LAT
N_COND = 8
LANES = 128

W_IN_LR = 3584
W_IN_TQ = 3616
W_IN_GATE = 4384
N_IN = 7456
PA_W = 3584
PA_GLA = 1536
PS_W = 896
PS_KV = 512
PS_LR = 768
PG_W = 3 * D_MODEL
ROUTE_E0 = N_EXPERT_GROUPS

TM = 1024
TM_MERGE = 256
VMEM_LIMIT = 56 * 1024 * 1024


def _cparams(*sem):
    return pltpu.CompilerParams(dimension_semantics=sem, vmem_limit_bytes=VMEM_LIMIT)


def _cond_row(i, tm):
    n_ctx = T_CTX // tm
    per = DEC_SEQ // tm
    return jnp.where(i < n_ctx, 0, 1 + (i - n_ctx) // per)


def _mm(a, b):
    return jnp.dot(a, b, preferred_element_type=F32)


def _dot_t(a, b):
    return lax.dot_general(a, b, (((1,), (1,)), ((), ())), preferred_element_type=F32)


def _dot_ta(a, b):
    return lax.dot_general(a, b, (((0,), (0,)), ((), ())), preferred_element_type=F32)


def _rms(x):
    return x * lax.rsqrt(jnp.mean(x * x, axis=-1, keepdims=True) + EPS)


def _mod_norm(x, g, mod, shift_row):
    return _rms(x) * g * (1.0 + mod[shift_row + 1:shift_row + 2, :]) + mod[shift_row:shift_row + 1, :]


def _mod_kernel(c_ref, w_ref, b_ref, o_ref):
    c = c_ref[...]
    s = (c * jax.nn.sigmoid(c)).astype(BF16)
    o_ref[...] = _mm(s, w_ref[...].astype(BF16)) + b_ref[...]


def _modulation(cond, w_ada, b_ada):
    tn = 1536
    return pl.pallas_call(
        _mod_kernel,
        grid=(DEPTH, 6 * D_MODEL // tn),
        in_specs=[
            pl.BlockSpec((N_COND, D_MODEL), lambda l, j: (0, 0)),
            pl.BlockSpec((None, D_MODEL, tn), lambda l, j: (l, 0, j)),
            pl.BlockSpec((None, 1, tn), lambda l, j: (l, 0, j)),
        ],
        out_specs=pl.BlockSpec((None, N_COND, tn), lambda l, j: (l, 0, j)),
        out_shape=jax.ShapeDtypeStruct((DEPTH, N_COND, 6 * D_MODEL), F32),
        compiler_params=_cparams("parallel", "parallel"),
        name="modulation",
    )(cond, w_ada, b_ada.reshape(DEPTH, 1, 6 * D_MODEL))


def _prenorm_kernel(xp_ref, xs_ref, mod_ref, g_ref, x_ref, h_ref):
    x = jnp.where(pl.program_id(0) < T_CTX // TM, xp_ref[...], xs_ref[...])
    x_ref[...] = x
    h_ref[...] = _mod_norm(x, g_ref[...], mod_ref[...], 0).astype(BF16)


def _prenorm(xp, xs, mod_l, g):
    n_ctx = T_CTX // TM
    return pl.pallas_call(
        _prenorm_kernel,
        grid=(T_ALL // TM,),
        in_specs=[
            pl.BlockSpec((TM, D_MODEL), lambda i: (jnp.minimum(i, n_ctx - 1), 0)),
            pl.BlockSpec((TM, D_MODEL), lambda i: (jnp.maximum(i - n_ctx, 0), 0)),
            pl.BlockSpec((None, 6, D_MODEL), lambda i: (_cond_row(i, TM), 0, 0)),
            pl.BlockSpec((1, D_MODEL), lambda i: (0, 0)),
        ],
        out_specs=[pl.BlockSpec((TM, D_MODEL), lambda i: (i, 0))] * 2,
        out_shape=[jax.ShapeDtypeStruct((T_ALL, D_MODEL), F32), jax.ShapeDtypeStruct((T_ALL, D_MODEL), BF16)],
        compiler_params=_cparams("parallel"),
        name="prenorm",
    )(xp, xs, mod_l, g)


def _proj_kernel(h_ref, *refs):
    w_refs, (o_ref, wb_ref) = refs[:-2], refs[-2:]

    @pl.when(pl.program_id(1) == 0)
    def _():
        col = 0
        for w_ref in w_refs:
            n = w_ref.shape[0]
            wb_ref[:, col:col + n] = w_ref[...].T.astype(BF16)
            col += n

    o_ref[...] = _mm(h_ref[...], wb_ref[...]).astype(o_ref.dtype)


def _proj(h, w_t, layer, pieces, n_tiles, out_dtype, name):
    tn = sum(n for _, n in pieces)

    def w_spec(c0, n):
        return pl.BlockSpec((pl.Squeezed(), pl.Element(n), pl.Element(D_MODEL)),
                            lambda j, i: (layer, pl.multiple_of(c0 + j * tn, 8), 0))

    return pl.pallas_call(
        _proj_kernel,
        grid=(n_tiles, T_ALL // TM),
        in_specs=[pl.BlockSpec((TM, D_MODEL), lambda j, i: (i, 0))] + [w_spec(c0, n) for c0, n in pieces],
        out_specs=pl.BlockSpec((TM, tn), lambda j, i: (i, j)),
        out_shape=jax.ShapeDtypeStruct((T_ALL, n_tiles * tn), out_dtype),
        scratch_shapes=[pltpu.VMEM((D_MODEL, tn), BF16)],
        compiler_params=_cparams("parallel", "arbitrary"),
        name=name,
    )(h, *([w_t] * len(pieces)))


GLA_GROUP = 256


def _log_sigmoid(z):
    return jnp.minimum(z, 0.0) - jnp.log1p(jnp.exp(-jnp.abs(z)))


def _split3(x):
    hi = x.astype(BF16)
    r1 = x - hi.astype(F32)
    mid = r1.astype(BF16)
    lo = (r1 - mid.astype(F32)).astype(BF16)
    return hi, mid, lo


def _gla_keep(d):
    ri = lax.broadcasted_iota(jnp.int32, (GLA_GROUP, GLA_GROUP), 0)
    ci = lax.broadcasted_iota(jnp.int32, (GLA_GROUP, GLA_GROUP), 1)
    if d == 0:
        return (ci <= ri) & (ci >= (ri & ~(GLA_CHUNK - 1)))
    return (ci >= ri) & (ci <= (ri | (GLA_CHUNK - 1)))


def _gla_group(q, k, v, lr, wg_hi, wg_lo, bg, states, d, keep):
    C = GLA_CHUNK
    nc = GLA_GROUP // C
    width = q.shape[-1]
    lr_hi = lr.astype(BF16)
    lr_lo = (lr - lr_hi.astype(F32)).astype(BF16)
    z = _mm(lr_hi, wg_hi) + _mm(lr_lo, wg_hi) + _mm(lr_hi, wg_lo) + bg
    la = _log_sigmoid(z) / GLA_TAU
    tri = jnp.where(keep, 1.0, 0.0).astype(BF16)
    la_hi, la_mid, la_lo = _split3(la)
    b = _mm(tri, la_hi) + _mm(tri, la_mid) + _mm(tri, la_lo)
    edge = C - 1 if d == 0 else 0
    b_last = [b[c * C + edge:c * C + edge + 1, :] for c in range(nc)]
    bl = jnp.concatenate([jnp.broadcast_to(t, (C, width)) for t in b_last], axis=0)
    q_in = (q * (GLA_DK ** -0.5) * jnp.exp(b)).astype(BF16)
    k_in = (k * jnp.exp(-b)).astype(BF16)
    k_end = (k * jnp.exp(bl - b)).astype(BF16)
    vb = v.astype(BF16)
    dec = [jnp.exp(t) for t in b_last]
    order = range(nc) if d == 0 else range(nc - 1, -1, -1)
    outs, new_states = [], []
    for h, st in enumerate(states):
        cs = slice(h * GLA_DK, (h + 1) * GLA_DK)
        att = jnp.where(keep, _dot_t(q_in[:, cs], k_in[:, cs]), 0.0)
        o = _mm(att.astype(BF16), vb[:, cs])
        pieces = [None] * nc
        for c in order:
            rs = slice(c * C, (c + 1) * C)
            pieces[c] = o[rs] + _dot_t(q_in[rs, cs], st.astype(BF16))
            st = st * dec[c][:, cs] + _dot_ta(vb[rs, cs], k_end[rs, cs])
        outs.append(jnp.concatenate(pieces, axis=0))
        new_states.append(st)
    return outs, new_states


def _gla_finish(o, r, ng):
    o = o * lax.rsqrt(jnp.mean(o * o, axis=-1, keepdims=True) + EPS)
    return o * ng * (r * jax.nn.sigmoid(r))


def _gla_ctx_kernel(q_ref, k_ref, v_ref, r_ref, lr_ref, wgh_ref, wgl_ref, bg_ref, ng_ref, *rest):
    y_ref, s_ref = rest[-2:]
    q, k, v = q_ref[...].astype(F32), k_ref[...].astype(F32), v_ref[...].astype(F32)
    lr = lr_ref[...]
    zero = jnp.zeros((GLA_DV, GLA_DK), F32)
    o_dir = []
    for d in range(2):
        outs, sts = _gla_group(q, k, v, lr, wgh_ref[d], wgl_ref[d], bg_ref[d], [zero] * GLA_HEADS, d,
                               _gla_keep(d))
        o_dir.append(outs)
        for h in range(GLA_HEADS):
            s_ref[d, h] = sts[h].T
    for h in range(GLA_HEADS):
        cs = slice(h * GLA_DV, (h + 1) * GLA_DV)
        y = _gla_finish(o_dir[0][h] + o_dir[1][h], r_ref[:, cs].astype(F32), ng_ref[:, cs])
        y_ref[:, cs] = y.astype(y_ref.dtype)


def _gla_lat_kernel(q_ref, k_ref, v_ref, r_ref, lr_ref, wgh_ref, wgl_ref, bg_ref, ng_ref, s0_ref, yin_ref,
                    y_ref, of_ref, ob_ref):
    del yin_ref
    n_groups = DEC_SEQ // GLA_GROUP
    keep_f, keep_b = _gla_keep(0), _gla_keep(1)

    def load(rows):
        return (q_ref[rows, :].astype(F32), k_ref[rows, :].astype(F32), v_ref[rows, :].astype(F32),
                lr_ref[rows, :])

    def body(g, carry):
        st_f, st_b = carry
        rf = pl.ds(pl.multiple_of(g * GLA_GROUP, GLA_GROUP), GLA_GROUP)
        rb = pl.ds(pl.multiple_of((n_groups - 1 - g) * GLA_GROUP, GLA_GROUP), GLA_GROUP)
        (o,), (st_f,) = _gla_group(*load(rf), wgh_ref[0], wgl_ref[0], bg_ref[0], [st_f], 0, keep_f)
        of_ref[rf, :] = o
        (o,), (st_b,) = _gla_group(*load(rb), wgh_ref[1], wgl_ref[1], bg_ref[1], [st_b], 1, keep_b)
        ob_ref[rb, :] = o
        return st_f, st_b

    lax.fori_loop(0, n_groups, body, (s0_ref[0].T, s0_ref[1].T))
    y = _gla_finish(of_ref[...] + ob_ref[...], r_ref[...].astype(F32), ng_ref[...])
    y_ref[...] = y.astype(y_ref.dtype)


def _gla_ctx(pa, ps, wg_hi, wg_lo, bg, ng, states_prev, layer):
    cb = PA_GLA // GLA_W
    const = lambda shape: pl.BlockSpec(shape, lambda s: (0,) * len(shape))
    in_specs = [pl.BlockSpec((SEQ, GLA_W), lambda s, j=j: (s, cb + j)) for j in range(4)] + [
        pl.BlockSpec((SEQ, LANES), lambda s: (s, PS_LR // LANES)),
        const((2, LANES, GLA_W)), const((2, LANES, GLA_W)), const((2, 1, GLA_W)), const((1, GLA_W))]
    args = [pa, pa, pa, pa, ps, wg_hi, wg_lo, bg, ng]
    aliases = {}
    if states_prev is not None:
        in_specs.append(pl.BlockSpec(memory_space=pl.ANY))
        args.append(states_prev)
        aliases = {len(args) - 1: 1}
    return pl.pallas_call(
        _gla_ctx_kernel,
        grid=(BATCH,),
        in_specs=in_specs,
        out_specs=[pl.BlockSpec((SEQ, GLA_W), lambda s: (s, 0)),
                   pl.BlockSpec((None, None, 2, GLA_HEADS, GLA_DK, GLA_DV), lambda s: (s, layer, 0, 0, 0, 0))],
        out_shape=[jax.ShapeDtypeStruct((T_ALL, GLA_W), BF16),
                   jax.ShapeDtypeStruct((BATCH, DEPTH, 2, GLA_HEADS, GLA_DK, GLA_DV), F32)],
        input_output_aliases=aliases,
        compiler_params=_cparams("parallel"),
        name="gla_ctx",
    )(*args)


def _gla_lat(pa, ps, wg_hi, wg_lo, bg, ng, s0, y):
    rb0 = T_CTX // DEC_SEQ
    cb = PA_GLA // LANES

    def col(piece):
        return pl.BlockSpec((DEC_SEQ, LANES), lambda s, h: (rb0 + s, cb + GLA_HEADS * piece + h))

    return pl.pallas_call(
        _gla_lat_kernel,
        grid=(DEC_BATCH, GLA_HEADS),
        in_specs=[col(0), col(1), col(2), col(3),
                  pl.BlockSpec((DEC_SEQ, LANES), lambda s, h: (rb0 + s, PS_LR // LANES)),
                  pl.BlockSpec((2, LANES, LANES), lambda s, h: (0, 0, h)),
                  pl.BlockSpec((2, LANES, LANES), lambda s, h: (0, 0, h)),
                  pl.BlockSpec((2, 1, LANES), lambda s, h: (0, 0, h)),
                  pl.BlockSpec((1, LANES), lambda s, h: (0, h)),
                  pl.BlockSpec((None, 2, None, GLA_DK, GLA_DV), lambda s, h: (s, 0, h, 0, 0)),
                  pl.BlockSpec(memory_space=pl.ANY)],
        out_specs=pl.BlockSpec((DEC_SEQ, LANES), lambda s, h: (rb0 + s, h)),
        out_shape=jax.ShapeDtypeStruct((T_ALL, GLA_W), BF16),
        scratch_shapes=[pltpu.VMEM((DEC_SEQ, GLA_DV), F32), pltpu.VMEM((DEC_SEQ, GLA_DV), F32)],
        input_output_aliases={10: 0},
        compiler_params=_cparams("parallel", "parallel"),
        name="gla_lat",
    )(pa, pa, pa, pa, ps, wg_hi, wg_lo, bg, ng, s0, y)


def _attend(q, sink, k_parts, v_parts, masks):
    scores = []
    for k, mask in zip(k_parts, masks):
        s = _dot_t(q, k) * (HEAD_DIM ** -0.5)
        if mask is not None:
            s = jnp.where(mask, s, NEG_INF)
        scores.append(s)
    m = jnp.broadcast_to(sink, (q.shape[0], 1)).astype(F32)
    for s in scores:
        m = jnp.maximum(m, jnp.max(s, axis=-1, keepdims=True))
    den = jnp.exp(sink - m)
    o = None
    for s, v in zip(scores, v_parts):
        e = jnp.exp(s - m)
        den = den + jnp.sum(e, axis=-1, keepdims=True)
        pv = _mm(e.astype(BF16), v)
        o = pv if o is None else o + pv
    return o / den


def _head(a, j):
    return a[:, j * HEAD_DIM:(j + 1) * HEAD_DIM]


ATT_ROWS_CTX = 128


def _attn_ctx_kernel(sink_ref, q_ref, kv_ref, *rest):
    o_ref = rest[-1]
    q = q_ref[...]
    kv_all = kv_ref[...]
    outs = []
    for h in range(ATT_HEADS):
        kv = h // ATT_GROUP
        k = _head(kv_all, kv).astype(BF16)
        v = _head(kv_all, ATT_KV_HEADS + kv).astype(BF16)
        qh = _head(q, h).astype(BF16)
        blocks = [_attend(qh[r:r + ATT_ROWS_CTX], sink_ref[h], [k], [v], [None])
                  for r in range(0, SEQ, ATT_ROWS_CTX)]
        outs.append(jnp.concatenate(blocks, axis=0))
    o_ref[...] = jnp.concatenate(outs, axis=-1).astype(o_ref.dtype)


def _attn_ctx(ps, sink):
    return pl.pallas_call(
        _attn_ctx_kernel,
        grid=(BATCH,),
        in_specs=[
            pl.BlockSpec(memory_space=pltpu.SMEM),
            pl.BlockSpec((SEQ, ATT_W), lambda s: (s, 0)),
            pl.BlockSpec((SEQ, 2 * LANES), lambda s: (s, PS_KV // (2 * LANES))),
        ],
        out_specs=pl.BlockSpec((SEQ, ATT_W), lambda s: (s, 0)),
        out_shape=jax.ShapeDtypeStruct((T_ALL, ATT_W), BF16),
        compiler_params=_cparams("parallel"),
        name="attn_ctx",
    )(sink, ps, ps)


def _rope(x, cos, sin_signed):
    lane = lax.broadcasted_iota(jnp.int32, x.shape, 1)
    partner = jnp.where((lane & 31) < 16, pltpu.roll(x, LANES - 16, 1), pltpu.roll(x, 16, 1))
    return x * cos + partner * sin_signed


def _attn_lat_kernel(sink_ref, q_ref, kvp_ref, kvc_ref, kvn_ref, kctx_ref, vctx_ref, cos_ref, sin_ref, yin_ref,
                     o_ref):
    del yin_ref
    n = pl.program_id(1)
    n_blk = DEC_SEQ // BLOCK
    start = n * BLOCK

    def table(ref, blk):
        blk = jnp.clip(blk, 0, n_blk - 1)
        return ref[pl.ds(pl.multiple_of(blk * BLOCK, BLOCK), BLOCK), :]

    k_loc, v_loc = [], []
    for off, ref in ((-1, kvp_ref), (0, kvc_ref), (1, kvn_ref)):
        k_loc.append(_rope(ref[:, :LANES], table(cos_ref, n + off), table(sin_ref, n + off)))
        v_loc.append(ref[:, LANES:])
    k_loc = jnp.concatenate(k_loc, axis=0)
    v_loc = jnp.concatenate(v_loc, axis=0)
    k_ctx = kctx_ref[...]
    v_ctx = vctx_ref[...]

    span = 3 * BLOCK
    stacked = (ATT_GROUP * BLOCK, span)
    qpos = start + (lax.broadcasted_iota(jnp.int32, stacked, 0) & (BLOCK - 1))
    kpos = start - WINDOW + lax.broadcasted_iota(jnp.int32, stacked, 1)
    valid = (jnp.abs(qpos - kpos) <= WINDOW) & (kpos >= 0) & (kpos < DEC_SEQ)

    cos_q = table(cos_ref, n)
    sin_q = table(sin_ref, n)
    q_heads = []
    for pair in range(ATT_HEADS // 2):
        qr = _rope(q_ref[:, pair * LANES:(pair + 1) * LANES], cos_q, sin_q)
        q_heads += [_head(qr, 0).astype(BF16), _head(qr, 1).astype(BF16)]
    outs = []
    for kv in range(ATT_KV_HEADS):
        heads = range(kv * ATT_GROUP, (kv + 1) * ATT_GROUP)
        q = jnp.concatenate([q_heads[h] for h in heads], axis=0)
        sink = jnp.concatenate([jnp.full((BLOCK, 1), sink_ref[h], F32) for h in heads], axis=0)
        o = _attend(q, sink, [_head(k_ctx, kv).astype(BF16), _head(k_loc, kv).astype(BF16)],
                    [_head(v_ctx, kv).astype(BF16), _head(v_loc, kv).astype(BF16)], [None, valid])
        outs += [o[g * BLOCK:(g + 1) * BLOCK] for g in range(ATT_GROUP)]
    o_ref[...] = jnp.concatenate(outs, axis=-1).astype(o_ref.dtype)


def _attn_lat(ps, sink, k_ctx, v_ctx, cos_t, sin_t, y):
    n_blk = DEC_SEQ // BLOCK
    rb0 = T_CTX // BLOCK

    def kv_spec(off):
        return pl.BlockSpec(
            (BLOCK, 2 * LANES),
            lambda b, n: (rb0 + b * n_blk + jnp.clip(n + off, 0, n_blk - 1), PS_KV // (2 * LANES)))

    return pl.pallas_call(
        _attn_lat_kernel,
        grid=(DEC_BATCH, n_blk),
        in_specs=[
            pl.BlockSpec(memory_space=pltpu.SMEM),
            pl.BlockSpec((BLOCK, ATT_W), lambda b, n: (rb0 + b * n_blk + n, 0)),
            kv_spec(-1), kv_spec(0), kv_spec(1),
            pl.BlockSpec((None, PAST_LEN, LANES), lambda b, n: (b, 0, 0)),
            pl.BlockSpec((None, PAST_LEN, LANES), lambda b, n: (b, 0, 0)),
            pl.BlockSpec((DEC_SEQ, LANES), lambda b, n: (0, 0)),
            pl.BlockSpec((DEC_SEQ, LANES), lambda b, n: (0, 0)),
            pl.BlockSpec(memory_space=pl.ANY),
        ],
        out_specs=pl.BlockSpec((BLOCK, ATT_W), lambda b, n: (rb0 + b * n_blk + n, 0)),
        out_shape=jax.ShapeDtypeStruct((T_ALL, ATT_W), BF16),
        input_output_aliases={9: 0},
        compiler_params=_cparams("parallel", "parallel"),
        name="attn_lat",
    )(sink, ps, ps, ps, ps, k_ctx, v_ctx, cos_t, sin_t, y)


def _rope_tables():
    pos = np.arange(DEC_SEQ)
    n_freq = HEAD_DIM // 4
    inv = jnp.asarray(ROPE_THETA, F32) ** (-jnp.arange(n_freq, dtype=F32) / n_freq)
    row = jnp.asarray(pos // GRID_W, F32)
    colp = jnp.asarray(pos % GRID_W, F32)
    ang_r = row[:, None] * inv[None, :]
    ang_c = colp[:, None] * inv[None, :]
    cos = jnp.concatenate([jnp.cos(ang_r)] * 2 + [jnp.cos(ang_c)] * 2, axis=-1)
    sin = jnp.concatenate([-jnp.sin(ang_r), jnp.sin(ang_r), -jnp.sin(ang_c), jnp.sin(ang_c)], axis=-1)
    return jnp.tile(cos, (1, 2)), jnp.tile(sin, (1, 2))


def _split_dot(a, w_hi, w_lo):
    a_hi = a.astype(BF16)
    a_lo = (a - a_hi.astype(F32)).astype(BF16)
    return _mm(a_hi, w_hi) + _mm(a_lo, w_hi) + _mm(a_hi, w_lo)


def _route(logits):
    lane_i = lax.broadcasted_iota(jnp.int32, logits.shape, 1)
    lane = lane_i.astype(F32)
    big = jnp.float32(1 << 20)
    is_g = lane_i < N_EXPERT_GROUPS
    lg = jnp.where(is_g, logits, -jnp.inf)
    m_g = jnp.max(lg, axis=-1, keepdims=True)
    grp = jnp.min(jnp.where(lg == m_g, lane, big), axis=-1, keepdims=True)
    z_g = jnp.sum(jnp.where(is_g, jnp.exp(lg - m_g), 0.0), axis=-1, keepdims=True)
    p_grp = 1.0 / z_g

    e_idx = lane_i - ROUTE_E0
    e_grp = (e_idx >> 2).astype(F32)
    sel = (e_idx >= 0) & (e_idx < N_EXPERTS) & (e_grp == grp)
    le = jnp.where(sel, logits, -jnp.inf)
    m_e = jnp.max(le, axis=-1, keepdims=True)
    ex = jnp.where(sel, jnp.exp(le - m_e), 0.0)
    pe = ex / jnp.sum(ex, axis=-1, keepdims=True)
    pe = jnp.where(sel, pe, -1.0)
    v1 = jnp.max(pe, axis=-1, keepdims=True)
    i1 = jnp.min(jnp.where(pe == v1, lane, big), axis=-1, keepdims=True)
    pe2 = jnp.where(lane == i1, -1.0, pe)
    v2 = jnp.max(pe2, axis=-1, keepdims=True)
    i2 = jnp.min(jnp.where(pe2 == v2, lane, big), axis=-1, keepdims=True)
    tot = v1 + v2
    return (jnp.where(lane == i1, p_grp * (v1 / tot), 0.0)
            + jnp.where(lane == i2, p_grp * (v2 / tot), 0.0)
            + jnp.where(lane_i == LANES - 1, grp, 0.0))


HALO = 16


def _merge_kernel(gate_ref, conv_ref, cprev_ref, cnext_ref, yb_ref, yc_ref, x_ref, mod_ref, cw_ref,
                  wb_ref, wo_ref, g2_ref, wr_hi_ref, wr_lo_ref, br_ref,
                  x1_ref, h2_ref, rg_ref, wbb_ref, wob_ref):
    tm = TM_MERGE
    i = pl.program_id(0)

    @pl.when(i == 0)
    def _():
        wbb_ref[...] = wb_ref[...].astype(BF16)
        wob_ref[...] = wo_ref[...].astype(BF16)

    def gated(ref):
        return ref[:, CONV_W:2 * CONV_W].astype(F32) * ref[:, 2 * CONV_W:3 * CONV_W].astype(F32)

    a_b = conv_ref[:, 0:CONV_W].astype(F32)
    u = gated(conv_ref)
    u_before = gated(cprev_ref)[HALO - 1:HALO]
    u_after = gated(cnext_ref)[0:1]
    r = lax.broadcasted_iota(jnp.int32, (tm, 1), 0)
    g_row = i * tm + r
    seq_mask = jnp.where(g_row < T_CTX, SEQ - 1, DEC_SEQ - 1)
    first = (g_row & seq_mask) == 0
    last = ((g_row + 1) & seq_mask) == 0
    u_prev = jnp.where(r == 0, u_before, pltpu.roll(u, 1, 0))
    u_next = jnp.where(r == tm - 1, u_after, pltpu.roll(u, tm - 1, 0))
    u_prev = jnp.where(first, 0.0, u_prev)
    u_next = jnp.where(last, 0.0, u_next)
    y_a = a_b * (u_prev * cw_ref[0:1, :] + u * cw_ref[1:2, :] + u_next * cw_ref[2:3, :])

    def branch(j, y):
        m = gate_ref[:, j * D_MODEL:(j + 1) * D_MODEL].astype(F32)
        return jax.nn.sigmoid(m) * _mm(y, wbb_ref[j])

    z = branch(0, y_a.astype(BF16)) + branch(1, yb_ref[...]) + branch(2, yc_ref[...])
    x1 = x_ref[...] + mod_ref[2:3, :] * _mm(z.astype(BF16), wob_ref[...])
    x1_ref[...] = x1
    h2 = _mod_norm(x1, g2_ref[...], mod_ref[...], 3)
    h2_ref[...] = h2.astype(BF16)
    logits = _split_dot(h2, wr_hi_ref[...], wr_lo_ref[...]) + br_ref[...]
    rg_ref[...] = _route(logits)


def _merge(pg, pa, y_b, y_c, x, mod_l, conv_w, layer, wb, wo, g2, wr_hi, wr_lo, br):
    tm = TM_MERGE
    n_tiles = T_ALL // tm
    hb = tm // HALO
    const = lambda shape: pl.BlockSpec(shape, lambda i: (0,) * len(shape))
    return pl.pallas_call(
        _merge_kernel,
        grid=(n_tiles,),
        in_specs=[
            pl.BlockSpec((tm, PG_W), lambda i: (i, 0)),
            pl.BlockSpec((tm, 3 * CONV_W), lambda i: (i, 0)),
            pl.BlockSpec((HALO, 3 * CONV_W), lambda i: (jnp.maximum(i * hb - 1, 0), 0)),
            pl.BlockSpec((HALO, 3 * CONV_W), lambda i: (jnp.minimum((i + 1) * hb, n_tiles * hb - 1), 0)),
            pl.BlockSpec((tm, GLA_W), lambda i: (i, 0)),
            pl.BlockSpec((tm, ATT_W), lambda i: (i, 0)),
            pl.BlockSpec((tm, D_MODEL), lambda i: (i, 0)),
            pl.BlockSpec((None, 6, D_MODEL), lambda i: (_cond_row(i, tm), 0, 0)),
            const((CONV_K, CONV_W)),
            pl.BlockSpec((None, 3, 512, D_MODEL), lambda i: (layer, 0, 0, 0)),
            pl.BlockSpec((None, D_MODEL, D_MODEL), lambda i: (layer, 0, 0)),
            const((1, D_MODEL)),
            const((D_MODEL, LANES)),
            const((D_MODEL, LANES)),
            const((1, LANES)),
        ],
        out_specs=[
            pl.BlockSpec((tm, D_MODEL), lambda i: (i, 0)),
            pl.BlockSpec((tm, D_MODEL), lambda i: (i, 0)),
            pl.BlockSpec((tm, LANES), lambda i: (i, 0)),
        ],
        out_shape=[
            jax.ShapeDtypeStruct((T_ALL, D_MODEL), F32),
            jax.ShapeDtypeStruct((T_ALL, D_MODEL), BF16),
            jax.ShapeDtypeStruct((T_ALL, LANES), F32),
        ],
        scratch_shapes=[pltpu.VMEM((3, 512, D_MODEL), BF16), pltpu.VMEM((D_MODEL, D_MODEL), BF16)],
        compiler_params=_cparams("arbitrary"),
        name="merge",
    )(pg, pa, pa, pa, y_b, y_c, x, mod_l, conv_w, wb, wo, g2, wr_hi, wr_lo, br)


SUB = 128
MOE_SLOTS = TM + N_EXPERT_GROUPS * SUB


def _dispatch_tables(rg):
    n_tiles = T_ALL // TM
    grp = rg[:, LANES - 1].astype(jnp.int32).reshape(n_tiles, TM)
    onehot = (grp[..., None] == jnp.arange(N_EXPERT_GROUPS, dtype=jnp.int32)).astype(jnp.int32)
    cnt = onehot.sum(axis=1)
    nblk = (cnt + SUB - 1) // SUB
    bstart = jnp.cumsum(nblk, axis=1) - nblk
    rank = jnp.cumsum(onehot, axis=1) - onehot
    slot = ((bstart[:, None, :] * SUB + rank) * onehot).sum(axis=-1)
    return (bstart.reshape(-1), nblk.reshape(-1), slot.reshape(n_tiles, 1, TM), slot.reshape(T_ALL, 1))


def _moe_kernel(bstart_ref, nblk_ref, h_ref, rg_ref, srow_ref, scol_ref, x1_ref, mod_ref, wg_ref, wu_ref,
                wd_ref, g_ref, *rest, final, tile0):
    if final:
        y_ref, hs_ref, gs_ref, os_ref = rest
    else:
        modn_ref, x2_ref, hn_ref, hs_ref, gs_ref, os_ref = rest
    i = pl.program_id(0)
    grp = pl.program_id(1)

    @pl.when(grp == 0)
    def _():
        slot_of_token = srow_ref[...]
        pm = jnp.where(lax.broadcasted_iota(jnp.int32, (MOE_SLOTS, TM), 0) == slot_of_token, 1.0, 0.0)
        pm = pm.astype(BF16)
        hs_ref[...] = _mm(pm, h_ref[...]).astype(BF16)
        r_hi, r_mid, r_lo = _split3(rg_ref[...])
        gs_ref[...] = _mm(pm, r_hi) + _mm(pm, r_mid) + _mm(pm, r_lo)
        os_ref[...] = jnp.zeros_like(os_ref)

    k = (tile0 + i) * N_EXPERT_GROUPS + grp
    first = bstart_ref[k]
    lane = lax.broadcasted_iota(jnp.int32, (SUB, LANES), 1)

    def block(b, carry):
        rows = pl.ds(pl.multiple_of((first + b) * SUB, SUB), SUB)
        x = hs_ref[rows, :]
        gates = gs_ref[rows, :]
        acc = jnp.zeros((SUB, D_MODEL), F32)
        for e in range(EXPERTS_PER_GROUP):
            ge = _mm(x, wg_ref[e])
            ue = _mm(x, wu_ref[e])
            w = jnp.sum(jnp.where(lane == ROUTE_E0 + grp * EXPERTS_PER_GROUP + e, gates, 0.0),
                        axis=-1, keepdims=True)
            hid = (ge * jax.nn.sigmoid(ge)) * ue * w
            acc = acc + _mm(hid.astype(BF16), wd_ref[e])
        os_ref[rows, :] = acc.astype(BF16)
        return carry

    lax.fori_loop(0, nblk_ref[k], block, 0)

    @pl.when(grp == N_EXPERT_GROUPS - 1)
    def _():
        slot_col = scol_ref[...]
        pt = jnp.where(lax.broadcasted_iota(jnp.int32, (TM, MOE_SLOTS), 1) == slot_col, 1.0, 0.0)
        moe = _mm(pt.astype(BF16), os_ref[...])
        x2 = x1_ref[...] + mod_ref[5:6, :] * moe
        if final:
            y_ref[...] = _rms(x2) * g_ref[...]
        else:
            x2_ref[...] = x2
            hn_ref[...] = _mod_norm(x2, g_ref[...], modn_ref[...], 0).astype(BF16)


def _moe(tables, h2, rg, x1, mod_l, layer, w_eg, w_eu, w_ed, g, mod_next, tile0, n_tiles):
    bstart, nblk, slot_row, slot_col = tables
    final = mod_next is None
    row = lambda i, e, *_: (tile0 + i, 0)
    mod_spec = lambda: pl.BlockSpec((None, 6, D_MODEL), lambda i, e, *_: (_cond_row(tile0 + i, TM), 0, 0))
    group = lambda i, e, *_: (layer, e, 0, 0)
    in_specs = [
        pl.BlockSpec((TM, D_MODEL), row),
        pl.BlockSpec((TM, LANES), row),
        pl.BlockSpec((None, 1, TM), lambda i, e, *_: (tile0 + i, 0, 0)),
        pl.BlockSpec((TM, 1), row),
        pl.BlockSpec((TM, D_MODEL), row),
        mod_spec(),
        pl.BlockSpec((None, EXPERTS_PER_GROUP, D_MODEL, D_EXPERT), group),
        pl.BlockSpec((None, EXPERTS_PER_GROUP, D_MODEL, D_EXPERT), group),
        pl.BlockSpec((None, EXPERTS_PER_GROUP, D_EXPERT, D_MODEL), group),
        pl.BlockSpec((1, D_MODEL), lambda i, e, *_: (0, 0)),
    ]
    args = [h2, rg, slot_row, slot_col, x1, mod_l, w_eg, w_eu, w_ed, g]
    if final:
        out_specs = pl.BlockSpec((TM, D_MODEL), lambda i, e, *_: (i, 0))
        out_shape = jax.ShapeDtypeStruct((n_tiles * TM, D_MODEL), F32)
    else:
        in_specs.append(mod_spec())
        args.append(mod_next)
        out_specs = [pl.BlockSpec((TM, D_MODEL), row)] * 2
        out_shape = [jax.ShapeDtypeStruct((T_ALL, D_MODEL), F32), jax.ShapeDtypeStruct((T_ALL, D_MODEL), BF16)]
    return pl.pallas_call(
        functools.partial(_moe_kernel, final=final, tile0=tile0),
        grid_spec=pltpu.PrefetchScalarGridSpec(
            num_scalar_prefetch=2,
            grid=(n_tiles, N_EXPERT_GROUPS),
            in_specs=in_specs,
            out_specs=out_specs,
            scratch_shapes=[pltpu.VMEM((MOE_SLOTS, D_MODEL), BF16), pltpu.VMEM((MOE_SLOTS, LANES), F32),
                            pltpu.VMEM((MOE_SLOTS, D_MODEL), BF16)],
        ),
        out_shape=out_shape,
        compiler_params=_cparams("parallel", "arbitrary"),
        name="moe_final" if final else "moe",
    )(bstart, nblk, *args)


def _cache_kernel(*refs):
    ps_refs, (k_ref, v_ref) = refs[:DEPTH], refs[DEPTH:]
    n_seq = TM // SEQ
    for l, ps_ref in enumerate(ps_refs):
        k_ref[:, l] = ps_ref[:, :LANES].reshape(n_seq, SEQ, LANES)
        v_ref[:, l] = ps_ref[:, LANES:].reshape(n_seq, SEQ, LANES)


def _cache(ps_layers):
    n_seq = TM // SEQ
    out = pl.BlockSpec((n_seq, DEPTH, SEQ, LANES), lambda i: (i, 0, 0, 0))
    return pl.pallas_call(
        _cache_kernel,
        grid=(T_CTX // TM,),
        in_specs=[pl.BlockSpec((TM, 2 * LANES), lambda i: (i, PS_KV // (2 * LANES)))] * DEPTH,
        out_specs=[out, out],
        out_shape=[jax.ShapeDtypeStruct((BATCH, DEPTH, SEQ, LANES), F32)] * 2,
        compiler_params=_cparams("parallel"),
        name="cache",
    )(*ps_layers)


def _prep_gla_gate(w_gate):
    w = jnp.zeros((2, LANES, GLA_W), F32)
    w = w.at[0, 0:GLA_RANK].set(w_gate[0]).at[1, GLA_RANK:2 * GLA_RANK].set(w_gate[1])
    hi = w.astype(BF16)
    return hi, (w - hi.astype(F32)).astype(BF16)


def _prep_router(w_rg, b_rg, w_re, b_re):
    w = jnp.zeros((D_MODEL, LANES), F32)
    w = w.at[:, :N_EXPERT_GROUPS].set(w_rg).at[:, ROUTE_E0:ROUTE_E0 + N_EXPERTS].set(w_re)
    b = jnp.zeros((1, LANES), F32)
    b = b.at[0, :N_EXPERT_GROUPS].set(b_rg).at[0, ROUTE_E0:ROUTE_E0 + N_EXPERTS].set(b_re)
    w_hi = w.astype(BF16)
    w_lo = (w - w_hi.astype(F32)).astype(BF16)
    return w_hi, w_lo, b


def kernel(x_prompt, x_sample, state_gla, cache_k, cache_v, c, c_ctx, w_ada, b_ada, norm1_g, norm2_g, w_in,
           conv_w, gla_w_gate, gla_b_gate, gla_norm_g, attn_sink, w_branch, w_out, w_route_group,
           b_route_group, w_route_expert, b_route_expert, w_exp_gate, w_exp_up, w_exp_down, final_norm_g):
    cond = jnp.zeros((N_COND, D_MODEL), F32).at[0].set(c_ctx).at[1:1 + DEC_BATCH].set(c)
    mod = _modulation(cond, w_ada, b_ada).reshape(DEPTH, N_COND, 6, D_MODEL)
    cos_t, sin_t = _rope_tables()
    row = lambda v: v.reshape(1, -1)

    x, h = _prenorm(x_prompt.reshape(T_CTX, D_MODEL), x_sample.reshape(T_LAT, D_MODEL), mod[0], row(norm1_g[0]))
    states = None
    ps_layers = []
    y_prompt = y_sample = None
    w_t = jnp.swapaxes(w_in, 1, 2)
    experts = (w_exp_gate.astype(BF16), w_exp_up.astype(BF16), w_exp_down.astype(BF16))
    for l in range(DEPTH):
        pa = _proj(h, w_t, l, [(0, PA_W // 2)], 2, BF16, "proj_a")
        ps = _proj(h, w_t, l, [(W_IN_TQ, W_IN_GATE - W_IN_TQ), (W_IN_LR, LANES)], 1, F32, "proj_s")
        pg = _proj(h, w_t, l, [(W_IN_GATE, PG_W // 2)], 2, BF16, "proj_g")
        ps_layers.append(ps)

        wg_hi, wg_lo = _prep_gla_gate(gla_w_gate[l])
        bg = gla_b_gate[l].reshape(2, 1, GLA_W)
        ng = row(gla_norm_g[l])
        y_b, states = _gla_ctx(pa, ps, wg_hi, wg_lo, bg, ng, states, l)
        y_b = _gla_lat(pa, ps, wg_hi, wg_lo, bg, ng, state_gla[:, l], y_b)
        y_c = _attn_ctx(ps, attn_sink[l])
        y_c = _attn_lat(ps, attn_sink[l], cache_k[:, l].reshape(DEC_BATCH, PAST_LEN, LANES),
                        cache_v[:, l].reshape(DEC_BATCH, PAST_LEN, LANES), cos_t, sin_t, y_c)

        wr_hi, wr_lo, br = _prep_router(w_route_group[l], b_route_group[l], w_route_expert[l], b_route_expert[l])
        x1, h2, rg = _merge(pg, pa, y_b, y_c, x, mod[l], conv_w[l], l, w_branch, w_out, row(norm2_g[l]),
                            wr_hi, wr_lo, br)
        moe_in = (_dispatch_tables(rg), h2, rg, x1, mod[l], l, *experts)
        if l + 1 < DEPTH:
            x, h = _moe(*moe_in, row(norm1_g[l + 1]), mod[l + 1], 0, T_ALL // TM)
        else:
            gf = row(final_norm_g)
            y_prompt = _moe(*moe_in, gf, None, 0, T_CTX // TM)
            y_sample = _moe(*moe_in, gf, None, T_CTX // TM, T_LAT // TM)

    new_k, new_v = _cache(ps_layers)
    kv_shape = (BATCH, DEPTH, SEQ, ATT_KV_HEADS, HEAD_DIM)
    return (y_prompt.reshape(BATCH, SEQ, D_MODEL), y_sample.reshape(DEC_BATCH, DEC_SEQ, D_MODEL), states,
            new_k.reshape(kv_shape), new_v.reshape(kv_shape))
```

```python
import functools

import jax
import jax.numpy as jnp
import numpy as np
from jax import lax
from jax.experimental import pallas as pl
from jax.experimental.pallas import tpu as pltpu

F32 = jnp.float32
BF16 = jnp.bfloat16

D_MODEL = 1024
BATCH = 32
SEQ = 256
DEPTH = 2
DEC_BATCH = 2
DEC_SEQ = 2048
PAST_LEN = 512
GRID_W = 64
EPS = 1e-6
CONV_W = 512
CONV_K = 3
GLA_HEADS = 4
GLA_DK = 128
GLA_DV = 128
GLA_W = GLA_HEADS * GLA_DV
GLA_RANK = 16
GLA_TAU = 16.0
GLA_CHUNK = 64
ATT_HEADS = 8
ATT_KV_HEADS = 2
ATT_GROUP = ATT_HEADS // ATT_KV_HEADS
HEAD_DIM = 64
ATT_W = ATT_HEADS * HEAD_DIM
WINDOW = 128
BLOCK = 128
ROPE_THETA = 10000.0
N_EXPERT_GROUPS = 4
EXPERTS_PER_GROUP = 4
N_EXPERTS = 16
D_EXPERT = 256
NEG_INF = -1e30

T_CTX = BATCH * SEQ
T_LAT = DEC_BATCH * DEC_SEQ
T_ALL = T_CTX + T_LAT
N_COND = 8
LANES = 128

W_IN_LR = 3584
W_IN_TQ = 3616
W_IN_GATE = 4384
N_IN = 7456
PA_W = 3584
PA_GLA = 1536
PS_W = 896
PS_KV = 512
PS_LR = 768
PG_W = 3 * D_MODEL
ROUTE_E0 = N_EXPERT_GROUPS

TM = 1024
TM_MERGE = 256
VMEM_LIMIT = 56 * 1024 * 1024


def _cparams(*sem):
    return pltpu.CompilerParams(dimension_semantics=sem, vmem_limit_bytes=VMEM_LIMIT)


def _cond_row(i, tm):
    n_ctx = T_CTX // tm
    per = DEC_SEQ // tm
    return jnp.where(i < n_ctx, 0, 1 + (i - n_ctx) // per)


def _mm(a, b):
    return jnp.dot(a, b, preferred_element_type=F32)


def _dot_t(a, b):
    return lax.dot_general(a, b, (((1,), (1,)), ((), ())), preferred_element_type=F32)


def _dot_ta(a, b):
    return lax.dot_general(a, b, (((0,), (0,)), ((), ())), preferred_element_type=F32)


def _rms(x):
    return x * lax.rsqrt(jnp.mean(x * x, axis=-1, keepdims=True) + EPS)


def _mod_norm(x, g, mod, shift_row):
    return _rms(x) * g * (1.0 + mod[shift_row + 1:shift_row + 2, :]) + mod[shift_row:shift_row + 1, :]


def _mod_kernel(c_ref, w_ref, b_ref, o_ref):
    c = c_ref[...]
    s = (c * jax.nn.sigmoid(c)).astype(BF16)
    o_ref[...] = _mm(s, w_ref[...].astype(BF16)) + b_ref[...]


def _modulation(cond, w_ada, b_ada):
    tn = 1536
    return pl.pallas_call(
        _mod_kernel,
        grid=(DEPTH, 6 * D_MODEL // tn),
        in_specs=[
            pl.BlockSpec((N_COND, D_MODEL), lambda l, j: (0, 0)),
            pl.BlockSpec((None, D_MODEL, tn), lambda l, j: (l, 0, j)),
            pl.BlockSpec((None, 1, tn), lambda l, j: (l, 0, j)),
        ],
        out_specs=pl.BlockSpec((None, N_COND, tn), lambda l, j: (l, 0, j)),
        out_shape=jax.ShapeDtypeStruct((DEPTH, N_COND, 6 * D_MODEL), F32),
        compiler_params=_cparams("parallel", "parallel"),
        name="modulation",
    )(cond, w_ada, b_ada.reshape(DEPTH, 1, 6 * D_MODEL))


def _prenorm_kernel(xp_ref, xs_ref, mod_ref, g_ref, x_ref, h_ref):
    x = jnp.where(pl.program_id(0) < T_CTX // TM, xp_ref[...], xs_ref[...])
    x_ref[...] = x
    h_ref[...] = _mod_norm(x, g_ref[...], mod_ref[...], 0).astype(BF16)


def _prenorm(xp, xs, mod_l, g):
    n_ctx = T_CTX // TM
    return pl.pallas_call(
        _prenorm_kernel,
        grid=(T_ALL // TM,),
        in_specs=[
            pl.BlockSpec((TM, D_MODEL), lambda i: (jnp.minimum(i, n_ctx - 1), 0)),
            pl.BlockSpec((TM, D_MODEL), lambda i: (jnp.maximum(i - n_ctx, 0), 0)),
            pl.BlockSpec((None, 6, D_MODEL), lambda i: (_cond_row(i, TM), 0, 0)),
            pl.BlockSpec((1, D_MODEL), lambda i: (0, 0)),
        ],
        out_specs=[pl.BlockSpec((TM, D_MODEL), lambda i: (i, 0))] * 2,
        out_shape=[jax.ShapeDtypeStruct((T_ALL, D_MODEL), F32), jax.ShapeDtypeStruct((T_ALL, D_MODEL), BF16)],
        compiler_params=_cparams("parallel"),
        name="prenorm",
    )(xp, xs, mod_l, g)


def _proj_kernel(h_ref, *refs):
    w_refs, (o_ref, wb_ref) = refs[:-2], refs[-2:]

    @pl.when(pl.program_id(1) == 0)
    def _():
        col = 0
        for w_ref in w_refs:
            n = w_ref.shape[0]
            wb_ref[:, col:col + n] = w_ref[...].T.astype(BF16)
            col += n

    o_ref[...] = _mm(h_ref[...], wb_ref[...]).astype(o_ref.dtype)


def _proj(h, w_t, layer, pieces, n_tiles, out_dtype, name):
    tn = sum(n for _, n in pieces)

    def w_spec(c0, n):
        return pl.BlockSpec((pl.Squeezed(), pl.Element(n), pl.Element(D_MODEL)),
                            lambda j, i: (layer, pl.multiple_of(c0 + j * tn, 8), 0))

    return pl.pallas_call(
        _proj_kernel,
        grid=(n_tiles, T_ALL // TM),
        in_specs=[pl.BlockSpec((TM, D_MODEL), lambda j, i: (i, 0))] + [w_spec(c0, n) for c0, n in pieces],
        out_specs=pl.BlockSpec((TM, tn), lambda j, i: (i, j)),
        out_shape=jax.ShapeDtypeStruct((T_ALL, n_tiles * tn), out_dtype),
        scratch_shapes=[pltpu.VMEM((D_MODEL, tn), BF16)],
        compiler_params=_cparams("parallel", "arbitrary"),
        name=name,
    )(h, *([w_t] * len(pieces)))


GLA_GROUP = 256


def _log_sigmoid(z):
    return jnp.minimum(z, 0.0) - jnp.log(1.0 + jnp.exp(-jnp.abs(z)))


def _split3(x):
    hi = x.astype(BF16)
    r1 = x - hi.astype(F32)
    mid = r1.astype(BF16)
    lo = (r1 - mid.astype(F32)).astype(BF16)
    return hi, mid, lo


def _gla_keep(d):
    ri = lax.broadcasted_iota(jnp.int32, (GLA_GROUP, GLA_GROUP), 0)
    ci = lax.broadcasted_iota(jnp.int32, (GLA_GROUP, GLA_GROUP), 1)
    if d == 0:
        return (ci <= ri) & (ci >= (ri & ~(GLA_CHUNK - 1)))
    return (ci >= ri) & (ci <= (ri | (GLA_CHUNK - 1)))


def _gla_group(q, k, v, lr, wg_hi, wg_lo, bg, states, d, keep):
    C = GLA_CHUNK
    nc = GLA_GROUP // C
    width = q.shape[-1]
    lr_hi = lr.astype(BF16)
    lr_lo = (lr - lr_hi.astype(F32)).astype(BF16)
    z = _mm(lr_hi, wg_hi) + _mm(lr_lo, wg_hi) + _mm(lr_hi, wg_lo) + bg
    la = _log_sigmoid(z) / GLA_TAU
    tri = jnp.where(keep, 1.0, 0.0).astype(BF16)
    la_hi, la_mid, la_lo = _split3(la)
    b = _mm(tri, la_hi) + _mm(tri, la_mid) + _mm(tri, la_lo)
    edge = C - 1 if d == 0 else 0
    b_last = [b[c * C + edge:c * C + edge + 1, :] for c in range(nc)]
    bl = jnp.concatenate([jnp.broadcast_to(t, (C, width)) for t in b_last], axis=0)
    q_in = (q * (GLA_DK ** -0.5) * jnp.exp(b)).astype(BF16)
    k_in = (k * jnp.exp(-b)).astype(BF16)
    k_end = (k * jnp.exp(bl - b)).astype(BF16)
    vb = v.astype(BF16)
    dec = [jnp.exp(t) for t in b_last]
    order = range(nc) if d == 0 else range(nc - 1, -1, -1)
    outs, new_states = [], []
    for h, st in enumerate(states):
        cs = slice(h * GLA_DK, (h + 1) * GLA_DK)
        att = jnp.where(keep, _dot_t(q_in[:, cs], k_in[:, cs]), 0.0)
        o = _mm(att.astype(BF16), vb[:, cs])
        pieces = [None] * nc
        for c in order:
            rs = slice(c * C, (c + 1) * C)
            pieces[c] = o[rs] + _dot_t(q_in[rs, cs], st.astype(BF16))
            st = st * dec[c][:, cs] + _dot_ta(vb[rs, cs], k_end[rs, cs])
        outs.append(jnp.concatenate(pieces, axis=0))
        new_states.append(st)
    return outs, new_states


def _gla_finish(o, r, ng):
    o = o * lax.rsqrt(jnp.mean(o * o, axis=-1, keepdims=True) + EPS)
    return o * ng * (r * jax.nn.sigmoid(r))


def _gla_ctx_kernel(q_ref, k_ref, v_ref, r_ref, lr_ref, wgh_ref, wgl_ref, bg_ref, ng_ref, *rest):
    y_ref, s_ref = rest[-2:]
    q, k, v = q_ref[...].astype(F32), k_ref[...].astype(F32), v_ref[...].astype(F32)
    lr = lr_ref[...]
    zero = jnp.zeros((GLA_DV, GLA_DK), F32)
    o_dir = []
    for d in range(2):
        outs, sts = _gla_group(q, k, v, lr, wgh_ref[d], wgl_ref[d], bg_ref[d], [zero] * GLA_HEADS, d,
                               _gla_keep(d))
        o_dir.append(outs)
        for h in range(GLA_HEADS):
            s_ref[d, h] = sts[h].T
    for h in range(GLA_HEADS):
        cs = slice(h * GLA_DV, (h + 1) * GLA_DV)
        y = _gla_finish(o_dir[0][h] + o_dir[1][h], r_ref[:, cs].astype(F32), ng_ref[:, cs])
        y_ref[:, cs] = y.astype(y_ref.dtype)


def _gla_lat_kernel(q_ref, k_ref, v_ref, r_ref, lr_ref, wgh_ref, wgl_ref, bg_ref, ng_ref, s0_ref, yin_ref,
                    y_ref, of_ref, ob_ref):
    del yin_ref
    n_groups = DEC_SEQ // GLA_GROUP
    keep_f, keep_b = _gla_keep(0), _gla_keep(1)

    def load(rows):
        return (q_ref[rows, :].astype(F32), k_ref[rows, :].astype(F32), v_ref[rows, :].astype(F32),
                lr_ref[rows, :])

    def body(g, carry):
        st_f, st_b = carry
        rf = pl.ds(pl.multiple_of(g * GLA_GROUP, GLA_GROUP), GLA_GROUP)
        rb = pl.ds(pl.multiple_of((n_groups - 1 - g) * GLA_GROUP, GLA_GROUP), GLA_GROUP)
        outs, st_f = _gla_group(*load(rf), wgh_ref[0], wgl_ref[0], bg_ref[0], list(st_f), 0, keep_f)
        of_ref[rf, :] = jnp.concatenate(outs, axis=-1)
        outs, st_b = _gla_group(*load(rb), wgh_ref[1], wgl_ref[1], bg_ref[1], list(st_b), 1, keep_b)
        ob_ref[rb, :] = jnp.concatenate(outs, axis=-1)
        return tuple(st_f), tuple(st_b)

    init = tuple(tuple(s0_ref[d, h].T for h in range(GLA_HEADS)) for d in range(2))
    lax.fori_loop(0, n_groups, body, init)
    for h in range(GLA_HEADS):
        cs = slice(h * GLA_DV, (h + 1) * GLA_DV)
        y = _gla_finish(of_ref[:, cs] + ob_ref[:, cs], r_ref[:, cs].astype(F32), ng_ref[:, cs])
        y_ref[:, cs] = y.astype(y_ref.dtype)


def _gla_ctx(pa, ps, wg_hi, wg_lo, bg, ng, states_prev, layer):
    cb = PA_GLA // GLA_W
    const = lambda shape: pl.BlockSpec(shape, lambda s: (0,) * len(shape))
    in_specs = [pl.BlockSpec((SEQ, GLA_W), lambda s, j=j: (s, cb + j)) for j in range(4)] + [
        pl.BlockSpec((SEQ, LANES), lambda s: (s, PS_LR // LANES)),
        const((2, LANES, GLA_W)), const((2, LANES, GLA_W)), const((2, 1, GLA_W)), const((1, GLA_W))]
    args = [pa, pa, pa, pa, ps, wg_hi, wg_lo, bg, ng]
    aliases = {}
    if states_prev is not None:
        in_specs.append(pl.BlockSpec(memory_space=pl.ANY))
        args.append(states_prev)
        aliases = {len(args) - 1: 1}
    return pl.pallas_call(
        _gla_ctx_kernel,
        grid=(BATCH,),
        in_specs=in_specs,
        out_specs=[pl.BlockSpec((SEQ, GLA_W), lambda s: (s, 0)),
                   pl.BlockSpec((None, None, 2, GLA_HEADS, GLA_DK, GLA_DV), lambda s: (s, layer, 0, 0, 0, 0))],
        out_shape=[jax.ShapeDtypeStruct((T_ALL, GLA_W), BF16),
                   jax.ShapeDtypeStruct((BATCH, DEPTH, 2, GLA_HEADS, GLA_DK, GLA_DV), F32)],
        input_output_aliases=aliases,
        compiler_params=_cparams("parallel"),
        name="gla_ctx",
    )(*args)


def _gla_lat(pa, ps, wg_hi, wg_lo, bg, ng, s0, y):
    rb0 = T_CTX // DEC_SEQ
    cb = PA_GLA // GLA_W
    const = lambda shape: pl.BlockSpec(shape, lambda s: (0,) * len(shape))
    return pl.pallas_call(
        _gla_lat_kernel,
        grid=(DEC_BATCH,),
        in_specs=[pl.BlockSpec((DEC_SEQ, GLA_W), lambda s, j=j: (rb0 + s, cb + j)) for j in range(4)] + [
            pl.BlockSpec((DEC_SEQ, LANES), lambda s: (rb0 + s, PS_LR // LANES)),
            const((2, LANES, GLA_W)), const((2, LANES, GLA_W)), const((2, 1, GLA_W)), const((1, GLA_W)),
            pl.BlockSpec((None, 2, GLA_HEADS, GLA_DK, GLA_DV), lambda s: (s, 0, 0, 0, 0)),
            pl.BlockSpec(memory_space=pl.ANY)],
        out_specs=pl.BlockSpec((DEC_SEQ, GLA_W), lambda s: (rb0 + s, 0)),
        out_shape=jax.ShapeDtypeStruct((T_ALL, GLA_W), BF16),
        scratch_shapes=[pltpu.VMEM((DEC_SEQ, GLA_W), F32), pltpu.VMEM((DEC_SEQ, GLA_W), F32)],
        input_output_aliases={10: 0},
        compiler_params=_cparams("parallel"),
        name="gla_lat",
    )(pa, pa, pa, pa, ps, wg_hi, wg_lo, bg, ng, s0, y)


def _attend(q, sink, k_parts, v_parts, masks):
    scores = []
    for k, mask in zip(k_parts, masks):
        s = _dot_t(q, k) * (HEAD_DIM ** -0.5)
        if mask is not None:
            s = jnp.where(mask, s, NEG_INF)
        scores.append(s)
    m = jnp.broadcast_to(sink, (q.shape[0], 1)).astype(F32)
    for s in scores:
        m = jnp.maximum(m, jnp.max(s, axis=-1, keepdims=True))
    den = jnp.exp(sink - m)
    o = None
    for s, v in zip(scores, v_parts):
        e = jnp.exp(s - m)
        den = den + jnp.sum(e, axis=-1, keepdims=True)
        pv = _mm(e.astype(BF16), v)
        o = pv if o is None else o + pv
    return o / den


def _head(a, j):
    return a[:, j * HEAD_DIM:(j + 1) * HEAD_DIM]


ATT_ROWS_CTX = 128


def _attn_ctx_kernel(sink_ref, q_ref, kv_ref, *rest):
    o_ref = rest[-1]
    q = q_ref[...]
    kv_all = kv_ref[...]
    outs = []
    for h in range(ATT_HEADS):
        kv = h // ATT_GROUP
        k = _head(kv_all, kv).astype(BF16)
        v = _head(kv_all, ATT_KV_HEADS + kv).astype(BF16)
        qh = _head(q, h).astype(BF16)
        blocks = [_attend(qh[r:r + ATT_ROWS_CTX], sink_ref[h], [k], [v], [None])
                  for r in range(0, SEQ, ATT_ROWS_CTX)]
        outs.append(jnp.concatenate(blocks, axis=0))
    o_ref[...] = jnp.concatenate(outs, axis=-1).astype(o_ref.dtype)


def _attn_ctx(ps, sink):
    return pl.pallas_call(
        _attn_ctx_kernel,
        grid=(BATCH,),
        in_specs=[
            pl.BlockSpec(memory_space=pltpu.SMEM),
            pl.BlockSpec((SEQ, ATT_W), lambda s: (s, 0)),
            pl.BlockSpec((SEQ, 2 * LANES), lambda s: (s, PS_KV // (2 * LANES))),
        ],
        out_specs=pl.BlockSpec((SEQ, ATT_W), lambda s: (s, 0)),
        out_shape=jax.ShapeDtypeStruct((T_ALL, ATT_W), BF16),
        compiler_params=_cparams("parallel"),
        name="attn_ctx",
    )(sink, ps, ps)


def _rope(x, cos, sin_signed):
    lane = lax.broadcasted_iota(jnp.int32, x.shape, 1)
    partner = jnp.where((lane & 31) < 16, pltpu.roll(x, LANES - 16, 1), pltpu.roll(x, 16, 1))
    return x * cos + partner * sin_signed


def _attn_lat_kernel(sink_ref, q_ref, kvp_ref, kvc_ref, kvn_ref, kctx_ref, vctx_ref, cos_ref, sin_ref, yin_ref,
                     o_ref):
    del yin_ref
    n = pl.program_id(1)
    n_blk = DEC_SEQ // BLOCK
    start = n * BLOCK

    def table(ref, blk):
        blk = jnp.clip(blk, 0, n_blk - 1)
        return ref[pl.ds(pl.multiple_of(blk * BLOCK, BLOCK), BLOCK), :]

    k_loc, v_loc = [], []
    for off, ref in ((-1, kvp_ref), (0, kvc_ref), (1, kvn_ref)):
        k_loc.append(_rope(ref[:, :LANES], table(cos_ref, n + off), table(sin_ref, n + off)))
        v_loc.append(ref[:, LANES:])
    k_loc = jnp.concatenate(k_loc, axis=0)
    v_loc = jnp.concatenate(v_loc, axis=0)
    k_ctx = kctx_ref[...]
    v_ctx = vctx_ref[...]

    span = 3 * BLOCK
    stacked = (ATT_GROUP * BLOCK, span)
    qpos = start + (lax.broadcasted_iota(jnp.int32, stacked, 0) & (BLOCK - 1))
    kpos = start - WINDOW + lax.broadcasted_iota(jnp.int32, stacked, 1)
    valid = (jnp.abs(qpos - kpos) <= WINDOW) & (kpos >= 0) & (kpos < DEC_SEQ)

    cos_q = table(cos_ref, n)
    sin_q = table(sin_ref, n)
    q_heads = []
    for pair in range(ATT_HEADS // 2):
        qr = _rope(q_ref[:, pair * LANES:(pair + 1) * LANES], cos_q, sin_q)
        q_heads += [_head(qr, 0).astype(BF16), _head(qr, 1).astype(BF16)]
    outs = []
    for kv in range(ATT_KV_HEADS):
        heads = range(kv * ATT_GROUP, (kv + 1) * ATT_GROUP)
        q = jnp.concatenate([q_heads[h] for h in heads], axis=0)
        sink = jnp.concatenate([jnp.full((BLOCK, 1), sink_ref[h], F32) for h in heads], axis=0)
        o = _attend(q, sink, [_head(k_ctx, kv).astype(BF16), _head(k_loc, kv).astype(BF16)],
                    [_head(v_ctx, kv).astype(BF16), _head(v_loc, kv).astype(BF16)], [None, valid])
        outs += [o[g * BLOCK:(g + 1) * BLOCK] for g in range(ATT_GROUP)]
    o_ref[...] = jnp.concatenate(outs, axis=-1).astype(o_ref.dtype)


def _attn_lat(ps, sink, k_ctx, v_ctx, cos_t, sin_t, y):
    n_blk = DEC_SEQ // BLOCK
    rb0 = T_CTX // BLOCK

    def kv_spec(off):
        return pl.BlockSpec(
            (BLOCK, 2 * LANES),
            lambda b, n: (rb0 + b * n_blk + jnp.clip(n + off, 0, n_blk - 1), PS_KV // (2 * LANES)))

    return pl.pallas_call(
        _attn_lat_kernel,
        grid=(DEC_BATCH, n_blk),
        in_specs=[
            pl.BlockSpec(memory_space=pltpu.SMEM),
            pl.BlockSpec((BLOCK, ATT_W), lambda b, n: (rb0 + b * n_blk + n, 0)),
            kv_spec(-1), kv_spec(0), kv_spec(1),
            pl.BlockSpec((None, PAST_LEN, LANES), lambda b, n: (b, 0, 0)),
            pl.BlockSpec((None, PAST_LEN, LANES), lambda b, n: (b, 0, 0)),
            pl.BlockSpec((DEC_SEQ, LANES), lambda b, n: (0, 0)),
            pl.BlockSpec((DEC_SEQ, LANES), lambda b, n: (0, 0)),
            pl.BlockSpec(memory_space=pl.ANY),
        ],
        out_specs=pl.BlockSpec((BLOCK, ATT_W), lambda b, n: (rb0 + b * n_blk + n, 0)),
        out_shape=jax.ShapeDtypeStruct((T_ALL, ATT_W), BF16),
        input_output_aliases={9: 0},
        compiler_params=_cparams("parallel", "parallel"),
        name="attn_lat",
    )(sink, ps, ps, ps, ps, k_ctx, v_ctx, cos_t, sin_t, y)


def _rope_tables():
    pos = np.arange(DEC_SEQ)
    n_freq = HEAD_DIM // 4
    inv = jnp.asarray(ROPE_THETA, F32) ** (-jnp.arange(n_freq, dtype=F32) / n_freq)
    row = jnp.asarray(pos // GRID_W, F32)
    colp = jnp.asarray(pos % GRID_W, F32)
    ang_r = row[:, None] * inv[None, :]
    ang_c = colp[:, None] * inv[None, :]
    cos = jnp.concatenate([jnp.cos(ang_r)] * 2 + [jnp.cos(ang_c)] * 2, axis=-1)
    sin = jnp.concatenate([-jnp.sin(ang_r), jnp.sin(ang_r), -jnp.sin(ang_c), jnp.sin(ang_c)], axis=-1)
    return jnp.tile(cos, (1, 2)), jnp.tile(sin, (1, 2))


def _split_dot(a, w_hi, w_lo):
    a_hi = a.astype(BF16)
    a_lo = (a - a_hi.astype(F32)).astype(BF16)
    return _mm(a_hi, w_hi) + _mm(a_lo, w_hi) + _mm(a_hi, w_lo)


def _route(logits):
    lane_i = lax.broadcasted_iota(jnp.int32, logits.shape, 1)
    lane = lane_i.astype(F32)
    big = jnp.float32(1 << 20)
    is_g = lane_i < N_EXPERT_GROUPS
    lg = jnp.where(is_g, logits, -jnp.inf)
    m_g = jnp.max(lg, axis=-1, keepdims=True)
    grp = jnp.min(jnp.where(lg == m_g, lane, big), axis=-1, keepdims=True)
    z_g = jnp.sum(jnp.where(is_g, jnp.exp(lg - m_g), 0.0), axis=-1, keepdims=True)
    p_grp = 1.0 / z_g

    e_idx = lane_i - ROUTE_E0
    e_grp = (e_idx >> 2).astype(F32)
    sel = (e_idx >= 0) & (e_idx < N_EXPERTS) & (e_grp == grp)
    le = jnp.where(sel, logits, -jnp.inf)
    m_e = jnp.max(le, axis=-1, keepdims=True)
    ex = jnp.where(sel, jnp.exp(le - m_e), 0.0)
    pe = ex / jnp.sum(ex, axis=-1, keepdims=True)
    pe = jnp.where(sel, pe, -1.0)
    v1 = jnp.max(pe, axis=-1, keepdims=True)
    i1 = jnp.min(jnp.where(pe == v1, lane, big), axis=-1, keepdims=True)
    pe2 = jnp.where(lane == i1, -1.0, pe)
    v2 = jnp.max(pe2, axis=-1, keepdims=True)
    i2 = jnp.min(jnp.where(pe2 == v2, lane, big), axis=-1, keepdims=True)
    tot = v1 + v2
    return (jnp.where(lane == i1, p_grp * (v1 / tot), 0.0)
            + jnp.where(lane == i2, p_grp * (v2 / tot), 0.0)
            + jnp.where(lane_i == LANES - 1, grp, 0.0))


HALO = 16


def _merge_kernel(gate_ref, conv_ref, cprev_ref, cnext_ref, yb_ref, yc_ref, x_ref, mod_ref, cw_ref,
                  wb_ref, wo_ref, g2_ref, wr_hi_ref, wr_lo_ref, br_ref,
                  x1_ref, h2_ref, rg_ref, wbb_ref, wob_ref):
    tm = TM_MERGE
    i = pl.program_id(0)

    @pl.when(i == 0)
    def _():
        wbb_ref[...] = wb_ref[...].astype(BF16)
        wob_ref[...] = wo_ref[...].astype(BF16)

    def gated(ref):
        return ref[:, CONV_W:2 * CONV_W].astype(F32) * ref[:, 2 * CONV_W:3 * CONV_W].astype(F32)

    a_b = conv_ref[:, 0:CONV_W].astype(F32)
    u = gated(conv_ref)
    u_before = gated(cprev_ref)[HALO - 1:HALO]
    u_after = gated(cnext_ref)[0:1]
    r = lax.broadcasted_iota(jnp.int32, (tm, 1), 0)
    g_row = i * tm + r
    seq_mask = jnp.where(g_row < T_CTX, SEQ - 1, DEC_SEQ - 1)
    first = (g_row & seq_mask) == 0
    last = ((g_row + 1) & seq_mask) == 0
    u_prev = jnp.where(r == 0, u_before, pltpu.roll(u, 1, 0))
    u_next = jnp.where(r == tm - 1, u_after, pltpu.roll(u, tm - 1, 0))
    u_prev = jnp.where(first, 0.0, u_prev)
    u_next = jnp.where(last, 0.0, u_next)
    y_a = a_b * (u_prev * cw_ref[0:1, :] + u * cw_ref[1:2, :] + u_next * cw_ref[2:3, :])

    def branch(j, y):
        m = gate_ref[:, j * D_MODEL:(j + 1) * D_MODEL].astype(F32)
        return jax.nn.sigmoid(m) * _mm(y, wbb_ref[j])

    z = branch(0, y_a.astype(BF16)) + branch(1, yb_ref[...]) + branch(2, yc_ref[...])
    x1 = x_ref[...] + mod_ref[2:3, :] * _mm(z.astype(BF16), wob_ref[...])
    x1_ref[...] = x1
    h2 = _mod_norm(x1, g2_ref[...], mod_ref[...], 3)
    h2_ref[...] = h2.astype(BF16)
    logits = _split_dot(h2, wr_hi_ref[...], wr_lo_ref[...]) + br_ref[...]
    rg_ref[...] = _route(logits)


def _merge(pg, pa, y_b, y_c, x, mod_l, conv_w, layer, wb, wo, g2, wr_hi, wr_lo, br):
    tm = TM_MERGE
    n_tiles = T_ALL // tm
    hb = tm // HALO
    const = lambda shape: pl.BlockSpec(shape, lambda i: (0,) * len(shape))
    return pl.pallas_call(
        _merge_kernel,
        grid=(n_tiles,),
        in_specs=[
            pl.BlockSpec((tm, PG_W), lambda i: (i, 0)),
            pl.BlockSpec((tm, 3 * CONV_W), lambda i: (i, 0)),
            pl.BlockSpec((HALO, 3 * CONV_W), lambda i: (jnp.maximum(i * hb - 1, 0), 0)),
            pl.BlockSpec((HALO, 3 * CONV_W), lambda i: (jnp.minimum((i + 1) * hb, n_tiles * hb - 1), 0)),
            pl.BlockSpec((tm, GLA_W), lambda i: (i, 0)),
            pl.BlockSpec((tm, ATT_W), lambda i: (i, 0)),
            pl.BlockSpec((tm, D_MODEL), lambda i: (i, 0)),
            pl.BlockSpec((None, 6, D_MODEL), lambda i: (_cond_row(i, tm), 0, 0)),
            const((CONV_K, CONV_W)),
            pl.BlockSpec((None, 3, 512, D_MODEL), lambda i: (layer, 0, 0, 0)),
            pl.BlockSpec((None, D_MODEL, D_MODEL), lambda i: (layer, 0, 0)),
            const((1, D_MODEL)),
            const((D_MODEL, LANES)),
            const((D_MODEL, LANES)),
            const((1, LANES)),
        ],
        out_specs=[
            pl.BlockSpec((tm, D_MODEL), lambda i: (i, 0)),
            pl.BlockSpec((tm, D_MODEL), lambda i: (i, 0)),
            pl.BlockSpec((tm, LANES), lambda i: (i, 0)),
        ],
        out_shape=[
            jax.ShapeDtypeStruct((T_ALL, D_MODEL), F32),
            jax.ShapeDtypeStruct((T_ALL, D_MODEL), BF16),
            jax.ShapeDtypeStruct((T_ALL, LANES), F32),
        ],
        scratch_shapes=[pltpu.VMEM((3, 512, D_MODEL), BF16), pltpu.VMEM((D_MODEL, D_MODEL), BF16)],
        compiler_params=_cparams("arbitrary"),
        name="merge",
    )(pg, pa, pa, pa, y_b, y_c, x, mod_l, conv_w, wb, wo, g2, wr_hi, wr_lo, br)


SUB = 128
MOE_SLOTS = TM + N_EXPERT_GROUPS * SUB


def _dispatch_tables(rg):
    n_tiles = T_ALL // TM
    grp = rg[:, LANES - 1].astype(jnp.int32).reshape(n_tiles, TM)
    onehot = (grp[..., None] == jnp.arange(N_EXPERT_GROUPS, dtype=jnp.int32)).astype(jnp.int32)
    cnt = onehot.sum(axis=1)
    nblk = (cnt + SUB - 1) // SUB
    bstart = jnp.cumsum(nblk, axis=1) - nblk
    rank = jnp.cumsum(onehot, axis=1) - onehot
    slot = ((bstart[:, None, :] * SUB + rank) * onehot).sum(axis=-1)
    return (bstart.reshape(-1), nblk.reshape(-1), slot.reshape(n_tiles, 1, TM), slot.reshape(T_ALL, 1))


def _moe_kernel(bstart_ref, nblk_ref, h_ref, rg_ref, srow_ref, scol_ref, x1_ref, mod_ref, wg_ref, wu_ref,
                wd_ref, g_ref, *rest, final, tile0):
    if final:
        y_ref, hs_ref, gs_ref, os_ref = rest
    else:
        modn_ref, x2_ref, hn_ref, hs_ref, gs_ref, os_ref = rest
    i = pl.program_id(0)
    grp = pl.program_id(1)

    @pl.when(grp == 0)
    def _():
        slot_of_token = srow_ref[...]
        pm = jnp.where(lax.broadcasted_iota(jnp.int32, (MOE_SLOTS, TM), 0) == slot_of_token, 1.0, 0.0)
        pm = pm.astype(BF16)
        hs_ref[...] = _mm(pm, h_ref[...]).astype(BF16)
        r_hi, r_mid, r_lo = _split3(rg_ref[...])
        gs_ref[...] = _mm(pm, r_hi) + _mm(pm, r_mid) + _mm(pm, r_lo)
        os_ref[...] = jnp.zeros_like(os_ref)

    k = (tile0 + i) * N_EXPERT_GROUPS + grp
    first = bstart_ref[k]
    n_blocks = nblk_ref[k]

    def experts(block0, n_sub):
        n_rows = n_sub * SUB
        rows = pl.ds(pl.multiple_of(block0 * SUB, SUB), n_rows)
        lane = lax.broadcasted_iota(jnp.int32, (n_rows, LANES), 1)
        x = hs_ref[rows, :]
        gates = gs_ref[rows, :]
        acc = jnp.zeros((n_rows, D_MODEL), F32)
        for e in range(EXPERTS_PER_GROUP):
            ge = _mm(x, wg_ref[e])
            ue = _mm(x, wu_ref[e])
            w = jnp.sum(jnp.where(lane == ROUTE_E0 + grp * EXPERTS_PER_GROUP + e, gates, 0.0),
                        axis=-1, keepdims=True)
            hid = (ge * jax.nn.sigmoid(ge)) * ue * w
            acc = acc + _mm(hid.astype(BF16), wd_ref[e])
        os_ref[rows, :] = acc.astype(BF16)

    def pair(p, carry):
        experts(first + 2 * p, 2)
        return carry

    lax.fori_loop(0, n_blocks // 2, pair, 0)

    @pl.when(n_blocks % 2 == 1)
    def _():
        experts(first + n_blocks - 1, 1)

    @pl.when(grp == N_EXPERT_GROUPS - 1)
    def _():
        slot_col = scol_ref[...]
        pt = jnp.where(lax.broadcasted_iota(jnp.int32, (TM, MOE_SLOTS), 1) == slot_col, 1.0, 0.0)
        moe = _mm(pt.astype(BF16), os_ref[...])
        x2 = x1_ref[...] + mod_ref[5:6, :] * moe
        if final:
            y_ref[...] = _rms(x2) * g_ref[...]
        else:
            x2_ref[...] = x2
            hn_ref[...] = _mod_norm(x2, g_ref[...], modn_ref[...], 0).astype(BF16)


def _moe(tables, h2, rg, x1, mod_l, layer, w_eg, w_eu, w_ed, g, mod_next, tile0, n_tiles):
    bstart, nblk, slot_row, slot_col = tables
    final = mod_next is None
    row = lambda i, e, *_: (tile0 + i, 0)
    mod_spec = lambda: pl.BlockSpec((None, 6, D_MODEL), lambda i, e, *_: (_cond_row(tile0 + i, TM), 0, 0))
    group = lambda i, e, *_: (layer, e, 0, 0)
    in_specs = [
        pl.BlockSpec((TM, D_MODEL), row),
        pl.BlockSpec((TM, LANES), row),
        pl.BlockSpec((None, 1, TM), lambda i, e, *_: (tile0 + i, 0, 0)),
        pl.BlockSpec((TM, 1), row),
        pl.BlockSpec((TM, D_MODEL), row),
        mod_spec(),
        pl.BlockSpec((None, EXPERTS_PER_GROUP, D_MODEL, D_EXPERT), group),
        pl.BlockSpec((None, EXPERTS_PER_GROUP, D_MODEL, D_EXPERT), group),
        pl.BlockSpec((None, EXPERTS_PER_GROUP, D_EXPERT, D_MODEL), group),
        pl.BlockSpec((1, D_MODEL), lambda i, e, *_: (0, 0)),
    ]
    args = [h2, rg, slot_row, slot_col, x1, mod_l, w_eg, w_eu, w_ed, g]
    if final:
        out_specs = pl.BlockSpec((TM, D_MODEL), lambda i, e, *_: (i, 0))
        out_shape = jax.ShapeDtypeStruct((n_tiles * TM, D_MODEL), F32)
    else:
        in_specs.append(mod_spec())
        args.append(mod_next)
        out_specs = [pl.BlockSpec((TM, D_MODEL), row)] * 2
        out_shape = [jax.ShapeDtypeStruct((T_ALL, D_MODEL), F32), jax.ShapeDtypeStruct((T_ALL, D_MODEL), BF16)]
    return pl.pallas_call(
        functools.partial(_moe_kernel, final=final, tile0=tile0),
        grid_spec=pltpu.PrefetchScalarGridSpec(
            num_scalar_prefetch=2,
            grid=(n_tiles, N_EXPERT_GROUPS),
            in_specs=in_specs,
            out_specs=out_specs,
            scratch_shapes=[pltpu.VMEM((MOE_SLOTS, D_MODEL), BF16), pltpu.VMEM((MOE_SLOTS, LANES), F32),
                            pltpu.VMEM((MOE_SLOTS, D_MODEL), BF16)],
        ),
        out_shape=out_shape,
        compiler_params=_cparams("parallel", "arbitrary"),
        name="moe_final" if final else "moe",
    )(bstart, nblk, *args)


def _cache_kernel(*refs):
    ps_refs, (k_ref, v_ref) = refs[:DEPTH], refs[DEPTH:]
    n_seq = TM // SEQ
    for l, ps_ref in enumerate(ps_refs):
        k_ref[:, l] = ps_ref[:, :LANES].reshape(n_seq, SEQ, LANES)
        v_ref[:, l] = ps_ref[:, LANES:].reshape(n_seq, SEQ, LANES)


def _cache(ps_layers):
    n_seq = TM // SEQ
    out = pl.BlockSpec((n_seq, DEPTH, SEQ, LANES), lambda i: (i, 0, 0, 0))
    return pl.pallas_call(
        _cache_kernel,
        grid=(T_CTX // TM,),
        in_specs=[pl.BlockSpec((TM, 2 * LANES), lambda i: (i, PS_KV // (2 * LANES)))] * DEPTH,
        out_specs=[out, out],
        out_shape=[jax.ShapeDtypeStruct((BATCH, DEPTH, SEQ, LANES), F32)] * 2,
        compiler_params=_cparams("parallel"),
        name="cache",
    )(*ps_layers)


def _prep_gla_gate(w_gate):
    w = jnp.zeros((2, LANES, GLA_W), F32)
    w = w.at[0, 0:GLA_RANK].set(w_gate[0]).at[1, GLA_RANK:2 * GLA_RANK].set(w_gate[1])
    hi = w.astype(BF16)
    return hi, (w - hi.astype(F32)).astype(BF16)


def _prep_router(w_rg, b_rg, w_re, b_re):
    w = jnp.zeros((D_MODEL, LANES), F32)
    w = w.at[:, :N_EXPERT_GROUPS].set(w_rg).at[:, ROUTE_E0:ROUTE_E0 + N_EXPERTS].set(w_re)
    b = jnp.zeros((1, LANES), F32)
    b = b.at[0, :N_EXPERT_GROUPS].set(b_rg).at[0, ROUTE_E0:ROUTE_E0 + N_EXPERTS].set(b_re)
    w_hi = w.astype(BF16)
    w_lo = (w - w_hi.astype(F32)).astype(BF16)
    return w_hi, w_lo, b


def kernel(x_prompt, x_sample, state_gla, cache_k, cache_v, c, c_ctx, w_ada, b_ada, norm1_g, norm2_g, w_in,
           conv_w, gla_w_gate, gla_b_gate, gla_norm_g, attn_sink, w_branch, w_out, w_route_group,
           b_route_group, w_route_expert, b_route_expert, w_exp_gate, w_exp_up, w_exp_down, final_norm_g):
    cond = jnp.zeros((N_COND, D_MODEL), F32).at[0].set(c_ctx).at[1:1 + DEC_BATCH].set(c)
    mod = _modulation(cond, w_ada, b_ada).reshape(DEPTH, N_COND, 6, D_MODEL)
    cos_t, sin_t = _rope_tables()
    row = lambda v: v.reshape(1, -1)

    x, h = _prenorm(x_prompt.reshape(T_CTX, D_MODEL), x_sample.reshape(T_LAT, D_MODEL), mod[0], row(norm1_g[0]))
    states = None
    ps_layers = []
    y_prompt = y_sample = None
    w_t = jnp.swapaxes(w_in, 1, 2)
    experts = (w_exp_gate.astype(BF16), w_exp_up.astype(BF16), w_exp_down.astype(BF16))
    for l in range(DEPTH):
        pa = _proj(h, w_t, l, [(0, PA_W // 2)], 2, BF16, "proj_a")
        ps = _proj(h, w_t, l, [(W_IN_TQ, W_IN_GATE - W_IN_TQ), (W_IN_LR, LANES)], 1, F32, "proj_s")
        pg = _proj(h, w_t, l, [(W_IN_GATE, PG_W // 2)], 2, BF16, "proj_g")
        ps_layers.append(ps)

        wg_hi, wg_lo = _prep_gla_gate(gla_w_gate[l])
        bg = gla_b_gate[l].reshape(2, 1, GLA_W)
        ng = row(gla_norm_g[l])
        y_b, states = _gla_ctx(pa, ps, wg_hi, wg_lo, bg, ng, states, l)
        y_b = _gla_lat(pa, ps, wg_hi, wg_lo, bg, ng, state_gla[:, l], y_b)
        y_c = _attn_ctx(ps, attn_sink[l])
        y_c = _attn_lat(ps, attn_sink[l], cache_k[:, l].reshape(DEC_BATCH, PAST_LEN, LANES),
                        cache_v[:, l].reshape(DEC_BATCH, PAST_LEN, LANES), cos_t, sin_t, y_c)

        wr_hi, wr_lo, br = _prep_router(w_route_group[l], b_route_group[l], w_route_expert[l], b_route_expert[l])
        x1, h2, rg = _merge(pg, pa, y_b, y_c, x, mod[l], conv_w[l], l, w_branch, w_out, row(norm2_g[l]),
                            wr_hi, wr_lo, br)
        moe_in = (_dispatch_tables(rg), h2, rg, x1, mod[l], l, *experts)
        if l + 1 < DEPTH:
            x, h = _moe(*moe_in, row(norm1_g[l + 1]), mod[l + 1], 0, T_ALL // TM)
        else:
            gf = row(final_norm_g)
            y_prompt = _moe(*moe_in, gf, None, 0, T_CTX // TM)
            y_sample = _moe(*moe_in, gf, None, T_CTX // TM, T_LAT // TM)

    new_k, new_v = _cache(ps_layers)
    kv_shape = (BATCH, DEPTH, SEQ, ATT_KV_HEADS, HEAD_DIM)
    return (y_prompt.reshape(BATCH, SEQ, D_MODEL), y_sample.reshape(DEC_BATCH, DEC_SEQ, D_MODEL), states,
            new_k.reshape(kv_shape), new_v.reshape(kv_shape))
```

```python
import functools

import jax
import jax.numpy as jnp
import numpy as np
from jax import lax
from jax.experimental import pallas as pl
from jax.experimental.pallas import tpu as pltpu

F32 = jnp.float32
BF16 = jnp.bfloat16

D_MODEL = 1024
BATCH = 32
SEQ = 256
DEPTH = 2
DEC_BATCH = 2
DEC_SEQ = 2048
PAST_LEN = 512
GRID_W = 64
EPS = 1e-6
CONV_W = 512
CONV_K = 3
GLA_HEADS = 4
GLA_DK = 128
GLA_DV = 128
GLA_W = GLA_HEADS * GLA_DV
GLA_RANK = 16
GLA_TAU = 16.0
GLA_CHUNK = 64
ATT_HEADS = 8
ATT_KV_HEADS = 2
ATT_GROUP = ATT_HEADS // ATT_KV_HEADS
HEAD_DIM = 64
ATT_W = ATT_HEADS * HEAD_DIM
WINDOW = 128
BLOCK = 128
ROPE_THETA = 10000.0
N_EXPERT_GROUPS = 4
EXPERTS_PER_GROUP = 4
N_EXPERTS = 16
D_EXPERT = 256
NEG_INF = -1e30

T_CTX = BATCH * SEQ
T_LAT = DEC_BATCH * DEC_SEQ
T_ALL = T_CTX + T_LAT
N_COND = 8
LANES = 128

W_IN_LR = 3584
W_IN_TQ = 3616
W_IN_GATE = 4384
N_IN = 7456
PA_W = 3584
PA_GLA = 1536
PS_W = 896
PS_KV = 512
PS_LR = 768
PG_W = 3 * D_MODEL
ROUTE_E0 = N_EXPERT_GROUPS

TM = 1024
TM_MERGE = 512
MERGE_ROWS = 256
VMEM_LIMIT = 56 * 1024 * 1024


def _cparams(*sem):
    return pltpu.CompilerParams(dimension_semantics=sem, vmem_limit_bytes=VMEM_LIMIT)


def _cond_row(i, tm):
    n_ctx = T_CTX // tm
    per = DEC_SEQ // tm
    return jnp.where(i < n_ctx, 0, 1 + (i - n_ctx) // per)


def _mm(a, b):
    return jnp.dot(a, b, preferred_element_type=F32)


def _dot_t(a, b):
    return lax.dot_general(a, b, (((1,), (1,)), ((), ())), preferred_element_type=F32)


def _dot_ta(a, b):
    return lax.dot_general(a, b, (((0,), (0,)), ((), ())), preferred_element_type=F32)


def _rms(x):
    return x * lax.rsqrt(jnp.mean(x * x, axis=-1, keepdims=True) + EPS)


def _mod_norm(x, g, mod, shift_row):
    return _rms(x) * g * (1.0 + mod[shift_row + 1:shift_row + 2, :]) + mod[shift_row:shift_row + 1, :]


def _mod_kernel(c_ref, w_ref, b_ref, o_ref):
    c = c_ref[...]
    s = (c * jax.nn.sigmoid(c)).astype(BF16)
    o_ref[...] = _mm(s, w_ref[...].astype(BF16)) + b_ref[...]


def _modulation(cond, w_ada, b_ada):
    tn = 1536
    return pl.pallas_call(
        _mod_kernel,
        grid=(DEPTH, 6 * D_MODEL // tn),
        in_specs=[
            pl.BlockSpec((N_COND, D_MODEL), lambda l, j: (0, 0)),
            pl.BlockSpec((None, D_MODEL, tn), lambda l, j: (l, 0, j)),
            pl.BlockSpec((None, 1, tn), lambda l, j: (l, 0, j)),
        ],
        out_specs=pl.BlockSpec((None, N_COND, tn), lambda l, j: (l, 0, j)),
        out_shape=jax.ShapeDtypeStruct((DEPTH, N_COND, 6 * D_MODEL), F32),
        compiler_params=_cparams("parallel", "parallel"),
        name="modulation",
    )(cond, w_ada, b_ada.reshape(DEPTH, 1, 6 * D_MODEL))


def _prenorm_kernel(xp_ref, xs_ref, mod_ref, g_ref, x_ref, h_ref):
    x = jnp.where(pl.program_id(0) < T_CTX // TM, xp_ref[...], xs_ref[...])
    x_ref[...] = x
    h_ref[...] = _mod_norm(x, g_ref[...], mod_ref[...], 0).astype(BF16)


def _prenorm(xp, xs, mod_l, g):
    n_ctx = T_CTX // TM
    return pl.pallas_call(
        _prenorm_kernel,
        grid=(T_ALL // TM,),
        in_specs=[
            pl.BlockSpec((TM, D_MODEL), lambda i: (jnp.minimum(i, n_ctx - 1), 0)),
            pl.BlockSpec((TM, D_MODEL), lambda i: (jnp.maximum(i - n_ctx, 0), 0)),
            pl.BlockSpec((None, 6, D_MODEL), lambda i: (_cond_row(i, TM), 0, 0)),
            pl.BlockSpec((1, D_MODEL), lambda i: (0, 0)),
        ],
        out_specs=[pl.BlockSpec((TM, D_MODEL), lambda i: (i, 0))] * 2,
        out_shape=[jax.ShapeDtypeStruct((T_ALL, D_MODEL), F32), jax.ShapeDtypeStruct((T_ALL, D_MODEL), BF16)],
        compiler_params=_cparams("parallel"),
        name="prenorm",
    )(xp, xs, mod_l, g)


def _proj_kernel(h_ref, *refs, act):
    w_refs, (o_ref, wb_ref) = refs[:-2], refs[-2:]

    @pl.when(pl.program_id(1) == 0)
    def _():
        col = 0
        for w_ref in w_refs:
            n = w_ref.shape[0]
            wb_ref[:, col:col + n] = w_ref[...].T.astype(BF16)
            col += n

    y = _mm(h_ref[...], wb_ref[...])
    if act == "sigmoid":
        y = jax.nn.sigmoid(y)
    if act == "silu_tail":
        last = pl.program_id(0) == pl.num_programs(0) - 1
        tail = y[:, -GLA_W:]
        o_ref[:, :-GLA_W] = y[:, :-GLA_W].astype(o_ref.dtype)
        o_ref[:, -GLA_W:] = jnp.where(last, tail * jax.nn.sigmoid(tail), tail).astype(o_ref.dtype)
    else:
        o_ref[...] = y.astype(o_ref.dtype)


def _proj(h, w_t, layer, pieces, n_tiles, out_dtype, name, act=None):
    tn = sum(n for _, n in pieces)

    def w_spec(c0, n):
        return pl.BlockSpec((pl.Squeezed(), pl.Element(n), pl.Element(D_MODEL)),
                            lambda j, i: (layer, pl.multiple_of(c0 + j * tn, 8), 0))

    return pl.pallas_call(
        functools.partial(_proj_kernel, act=act),
        grid=(n_tiles, T_ALL // TM),
        in_specs=[pl.BlockSpec((TM, D_MODEL), lambda j, i: (i, 0))] + [w_spec(c0, n) for c0, n in pieces],
        out_specs=pl.BlockSpec((TM, tn), lambda j, i: (i, j)),
        out_shape=jax.ShapeDtypeStruct((T_ALL, n_tiles * tn), out_dtype),
        scratch_shapes=[pltpu.VMEM((D_MODEL, tn), BF16)],
        compiler_params=_cparams("parallel", "arbitrary"),
        name=name,
    )(h, *([w_t] * len(pieces)))


GLA_GROUP = 256
GLA_CTX_SEQS = 2


def _log_sigmoid(z):
    return jnp.minimum(z, 0.0) - jnp.log(1.0 + jnp.exp(-jnp.abs(z)))


def _split3(x):
    hi = x.astype(BF16)
    r1 = x - hi.astype(F32)
    mid = r1.astype(BF16)
    lo = (r1 - mid.astype(F32)).astype(BF16)
    return hi, mid, lo


def _gla_keep(d):
    ri = lax.broadcasted_iota(jnp.int32, (GLA_GROUP, GLA_GROUP), 0)
    ci = lax.broadcasted_iota(jnp.int32, (GLA_GROUP, GLA_GROUP), 1)
    if d == 0:
        return (ci <= ri) & (ci >= (ri & ~(GLA_CHUNK - 1)))
    return (ci >= ri) & (ci <= (ri | (GLA_CHUNK - 1)))


def _gla_group(q, k, v, lr, wg_hi, wg_lo, bg, states, d, keep):
    C = GLA_CHUNK
    nc = GLA_GROUP // C
    width = q.shape[-1]
    lr_hi = lr.astype(BF16)
    lr_lo = (lr - lr_hi.astype(F32)).astype(BF16)
    z = _mm(lr_hi, wg_hi) + _mm(lr_lo, wg_hi) + _mm(lr_hi, wg_lo) + bg
    la = _log_sigmoid(z) / GLA_TAU
    tri = jnp.where(keep, 1.0, 0.0).astype(BF16)
    la_hi, la_mid, la_lo = _split3(la)
    b = _mm(tri, la_hi) + _mm(tri, la_mid) + _mm(tri, la_lo)
    edge = C - 1 if d == 0 else 0
    b_last = [b[c * C + edge:c * C + edge + 1, :] for c in range(nc)]
    bl = jnp.concatenate([jnp.broadcast_to(t, (C, width)) for t in b_last], axis=0)
    q_in = (q * (GLA_DK ** -0.5) * jnp.exp(b)).astype(BF16)
    k_in = (k * jnp.exp(-b)).astype(BF16)
    k_end = (k * jnp.exp(bl - b)).astype(BF16)
    vb = v.astype(BF16)
    dec = [jnp.exp(t) for t in b_last]
    order = range(nc) if d == 0 else range(nc - 1, -1, -1)
    outs, new_states = [], []
    for h, st in enumerate(states):
        cs = slice(h * GLA_DK, (h + 1) * GLA_DK)
        att = jnp.where(keep, _dot_t(q_in[:, cs], k_in[:, cs]), 0.0)
        o = _mm(att.astype(BF16), vb[:, cs])
        pieces = [None] * nc
        for c in order:
            rs = slice(c * C, (c + 1) * C)
            pieces[c] = o[rs] + _dot_t(q_in[rs, cs], st.astype(BF16))
            st = st * dec[c][:, cs] + _dot_ta(vb[rs, cs], k_end[rs, cs])
        outs.append(jnp.concatenate(pieces, axis=0))
        new_states.append(st)
    return outs, new_states


def _gla_finish(o, r_act, ng):
    o = o * lax.rsqrt(jnp.mean(o * o, axis=-1, keepdims=True) + EPS)
    return o * ng * r_act


def _gla_ctx_kernel(q_ref, k_ref, v_ref, r_ref, lr_ref, wgh_ref, wgl_ref, bg_ref, ng_ref, *rest):
    y_ref, s_ref = rest[-2:]
    zero = jnp.zeros((GLA_DV, GLA_DK), F32)
    keep = [_gla_keep(0), _gla_keep(1)]
    for s in range(GLA_CTX_SEQS):
        rows = slice(s * SEQ, (s + 1) * SEQ)
        q, k, v = (ref[rows, :].astype(F32) for ref in (q_ref, k_ref, v_ref))
        lr = lr_ref[rows, :]
        o_dir = []
        for d in range(2):
            outs, sts = _gla_group(q, k, v, lr, wgh_ref[d], wgl_ref[d], bg_ref[d], [zero] * GLA_HEADS, d,
                                   keep[d])
            o_dir.append(outs)
            for h in range(GLA_HEADS):
                s_ref[s, d, h] = sts[h].T
        for h in range(GLA_HEADS):
            cs = slice(h * GLA_DV, (h + 1) * GLA_DV)
            y = _gla_finish(o_dir[0][h] + o_dir[1][h], r_ref[rows, cs].astype(F32), ng_ref[:, cs])
            y_ref[rows, cs] = y.astype(y_ref.dtype)


def _gla_lat_kernel(q_ref, k_ref, v_ref, r_ref, lr_ref, wgh_ref, wgl_ref, bg_ref, ng_ref, s0_ref, yin_ref,
                    y_ref, of_ref, ob_ref):
    del yin_ref
    n_groups = DEC_SEQ // GLA_GROUP
    keep_f, keep_b = _gla_keep(0), _gla_keep(1)

    def load(rows):
        return (q_ref[rows, :].astype(F32), k_ref[rows, :].astype(F32), v_ref[rows, :].astype(F32),
                lr_ref[rows, :])

    def body(g, carry):
        st_f, st_b = carry
        rf = pl.ds(pl.multiple_of(g * GLA_GROUP, GLA_GROUP), GLA_GROUP)
        rb = pl.ds(pl.multiple_of((n_groups - 1 - g) * GLA_GROUP, GLA_GROUP), GLA_GROUP)
        outs, st_f = _gla_group(*load(rf), wgh_ref[0], wgl_ref[0], bg_ref[0], list(st_f), 0, keep_f)
        of_ref[rf, :] = jnp.concatenate(outs, axis=-1)
        outs, st_b = _gla_group(*load(rb), wgh_ref[1], wgl_ref[1], bg_ref[1], list(st_b), 1, keep_b)
        ob_ref[rb, :] = jnp.concatenate(outs, axis=-1)
        return tuple(st_f), tuple(st_b)

    init = tuple(tuple(s0_ref[d, h].T for h in range(GLA_HEADS)) for d in range(2))
    lax.fori_loop(0, n_groups, body, init)
    for h in range(GLA_HEADS):
        cs = slice(h * GLA_DV, (h + 1) * GLA_DV)
        y = _gla_finish(of_ref[:, cs] + ob_ref[:, cs], r_ref[:, cs].astype(F32), ng_ref[:, cs])
        y_ref[:, cs] = y.astype(y_ref.dtype)


def _gla_ctx(pa, ps, wg_hi, wg_lo, bg, ng, states_prev, layer):
    cb = PA_GLA // GLA_W
    rows = GLA_CTX_SEQS * SEQ
    const = lambda shape: pl.BlockSpec(shape, lambda s: (0,) * len(shape))
    in_specs = [pl.BlockSpec((rows, GLA_W), lambda s, j=j: (s, cb + j)) for j in range(4)] + [
        pl.BlockSpec((rows, LANES), lambda s: (s, PS_LR // LANES)),
        const((2, LANES, GLA_W)), const((2, LANES, GLA_W)), const((2, 1, GLA_W)), const((1, GLA_W))]
    args = [pa, pa, pa, pa, ps, wg_hi, wg_lo, bg, ng]
    aliases = {}
    if states_prev is not None:
        in_specs.append(pl.BlockSpec(memory_space=pl.ANY))
        args.append(states_prev)
        aliases = {len(args) - 1: 1}
    return pl.pallas_call(
        _gla_ctx_kernel,
        grid=(BATCH // GLA_CTX_SEQS,),
        in_specs=in_specs,
        out_specs=[pl.BlockSpec((rows, GLA_W), lambda s: (s, 0)),
                   pl.BlockSpec((GLA_CTX_SEQS, None, 2, GLA_HEADS, GLA_DK, GLA_DV),
                                lambda s: (s, layer, 0, 0, 0, 0))],
        out_shape=[jax.ShapeDtypeStruct((T_ALL, GLA_W), BF16),
                   jax.ShapeDtypeStruct((BATCH, DEPTH, 2, GLA_HEADS, GLA_DK, GLA_DV), F32)],
        input_output_aliases=aliases,
        compiler_params=_cparams("parallel"),
        name="gla_ctx",
    )(*args)


def _gla_lat(pa, ps, wg_hi, wg_lo, bg, ng, s0, y):
    rb0 = T_CTX // DEC_SEQ
    cb = PA_GLA // GLA_W
    const = lambda shape: pl.BlockSpec(shape, lambda s: (0,) * len(shape))
    return pl.pallas_call(
        _gla_lat_kernel,
        grid=(DEC_BATCH,),
        in_specs=[pl.BlockSpec((DEC_SEQ, GLA_W), lambda s, j=j: (rb0 + s, cb + j)) for j in range(4)] + [
            pl.BlockSpec((DEC_SEQ, LANES), lambda s: (rb0 + s, PS_LR // LANES)),
            const((2, LANES, GLA_W)), const((2, LANES, GLA_W)), const((2, 1, GLA_W)), const((1, GLA_W)),
            pl.BlockSpec((None, 2, GLA_HEADS, GLA_DK, GLA_DV), lambda s: (s, 0, 0, 0, 0)),
            pl.BlockSpec(memory_space=pl.ANY)],
        out_specs=pl.BlockSpec((DEC_SEQ, GLA_W), lambda s: (rb0 + s, 0)),
        out_shape=jax.ShapeDtypeStruct((T_ALL, GLA_W), BF16),
        scratch_shapes=[pltpu.VMEM((DEC_SEQ, GLA_W), F32), pltpu.VMEM((DEC_SEQ, GLA_W), F32)],
        input_output_aliases={10: 0},
        compiler_params=_cparams("parallel"),
        name="gla_lat",
    )(pa, pa, pa, pa, ps, wg_hi, wg_lo, bg, ng, s0, y)


def _attend(q, sink, k_parts, v_parts, masks):
    scores = []
    for k, mask in zip(k_parts, masks):
        s = _dot_t(q, k) * (HEAD_DIM ** -0.5)
        if mask is not None:
            s = jnp.where(mask, s, NEG_INF)
        scores.append(s)
    m = jnp.broadcast_to(sink, (q.shape[0], 1)).astype(F32)
    for s in scores:
        m = jnp.maximum(m, jnp.max(s, axis=-1, keepdims=True))
    den = jnp.exp(sink - m)
    o = None
    for s, v in zip(scores, v_parts):
        e = jnp.exp(s - m)
        den = den + jnp.sum(e, axis=-1, keepdims=True)
        pv = _mm(e.astype(BF16), v)
        o = pv if o is None else o + pv
    return o / den


def _head(a, j):
    return a[:, j * HEAD_DIM:(j + 1) * HEAD_DIM]


ATT_ROWS_CTX = 128


def _attn_ctx_kernel(sink_ref, q_ref, kv_ref, *rest):
    o_ref = rest[-1]
    q = q_ref[...]
    kv_all = kv_ref[...]
    outs = []
    for h in range(ATT_HEADS):
        kv = h // ATT_GROUP
        k = _head(kv_all, kv).astype(BF16)
        v = _head(kv_all, ATT_KV_HEADS + kv).astype(BF16)
        qh = _head(q, h).astype(BF16)
        blocks = [_attend(qh[r:r + ATT_ROWS_CTX], sink_ref[h], [k], [v], [None])
                  for r in range(0, SEQ, ATT_ROWS_CTX)]
        outs.append(jnp.concatenate(blocks, axis=0))
    o_ref[...] = jnp.concatenate(outs, axis=-1).astype(o_ref.dtype)


def _attn_ctx(ps, sink):
    return pl.pallas_call(
        _attn_ctx_kernel,
        grid=(BATCH,),
        in_specs=[
            pl.BlockSpec(memory_space=pltpu.SMEM),
            pl.BlockSpec((SEQ, ATT_W), lambda s: (s, 0)),
            pl.BlockSpec((SEQ, 2 * LANES), lambda s: (s, PS_KV // (2 * LANES))),
        ],
        out_specs=pl.BlockSpec((SEQ, ATT_W), lambda s: (s, 0)),
        out_shape=jax.ShapeDtypeStruct((T_ALL, ATT_W), BF16),
        compiler_params=_cparams("parallel"),
        name="attn_ctx",
    )(sink, ps, ps)


def _rope(x, cos, sin_signed):
    lane = lax.broadcasted_iota(jnp.int32, x.shape, 1)
    partner = jnp.where((lane & 31) < 16, pltpu.roll(x, LANES - 16, 1), pltpu.roll(x, 16, 1))
    return x * cos + partner * sin_signed


def _attn_lat_kernel(sink_ref, q_ref, kvp_ref, kvc_ref, kvn_ref, kctx_ref, vctx_ref, cos_ref, sin_ref, yin_ref,
                     o_ref):
    del yin_ref
    n = pl.program_id(1)
    n_blk = DEC_SEQ // BLOCK
    start = n * BLOCK

    def table(ref, blk):
        blk = jnp.clip(blk, 0, n_blk - 1)
        return ref[pl.ds(pl.multiple_of(blk * BLOCK, BLOCK), BLOCK), :]

    k_loc, v_loc = [], []
    for off, ref in ((-1, kvp_ref), (0, kvc_ref), (1, kvn_ref)):
        k_loc.append(_rope(ref[:, :LANES], table(cos_ref, n + off), table(sin_ref, n + off)))
        v_loc.append(ref[:, LANES:])
    k_loc = jnp.concatenate(k_loc, axis=0)
    v_loc = jnp.concatenate(v_loc, axis=0)
    k_ctx = kctx_ref[...]
    v_ctx = vctx_ref[...]

    span = 3 * BLOCK
    stacked = (ATT_GROUP * BLOCK, span)
    qpos = start + (lax.broadcasted_iota(jnp.int32, stacked, 0) & (BLOCK - 1))
    kpos = start - WINDOW + lax.broadcasted_iota(jnp.int32, stacked, 1)
    valid = (jnp.abs(qpos - kpos) <= WINDOW) & (kpos >= 0) & (kpos < DEC_SEQ)

    cos_q = table(cos_ref, n)
    sin_q = table(sin_ref, n)
    q_heads = []
    for pair in range(ATT_HEADS // 2):
        qr = _rope(q_ref[:, pair * LANES:(pair + 1) * LANES], cos_q, sin_q)
        q_heads += [_head(qr, 0).astype(BF16), _head(qr, 1).astype(BF16)]
    outs = []
    for kv in range(ATT_KV_HEADS):
        heads = range(kv * ATT_GROUP, (kv + 1) * ATT_GROUP)
        q = jnp.concatenate([q_heads[h] for h in heads], axis=0)
        sink = jnp.concatenate([jnp.full((BLOCK, 1), sink_ref[h], F32) for h in heads], axis=0)
        o = _attend(q, sink, [_head(k_ctx, kv).astype(BF16), _head(k_loc, kv).astype(BF16)],
                    [_head(v_ctx, kv).astype(BF16), _head(v_loc, kv).astype(BF16)], [None, valid])
        outs += [o[g * BLOCK:(g + 1) * BLOCK] for g in range(ATT_GROUP)]
    o_ref[...] = jnp.concatenate(outs, axis=-1).astype(o_ref.dtype)


def _attn_lat(ps, sink, k_ctx, v_ctx, cos_t, sin_t, y):
    n_blk = DEC_SEQ // BLOCK
    rb0 = T_CTX // BLOCK

    def kv_spec(off):
        return pl.BlockSpec(
            (BLOCK, 2 * LANES),
            lambda b, n: (rb0 + b * n_blk + jnp.clip(n + off, 0, n_blk - 1), PS_KV // (2 * LANES)))

    return pl.pallas_call(
        _attn_lat_kernel,
        grid=(DEC_BATCH, n_blk),
        in_specs=[
            pl.BlockSpec(memory_space=pltpu.SMEM),
            pl.BlockSpec((BLOCK, ATT_W), lambda b, n: (rb0 + b * n_blk + n, 0)),
            kv_spec(-1), kv_spec(0), kv_spec(1),
            pl.BlockSpec((None, PAST_LEN, LANES), lambda b, n: (b, 0, 0)),
            pl.BlockSpec((None, PAST_LEN, LANES), lambda b, n: (b, 0, 0)),
            pl.BlockSpec((DEC_SEQ, LANES), lambda b, n: (0, 0)),
            pl.BlockSpec((DEC_SEQ, LANES), lambda b, n: (0, 0)),
            pl.BlockSpec(memory_space=pl.ANY),
        ],
        out_specs=pl.BlockSpec((BLOCK, ATT_W), lambda b, n: (rb0 + b * n_blk + n, 0)),
        out_shape=jax.ShapeDtypeStruct((T_ALL, ATT_W), BF16),
        input_output_aliases={9: 0},
        compiler_params=_cparams("parallel", "parallel"),
        name="attn_lat",
    )(sink, ps, ps, ps, ps, k_ctx, v_ctx, cos_t, sin_t, y)


def _rope_tables():
    pos = np.arange(DEC_SEQ)
    n_freq = HEAD_DIM // 4
    inv = jnp.asarray(ROPE_THETA, F32) ** (-jnp.arange(n_freq, dtype=F32) / n_freq)
    row = jnp.asarray(pos // GRID_W, F32)
    colp = jnp.asarray(pos % GRID_W, F32)
    ang_r = row[:, None] * inv[None, :]
    ang_c = colp[:, None] * inv[None, :]
    cos = jnp.concatenate([jnp.cos(ang_r)] * 2 + [jnp.cos(ang_c)] * 2, axis=-1)
    sin = jnp.concatenate([-jnp.sin(ang_r), jnp.sin(ang_r), -jnp.sin(ang_c), jnp.sin(ang_c)], axis=-1)
    return jnp.tile(cos, (1, 2)), jnp.tile(sin, (1, 2))


def _split_dot(a, w_hi, w_lo):
    a_hi = a.astype(BF16)
    a_lo = (a - a_hi.astype(F32)).astype(BF16)
    return _mm(a_hi, w_hi) + _mm(a_lo, w_hi) + _mm(a_hi, w_lo)


def _route(logits):
    lane_i = lax.broadcasted_iota(jnp.int32, logits.shape, 1)
    lane = lane_i.astype(F32)
    big = jnp.float32(1 << 20)
    is_g = lane_i < N_EXPERT_GROUPS
    lg = jnp.where(is_g, logits, -jnp.inf)
    m_g = jnp.max(lg, axis=-1, keepdims=True)
    grp = jnp.min(jnp.where(lg == m_g, lane, big), axis=-1, keepdims=True)
    z_g = jnp.sum(jnp.where(is_g, jnp.exp(lg - m_g), 0.0), axis=-1, keepdims=True)
    p_grp = 1.0 / z_g

    e_idx = lane_i - ROUTE_E0
    e_grp = (e_idx >> 2).astype(F32)
    sel = (e_idx >= 0) & (e_idx < N_EXPERTS) & (e_grp == grp)
    le = jnp.where(sel, logits, -jnp.inf)
    m_e = jnp.max(le, axis=-1, keepdims=True)
    ex = jnp.where(sel, jnp.exp(le - m_e), 0.0)
    pe = ex / jnp.sum(ex, axis=-1, keepdims=True)
    pe = jnp.where(sel, pe, -1.0)
    v1 = jnp.max(pe, axis=-1, keepdims=True)
    i1 = jnp.min(jnp.where(pe == v1, lane, big), axis=-1, keepdims=True)
    pe2 = jnp.where(lane == i1, -1.0, pe)
    v2 = jnp.max(pe2, axis=-1, keepdims=True)
    i2 = jnp.min(jnp.where(pe2 == v2, lane, big), axis=-1, keepdims=True)
    tot = v1 + v2
    return (jnp.where(lane == i1, p_grp * (v1 / tot), 0.0)
            + jnp.where(lane == i2, p_grp * (v2 / tot), 0.0)
            + jnp.where(lane_i == LANES - 1, grp, 0.0))


HALO = 16


def _merge_kernel(gate_ref, conv_ref, cprev_ref, cnext_ref, yb_ref, yc_ref, x_ref, mod_ref, cw_ref,
                  wb_ref, wo_ref, g2_ref, wr_hi_ref, wr_lo_ref, br_ref,
                  x1_ref, h2_ref, rg_ref, wbb_ref, wob_ref):
    tm = TM_MERGE
    i = pl.program_id(0)

    @pl.when(i == 0)
    def _():
        wbb_ref[...] = wb_ref[...].astype(BF16)
        wob_ref[...] = wo_ref[...].astype(BF16)

    def gated(ref):
        return ref[:, CONV_W:2 * CONV_W].astype(F32) * ref[:, 2 * CONV_W:3 * CONV_W].astype(F32)

    a_b = conv_ref[:, 0:CONV_W].astype(F32)
    u = gated(conv_ref)
    u_before = gated(cprev_ref)[HALO - 1:HALO]
    u_after = gated(cnext_ref)[0:1]
    r = lax.broadcasted_iota(jnp.int32, (tm, 1), 0)
    g_row = i * tm + r
    seq_mask = jnp.where(g_row < T_CTX, SEQ - 1, DEC_SEQ - 1)
    first = (g_row & seq_mask) == 0
    last = ((g_row + 1) & seq_mask) == 0
    u_prev = jnp.where(r == 0, u_before, pltpu.roll(u, 1, 0))
    u_next = jnp.where(r == tm - 1, u_after, pltpu.roll(u, tm - 1, 0))
    u_prev = jnp.where(first, 0.0, u_prev)
    u_next = jnp.where(last, 0.0, u_next)
    y_a = (a_b * (u_prev * cw_ref[0:1, :] + u * cw_ref[1:2, :] + u_next * cw_ref[2:3, :])).astype(BF16)

    blocks = [slice(r0, r0 + MERGE_ROWS) for r0 in range(0, tm, MERGE_ROWS)]
    ys = [(y_a[rows], yb_ref[rows, :], yc_ref[rows, :]) for rows in blocks]
    branches = [[_mm(y, wbb_ref[j]) for j, y in enumerate(y3)] for y3 in ys]
    zs = []
    for rows, br3 in zip(blocks, branches):
        z = sum(gate_ref[rows, j * D_MODEL:(j + 1) * D_MODEL].astype(F32) * br3[j] for j in range(3))
        zs.append(z.astype(BF16))
    outs = [_mm(z, wob_ref[...]) for z in zs]
    h2s = []
    for rows, o in zip(blocks, outs):
        x1 = x_ref[rows, :] + mod_ref[2:3, :] * o
        x1_ref[rows, :] = x1
        h2 = _mod_norm(x1, g2_ref[...], mod_ref[...], 3)
        h2_ref[rows, :] = h2.astype(BF16)
        h2s.append(h2)
    logits = [_split_dot(h2, wr_hi_ref[...], wr_lo_ref[...]) + br_ref[...] for h2 in h2s]
    for rows, lg in zip(blocks, logits):
        rg_ref[rows, :] = _route(lg)


def _merge(pg, pa, y_b, y_c, x, mod_l, conv_w, layer, wb, wo, g2, wr_hi, wr_lo, br):
    tm = TM_MERGE
    n_tiles = T_ALL // tm
    hb = tm // HALO
    const = lambda shape: pl.BlockSpec(shape, lambda i: (0,) * len(shape))
    return pl.pallas_call(
        _merge_kernel,
        grid=(n_tiles,),
        in_specs=[
            pl.BlockSpec((tm, PG_W), lambda i: (i, 0)),
            pl.BlockSpec((tm, 3 * CONV_W), lambda i: (i, 0)),
            pl.BlockSpec((HALO, 3 * CONV_W), lambda i: (jnp.maximum(i * hb - 1, 0), 0)),
            pl.BlockSpec((HALO, 3 * CONV_W), lambda i: (jnp.minimum((i + 1) * hb, n_tiles * hb - 1), 0)),
            pl.BlockSpec((tm, GLA_W), lambda i: (i, 0)),
            pl.BlockSpec((tm, ATT_W), lambda i: (i, 0)),
            pl.BlockSpec((tm, D_MODEL), lambda i: (i, 0)),
            pl.BlockSpec((None, 6, D_MODEL), lambda i: (_cond_row(i, tm), 0, 0)),
            const((CONV_K, CONV_W)),
            pl.BlockSpec((None, 3, 512, D_MODEL), lambda i: (layer, 0, 0, 0)),
            pl.BlockSpec((None, D_MODEL, D_MODEL), lambda i: (layer, 0, 0)),
            const((1, D_MODEL)),
            const((D_MODEL, LANES)),
            const((D_MODEL, LANES)),
            const((1, LANES)),
        ],
        out_specs=[
            pl.BlockSpec((tm, D_MODEL), lambda i: (i, 0)),
            pl.BlockSpec((tm, D_MODEL), lambda i: (i, 0)),
            pl.BlockSpec((tm, LANES), lambda i: (i, 0)),
        ],
        out_shape=[
            jax.ShapeDtypeStruct((T_ALL, D_MODEL), F32),
            jax.ShapeDtypeStruct((T_ALL, D_MODEL), BF16),
            jax.ShapeDtypeStruct((T_ALL, LANES), F32),
        ],
        scratch_shapes=[pltpu.VMEM((3, 512, D_MODEL), BF16), pltpu.VMEM((D_MODEL, D_MODEL), BF16)],
        compiler_params=_cparams("arbitrary"),
        name="merge",
    )(pg, pa, pa, pa, y_b, y_c, x, mod_l, conv_w, wb, wo, g2, wr_hi, wr_lo, br)


SUB = 128
MOE_SLOTS = TM + N_EXPERT_GROUPS * SUB


def _dispatch_tables(rg):
    n_tiles = T_ALL // TM
    grp = rg[:, LANES - 1].astype(jnp.int32).reshape(n_tiles, TM)
    hot = grp[..., None] == jnp.arange(N_EXPERT_GROUPS, dtype=jnp.int32)
    onehot = hot.astype(jnp.int32)
    cnt = onehot.sum(axis=1)
    nblk = (cnt + SUB - 1) // SUB
    bstart = jnp.cumsum(nblk, axis=1) - nblk
    before = jnp.tril(jnp.ones((TM, TM), BF16), -1)
    rank = jnp.einsum("ts,nsg->ntg", before, hot.astype(BF16), preferred_element_type=F32).astype(jnp.int32)
    slot = ((bstart[:, None, :] * SUB + rank) * onehot).sum(axis=-1)
    return (bstart.reshape(-1), nblk.reshape(-1), slot.reshape(n_tiles, 1, TM), slot.reshape(T_ALL, 1))


def _moe_kernel(bstart_ref, nblk_ref, h_ref, rg_ref, srow_ref, scol_ref, x1_ref, mod_ref, wg_ref, wu_ref,
                wd_ref, g_ref, *rest, final, tile0):
    if final:
        y_ref, hs_ref, gs_ref, os_ref = rest
    else:
        modn_ref, x2_ref, hn_ref, hs_ref, gs_ref, os_ref = rest
    i = pl.program_id(0)
    grp = pl.program_id(1)

    @pl.when(grp == 0)
    def _():
        slot_of_token = srow_ref[...]
        pm = jnp.where(lax.broadcasted_iota(jnp.int32, (MOE_SLOTS, TM), 0) == slot_of_token, 1.0, 0.0)
        pm = pm.astype(BF16)
        hs_ref[...] = _mm(pm, h_ref[...]).astype(BF16)
        r_hi, r_mid, r_lo = _split3(rg_ref[...])
        gs_ref[...] = _mm(pm, r_hi) + _mm(pm, r_mid) + _mm(pm, r_lo)
        os_ref[...] = jnp.zeros_like(os_ref)

    k = (tile0 + i) * N_EXPERT_GROUPS + grp
    first = bstart_ref[k]
    n_blocks = nblk_ref[k]

    def experts(block0, n_sub):
        n_rows = n_sub * SUB
        rows = pl.ds(pl.multiple_of(block0 * SUB, SUB), n_rows)
        lane = lax.broadcasted_iota(jnp.int32, (n_rows, LANES), 1)
        x = hs_ref[rows, :]
        gates = gs_ref[rows, :]
        acc = jnp.zeros((n_rows, D_MODEL), F32)
        for e in range(EXPERTS_PER_GROUP):
            ge = _mm(x, wg_ref[e])
            ue = _mm(x, wu_ref[e])
            w = jnp.sum(jnp.where(lane == ROUTE_E0 + grp * EXPERTS_PER_GROUP + e, gates, 0.0),
                        axis=-1, keepdims=True)
            hid = (ge * jax.nn.sigmoid(ge)) * ue * w
            acc = acc + _mm(hid.astype(BF16), wd_ref[e])
        os_ref[rows, :] = acc.astype(BF16)

    def pair(p, carry):
        experts(first + 2 * p, 2)
        return carry

    lax.fori_loop(0, n_blocks // 2, pair, 0)

    @pl.when(n_blocks % 2 == 1)
    def _():
        experts(first + n_blocks - 1, 1)

    @pl.when(grp == N_EXPERT_GROUPS - 1)
    def _():
        slot_col = scol_ref[...]
        pt = jnp.where(lax.broadcasted_iota(jnp.int32, (TM, MOE_SLOTS), 1) == slot_col, 1.0, 0.0)
        moe = _mm(pt.astype(BF16), os_ref[...])
        x2 = x1_ref[...] + mod_ref[5:6, :] * moe
        if final:
            y_ref[...] = _rms(x2) * g_ref[...]
        else:
            x2_ref[...] = x2
            hn_ref[...] = _mod_norm(x2, g_ref[...], modn_ref[...], 0).astype(BF16)


def _moe(tables, h2, rg, x1, mod_l, layer, w_eg, w_eu, w_ed, g, mod_next, tile0, n_tiles):
    bstart, nblk, slot_row, slot_col = tables
    final = mod_next is None
    row = lambda i, e, *_: (tile0 + i, 0)
    mod_spec = lambda: pl.BlockSpec((None, 6, D_MODEL), lambda i, e, *_: (_cond_row(tile0 + i, TM), 0, 0))
    group = lambda i, e, *_: (layer, e, 0, 0)
    in_specs = [
        pl.BlockSpec((TM, D_MODEL), row),
        pl.BlockSpec((TM, LANES), row),
        pl.BlockSpec((None, 1, TM), lambda i, e, *_: (tile0 + i, 0, 0)),
        pl.BlockSpec((TM, 1), row),
        pl.BlockSpec((TM, D_MODEL), row),
        mod_spec(),
        pl.BlockSpec((None, EXPERTS_PER_GROUP, D_MODEL, D_EXPERT), group),
        pl.BlockSpec((None, EXPERTS_PER_GROUP, D_MODEL, D_EXPERT), group),
        pl.BlockSpec((None, EXPERTS_PER_GROUP, D_EXPERT, D_MODEL), group),
        pl.BlockSpec((1, D_MODEL), lambda i, e, *_: (0, 0)),
    ]
    args = [h2, rg, slot_row, slot_col, x1, mod_l, w_eg, w_eu, w_ed, g]
    if final:
        out_specs = pl.BlockSpec((TM, D_MODEL), lambda i, e, *_: (i, 0))
        out_shape = jax.ShapeDtypeStruct((n_tiles * TM, D_MODEL), F32)
    else:
        in_specs.append(mod_spec())
        args.append(mod_next)
        out_specs = [pl.BlockSpec((TM, D_MODEL), row)] * 2
        out_shape = [jax.ShapeDtypeStruct((T_ALL, D_MODEL), F32), jax.ShapeDtypeStruct((T_ALL, D_MODEL), BF16)]
    return pl.pallas_call(
        functools.partial(_moe_kernel, final=final, tile0=tile0),
        grid_spec=pltpu.PrefetchScalarGridSpec(
            num_scalar_prefetch=2,
            grid=(n_tiles, N_EXPERT_GROUPS),
            in_specs=in_specs,
            out_specs=out_specs,
            scratch_shapes=[pltpu.VMEM((MOE_SLOTS, D_MODEL), BF16), pltpu.VMEM((MOE_SLOTS, LANES), F32),
                            pltpu.VMEM((MOE_SLOTS, D_MODEL), BF16)],
        ),
        out_shape=out_shape,
        compiler_params=_cparams("parallel", "arbitrary"),
        name="moe_final" if final else "moe",
    )(bstart, nblk, *args)


def _cache_kernel(*refs):
    ps_refs, (k_ref, v_ref) = refs[:DEPTH], refs[DEPTH:]
    n_seq = TM // SEQ
    for l, ps_ref in enumerate(ps_refs):
        k_ref[:, l] = ps_ref[:, :LANES].reshape(n_seq, SEQ, LANES)
        v_ref[:, l] = ps_ref[:, LANES:].reshape(n_seq, SEQ, LANES)


def _cache(ps_layers):
    n_seq = TM // SEQ
    out = pl.BlockSpec((n_seq, DEPTH, SEQ, LANES), lambda i: (i, 0, 0, 0))
    return pl.pallas_call(
        _cache_kernel,
        grid=(T_CTX // TM,),
        in_specs=[pl.BlockSpec((TM, 2 * LANES), lambda i: (i, PS_KV // (2 * LANES)))] * DEPTH,
        out_specs=[out, out],
        out_shape=[jax.ShapeDtypeStruct((BATCH, DEPTH, SEQ, LANES), F32)] * 2,
        compiler_params=_cparams("parallel"),
        name="cache",
    )(*ps_layers)


def _prep_gla_gate(w_gate):
    w = jnp.zeros((2, LANES, GLA_W), F32)
    w = w.at[0, 0:GLA_RANK].set(w_gate[0]).at[1, GLA_RANK:2 * GLA_RANK].set(w_gate[1])
    hi = w.astype(BF16)
    return hi, (w - hi.astype(F32)).astype(BF16)


def _prep_router(w_rg, b_rg, w_re, b_re):
    w = jnp.zeros((D_MODEL, LANES), F32)
    w = w.at[:, :N_EXPERT_GROUPS].set(w_rg).at[:, ROUTE_E0:ROUTE_E0 + N_EXPERTS].set(w_re)
    b = jnp.zeros((1, LANES), F32)
    b = b.at[0, :N_EXPERT_GROUPS].set(b_rg).at[0, ROUTE_E0:ROUTE_E0 + N_EXPERTS].set(b_re)
    w_hi = w.astype(BF16)
    w_lo = (w - w_hi.astype(F32)).astype(BF16)
    return w_hi, w_lo, b


def kernel(x_prompt, x_sample, state_gla, cache_k, cache_v, c, c_ctx, w_ada, b_ada, norm1_g, norm2_g, w_in,
           conv_w, gla_w_gate, gla_b_gate, gla_norm_g, attn_sink, w_branch, w_out, w_route_group,
           b_route_group, w_route_expert, b_route_expert, w_exp_gate, w_exp_up, w_exp_down, final_norm_g):
    cond = jnp.zeros((N_COND, D_MODEL), F32).at[0].set(c_ctx).at[1:1 + DEC_BATCH].set(c)
    mod = _modulation(cond, w_ada, b_ada).reshape(DEPTH, N_COND, 6, D_MODEL)
    cos_t, sin_t = _rope_tables()
    row = lambda v: v.reshape(1, -1)

    x, h = _prenorm(x_prompt.reshape(T_CTX, D_MODEL), x_sample.reshape(T_LAT, D_MODEL), mod[0], row(norm1_g[0]))
    states = None
    ps_layers = []
    y_prompt = y_sample = None
    w_t = jnp.swapaxes(w_in, 1, 2)
    experts = (w_exp_gate.astype(BF16), w_exp_up.astype(BF16), w_exp_down.astype(BF16))
    for l in range(DEPTH):
        pa = _proj(h, w_t, l, [(0, PA_W // 2)], 2, BF16, "proj_a", act="silu_tail")
        ps = _proj(h, w_t, l, [(W_IN_TQ, W_IN_GATE - W_IN_TQ), (W_IN_LR, LANES)], 1, F32, "proj_s")
        pg = _proj(h, w_t, l, [(W_IN_GATE, PG_W // 2)], 2, BF16, "proj_g", act="sigmoid")
        ps_layers.append(ps)

        wg_hi, wg_lo = _prep_gla_gate(gla_w_gate[l])
        bg = gla_b_gate[l].reshape(2, 1, GLA_W)
        ng = row(gla_norm_g[l])
        y_b, states = _gla_ctx(pa, ps, wg_hi, wg_lo, bg, ng, states, l)
        y_b = _gla_lat(pa, ps, wg_hi, wg_lo, bg, ng, state_gla[:, l], y_b)
        y_c = _attn_ctx(ps, attn_sink[l])
        y_c = _attn_lat(ps, attn_sink[l], cache_k[:, l].reshape(DEC_BATCH, PAST_LEN, LANES),
                        cache_v[:, l].reshape(DEC_BATCH, PAST_LEN, LANES), cos_t, sin_t, y_c)

        wr_hi, wr_lo, br = _prep_router(w_route_group[l], b_route_group[l], w_route_expert[l], b_route_expert[l])
        x1, h2, rg = _merge(pg, pa, y_b, y_c, x, mod[l], conv_w[l], l, w_branch, w_out, row(norm2_g[l]),
                            wr_hi, wr_lo, br)
        moe_in = (_dispatch_tables(rg), h2, rg, x1, mod[l], l, *experts)
        if l + 1 < DEPTH:
            x, h = _moe(*moe_in, row(norm1_g[l + 1]), mod[l + 1], 0, T_ALL // TM)
        else:
            gf = row(final_norm_g)
            y_prompt = _moe(*moe_in, gf, None, 0, T_CTX // TM)
            y_sample = _moe(*moe_in, gf, None, T_CTX // TM, T_LAT // TM)

    new_k, new_v = _cache(ps_layers)
    kv_shape = (BATCH, DEPTH, SEQ, ATT_KV_HEADS, HEAD_DIM)
    return (y_prompt.reshape(BATCH, SEQ, D_MODEL), y_sample.reshape(DEC_BATCH, DEC_SEQ, D_MODEL), states,
            new_k.reshape(kv_shape), new_v.reshape(kv_shape))
```

```python
import functools

import jax
import jax.numpy as jnp
import numpy as np
from jax import lax
from jax.experimental import pallas as pl
from jax.experimental.pallas import tpu as pltpu

F32 = jnp.float32
BF16 = jnp.bfloat16

D_MODEL = 1024
BATCH = 32
SEQ = 256
DEPTH = 2
DEC_BATCH = 2
DEC_SEQ = 2048
PAST_LEN = 512
GRID_W = 64
EPS = 1e-6
CONV_W = 512
CONV_K = 3
GLA_HEADS = 4
GLA_DK = 128
GLA_DV = 128
GLA_W = GLA_HEADS * GLA_DV
GLA_RANK = 16
GLA_TAU = 16.0
GLA_CHUNK = 64
ATT_HEADS = 8
ATT_KV_HEADS = 2
ATT_GROUP = ATT_HEADS // ATT_KV_HEADS
HEAD_DIM = 64
ATT_W = ATT_HEADS * HEAD_DIM
WINDOW = 128
BLOCK = 128
ROPE_THETA = 10000.0
N_EXPERT_GROUPS = 4
EXPERTS_PER_GROUP = 4
N_EXPERTS = 16
D_EXPERT = 256
NEG_INF = -1e30

T_CTX = BATCH * SEQ
T_LAT = DEC_BATCH * DEC_SEQ
T_ALL = T_CTX + T_LAT
N_COND = 8
LANES = 128

W_IN_LR = 3584
W_IN_TQ = 3616
W_IN_GATE = 4384
N_IN = 7456
PA_W = 3584
PA_GLA = 1536
PS_W = 896
PS_KV = 512
PS_LR = 768
PG_W = 3 * D_MODEL
ROUTE_E0 = N_EXPERT_GROUPS

TM = 1024
TM_MERGE = 512
MERGE_ROWS = 256
VMEM_LIMIT = 56 * 1024 * 1024


def _cparams(*sem):
    return pltpu.CompilerParams(dimension_semantics=sem, vmem_limit_bytes=VMEM_LIMIT)


def _cond_row(i, tm):
    n_ctx = T_CTX // tm
    per = DEC_SEQ // tm
    return jnp.where(i < n_ctx, 0, 1 + (i - n_ctx) // per)


def _mm(a, b):
    return jnp.dot(a, b, preferred_element_type=F32)


def _dot_t(a, b):
    return lax.dot_general(a, b, (((1,), (1,)), ((), ())), preferred_element_type=F32)


def _dot_ta(a, b):
    return lax.dot_general(a, b, (((0,), (0,)), ((), ())), preferred_element_type=F32)


def _rms(x):
    return x * lax.rsqrt(jnp.mean(x * x, axis=-1, keepdims=True) + EPS)


def _mod_norm(x, g, mod, shift_row):
    return _rms(x) * g * (1.0 + mod[shift_row + 1:shift_row + 2, :]) + mod[shift_row:shift_row + 1, :]


def _mod_kernel(c_ref, w_ref, b_ref, o_ref):
    c = c_ref[...]
    s = (c * jax.nn.sigmoid(c)).astype(BF16)
    o_ref[...] = _mm(s, w_ref[...].astype(BF16)) + b_ref[...]


def _modulation(cond, w_ada, b_ada):
    tn = 1536
    return pl.pallas_call(
        _mod_kernel,
        grid=(DEPTH, 6 * D_MODEL // tn),
        in_specs=[
            pl.BlockSpec((N_COND, D_MODEL), lambda l, j: (0, 0)),
            pl.BlockSpec((None, D_MODEL, tn), lambda l, j: (l, 0, j)),
            pl.BlockSpec((None, 1, tn), lambda l, j: (l, 0, j)),
        ],
        out_specs=pl.BlockSpec((None, N_COND, tn), lambda l, j: (l, 0, j)),
        out_shape=jax.ShapeDtypeStruct((DEPTH, N_COND, 6 * D_MODEL), F32),
        compiler_params=_cparams("parallel", "parallel"),
        name="modulation",
    )(cond, w_ada, b_ada.reshape(DEPTH, 1, 6 * D_MODEL))


def _prenorm_kernel(xp_ref, xs_ref, mod_ref, g_ref, x_ref, h_ref):
    x = jnp.where(pl.program_id(0) < T_CTX // TM, xp_ref[...], xs_ref[...])
    x_ref[...] = x
    h_ref[...] = _mod_norm(x, g_ref[...], mod_ref[...], 0).astype(BF16)


def _prenorm(xp, xs, mod_l, g):
    n_ctx = T_CTX // TM
    return pl.pallas_call(
        _prenorm_kernel,
        grid=(T_ALL // TM,),
        in_specs=[
            pl.BlockSpec((TM, D_MODEL), lambda i: (jnp.minimum(i, n_ctx - 1), 0)),
            pl.BlockSpec((TM, D_MODEL), lambda i: (jnp.maximum(i - n_ctx, 0), 0)),
            pl.BlockSpec((None, 6, D_MODEL), lambda i: (_cond_row(i, TM), 0, 0)),
            pl.BlockSpec((1, D_MODEL), lambda i: (0, 0)),
        ],
        out_specs=[pl.BlockSpec((TM, D_MODEL), lambda i: (i, 0))] * 2,
        out_shape=[jax.ShapeDtypeStruct((T_ALL, D_MODEL), F32), jax.ShapeDtypeStruct((T_ALL, D_MODEL), BF16)],
        compiler_params=_cparams("parallel"),
        name="prenorm",
    )(xp, xs, mod_l, g)


def _proj_kernel(h_ref, *refs, act):
    w_refs, (o_ref, wb_ref) = refs[:-2], refs[-2:]

    @pl.when(pl.program_id(1) == 0)
    def _():
        col = 0
        for w_ref in w_refs:
            n = w_ref.shape[0]
            wb_ref[:, col:col + n] = w_ref[...].T.astype(BF16)
            col += n

    def sigmoid(t):
        return 0.5 * jnp.tanh(0.5 * t) + 0.5

    y = _mm(h_ref[...], wb_ref[...])
    if act == "sigmoid":
        y = sigmoid(y)
    if act == "silu_tail":
        last = pl.program_id(0) == pl.num_programs(0) - 1
        tail = y[:, -GLA_W:]
        o_ref[:, :-GLA_W] = y[:, :-GLA_W].astype(o_ref.dtype)
        o_ref[:, -GLA_W:] = jnp.where(last, tail * sigmoid(tail), tail).astype(o_ref.dtype)
    else:
        o_ref[...] = y.astype(o_ref.dtype)


def _proj(h, w_t, layer, pieces, n_tiles, out_dtype, name, act=None):
    tn = sum(n for _, n in pieces)

    def w_spec(c0, n):
        return pl.BlockSpec((pl.Squeezed(), pl.Element(n), pl.Element(D_MODEL)),
                            lambda j, i: (layer, pl.multiple_of(c0 + j * tn, 8), 0))

    return pl.pallas_call(
        functools.partial(_proj_kernel, act=act),
        grid=(n_tiles, T_ALL // TM),
        in_specs=[pl.BlockSpec((TM, D_MODEL), lambda j, i: (i, 0))] + [w_spec(c0, n) for c0, n in pieces],
        out_specs=pl.BlockSpec((TM, tn), lambda j, i: (i, j)),
        out_shape=jax.ShapeDtypeStruct((T_ALL, n_tiles * tn), out_dtype),
        scratch_shapes=[pltpu.VMEM((D_MODEL, tn), BF16)],
        compiler_params=_cparams("parallel", "arbitrary"),
        name=name,
    )(h, *([w_t] * len(pieces)))


GLA_GROUP = 256
GLA_CTX_SEQS = 2


def _log_sigmoid(z):
    return jnp.minimum(z, 0.0) - jnp.log(1.0 + jnp.exp(-jnp.abs(z)))


def _split3(x):
    hi = x.astype(BF16)
    r1 = x - hi.astype(F32)
    mid = r1.astype(BF16)
    lo = (r1 - mid.astype(F32)).astype(BF16)
    return hi, mid, lo


def _gla_keep(d):
    ri = lax.broadcasted_iota(jnp.int32, (GLA_GROUP, GLA_GROUP), 0)
    ci = lax.broadcasted_iota(jnp.int32, (GLA_GROUP, GLA_GROUP), 1)
    if d == 0:
        return (ci <= ri) & (ci >= (ri & ~(GLA_CHUNK - 1)))
    return (ci >= ri) & (ci <= (ri | (GLA_CHUNK - 1)))


def _gla_group(q, k, v, lr, wg_hi, wg_lo, bg, states, d, keep):
    C = GLA_CHUNK
    nc = GLA_GROUP // C
    width = q.shape[-1]
    lr_hi = lr.astype(BF16)
    lr_lo = (lr - lr_hi.astype(F32)).astype(BF16)
    z = _mm(lr_hi, wg_hi) + _mm(lr_lo, wg_hi) + _mm(lr_hi, wg_lo) + bg
    la = _log_sigmoid(z) / GLA_TAU
    tri = jnp.where(keep, 1.0, 0.0).astype(BF16)
    la_hi, la_mid, la_lo = _split3(la)
    b = _mm(tri, la_hi) + _mm(tri, la_mid) + _mm(tri, la_lo)
    edge = C - 1 if d == 0 else 0
    b_last = [b[c * C + edge:c * C + edge + 1, :] for c in range(nc)]
    bl = jnp.concatenate([jnp.broadcast_to(t, (C, width)) for t in b_last], axis=0)
    q_in = (q * (GLA_DK ** -0.5) * jnp.exp(b)).astype(BF16)
    k_in = (k * jnp.exp(-b)).astype(BF16)
    k_end = (k * jnp.exp(bl - b)).astype(BF16)
    vb = v.astype(BF16)
    dec = [jnp.exp(t) for t in b_last]
    order = range(nc) if d == 0 else range(nc - 1, -1, -1)
    outs, new_states = [], []
    for h, st in enumerate(states):
        cs = slice(h * GLA_DK, (h + 1) * GLA_DK)
        att = jnp.where(keep, _dot_t(q_in[:, cs], k_in[:, cs]), 0.0)
        o = _mm(att.astype(BF16), vb[:, cs])
        pieces = [None] * nc
        for c in order:
            rs = slice(c * C, (c + 1) * C)
            pieces[c] = o[rs] + _dot_t(q_in[rs, cs], st.astype(BF16))
            st = st * dec[c][:, cs] + _dot_ta(vb[rs, cs], k_end[rs, cs])
        outs.append(jnp.concatenate(pieces, axis=0))
        new_states.append(st)
    return outs, new_states


def _gla_finish(o, r_act, ng):
    o = o * lax.rsqrt(jnp.mean(o * o, axis=-1, keepdims=True) + EPS)
    return o * ng * r_act


def _gla_ctx_kernel(q_ref, k_ref, v_ref, r_ref, lr_ref, wgh_ref, wgl_ref, bg_ref, ng_ref, *rest):
    y_ref, s_ref = rest[-2:]
    zero = jnp.zeros((GLA_DV, GLA_DK), F32)
    keep = [_gla_keep(0), _gla_keep(1)]
    for s in range(GLA_CTX_SEQS):
        rows = slice(s * SEQ, (s + 1) * SEQ)
        q, k, v = (ref[rows, :].astype(F32) for ref in (q_ref, k_ref, v_ref))
        lr = lr_ref[rows, :]
        o_dir = []
        for d in range(2):
            outs, sts = _gla_group(q, k, v, lr, wgh_ref[d], wgl_ref[d], bg_ref[d], [zero] * GLA_HEADS, d,
                                   keep[d])
            o_dir.append(outs)
            for h in range(GLA_HEADS):
                s_ref[s, d, h] = sts[h].T
        for h in range(GLA_HEADS):
            cs = slice(h * GLA_DV, (h + 1) * GLA_DV)
            y = _gla_finish(o_dir[0][h] + o_dir[1][h], r_ref[rows, cs].astype(F32), ng_ref[:, cs])
            y_ref[rows, cs] = y.astype(y_ref.dtype)


def _gla_lat_kernel(q_ref, k_ref, v_ref, r_ref, lr_ref, wgh_ref, wgl_ref, bg_ref, ng_ref, s0_ref, yin_ref,
                    y_ref, of_ref, ob_ref):
    del yin_ref
    n_groups = DEC_SEQ // GLA_GROUP
    keep_f, keep_b = _gla_keep(0), _gla_keep(1)

    def load(rows):
        return (q_ref[rows, :].astype(F32), k_ref[rows, :].astype(F32), v_ref[rows, :].astype(F32),
                lr_ref[rows, :])

    def body(g, carry):
        st_f, st_b = carry
        rf = pl.ds(pl.multiple_of(g * GLA_GROUP, GLA_GROUP), GLA_GROUP)
        rb = pl.ds(pl.multiple_of((n_groups - 1 - g) * GLA_GROUP, GLA_GROUP), GLA_GROUP)
        outs, st_f = _gla_group(*load(rf), wgh_ref[0], wgl_ref[0], bg_ref[0], list(st_f), 0, keep_f)
        of_ref[rf, :] = jnp.concatenate(outs, axis=-1)
        outs, st_b = _gla_group(*load(rb), wgh_ref[1], wgl_ref[1], bg_ref[1], list(st_b), 1, keep_b)
        ob_ref[rb, :] = jnp.concatenate(outs, axis=-1)
        return tuple(st_f), tuple(st_b)

    init = tuple(tuple(s0_ref[d, h].T for h in range(GLA_HEADS)) for d in range(2))
    lax.fori_loop(0, n_groups, body, init)
    for h in range(GLA_HEADS):
        cs = slice(h * GLA_DV, (h + 1) * GLA_DV)
        y = _gla_finish(of_ref[:, cs] + ob_ref[:, cs], r_ref[:, cs].astype(F32), ng_ref[:, cs])
        y_ref[:, cs] = y.astype(y_ref.dtype)


def _gla_ctx(pa, ps, wg_hi, wg_lo, bg, ng, states_prev, layer):
    cb = PA_GLA // GLA_W
    rows = GLA_CTX_SEQS * SEQ
    const = lambda shape: pl.BlockSpec(shape, lambda s: (0,) * len(shape))
    in_specs = [pl.BlockSpec((rows, GLA_W), lambda s, j=j: (s, cb + j)) for j in range(4)] + [
        pl.BlockSpec((rows, LANES), lambda s: (s, PS_LR // LANES)),
        const((2, LANES, GLA_W)), const((2, LANES, GLA_W)), const((2, 1, GLA_W)), const((1, GLA_W))]
    args = [pa, pa, pa, pa, ps, wg_hi, wg_lo, bg, ng]
    aliases = {}
    if states_prev is not None:
        in_specs.append(pl.BlockSpec(memory_space=pl.ANY))
        args.append(states_prev)
        aliases = {len(args) - 1: 1}
    return pl.pallas_call(
        _gla_ctx_kernel,
        grid=(BATCH // GLA_CTX_SEQS,),
        in_specs=in_specs,
        out_specs=[pl.BlockSpec((rows, GLA_W), lambda s: (s, 0)),
                   pl.BlockSpec((GLA_CTX_SEQS, None, 2, GLA_HEADS, GLA_DK, GLA_DV),
                                lambda s: (s, layer, 0, 0, 0, 0))],
        out_shape=[jax.ShapeDtypeStruct((T_ALL, GLA_W), BF16),
                   jax.ShapeDtypeStruct((BATCH, DEPTH, 2, GLA_HEADS, GLA_DK, GLA_DV), F32)],
        input_output_aliases=aliases,
        compiler_params=_cparams("parallel"),
        name="gla_ctx",
    )(*args)


def _gla_lat(pa, ps, wg_hi, wg_lo, bg, ng, s0, y):
    rb0 = T_CTX // DEC_SEQ
    cb = PA_GLA // GLA_W
    const = lambda shape: pl.BlockSpec(shape, lambda s: (0,) * len(shape))
    return pl.pallas_call(
        _gla_lat_kernel,
        grid=(DEC_BATCH,),
        in_specs=[pl.BlockSpec((DEC_SEQ, GLA_W), lambda s, j=j: (rb0 + s, cb + j)) for j in range(4)] + [
            pl.BlockSpec((DEC_SEQ, LANES), lambda s: (rb0 + s, PS_LR // LANES)),
            const((2, LANES, GLA_W)), const((2, LANES, GLA_W)), const((2, 1, GLA_W)), const((1, GLA_W)),
            pl.BlockSpec((None, 2, GLA_HEADS, GLA_DK, GLA_DV), lambda s: (s, 0, 0, 0, 0)),
            pl.BlockSpec(memory_space=pl.ANY)],
        out_specs=pl.BlockSpec((DEC_SEQ, GLA_W), lambda s: (rb0 + s, 0)),
        out_shape=jax.ShapeDtypeStruct((T_ALL, GLA_W), BF16),
        scratch_shapes=[pltpu.VMEM((DEC_SEQ, GLA_W), F32), pltpu.VMEM((DEC_SEQ, GLA_W), F32)],
        input_output_aliases={10: 0},
        compiler_params=_cparams("parallel"),
        name="gla_lat",
    )(pa, pa, pa, pa, ps, wg_hi, wg_lo, bg, ng, s0, y)


def _attend(q, sink, k_parts, v_parts, masks):
    scores = []
    for k, mask in zip(k_parts, masks):
        s = _dot_t(q, k) * (HEAD_DIM ** -0.5)
        if mask is not None:
            s = jnp.where(mask, s, NEG_INF)
        scores.append(s)
    m = jnp.broadcast_to(sink, (q.shape[0], 1)).astype(F32)
    for s in scores:
        m = jnp.maximum(m, jnp.max(s, axis=-1, keepdims=True))
    den = jnp.exp(sink - m)
    o = None
    for s, v in zip(scores, v_parts):
        e = jnp.exp(s - m)
        den = den + jnp.sum(e, axis=-1, keepdims=True)
        pv = _mm(e.astype(BF16), v)
        o = pv if o is None else o + pv
    return o / den


def _head(a, j):
    return a[:, j * HEAD_DIM:(j + 1) * HEAD_DIM]


ATT_ROWS_CTX = 128


def _attn_ctx_kernel(sink_ref, q_ref, kv_ref, *rest):
    o_ref = rest[-1]
    q = q_ref[...]
    kv_all = kv_ref[...]
    outs = []
    for h in range(ATT_HEADS):
        kv = h // ATT_GROUP
        k = _head(kv_all, kv).astype(BF16)
        v = _head(kv_all, ATT_KV_HEADS + kv).astype(BF16)
        qh = _head(q, h).astype(BF16)
        blocks = [_attend(qh[r:r + ATT_ROWS_CTX], sink_ref[h], [k], [v], [None])
                  for r in range(0, SEQ, ATT_ROWS_CTX)]
        outs.append(jnp.concatenate(blocks, axis=0))
    o_ref[...] = jnp.concatenate(outs, axis=-1).astype(o_ref.dtype)


def _attn_ctx(ps, sink):
    return pl.pallas_call(
        _attn_ctx_kernel,
        grid=(BATCH,),
        in_specs=[
            pl.BlockSpec(memory_space=pltpu.SMEM),
            pl.BlockSpec((SEQ, ATT_W), lambda s: (s, 0)),
            pl.BlockSpec((SEQ, 2 * LANES), lambda s: (s, PS_KV // (2 * LANES))),
        ],
        out_specs=pl.BlockSpec((SEQ, ATT_W), lambda s: (s, 0)),
        out_shape=jax.ShapeDtypeStruct((T_ALL, ATT_W), BF16),
        compiler_params=_cparams("parallel"),
        name="attn_ctx",
    )(sink, ps, ps)


def _rope(x, cos, sin_signed):
    lane = lax.broadcasted_iota(jnp.int32, x.shape, 1)
    partner = jnp.where((lane & 31) < 16, pltpu.roll(x, LANES - 16, 1), pltpu.roll(x, 16, 1))
    return x * cos + partner * sin_signed


def _attn_lat_kernel(sink_ref, q_ref, kvp_ref, kvc_ref, kvn_ref, kctx_ref, vctx_ref, cos_ref, sin_ref, yin_ref,
                     o_ref):
    del yin_ref
    n = pl.program_id(1)
    n_blk = DEC_SEQ // BLOCK
    start = n * BLOCK

    def table(ref, blk):
        blk = jnp.clip(blk, 0, n_blk - 1)
        return ref[pl.ds(pl.multiple_of(blk * BLOCK, BLOCK), BLOCK), :]

    k_loc, v_loc = [], []
    for off, ref in ((-1, kvp_ref), (0, kvc_ref), (1, kvn_ref)):
        k_loc.append(_rope(ref[:, :LANES], table(cos_ref, n + off), table(sin_ref, n + off)))
        v_loc.append(ref[:, LANES:])
    k_loc = jnp.concatenate(k_loc, axis=0)
    v_loc = jnp.concatenate(v_loc, axis=0)
    k_ctx = kctx_ref[...]
    v_ctx = vctx_ref[...]

    span = 3 * BLOCK
    stacked = (ATT_GROUP * BLOCK, span)
    qpos = start + (lax.broadcasted_iota(jnp.int32, stacked, 0) & (BLOCK - 1))
    kpos = start - WINDOW + lax.broadcasted_iota(jnp.int32, stacked, 1)
    valid = (jnp.abs(qpos - kpos) <= WINDOW) & (kpos >= 0) & (kpos < DEC_SEQ)

    cos_q = table(cos_ref, n)
    sin_q = table(sin_ref, n)
    q_heads = []
    for pair in range(ATT_HEADS // 2):
        qr = _rope(q_ref[:, pair * LANES:(pair + 1) * LANES], cos_q, sin_q)
        q_heads += [_head(qr, 0).astype(BF16), _head(qr, 1).astype(BF16)]
    outs = []
    for kv in range(ATT_KV_HEADS):
        heads = range(kv * ATT_GROUP, (kv + 1) * ATT_GROUP)
        q = jnp.concatenate([q_heads[h] for h in heads], axis=0)
        sink = jnp.concatenate([jnp.full((BLOCK, 1), sink_ref[h], F32) for h in heads], axis=0)
        o = _attend(q, sink, [_head(k_ctx, kv).astype(BF16), _head(k_loc, kv).astype(BF16)],
                    [_head(v_ctx, kv).astype(BF16), _head(v_loc, kv).astype(BF16)], [None, valid])
        outs += [o[g * BLOCK:(g + 1) * BLOCK] for g in range(ATT_GROUP)]
    o_ref[...] = jnp.concatenate(outs, axis=-1).astype(o_ref.dtype)


def _attn_lat(ps, sink, k_ctx, v_ctx, cos_t, sin_t, y):
    n_blk = DEC_SEQ // BLOCK
    rb0 = T_CTX // BLOCK

    def kv_spec(off):
        return pl.BlockSpec(
            (BLOCK, 2 * LANES),
            lambda b, n: (rb0 + b * n_blk + jnp.clip(n + off, 0, n_blk - 1), PS_KV // (2 * LANES)))

    return pl.pallas_call(
        _attn_lat_kernel,
        grid=(DEC_BATCH, n_blk),
        in_specs=[
            pl.BlockSpec(memory_space=pltpu.SMEM),
            pl.BlockSpec((BLOCK, ATT_W), lambda b, n: (rb0 + b * n_blk + n, 0)),
            kv_spec(-1), kv_spec(0), kv_spec(1),
            pl.BlockSpec((None, PAST_LEN, LANES), lambda b, n: (b, 0, 0)),
            pl.BlockSpec((None, PAST_LEN, LANES), lambda b, n: (b, 0, 0)),
            pl.BlockSpec((DEC_SEQ, LANES), lambda b, n: (0, 0)),
            pl.BlockSpec((DEC_SEQ, LANES), lambda b, n: (0, 0)),
            pl.BlockSpec(memory_space=pl.ANY),
        ],
        out_specs=pl.BlockSpec((BLOCK, ATT_W), lambda b, n: (rb0 + b * n_blk + n, 0)),
        out_shape=jax.ShapeDtypeStruct((T_ALL, ATT_W), BF16),
        input_output_aliases={9: 0},
        compiler_params=_cparams("parallel", "parallel"),
        name="attn_lat",
    )(sink, ps, ps, ps, ps, k_ctx, v_ctx, cos_t, sin_t, y)


def _rope_tables():
    pos = np.arange(DEC_SEQ)
    n_freq = HEAD_DIM // 4
    inv = jnp.asarray(ROPE_THETA, F32) ** (-jnp.arange(n_freq, dtype=F32) / n_freq)
    row = jnp.asarray(pos // GRID_W, F32)
    colp = jnp.asarray(pos % GRID_W, F32)
    ang_r = row[:, None] * inv[None, :]
    ang_c = colp[:, None] * inv[None, :]
    cos = jnp.concatenate([jnp.cos(ang_r)] * 2 + [jnp.cos(ang_c)] * 2, axis=-1)
    sin = jnp.concatenate([-jnp.sin(ang_r), jnp.sin(ang_r), -jnp.sin(ang_c), jnp.sin(ang_c)], axis=-1)
    return jnp.tile(cos, (1, 2)), jnp.tile(sin, (1, 2))


def _split_dot(a, w_hi, w_lo):
    a_hi = a.astype(BF16)
    a_lo = (a - a_hi.astype(F32)).astype(BF16)
    return _mm(a_hi, w_hi) + _mm(a_lo, w_hi) + _mm(a_hi, w_lo)


def _route(logits):
    lane_i = lax.broadcasted_iota(jnp.int32, logits.shape, 1)
    lane = lane_i.astype(F32)
    big = jnp.float32(1 << 20)
    is_g = lane_i < N_EXPERT_GROUPS
    lg = jnp.where(is_g, logits, -jnp.inf)
    m_g = jnp.max(lg, axis=-1, keepdims=True)
    grp = jnp.min(jnp.where(lg == m_g, lane, big), axis=-1, keepdims=True)
    z_g = jnp.sum(jnp.where(is_g, jnp.exp(lg - m_g), 0.0), axis=-1, keepdims=True)
    p_grp = 1.0 / z_g

    e_idx = lane_i - ROUTE_E0
    e_grp = (e_idx >> 2).astype(F32)
    sel = (e_idx >= 0) & (e_idx < N_EXPERTS) & (e_grp == grp)
    le = jnp.where(sel, logits, -jnp.inf)
    m_e = jnp.max(le, axis=-1, keepdims=True)
    ex = jnp.where(sel, jnp.exp(le - m_e), 0.0)
    pe = ex / jnp.sum(ex, axis=-1, keepdims=True)
    pe = jnp.where(sel, pe, -1.0)
    v1 = jnp.max(pe, axis=-1, keepdims=True)
    i1 = jnp.min(jnp.where(pe == v1, lane, big), axis=-1, keepdims=True)
    pe2 = jnp.where(lane == i1, -1.0, pe)
    v2 = jnp.max(pe2, axis=-1, keepdims=True)
    i2 = jnp.min(jnp.where(pe2 == v2, lane, big), axis=-1, keepdims=True)
    tot = v1 + v2
    return (jnp.where(lane == i1, p_grp * (v1 / tot), 0.0)
            + jnp.where(lane == i2, p_grp * (v2 / tot), 0.0)
            + jnp.where(lane_i == LANES - 1, grp, 0.0))


HALO = 16


def _merge_kernel(gate_ref, conv_ref, cprev_ref, cnext_ref, yb_ref, yc_ref, x_ref, mod_ref, cw_ref,
                  wb_ref, wo_ref, g2_ref, wr_hi_ref, wr_lo_ref, br_ref,
                  x1_ref, h2_ref, rg_ref, wbb_ref, wob_ref):
    tm = TM_MERGE
    i = pl.program_id(0)

    @pl.when(i == 0)
    def _():
        wbb_ref[...] = wb_ref[...].astype(BF16)
        wob_ref[...] = wo_ref[...].astype(BF16)

    def gated(ref):
        return ref[:, CONV_W:2 * CONV_W].astype(F32) * ref[:, 2 * CONV_W:3 * CONV_W].astype(F32)

    a_b = conv_ref[:, 0:CONV_W].astype(F32)
    u = gated(conv_ref)
    u_before = gated(cprev_ref)[HALO - 1:HALO]
    u_after = gated(cnext_ref)[0:1]
    r = lax.broadcasted_iota(jnp.int32, (tm, 1), 0)
    g_row = i * tm + r
    seq_mask = jnp.where(g_row < T_CTX, SEQ - 1, DEC_SEQ - 1)
    first = (g_row & seq_mask) == 0
    last = ((g_row + 1) & seq_mask) == 0
    u_prev = jnp.where(r == 0, u_before, pltpu.roll(u, 1, 0))
    u_next = jnp.where(r == tm - 1, u_after, pltpu.roll(u, tm - 1, 0))
    u_prev = jnp.where(first, 0.0, u_prev)
    u_next = jnp.where(last, 0.0, u_next)
    y_a = (a_b * (u_prev * cw_ref[0:1, :] + u * cw_ref[1:2, :] + u_next * cw_ref[2:3, :])).astype(BF16)

    blocks = [slice(r0, r0 + MERGE_ROWS) for r0 in range(0, tm, MERGE_ROWS)]
    ys = [(y_a[rows], yb_ref[rows, :], yc_ref[rows, :]) for rows in blocks]
    branches = [[_mm(y, wbb_ref[j]) for j, y in enumerate(y3)] for y3 in ys]
    zs = []
    for rows, br3 in zip(blocks, branches):
        z = sum(gate_ref[rows, j * D_MODEL:(j + 1) * D_MODEL].astype(F32) * br3[j] for j in range(3))
        zs.append(z.astype(BF16))
    outs = [_mm(z, wob_ref[...]) for z in zs]
    h2s = []
    for rows, o in zip(blocks, outs):
        x1 = x_ref[rows, :] + mod_ref[2:3, :] * o
        x1_ref[rows, :] = x1
        h2 = _mod_norm(x1, g2_ref[...], mod_ref[...], 3)
        h2_ref[rows, :] = h2.astype(BF16)
        h2s.append(h2)
    logits = [_split_dot(h2, wr_hi_ref[...], wr_lo_ref[...]) + br_ref[...] for h2 in h2s]
    for rows, lg in zip(blocks, logits):
        rg_ref[rows, :] = _route(lg)


def _merge(pg, pa, y_b, y_c, x, mod_l, conv_w, layer, wb, wo, g2, wr_hi, wr_lo, br):
    tm = TM_MERGE
    n_tiles = T_ALL // tm
    hb = tm // HALO
    const = lambda shape: pl.BlockSpec(shape, lambda i: (0,) * len(shape))
    return pl.pallas_call(
        _merge_kernel,
        grid=(n_tiles,),
        in_specs=[
            pl.BlockSpec((tm, PG_W), lambda i: (i, 0)),
            pl.BlockSpec((tm, 3 * CONV_W), lambda i: (i, 0)),
            pl.BlockSpec((HALO, 3 * CONV_W), lambda i: (jnp.maximum(i * hb - 1, 0), 0)),
            pl.BlockSpec((HALO, 3 * CONV_W), lambda i: (jnp.minimum((i + 1) * hb, n_tiles * hb - 1), 0)),
            pl.BlockSpec((tm, GLA_W), lambda i: (i, 0)),
            pl.BlockSpec((tm, ATT_W), lambda i: (i, 0)),
            pl.BlockSpec((tm, D_MODEL), lambda i: (i, 0)),
            pl.BlockSpec((None, 6, D_MODEL), lambda i: (_cond_row(i, tm), 0, 0)),
            const((CONV_K, CONV_W)),
            pl.BlockSpec((None, 3, 512, D_MODEL), lambda i: (layer, 0, 0, 0)),
            pl.BlockSpec((None, D_MODEL, D_MODEL), lambda i: (layer, 0, 0)),
            const((1, D_MODEL)),
            const((D_MODEL, LANES)),
            const((D_MODEL, LANES)),
            const((1, LANES)),
        ],
        out_specs=[
            pl.BlockSpec((tm, D_MODEL), lambda i: (i, 0)),
            pl.BlockSpec((tm, D_MODEL), lambda i: (i, 0)),
            pl.BlockSpec((tm, LANES), lambda i: (i, 0)),
        ],
        out_shape=[
            jax.ShapeDtypeStruct((T_ALL, D_MODEL), F32),
            jax.ShapeDtypeStruct((T_ALL, D_MODEL), BF16),
            jax.ShapeDtypeStruct((T_ALL, LANES), F32),
        ],
        scratch_shapes=[pltpu.VMEM((3, 512, D_MODEL), BF16), pltpu.VMEM((D_MODEL, D_MODEL), BF16)],
        compiler_params=_cparams("arbitrary"),
        name="merge",
    )(pg, pa, pa, pa, y_b, y_c, x, mod_l, conv_w, wb, wo, g2, wr_hi, wr_lo, br)


SUB = 128


def _dispatch_tables(rg):
    n_tiles = T_ALL // TM
    grp = rg[:, LANES - 1].astype(jnp.int32).reshape(n_tiles, TM)
    hot = grp[..., None] == jnp.arange(N_EXPERT_GROUPS, dtype=jnp.int32)
    onehot = hot.astype(jnp.int32)
    cnt = onehot.sum(axis=1)
    start = jnp.cumsum(cnt, axis=1) - cnt
    first = start // SUB
    nblk = jnp.where(cnt > 0, (start + cnt + SUB - 1) // SUB - first, 0)
    before = jnp.tril(jnp.ones((TM, TM), BF16), -1)
    rank = jnp.einsum("ts,nsg->ntg", before, hot.astype(BF16), preferred_element_type=F32).astype(jnp.int32)
    slot = ((start[:, None, :] + rank) * onehot).sum(axis=-1)
    return (first.reshape(-1), nblk.reshape(-1), slot.reshape(n_tiles, 1, TM), slot.reshape(T_ALL, 1))


def _moe_kernel(bstart_ref, nblk_ref, h_ref, rg_ref, srow_ref, scol_ref, x1_ref, mod_ref, wg_ref, wu_ref,
                wd_ref, g_ref, *rest, final, tile0):
    if final:
        y_ref, hs_ref, gs_ref, os_ref = rest
    else:
        modn_ref, x2_ref, hn_ref, hs_ref, gs_ref, os_ref = rest
    i = pl.program_id(0)
    grp = pl.program_id(1)

    @pl.when(grp == 0)
    def _():
        slot_of_token = srow_ref[...]
        pm = jnp.where(lax.broadcasted_iota(jnp.int32, (TM, TM), 0) == slot_of_token, 1.0, 0.0)
        moved = _mm(pm.astype(BF16), jnp.concatenate([h_ref[...], *_split3(rg_ref[...])], axis=-1))
        hs_ref[...] = moved[:, :D_MODEL].astype(BF16)
        gs_ref[...] = sum(moved[:, D_MODEL + j * LANES:D_MODEL + (j + 1) * LANES] for j in range(3))
        os_ref[...] = jnp.zeros_like(os_ref)

    k = (tile0 + i) * N_EXPERT_GROUPS + grp
    first = bstart_ref[k]
    n_blocks = nblk_ref[k]

    def experts(block0, n_sub):
        n_rows = n_sub * SUB
        rows = pl.ds(pl.multiple_of(block0 * SUB, SUB), n_rows)
        lane = lax.broadcasted_iota(jnp.int32, (n_rows, LANES), 1)
        x = hs_ref[rows, :]
        gates = gs_ref[rows, :]
        acc = jnp.zeros((n_rows, D_MODEL), F32)
        for e in range(EXPERTS_PER_GROUP):
            ge = _mm(x, wg_ref[e])
            ue = _mm(x, wu_ref[e])
            w = jnp.sum(jnp.where(lane == ROUTE_E0 + grp * EXPERTS_PER_GROUP + e, gates, 0.0),
                        axis=-1, keepdims=True)
            hid = (ge * jax.nn.sigmoid(ge)) * ue * w
            acc = acc + _mm(hid.astype(BF16), wd_ref[e])
        os_ref[rows, :] = (os_ref[rows, :].astype(F32) + acc).astype(BF16)

    odd = n_blocks % 2 == 1
    n_pairs = jnp.where(odd & (n_blocks >= 3), (n_blocks - 3) // 2, n_blocks // 2)

    def pair(p, carry):
        experts(first + 2 * p, 2)
        return carry

    lax.fori_loop(0, n_pairs, pair, 0)

    @pl.when(odd & (n_blocks >= 3))
    def _():
        experts(first + n_blocks - 3, 3)

    @pl.when(n_blocks == 1)
    def _():
        experts(first, 1)

    @pl.when(grp == N_EXPERT_GROUPS - 1)
    def _():
        slot_col = scol_ref[...]
        pt = jnp.where(lax.broadcasted_iota(jnp.int32, (TM, TM), 1) == slot_col, 1.0, 0.0)
        moe = _mm(pt.astype(BF16), os_ref[...])
        x2 = x1_ref[...] + mod_ref[5:6, :] * moe
        if final:
            y_ref[...] = _rms(x2) * g_ref[...]
        else:
            x2_ref[...] = x2
            hn_ref[...] = _mod_norm(x2, g_ref[...], modn_ref[...], 0).astype(BF16)


def _moe(tables, h2, rg, x1, mod_l, layer, w_eg, w_eu, w_ed, g, mod_next, tile0, n_tiles):
    bstart, nblk, slot_row, slot_col = tables
    final = mod_next is None
    row = lambda i, e, *_: (tile0 + i, 0)
    mod_spec = lambda: pl.BlockSpec((None, 6, D_MODEL), lambda i, e, *_: (_cond_row(tile0 + i, TM), 0, 0))
    group = lambda i, e, *_: (layer, e, 0, 0)
    in_specs = [
        pl.BlockSpec((TM, D_MODEL), row),
        pl.BlockSpec((TM, LANES), row),
        pl.BlockSpec((None, 1, TM), lambda i, e, *_: (tile0 + i, 0, 0)),
        pl.BlockSpec((TM, 1), row),
        pl.BlockSpec((TM, D_MODEL), row),
        mod_spec(),
        pl.BlockSpec((None, EXPERTS_PER_GROUP, D_MODEL, D_EXPERT), group),
        pl.BlockSpec((None, EXPERTS_PER_GROUP, D_MODEL, D_EXPERT), group),
        pl.BlockSpec((None, EXPERTS_PER_GROUP, D_EXPERT, D_MODEL), group),
        pl.BlockSpec((1, D_MODEL), lambda i, e, *_: (0, 0)),
    ]
    args = [h2, rg, slot_row, slot_col, x1, mod_l, w_eg, w_eu, w_ed, g]
    if final:
        out_specs = pl.BlockSpec((TM, D_MODEL), lambda i, e, *_: (i, 0))
        out_shape = jax.ShapeDtypeStruct((n_tiles * TM, D_MODEL), F32)
    else:
        in_specs.append(mod_spec())
        args.append(mod_next)
        out_specs = [pl.BlockSpec((TM, D_MODEL), row)] * 2
        out_shape = [jax.ShapeDtypeStruct((T_ALL, D_MODEL), F32), jax.ShapeDtypeStruct((T_ALL, D_MODEL), BF16)]
    return pl.pallas_call(
        functools.partial(_moe_kernel, final=final, tile0=tile0),
        grid_spec=pltpu.PrefetchScalarGridSpec(
            num_scalar_prefetch=2,
            grid=(n_tiles, N_EXPERT_GROUPS),
            in_specs=in_specs,
            out_specs=out_specs,
            scratch_shapes=[pltpu.VMEM((TM, D_MODEL), BF16), pltpu.VMEM((TM, LANES), F32),
                            pltpu.VMEM((TM, D_MODEL), BF16)],
        ),
        out_shape=out_shape,
        compiler_params=_cparams("parallel", "arbitrary"),
        name="moe_final" if final else "moe",
    )(bstart, nblk, *args)


def _cache_kernel(*refs):
    ps_refs, (k_ref, v_ref) = refs[:DEPTH], refs[DEPTH:]
    n_seq = TM // SEQ
    for l, ps_ref in enumerate(ps_refs):
        k_ref[:, l] = ps_ref[:, :LANES].reshape(n_seq, SEQ, LANES)
        v_ref[:, l] = ps_ref[:, LANES:].reshape(n_seq, SEQ, LANES)


def _cache(ps_layers):
    n_seq = TM // SEQ
    out = pl.BlockSpec((n_seq, DEPTH, SEQ, LANES), lambda i: (i, 0, 0, 0))
    return pl.pallas_call(
        _cache_kernel,
        grid=(T_CTX // TM,),
        in_specs=[pl.BlockSpec((TM, 2 * LANES), lambda i: (i, PS_KV // (2 * LANES)))] * DEPTH,
        out_specs=[out, out],
        out_shape=[jax.ShapeDtypeStruct((BATCH, DEPTH, SEQ, LANES), F32)] * 2,
        compiler_params=_cparams("parallel"),
        name="cache",
    )(*ps_layers)


def _prep_gla_gate(w_gate):
    w = jnp.zeros((2, LANES, GLA_W), F32)
    w = w.at[0, 0:GLA_RANK].set(w_gate[0]).at[1, GLA_RANK:2 * GLA_RANK].set(w_gate[1])
    hi = w.astype(BF16)
    return hi, (w - hi.astype(F32)).astype(BF16)


def _prep_router(w_rg, b_rg, w_re, b_re):
    w = jnp.zeros((D_MODEL, LANES), F32)
    w = w.at[:, :N_EXPERT_GROUPS].set(w_rg).at[:, ROUTE_E0:ROUTE_E0 + N_EXPERTS].set(w_re)
    b = jnp.zeros((1, LANES), F32)
    b = b.at[0, :N_EXPERT_GROUPS].set(b_rg).at[0, ROUTE_E0:ROUTE_E0 + N_EXPERTS].set(b_re)
    w_hi = w.astype(BF16)
    w_lo = (w - w_hi.astype(F32)).astype(BF16)
    return w_hi, w_lo, b


def kernel(x_prompt, x_sample, state_gla, cache_k, cache_v, c, c_ctx, w_ada, b_ada, norm1_g, norm2_g, w_in,
           conv_w, gla_w_gate, gla_b_gate, gla_norm_g, attn_sink, w_branch, w_out, w_route_group,
           b_route_group, w_route_expert, b_route_expert, w_exp_gate, w_exp_up, w_exp_down, final_norm_g):
    cond = jnp.zeros((N_COND, D_MODEL), F32).at[0].set(c_ctx).at[1:1 + DEC_BATCH].set(c)
    mod = _modulation(cond, w_ada, b_ada).reshape(DEPTH, N_COND, 6, D_MODEL)
    cos_t, sin_t = _rope_tables()
    row = lambda v: v.reshape(1, -1)

    x, h = _prenorm(x_prompt.reshape(T_CTX, D_MODEL), x_sample.reshape(T_LAT, D_MODEL), mod[0], row(norm1_g[0]))
    states = None
    ps_layers = []
    y_prompt = y_sample = None
    w_t = jnp.swapaxes(w_in, 1, 2)
    experts = (w_exp_gate.astype(BF16), w_exp_up.astype(BF16), w_exp_down.astype(BF16))
    for l in range(DEPTH):
        pa = _proj(h, w_t, l, [(0, PA_W // 2)], 2, BF16, "proj_a", act="silu_tail")
        ps = _proj(h, w_t, l, [(W_IN_TQ, W_IN_GATE - W_IN_TQ), (W_IN_LR, LANES)], 1, F32, "proj_s")
        pg = _proj(h, w_t, l, [(W_IN_GATE, PG_W // 2)], 2, BF16, "proj_g", act="sigmoid")
        ps_layers.append(ps)

        wg_hi, wg_lo = _prep_gla_gate(gla_w_gate[l])
        bg = gla_b_gate[l].reshape(2, 1, GLA_W)
        ng = row(gla_norm_g[l])
        y_b, states = _gla_ctx(pa, ps, wg_hi, wg_lo, bg, ng, states, l)
        y_b = _gla_lat(pa, ps, wg_hi, wg_lo, bg, ng, state_gla[:, l], y_b)
        y_c = _attn_ctx(ps, attn_sink[l])
        y_c = _attn_lat(ps, attn_sink[l], cache_k[:, l].reshape(DEC_BATCH, PAST_LEN, LANES),
                        cache_v[:, l].reshape(DEC_BATCH, PAST_LEN, LANES), cos_t, sin_t, y_c)

        wr_hi, wr_lo, br = _prep_router(w_route_group[l], b_route_group[l], w_route_expert[l], b_route_expert[l])
        x1, h2, rg = _merge(pg, pa, y_b, y_c, x, mod[l], conv_w[l], l, w_branch, w_out, row(norm2_g[l]),
                            wr_hi, wr_lo, br)
        moe_in = (_dispatch_tables(rg), h2, rg, x1, mod[l], l, *experts)
        if l + 1 < DEPTH:
            x, h = _moe(*moe_in, row(norm1_g[l + 1]), mod[l + 1], 0, T_ALL // TM)
        else:
            gf = row(final_norm_g)
            y_prompt = _moe(*moe_in, gf, None, 0, T_CTX // TM)
            y_sample = _moe(*moe_in, gf, None, T_CTX // TM, T_LAT // TM)

    new_k, new_v = _cache(ps_layers)
    kv_shape = (BATCH, DEPTH, SEQ, ATT_KV_HEADS, HEAD_DIM)
    return (y_prompt.reshape(BATCH, SEQ, D_MODEL), y_sample.reshape(DEC_BATCH, DEC_SEQ, D_MODEL), states,
            new_k.reshape(kv_shape), new_v.reshape(kv_shape))
```

```python
import functools

import jax
import jax.numpy as jnp
import numpy as np
from jax import lax
from jax.experimental import pallas as pl
from jax.experimental.pallas import tpu as pltpu

F32 = jnp.float32
BF16 = jnp.bfloat16

D_MODEL = 1024
BATCH = 32
SEQ = 256
DEPTH = 2
DEC_BATCH = 2
DEC_SEQ = 2048
PAST_LEN = 512
GRID_W = 64
EPS = 1e-6
CONV_W = 512
CONV_K = 3
GLA_HEADS = 4
GLA_DK = 128
GLA_DV = 128
GLA_W = GLA_HEADS * GLA_DV
GLA_RANK = 16
GLA_TAU = 16.0
GLA_CHUNK = 64
ATT_HEADS = 8
ATT_KV_HEADS = 2
ATT_GROUP = ATT_HEADS // ATT_KV_HEADS
HEAD_DIM = 64
ATT_W = ATT_HEADS * HEAD_DIM
WINDOW = 128
BLOCK = 128
ROPE_THETA = 10000.0
N_EXPERT_GROUPS = 4
EXPERTS_PER_GROUP = 4
N_EXPERTS = 16
D_EXPERT = 256
NEG_INF = -1e30

T_CTX = BATCH * SEQ
T_LAT = DEC_BATCH * DEC_SEQ
T_ALL = T_CTX + T_LAT
N_COND = 8
LANES = 128

W_IN_LR = 3584
W_IN_TQ = 3616
W_IN_GATE = 4384
N_IN = 7456
PA_W = 3584
PA_GLA = 1536
PS_W = 896
PS_KV = 512
PS_LR = 768
PG_W = 3 * D_MODEL
ROUTE_E0 = N_EXPERT_GROUPS

TM = 1024
TM_MERGE = 512
MERGE_ROWS = 256
VMEM_LIMIT = 56 * 1024 * 1024


def _cparams(*sem):
    return pltpu.CompilerParams(dimension_semantics=sem, vmem_limit_bytes=VMEM_LIMIT)


def _cond_row(i, tm):
    n_ctx = T_CTX // tm
    per = DEC_SEQ // tm
    return jnp.where(i < n_ctx, 0, 1 + (i - n_ctx) // per)


def _mm(a, b):
    return jnp.dot(a, b, preferred_element_type=F32)


def _dot_t(a, b):
    return lax.dot_general(a, b, (((1,), (1,)), ((), ())), preferred_element_type=F32)


def _dot_ta(a, b):
    return lax.dot_general(a, b, (((0,), (0,)), ((), ())), preferred_element_type=F32)


def _rms(x):
    return x * lax.rsqrt(jnp.mean(x * x, axis=-1, keepdims=True) + EPS)


def _mod_norm(x, g, mod, shift_row):
    return _rms(x) * g * (1.0 + mod[shift_row + 1:shift_row + 2, :]) + mod[shift_row:shift_row + 1, :]


def _mod_kernel(c_ref, w_ref, b_ref, o_ref):
    c = c_ref[...]
    s = (c * jax.nn.sigmoid(c)).astype(BF16)
    o_ref[...] = _mm(s, w_ref[...].astype(BF16)) + b_ref[...]


def _modulation(cond, w_ada, b_ada):
    tn = 1536
    return pl.pallas_call(
        _mod_kernel,
        grid=(DEPTH, 6 * D_MODEL // tn),
        in_specs=[
            pl.BlockSpec((N_COND, D_MODEL), lambda l, j: (0, 0)),
            pl.BlockSpec((None, D_MODEL, tn), lambda l, j: (l, 0, j)),
            pl.BlockSpec((None, 1, tn), lambda l, j: (l, 0, j)),
        ],
        out_specs=pl.BlockSpec((None, N_COND, tn), lambda l, j: (l, 0, j)),
        out_shape=jax.ShapeDtypeStruct((DEPTH, N_COND, 6 * D_MODEL), F32),
        compiler_params=_cparams("parallel", "parallel"),
        name="modulation",
    )(cond, w_ada, b_ada.reshape(DEPTH, 1, 6 * D_MODEL))


def _prenorm_kernel(xp_ref, xs_ref, mod_ref, g_ref, x_ref, h_ref):
    x = jnp.where(pl.program_id(0) < T_CTX // TM, xp_ref[...], xs_ref[...])
    x_ref[...] = x
    h_ref[...] = _mod_norm(x, g_ref[...], mod_ref[...], 0).astype(BF16)


def _prenorm(xp, xs, mod_l, g):
    n_ctx = T_CTX // TM
    return pl.pallas_call(
        _prenorm_kernel,
        grid=(T_ALL // TM,),
        in_specs=[
            pl.BlockSpec((TM, D_MODEL), lambda i: (jnp.minimum(i, n_ctx - 1), 0)),
            pl.BlockSpec((TM, D_MODEL), lambda i: (jnp.maximum(i - n_ctx, 0), 0)),
            pl.BlockSpec((None, 6, D_MODEL), lambda i: (_cond_row(i, TM), 0, 0)),
            pl.BlockSpec((1, D_MODEL), lambda i: (0, 0)),
        ],
        out_specs=[pl.BlockSpec((TM, D_MODEL), lambda i: (i, 0))] * 2,
        out_shape=[jax.ShapeDtypeStruct((T_ALL, D_MODEL), F32), jax.ShapeDtypeStruct((T_ALL, D_MODEL), BF16)],
        compiler_params=_cparams("parallel"),
        name="prenorm",
    )(xp, xs, mod_l, g)


def _proj_kernel(h_ref, *refs, act):
    w_refs, (o_ref, wb_ref) = refs[:-2], refs[-2:]

    @pl.when(pl.program_id(1) == 0)
    def _():
        col = 0
        for w_ref in w_refs:
            n = w_ref.shape[0]
            wb_ref[:, col:col + n] = w_ref[...].T.astype(BF16)
            col += n

    def sigmoid(t):
        return 0.5 * jnp.tanh(0.5 * t) + 0.5

    y = _mm(h_ref[...], wb_ref[...])
    if act == "sigmoid":
        y = sigmoid(y)
    if act == "silu_tail":
        last = pl.program_id(0) == pl.num_programs(0) - 1
        tail = y[:, -GLA_W:]
        o_ref[:, :-GLA_W] = y[:, :-GLA_W].astype(o_ref.dtype)
        o_ref[:, -GLA_W:] = jnp.where(last, tail * sigmoid(tail), tail).astype(o_ref.dtype)
    else:
        o_ref[...] = y.astype(o_ref.dtype)


def _proj(h, w_t, layer, pieces, n_tiles, out_dtype, name, act=None):
    tn = sum(n for _, n in pieces)

    def w_spec(c0, n):
        return pl.BlockSpec((pl.Squeezed(), pl.Element(n), pl.Element(D_MODEL)),
                            lambda j, i: (layer, pl.multiple_of(c0 + j * tn, 8), 0))

    return pl.pallas_call(
        functools.partial(_proj_kernel, act=act),
        grid=(n_tiles, T_ALL // TM),
        in_specs=[pl.BlockSpec((TM, D_MODEL), lambda j, i: (i, 0))] + [w_spec(c0, n) for c0, n in pieces],
        out_specs=pl.BlockSpec((TM, tn), lambda j, i: (i, j)),
        out_shape=jax.ShapeDtypeStruct((T_ALL, n_tiles * tn), out_dtype),
        scratch_shapes=[pltpu.VMEM((D_MODEL, tn), BF16)],
        compiler_params=_cparams("parallel", "arbitrary"),
        name=name,
    )(h, *([w_t] * len(pieces)))


GLA_GROUP = 256
GLA_CTX_SEQS = 2


def _log_sigmoid(z):
    return jnp.minimum(z, 0.0) - jnp.log(1.0 + jnp.exp(-jnp.abs(z)))


def _split3(x):
    hi = x.astype(BF16)
    r1 = x - hi.astype(F32)
    mid = r1.astype(BF16)
    lo = (r1 - mid.astype(F32)).astype(BF16)
    return hi, mid, lo


def _gla_keep(d):
    ri = lax.broadcasted_iota(jnp.int32, (GLA_GROUP, GLA_GROUP), 0)
    ci = lax.broadcasted_iota(jnp.int32, (GLA_GROUP, GLA_GROUP), 1)
    if d == 0:
        return (ci <= ri) & (ci >= (ri & ~(GLA_CHUNK - 1)))
    return (ci >= ri) & (ci <= (ri | (GLA_CHUNK - 1)))


def _gla_group(load, lr, wg_hi, wg_lo, bg, states, d, keep):
    C = GLA_CHUNK
    nc = GLA_GROUP // C
    lr_hi = lr.astype(BF16)
    lr_lo = (lr - lr_hi.astype(F32)).astype(BF16)
    z = _mm(lr_hi, wg_hi) + _mm(lr_lo, wg_hi) + _mm(lr_hi, wg_lo) + bg
    la = _log_sigmoid(z) / GLA_TAU
    tri = jnp.where(keep, 1.0, 0.0).astype(BF16)
    la_hi, la_mid, la_lo = _split3(la)
    b_all = _mm(tri, la_hi) + _mm(tri, la_mid) + _mm(tri, la_lo)
    edge = C - 1 if d == 0 else 0
    order = range(nc) if d == 0 else range(nc - 1, -1, -1)
    row_chunk = lax.broadcasted_iota(jnp.int32, (GLA_GROUP, GLA_DK), 0) // C
    in_chunk = [row_chunk == c for c in range(nc)]
    outs, new_states = [], []
    for h, st in enumerate(states):
        b = b_all[:, h * GLA_DK:(h + 1) * GLA_DK]
        b_last = [b[c * C + edge:c * C + edge + 1, :] for c in range(nc)]
        bl = jnp.concatenate([jnp.broadcast_to(t, (C, GLA_DK)) for t in b_last], axis=0)
        k = load(1, h)
        q_in = load(0, h) * (GLA_DK ** -0.5) * jnp.exp(b)
        k_in = (k * jnp.exp(-b)).astype(BF16)
        k_end = k * jnp.exp(bl - b)
        vb = load(2, h).astype(BF16)
        att = jnp.where(keep, _dot_t(q_in.astype(BF16), k_in), 0.0)
        o = _mm(att.astype(BF16), vb)
        k_spread = jnp.concatenate([jnp.where(in_chunk[c], k_end, 0.0) for c in range(nc)], axis=-1)
        q_spread = jnp.concatenate([jnp.where(in_chunk[c], q_in, 0.0) for c in range(nc)], axis=-1)
        ds_t = _dot_ta(vb, k_spread.astype(BF16))
        starts = [None] * nc
        for c in order:
            starts[c] = st
            st = st * jnp.exp(b_last[c]) + ds_t[:, c * GLA_DK:(c + 1) * GLA_DK]
        o = o + _dot_t(q_spread.astype(BF16), jnp.concatenate(starts, axis=-1).astype(BF16))
        outs.append(o)
        new_states.append(st)
    return outs, new_states


def _gla_loader(qkv_refs, rows):
    def load(i, h):
        return qkv_refs[i][rows, h * GLA_DK:(h + 1) * GLA_DK].astype(F32)
    return load


def _gla_finish(o, r_act, ng):
    o = o * lax.rsqrt(jnp.mean(o * o, axis=-1, keepdims=True) + EPS)
    return o * ng * r_act


def _gla_ctx_kernel(q_ref, k_ref, v_ref, r_ref, lr_ref, wgh_ref, wgl_ref, bg_ref, ng_ref, *rest):
    y_ref, s_ref = rest[-2:]
    zero = jnp.zeros((GLA_DV, GLA_DK), F32)
    keep = [_gla_keep(0), _gla_keep(1)]
    for s in range(GLA_CTX_SEQS):
        rows = slice(s * SEQ, (s + 1) * SEQ)
        load = _gla_loader((q_ref, k_ref, v_ref), rows)
        lr = lr_ref[rows, :]
        o_dir = []
        for d in range(2):
            outs, sts = _gla_group(load, lr, wgh_ref[d], wgl_ref[d], bg_ref[d], [zero] * GLA_HEADS, d,
                                   keep[d])
            o_dir.append(outs)
            for h in range(GLA_HEADS):
                s_ref[s, d, h] = sts[h].T
        for h in range(GLA_HEADS):
            cs = slice(h * GLA_DV, (h + 1) * GLA_DV)
            y = _gla_finish(o_dir[0][h] + o_dir[1][h], r_ref[rows, cs].astype(F32), ng_ref[:, cs])
            y_ref[rows, cs] = y.astype(y_ref.dtype)


def _gla_lat_kernel(q_ref, k_ref, v_ref, r_ref, lr_ref, wgh_ref, wgl_ref, bg_ref, ng_ref, s0_ref, yin_ref,
                    y_ref, of_ref, ob_ref):
    del yin_ref
    n_groups = DEC_SEQ // GLA_GROUP
    keep_f, keep_b = _gla_keep(0), _gla_keep(1)

    def body(g, carry):
        st_f, st_b = carry
        rf = pl.ds(pl.multiple_of(g * GLA_GROUP, GLA_GROUP), GLA_GROUP)
        rb = pl.ds(pl.multiple_of((n_groups - 1 - g) * GLA_GROUP, GLA_GROUP), GLA_GROUP)
        qkv = (q_ref, k_ref, v_ref)
        outs, st_f = _gla_group(_gla_loader(qkv, rf), lr_ref[rf, :], wgh_ref[0], wgl_ref[0], bg_ref[0],
                                list(st_f), 0, keep_f)
        of_ref[rf, :] = jnp.concatenate(outs, axis=-1)
        outs, st_b = _gla_group(_gla_loader(qkv, rb), lr_ref[rb, :], wgh_ref[1], wgl_ref[1], bg_ref[1],
                                list(st_b), 1, keep_b)
        ob_ref[rb, :] = jnp.concatenate(outs, axis=-1)
        return tuple(st_f), tuple(st_b)

    init = tuple(tuple(s0_ref[d, h].T for h in range(GLA_HEADS)) for d in range(2))
    lax.fori_loop(0, n_groups, body, init)
    for h in range(GLA_HEADS):
        cs = slice(h * GLA_DV, (h + 1) * GLA_DV)
        y = _gla_finish(of_ref[:, cs] + ob_ref[:, cs], r_ref[:, cs].astype(F32), ng_ref[:, cs])
        y_ref[:, cs] = y.astype(y_ref.dtype)


def _gla_ctx(pa, ps, wg_hi, wg_lo, bg, ng, states_prev, layer):
    cb = PA_GLA // GLA_W
    rows = GLA_CTX_SEQS * SEQ
    const = lambda shape: pl.BlockSpec(shape, lambda s: (0,) * len(shape))
    in_specs = [pl.BlockSpec((rows, GLA_W), lambda s, j=j: (s, cb + j)) for j in range(4)] + [
        pl.BlockSpec((rows, LANES), lambda s: (s, PS_LR // LANES)),
        const((2, LANES, GLA_W)), const((2, LANES, GLA_W)), const((2, 1, GLA_W)), const((1, GLA_W))]
    args = [pa, pa, pa, pa, ps, wg_hi, wg_lo, bg, ng]
    aliases = {}
    if states_prev is not None:
        in_specs.append(pl.BlockSpec(memory_space=pl.ANY))
        args.append(states_prev)
        aliases = {len(args) - 1: 1}
    return pl.pallas_call(
        _gla_ctx_kernel,
        grid=(BATCH // GLA_CTX_SEQS,),
        in_specs=in_specs,
        out_specs=[pl.BlockSpec((rows, GLA_W), lambda s: (s, 0)),
                   pl.BlockSpec((GLA_CTX_SEQS, None, 2, GLA_HEADS, GLA_DK, GLA_DV),
                                lambda s: (s, layer, 0, 0, 0, 0))],
        out_shape=[jax.ShapeDtypeStruct((T_ALL, GLA_W), BF16),
                   jax.ShapeDtypeStruct((BATCH, DEPTH, 2, GLA_HEADS, GLA_DK, GLA_DV), F32)],
        input_output_aliases=aliases,
        compiler_params=_cparams("parallel"),
        name="gla_ctx",
    )(*args)


def _gla_lat(pa, ps, wg_hi, wg_lo, bg, ng, s0, y):
    rb0 = T_CTX // DEC_SEQ
    cb = PA_GLA // GLA_W
    const = lambda shape: pl.BlockSpec(shape, lambda s: (0,) * len(shape))
    return pl.pallas_call(
        _gla_lat_kernel,
        grid=(DEC_BATCH,),
        in_specs=[pl.BlockSpec((DEC_SEQ, GLA_W), lambda s, j=j: (rb0 + s, cb + j)) for j in range(4)] + [
            pl.BlockSpec((DEC_SEQ, LANES), lambda s: (rb0 + s, PS_LR // LANES)),
            const((2, LANES, GLA_W)), const((2, LANES, GLA_W)), const((2, 1, GLA_W)), const((1, GLA_W)),
            pl.BlockSpec((None, 2, GLA_HEADS, GLA_DK, GLA_DV), lambda s: (s, 0, 0, 0, 0)),
            pl.BlockSpec(memory_space=pl.ANY)],
        out_specs=pl.BlockSpec((DEC_SEQ, GLA_W), lambda s: (rb0 + s, 0)),
        out_shape=jax.ShapeDtypeStruct((T_ALL, GLA_W), BF16),
        scratch_shapes=[pltpu.VMEM((DEC_SEQ, GLA_W), F32), pltpu.VMEM((DEC_SEQ, GLA_W), F32)],
        input_output_aliases={10: 0},
        compiler_params=_cparams("parallel"),
        name="gla_lat",
    )(pa, pa, pa, pa, ps, wg_hi, wg_lo, bg, ng, s0, y)


LOG2E = 1.4426950408889634
Q_SCALE = HEAD_DIM ** -0.5 * LOG2E


def _attend(q, sink, k_parts, v_parts, masks):
    scores = []
    for k, mask in zip(k_parts, masks):
        s = _dot_t(q, k)
        if mask is not None:
            s = jnp.where(mask, s, NEG_INF)
        scores.append(s)
    sink2 = sink * LOG2E
    m = jnp.broadcast_to(sink2, (q.shape[0], 1)).astype(F32)
    for s in scores:
        m = jnp.maximum(m, jnp.max(s, axis=-1, keepdims=True))
    den = jnp.exp2(sink2 - m)
    o = None
    for s, v in zip(scores, v_parts):
        e = jnp.exp2(s - m)
        den = den + jnp.sum(e, axis=-1, keepdims=True)
        pv = _mm(e.astype(BF16), v)
        o = pv if o is None else o + pv
    return o / den


def _head(a, j):
    return a[:, j * HEAD_DIM:(j + 1) * HEAD_DIM]


ATT_ROWS_CTX = 128


def _attn_ctx_kernel(sink_ref, q_ref, kv_ref, *rest):
    o_ref = rest[-1]
    q = q_ref[...] * Q_SCALE
    kv_all = kv_ref[...]
    ks = [_head(kv_all, kv).astype(BF16) for kv in range(ATT_KV_HEADS)]
    vs = [_head(kv_all, ATT_KV_HEADS + kv).astype(BF16) for kv in range(ATT_KV_HEADS)]
    outs = []
    for h in range(ATT_HEADS):
        kv = h // ATT_GROUP
        qh = _head(q, h).astype(BF16)
        blocks = [_attend(qh[r:r + ATT_ROWS_CTX], sink_ref[h], [ks[kv]], [vs[kv]], [None])
                  for r in range(0, SEQ, ATT_ROWS_CTX)]
        outs.append(jnp.concatenate(blocks, axis=0))
    o_ref[...] = jnp.concatenate(outs, axis=-1).astype(o_ref.dtype)


def _attn_ctx(ps, sink):
    return pl.pallas_call(
        _attn_ctx_kernel,
        grid=(BATCH,),
        in_specs=[
            pl.BlockSpec(memory_space=pltpu.SMEM),
            pl.BlockSpec((SEQ, ATT_W), lambda s: (s, 0)),
            pl.BlockSpec((SEQ, 2 * LANES), lambda s: (s, PS_KV // (2 * LANES))),
        ],
        out_specs=pl.BlockSpec((SEQ, ATT_W), lambda s: (s, 0)),
        out_shape=jax.ShapeDtypeStruct((T_ALL, ATT_W), BF16),
        compiler_params=_cparams("parallel"),
        name="attn_ctx",
    )(sink, ps, ps)


def _rope(x, cos, sin_signed):
    lane = lax.broadcasted_iota(jnp.int32, x.shape, 1)
    partner = jnp.where((lane & 31) < 16, pltpu.roll(x, LANES - 16, 1), pltpu.roll(x, 16, 1))
    return x * cos + partner * sin_signed


def _attn_lat_kernel(sink_ref, q_ref, kvp_ref, kvc_ref, kvn_ref, kctx_ref, vctx_ref, cos_ref, sin_ref, yin_ref,
                     o_ref):
    del yin_ref
    n = pl.program_id(1)
    n_blk = DEC_SEQ // BLOCK
    start = n * BLOCK

    def table(ref, blk):
        blk = jnp.clip(blk, 0, n_blk - 1)
        return ref[pl.ds(pl.multiple_of(blk * BLOCK, BLOCK), BLOCK), :]

    k_loc, v_loc = [], []
    for off, ref in ((-1, kvp_ref), (0, kvc_ref), (1, kvn_ref)):
        k_loc.append(_rope(ref[:, :LANES], table(cos_ref, n + off), table(sin_ref, n + off)))
        v_loc.append(ref[:, LANES:])
    k_loc = jnp.concatenate(k_loc, axis=0)
    v_loc = jnp.concatenate(v_loc, axis=0)
    k_ctx = kctx_ref[...]
    v_ctx = vctx_ref[...]

    span = 3 * BLOCK
    stacked = (ATT_GROUP * BLOCK, span)
    qpos = start + (lax.broadcasted_iota(jnp.int32, stacked, 0) & (BLOCK - 1))
    kpos = start - WINDOW + lax.broadcasted_iota(jnp.int32, stacked, 1)
    valid = (jnp.abs(qpos - kpos) <= WINDOW) & (kpos >= 0) & (kpos < DEC_SEQ)

    cos_q = table(cos_ref, n)
    sin_q = table(sin_ref, n)
    q_heads = []
    for pair in range(ATT_HEADS // 2):
        qr = _rope(q_ref[:, pair * LANES:(pair + 1) * LANES], cos_q, sin_q) * Q_SCALE
        q_heads += [_head(qr, 0).astype(BF16), _head(qr, 1).astype(BF16)]
    outs = []
    for kv in range(ATT_KV_HEADS):
        heads = range(kv * ATT_GROUP, (kv + 1) * ATT_GROUP)
        q = jnp.concatenate([q_heads[h] for h in heads], axis=0)
        sink = jnp.concatenate([jnp.full((BLOCK, 1), sink_ref[h], F32) for h in heads], axis=0)
        o = _attend(q, sink, [_head(k_ctx, kv).astype(BF16), _head(k_loc, kv).astype(BF16)],
                    [_head(v_ctx, kv).astype(BF16), _head(v_loc, kv).astype(BF16)], [None, valid])
        outs += [o[g * BLOCK:(g + 1) * BLOCK] for g in range(ATT_GROUP)]
    o_ref[...] = jnp.concatenate(outs, axis=-1).astype(o_ref.dtype)


def _attn_lat(ps, sink, k_ctx, v_ctx, cos_t, sin_t, y):
    n_blk = DEC_SEQ // BLOCK
    rb0 = T_CTX // BLOCK

    def kv_spec(off):
        return pl.BlockSpec(
            (BLOCK, 2 * LANES),
            lambda b, n: (rb0 + b * n_blk + jnp.clip(n + off, 0, n_blk - 1), PS_KV // (2 * LANES)))

    return pl.pallas_call(
        _attn_lat_kernel,
        grid=(DEC_BATCH, n_blk),
        in_specs=[
            pl.BlockSpec(memory_space=pltpu.SMEM),
            pl.BlockSpec((BLOCK, ATT_W), lambda b, n: (rb0 + b * n_blk + n, 0)),
            kv_spec(-1), kv_spec(0), kv_spec(1),
            pl.BlockSpec((None, PAST_LEN, LANES), lambda b, n: (b, 0, 0)),
            pl.BlockSpec((None, PAST_LEN, LANES), lambda b, n: (b, 0, 0)),
            pl.BlockSpec((DEC_SEQ, LANES), lambda b, n: (0, 0)),
            pl.BlockSpec((DEC_SEQ, LANES), lambda b, n: (0, 0)),
            pl.BlockSpec(memory_space=pl.ANY),
        ],
        out_specs=pl.BlockSpec((BLOCK, ATT_W), lambda b, n: (rb0 + b * n_blk + n, 0)),
        out_shape=jax.ShapeDtypeStruct((T_ALL, ATT_W), BF16),
        input_output_aliases={9: 0},
        compiler_params=_cparams("parallel", "parallel"),
        name="attn_lat",
    )(sink, ps, ps, ps, ps, k_ctx, v_ctx, cos_t, sin_t, y)


def _rope_tables():
    pos = np.arange(DEC_SEQ)
    n_freq = HEAD_DIM // 4
    inv = jnp.asarray(ROPE_THETA, F32) ** (-jnp.arange(n_freq, dtype=F32) / n_freq)
    row = jnp.asarray(pos // GRID_W, F32)
    colp = jnp.asarray(pos % GRID_W, F32)
    ang_r = row[:, None] * inv[None, :]
    ang_c = colp[:, None] * inv[None, :]
    cos = jnp.concatenate([jnp.cos(ang_r)] * 2 + [jnp.cos(ang_c)] * 2, axis=-1)
    sin = jnp.concatenate([-jnp.sin(ang_r), jnp.sin(ang_r), -jnp.sin(ang_c), jnp.sin(ang_c)], axis=-1)
    return jnp.tile(cos, (1, 2)), jnp.tile(sin, (1, 2))


def _split_dot(a, w_hi, w_lo):
    a_hi = a.astype(BF16)
    a_lo = (a - a_hi.astype(F32)).astype(BF16)
    return _mm(a_hi, w_hi) + _mm(a_lo, w_hi) + _mm(a_hi, w_lo)


def _route(logits):
    lane_i = lax.broadcasted_iota(jnp.int32, logits.shape, 1)
    lane = lane_i.astype(F32)
    big = jnp.float32(1 << 20)
    is_g = lane_i < N_EXPERT_GROUPS
    lg = jnp.where(is_g, logits, -jnp.inf)
    m_g = jnp.max(lg, axis=-1, keepdims=True)
    grp = jnp.min(jnp.where(lg == m_g, lane, big), axis=-1, keepdims=True)
    z_g = jnp.sum(jnp.where(is_g, jnp.exp(lg - m_g), 0.0), axis=-1, keepdims=True)
    p_grp = 1.0 / z_g

    e_idx = lane_i - ROUTE_E0
    e_grp = (e_idx >> 2).astype(F32)
    sel = (e_idx >= 0) & (e_idx < N_EXPERTS) & (e_grp == grp)
    le = jnp.where(sel, logits, -jnp.inf)
    m_e = jnp.max(le, axis=-1, keepdims=True)
    ex = jnp.where(sel, jnp.exp(le - m_e), 0.0)
    pe = ex / jnp.sum(ex, axis=-1, keepdims=True)
    pe = jnp.where(sel, pe, -1.0)
    v1 = jnp.max(pe, axis=-1, keepdims=True)
    i1 = jnp.min(jnp.where(pe == v1, lane, big), axis=-1, keepdims=True)
    pe2 = jnp.where(lane == i1, -1.0, pe)
    v2 = jnp.max(pe2, axis=-1, keepdims=True)
    i2 = jnp.min(jnp.where(pe2 == v2, lane, big), axis=-1, keepdims=True)
    tot = v1 + v2
    return (jnp.where(lane == i1, p_grp * (v1 / tot), 0.0)
            + jnp.where(lane == i2, p_grp * (v2 / tot), 0.0)
            + jnp.where(lane_i == LANES - 1, grp, 0.0))


HALO = 16


def _merge_kernel(gate_ref, conv_ref, cprev_ref, cnext_ref, yb_ref, yc_ref, x_ref, mod_ref, cw_ref,
                  wb_ref, wo_ref, g2_ref, wr_hi_ref, wr_lo_ref, br_ref,
                  x1_ref, h2_ref, rg_ref, wbb_ref, wob_ref):
    tm = TM_MERGE
    i = pl.program_id(0)

    @pl.when(i == 0)
    def _():
        wbb_ref[...] = wb_ref[...].astype(BF16)
        wob_ref[...] = wo_ref[...].astype(BF16)

    def gated(ref):
        return ref[:, CONV_W:2 * CONV_W].astype(F32) * ref[:, 2 * CONV_W:3 * CONV_W].astype(F32)

    a_b = conv_ref[:, 0:CONV_W].astype(F32)
    u = gated(conv_ref)
    u_before = gated(cprev_ref)[HALO - 1:HALO]
    u_after = gated(cnext_ref)[0:1]
    r = lax.broadcasted_iota(jnp.int32, (tm, 1), 0)
    g_row = i * tm + r
    seq_mask = jnp.where(g_row < T_CTX, SEQ - 1, DEC_SEQ - 1)
    first = (g_row & seq_mask) == 0
    last = ((g_row + 1) & seq_mask) == 0
    u_prev = jnp.where(r == 0, u_before, pltpu.roll(u, 1, 0))
    u_next = jnp.where(r == tm - 1, u_after, pltpu.roll(u, tm - 1, 0))
    u_prev = jnp.where(first, 0.0, u_prev)
    u_next = jnp.where(last, 0.0, u_next)
    y_a = (a_b * (u_prev * cw_ref[0:1, :] + u * cw_ref[1:2, :] + u_next * cw_ref[2:3, :])).astype(BF16)

    blocks = [slice(r0, r0 + MERGE_ROWS) for r0 in range(0, tm, MERGE_ROWS)]
    ys = [(y_a[rows], yb_ref[rows, :], yc_ref[rows, :]) for rows in blocks]
    branches = [[_mm(y, wbb_ref[j]) for j, y in enumerate(y3)] for y3 in ys]
    zs = []
    for rows, br3 in zip(blocks, branches):
        z = sum(gate_ref[rows, j * D_MODEL:(j + 1) * D_MODEL].astype(F32) * br3[j] for j in range(3))
        zs.append(z.astype(BF16))
    outs = [_mm(z, wob_ref[...]) for z in zs]
    h2s = []
    for rows, o in zip(blocks, outs):
        x1 = x_ref[rows, :] + mod_ref[2:3, :] * o
        x1_ref[rows, :] = x1
        h2 = _mod_norm(x1, g2_ref[...], mod_ref[...], 3)
        h2_ref[rows, :] = h2.astype(BF16)
        h2s.append(h2)
    logits = [_split_dot(h2, wr_hi_ref[...], wr_lo_ref[...]) + br_ref[...] for h2 in h2s]
    for rows, lg in zip(blocks, logits):
        rg_ref[rows, :] = _route(lg)


def _merge(pg, pa, y_b, y_c, x, mod_l, conv_w, layer, wb, wo, g2, wr_hi, wr_lo, br):
    tm = TM_MERGE
    n_tiles = T_ALL // tm
    hb = tm // HALO
    const = lambda shape: pl.BlockSpec(shape, lambda i: (0,) * len(shape))
    return pl.pallas_call(
        _merge_kernel,
        grid=(n_tiles,),
        in_specs=[
            pl.BlockSpec((tm, PG_W), lambda i: (i, 0)),
            pl.BlockSpec((tm, 3 * CONV_W), lambda i: (i, 0)),
            pl.BlockSpec((HALO, 3 * CONV_W), lambda i: (jnp.maximum(i * hb - 1, 0), 0)),
            pl.BlockSpec((HALO, 3 * CONV_W), lambda i: (jnp.minimum((i + 1) * hb, n_tiles * hb - 1), 0)),
            pl.BlockSpec((tm, GLA_W), lambda i: (i, 0)),
            pl.BlockSpec((tm, ATT_W), lambda i: (i, 0)),
            pl.BlockSpec((tm, D_MODEL), lambda i: (i, 0)),
            pl.BlockSpec((None, 6, D_MODEL), lambda i: (_cond_row(i, tm), 0, 0)),
            const((CONV_K, CONV_W)),
            pl.BlockSpec((None, 3, 512, D_MODEL), lambda i: (layer, 0, 0, 0)),
            pl.BlockSpec((None, D_MODEL, D_MODEL), lambda i: (layer, 0, 0)),
            const((1, D_MODEL)),
            const((D_MODEL, LANES)),
            const((D_MODEL, LANES)),
            const((1, LANES)),
        ],
        out_specs=[
            pl.BlockSpec((tm, D_MODEL), lambda i: (i, 0)),
            pl.BlockSpec((tm, D_MODEL), lambda i: (i, 0)),
            pl.BlockSpec((tm, LANES), lambda i: (i, 0)),
        ],
        out_shape=[
            jax.ShapeDtypeStruct((T_ALL, D_MODEL), F32),
            jax.ShapeDtypeStruct((T_ALL, D_MODEL), BF16),
            jax.ShapeDtypeStruct((T_ALL, LANES), F32),
        ],
        scratch_shapes=[pltpu.VMEM((3, 512, D_MODEL), BF16), pltpu.VMEM((D_MODEL, D_MODEL), BF16)],
        compiler_params=_cparams("arbitrary"),
        name="merge",
    )(pg, pa, pa, pa, y_b, y_c, x, mod_l, conv_w, wb, wo, g2, wr_hi, wr_lo, br)


SUB = 128


def _dispatch_tables(rg):
    n_tiles = T_ALL // TM
    grp = rg[:, LANES - 1].astype(jnp.int32).reshape(n_tiles, TM)
    hot = grp[..., None] == jnp.arange(N_EXPERT_GROUPS, dtype=jnp.int32)
    onehot = hot.astype(jnp.int32)
    cnt = onehot.sum(axis=1)
    start = jnp.cumsum(cnt, axis=1) - cnt
    first = start // SUB
    nblk = jnp.where(cnt > 0, (start + cnt + SUB - 1) // SUB - first, 0)
    before = jnp.tril(jnp.ones((TM, TM), BF16), -1)
    rank = jnp.einsum("ts,nsg->ntg", before, hot.astype(BF16), preferred_element_type=F32).astype(jnp.int32)
    slot = ((start[:, None, :] + rank) * onehot).sum(axis=-1)
    return (first.reshape(-1), nblk.reshape(-1), slot.reshape(n_tiles, 1, TM), slot.reshape(T_ALL, 1))


def _moe_kernel(bstart_ref, nblk_ref, h_ref, rg_ref, srow_ref, scol_ref, x1_ref, mod_ref, wg_ref, wu_ref,
                wd_ref, g_ref, *rest, final, tile0):
    if final:
        y_ref, hs_ref, gs_ref, os_ref = rest
    else:
        modn_ref, x2_ref, hn_ref, hs_ref, gs_ref, os_ref = rest
    i = pl.program_id(0)
    grp = pl.program_id(1)

    @pl.when(grp == 0)
    def _():
        slot_of_token = srow_ref[...]
        pm = jnp.where(lax.broadcasted_iota(jnp.int32, (TM, TM), 0) == slot_of_token, 1.0, 0.0)
        moved = _mm(pm.astype(BF16), jnp.concatenate([h_ref[...], *_split3(rg_ref[...])], axis=-1))
        hs_ref[...] = moved[:, :D_MODEL].astype(BF16)
        gs_ref[...] = sum(moved[:, D_MODEL + j * LANES:D_MODEL + (j + 1) * LANES] for j in range(3))
        os_ref[...] = jnp.zeros_like(os_ref)

    k = (tile0 + i) * N_EXPERT_GROUPS + grp
    first = bstart_ref[k]
    n_blocks = nblk_ref[k]

    def experts(block0, n_sub):
        n_rows = n_sub * SUB
        rows = pl.ds(pl.multiple_of(block0 * SUB, SUB), n_rows)
        lane = lax.broadcasted_iota(jnp.int32, (n_rows, LANES), 1)
        x = hs_ref[rows, :]
        gates = gs_ref[rows, :]
        acc = jnp.zeros((n_rows, D_MODEL), F32)
        for e in range(EXPERTS_PER_GROUP):
            ge = _mm(x, wg_ref[e])
            ue = _mm(x, wu_ref[e])
            w = jnp.sum(jnp.where(lane == ROUTE_E0 + grp * EXPERTS_PER_GROUP + e, gates, 0.0),
                        axis=-1, keepdims=True)
            hid = (ge * jax.nn.sigmoid(ge)) * ue * w
            acc = acc + _mm(hid.astype(BF16), wd_ref[e])
        os_ref[rows, :] = (os_ref[rows, :].astype(F32) + acc).astype(BF16)

    odd = n_blocks % 2 == 1
    n_pairs = jnp.where(odd & (n_blocks >= 3), (n_blocks - 3) // 2, n_blocks // 2)

    def pair(p, carry):
        experts(first + 2 * p, 2)
        return carry

    lax.fori_loop(0, n_pairs, pair, 0)

    @pl.when(odd & (n_blocks >= 3))
    def _():
        experts(first + n_blocks - 3, 3)

    @pl.when(n_blocks == 1)
    def _():
        experts(first, 1)

    @pl.when(grp == N_EXPERT_GROUPS - 1)
    def _():
        slot_col = scol_ref[...]
        pt = jnp.where(lax.broadcasted_iota(jnp.int32, (TM, TM), 1) == slot_col, 1.0, 0.0)
        moe = _mm(pt.astype(BF16), os_ref[...])
        x2 = x1_ref[...] + mod_ref[5:6, :] * moe
        if final:
            y_ref[...] = _rms(x2) * g_ref[...]
        else:
            x2_ref[...] = x2
            hn_ref[...] = _mod_norm(x2, g_ref[...], modn_ref[...], 0).astype(BF16)


def _moe(tables, h2, rg, x1, mod_l, layer, w_eg, w_eu, w_ed, g, mod_next, tile0, n_tiles):
    bstart, nblk, slot_row, slot_col = tables
    final = mod_next is None
    row = lambda i, e, *_: (tile0 + i, 0)
    mod_spec = lambda: pl.BlockSpec((None, 6, D_MODEL), lambda i, e, *_: (_cond_row(tile0 + i, TM), 0, 0))
    group = lambda i, e, *_: (layer, e, 0, 0)
    in_specs = [
        pl.BlockSpec((TM, D_MODEL), row),
        pl.BlockSpec((TM, LANES), row),
        pl.BlockSpec((None, 1, TM), lambda i, e, *_: (tile0 + i, 0, 0)),
        pl.BlockSpec((TM, 1), row),
        pl.BlockSpec((TM, D_MODEL), row),
        mod_spec(),
        pl.BlockSpec((None, EXPERTS_PER_GROUP, D_MODEL, D_EXPERT), group),
        pl.BlockSpec((None, EXPERTS_PER_GROUP, D_MODEL, D_EXPERT), group),
        pl.BlockSpec((None, EXPERTS_PER_GROUP, D_EXPERT, D_MODEL), group),
        pl.BlockSpec((1, D_MODEL), lambda i, e, *_: (0, 0)),
    ]
    args = [h2, rg, slot_row, slot_col, x1, mod_l, w_eg, w_eu, w_ed, g]
    if final:
        out_specs = pl.BlockSpec((TM, D_MODEL), lambda i, e, *_: (i, 0))
        out_shape = jax.ShapeDtypeStruct((n_tiles * TM, D_MODEL), F32)
    else:
        in_specs.append(mod_spec())
        args.append(mod_next)
        out_specs = [pl.BlockSpec((TM, D_MODEL), row)] * 2
        out_shape = [jax.ShapeDtypeStruct((T_ALL, D_MODEL), F32), jax.ShapeDtypeStruct((T_ALL, D_MODEL), BF16)]
    return pl.pallas_call(
        functools.partial(_moe_kernel, final=final, tile0=tile0),
        grid_spec=pltpu.PrefetchScalarGridSpec(
            num_scalar_prefetch=2,
            grid=(n_tiles, N_EXPERT_GROUPS),
            in_specs=in_specs,
            out_specs=out_specs,
            scratch_shapes=[pltpu.VMEM((TM, D_MODEL), BF16), pltpu.VMEM((TM, LANES), F32),
                            pltpu.VMEM((TM, D_MODEL), BF16)],
        ),
        out_shape=out_shape,
        compiler_params=_cparams("parallel", "arbitrary"),
        name="moe_final" if final else "moe",
    )(bstart, nblk, *args)


def _cache_kernel(*refs):
    ps_refs, (k_ref, v_ref) = refs[:DEPTH], refs[DEPTH:]
    n_seq = TM // SEQ
    for l, ps_ref in enumerate(ps_refs):
        k_ref[:, l] = ps_ref[:, :LANES].reshape(n_seq, SEQ, LANES)
        v_ref[:, l] = ps_ref[:, LANES:].reshape(n_seq, SEQ, LANES)


def _cache(ps_layers):
    n_seq = TM // SEQ
    out = pl.BlockSpec((n_seq, DEPTH, SEQ, LANES), lambda i: (i, 0, 0, 0))
    return pl.pallas_call(
        _cache_kernel,
        grid=(T_CTX // TM,),
        in_specs=[pl.BlockSpec((TM, 2 * LANES), lambda i: (i, PS_KV // (2 * LANES)))] * DEPTH,
        out_specs=[out, out],
        out_shape=[jax.ShapeDtypeStruct((BATCH, DEPTH, SEQ, LANES), F32)] * 2,
        compiler_params=_cparams("parallel"),
        name="cache",
    )(*ps_layers)


def _prep_gla_gate(w_gate):
    w = jnp.zeros((2, LANES, GLA_W), F32)
    w = w.at[0, 0:GLA_RANK].set(w_gate[0]).at[1, GLA_RANK:2 * GLA_RANK].set(w_gate[1])
    hi = w.astype(BF16)
    return hi, (w - hi.astype(F32)).astype(BF16)


def _prep_router(w_rg, b_rg, w_re, b_re):
    w = jnp.zeros((D_MODEL, LANES), F32)
    w = w.at[:, :N_EXPERT_GROUPS].set(w_rg).at[:, ROUTE_E0:ROUTE_E0 + N_EXPERTS].set(w_re)
    b = jnp.zeros((1, LANES), F32)
    b = b.at[0, :N_EXPERT_GROUPS].set(b_rg).at[0, ROUTE_E0:ROUTE_E0 + N_EXPERTS].set(b_re)
    w_hi = w.astype(BF16)
    w_lo = (w - w_hi.astype(F32)).astype(BF16)
    return w_hi, w_lo, b


def kernel(x_prompt, x_sample, state_gla, cache_k, cache_v, c, c_ctx, w_ada, b_ada, norm1_g, norm2_g, w_in,
           conv_w, gla_w_gate, gla_b_gate, gla_norm_g, attn_sink, w_branch, w_out, w_route_group,
           b_route_group, w_route_expert, b_route_expert, w_exp_gate, w_exp_up, w_exp_down, final_norm_g):
    cond = jnp.zeros((N_COND, D_MODEL), F32).at[0].set(c_ctx).at[1:1 + DEC_BATCH].set(c)
    mod = _modulation(cond, w_ada, b_ada).reshape(DEPTH, N_COND, 6, D_MODEL)
    cos_t, sin_t = _rope_tables()
    row = lambda v: v.reshape(1, -1)

    x, h = _prenorm(x_prompt.reshape(T_CTX, D_MODEL), x_sample.reshape(T_LAT, D_MODEL), mod[0], row(norm1_g[0]))
    states = None
    ps_layers = []
    y_prompt = y_sample = None
    w_t = jnp.swapaxes(w_in, 1, 2)
    experts = (w_exp_gate.astype(BF16), w_exp_up.astype(BF16), w_exp_down.astype(BF16))
    for l in range(DEPTH):
        pa = _proj(h, w_t, l, [(0, PA_W // 2)], 2, BF16, "proj_a", act="silu_tail")
        ps = _proj(h, w_t, l, [(W_IN_TQ, W_IN_GATE - W_IN_TQ), (W_IN_LR, LANES)], 1, F32, "proj_s")
        pg = _proj(h, w_t, l, [(W_IN_GATE, PG_W // 2)], 2, BF16, "proj_g", act="sigmoid")
        ps_layers.append(ps)

        wg_hi, wg_lo = _prep_gla_gate(gla_w_gate[l])
        bg = gla_b_gate[l].reshape(2, 1, GLA_W)
        ng = row(gla_norm_g[l])
        y_b, states = _gla_ctx(pa, ps, wg_hi, wg_lo, bg, ng, states, l)
        y_b = _gla_lat(pa, ps, wg_hi, wg_lo, bg, ng, state_gla[:, l], y_b)
        y_c = _attn_ctx(ps, attn_sink[l])
        y_c = _attn_lat(ps, attn_sink[l], cache_k[:, l].reshape(DEC_BATCH, PAST_LEN, LANES),
                        cache_v[:, l].reshape(DEC_BATCH, PAST_LEN, LANES), cos_t, sin_t, y_c)

        wr_hi, wr_lo, br = _prep_router(w_route_group[l], b_route_group[l], w_route_expert[l], b_route_expert[l])
        x1, h2, rg = _merge(pg, pa, y_b, y_c, x, mod[l], conv_w[l], l, w_branch, w_out, row(norm2_g[l]),
                            wr_hi, wr_lo, br)
        moe_in = (_dispatch_tables(rg), h2, rg, x1, mod[l], l, *experts)
        if l + 1 < DEPTH:
            x, h = _moe(*moe_in, row(norm1_g[l + 1]), mod[l + 1], 0, T_ALL // TM)
        else:
            gf = row(final_norm_g)
            y_prompt = _moe(*moe_in, gf, None, 0, T_CTX // TM)
            y_sample = _moe(*moe_in, gf, None, T_CTX // TM, T_LAT // TM)

    new_k, new_v = _cache(ps_layers)
    kv_shape = (BATCH, DEPTH, SEQ, ATT_KV_HEADS, HEAD_DIM)
    return (y_prompt.reshape(BATCH, SEQ, D_MODEL), y_sample.reshape(DEC_BATCH, DEC_SEQ, D_MODEL), states,
            new_k.reshape(kv_shape), new_v.reshape(kv_shape))
```

```python
import functools

import jax
import jax.numpy as jnp
import numpy as np
from jax import lax
from jax.experimental import pallas as pl
from jax.experimental.pallas import tpu as pltpu

F32 = jnp.float32
BF16 = jnp.bfloat16

D_MODEL = 1024
BATCH = 32
SEQ = 256
DEPTH = 2
DEC_BATCH = 2
DEC_SEQ = 2048
PAST_LEN = 512
GRID_W = 64
EPS = 1e-6
CONV_W = 512
CONV_K = 3
GLA_HEADS = 4
GLA_DK = 128
GLA_DV = 128
GLA_W = GLA_HEADS * GLA_DV
GLA_RANK = 16
GLA_TAU = 16.0
GLA_CHUNK = 64
ATT_HEADS = 8
ATT_KV_HEADS = 2
ATT_GROUP = ATT_HEADS // ATT_KV_HEADS
HEAD_DIM = 64
ATT_W = ATT_HEADS * HEAD_DIM
WINDOW = 128
BLOCK = 128
ROPE_THETA = 10000.0
N_EXPERT_GROUPS = 4
EXPERTS_PER_GROUP = 4
N_EXPERTS = 16
D_EXPERT = 256
NEG_INF = -1e30

T_CTX = BATCH * SEQ
T_LAT = DEC_BATCH * DEC_SEQ
T_ALL = T_CTX + T_LAT
N_COND = 8
LANES = 128

W_IN_LR = 3584
W_IN_TQ = 3616
W_IN_GATE = 4384
N_IN = 7456
PA_W = 3584
PA_GLA = 1536
PS_W = 896
PS_KV = 512
PS_LR = 768
PG_W = 3 * D_MODEL
ROUTE_E0 = N_EXPERT_GROUPS

TM = 1024
TM_MERGE = 512
MERGE_ROWS = 256
VMEM_LIMIT = 56 * 1024 * 1024


def _cparams(*sem):
    return pltpu.CompilerParams(dimension_semantics=sem, vmem_limit_bytes=VMEM_LIMIT)


def _cond_row(i, tm):
    n_ctx = T_CTX // tm
    per = DEC_SEQ // tm
    return jnp.where(i < n_ctx, 0, 1 + (i - n_ctx) // per)


def _mm(a, b):
    return jnp.dot(a, b, preferred_element_type=F32)


def _dot_t(a, b):
    return lax.dot_general(a, b, (((1,), (1,)), ((), ())), preferred_element_type=F32)


def _dot_ta(a, b):
    return lax.dot_general(a, b, (((0,), (0,)), ((), ())), preferred_element_type=F32)


def _rms(x):
    return x * lax.rsqrt(jnp.mean(x * x, axis=-1, keepdims=True) + EPS)


def _mod_norm(x, g, mod, shift_row):
    return _rms(x) * g * (1.0 + mod[shift_row + 1:shift_row + 2, :]) + mod[shift_row:shift_row + 1, :]


def _mod_kernel(c_ref, w_ref, b_ref, o_ref):
    c = c_ref[...]
    s = (c * jax.nn.sigmoid(c)).astype(BF16)
    o_ref[...] = _mm(s, w_ref[...].astype(BF16)) + b_ref[...]


def _modulation(cond, w_ada, b_ada):
    tn = 1536
    return pl.pallas_call(
        _mod_kernel,
        grid=(DEPTH, 6 * D_MODEL // tn),
        in_specs=[
            pl.BlockSpec((N_COND, D_MODEL), lambda l, j: (0, 0)),
            pl.BlockSpec((None, D_MODEL, tn), lambda l, j: (l, 0, j)),
            pl.BlockSpec((None, 1, tn), lambda l, j: (l, 0, j)),
        ],
        out_specs=pl.BlockSpec((None, N_COND, tn), lambda l, j: (l, 0, j)),
        out_shape=jax.ShapeDtypeStruct((DEPTH, N_COND, 6 * D_MODEL), F32),
        compiler_params=_cparams("parallel", "parallel"),
        name="modulation",
    )(cond, w_ada, b_ada.reshape(DEPTH, 1, 6 * D_MODEL))


def _split_x_specs(tm):
    n_ctx = T_CTX // tm
    return [pl.BlockSpec((tm, D_MODEL), lambda i, *_: (jnp.minimum(i, n_ctx - 1), 0)),
            pl.BlockSpec((tm, D_MODEL), lambda i, *_: (jnp.maximum(i - n_ctx, 0), 0))], n_ctx


def _prenorm_kernel(xp_ref, xs_ref, mod_ref, g_ref, h_ref):
    x = jnp.where(pl.program_id(0) < T_CTX // TM, xp_ref[...], xs_ref[...])
    h_ref[...] = _mod_norm(x, g_ref[...], mod_ref[...], 0).astype(BF16)


def _prenorm(xp, xs, mod_l, g):
    x_specs, _ = _split_x_specs(TM)
    return pl.pallas_call(
        _prenorm_kernel,
        grid=(T_ALL // TM,),
        in_specs=x_specs + [
            pl.BlockSpec((None, 6, D_MODEL), lambda i: (_cond_row(i, TM), 0, 0)),
            pl.BlockSpec((1, D_MODEL), lambda i: (0, 0)),
        ],
        out_specs=pl.BlockSpec((TM, D_MODEL), lambda i: (i, 0)),
        out_shape=jax.ShapeDtypeStruct((T_ALL, D_MODEL), BF16),
        compiler_params=_cparams("parallel"),
        name="prenorm",
    )(xp, xs, mod_l, g)


def _proj_kernel(h_ref, *refs, act):
    w_refs, (o_ref, wb_ref) = refs[:-2], refs[-2:]

    @pl.when(pl.program_id(1) == 0)
    def _():
        col = 0
        for w_ref in w_refs:
            n = w_ref.shape[0]
            wb_ref[:, col:col + n] = w_ref[...].T.astype(BF16)
            col += n

    def sigmoid(t):
        return 0.5 * jnp.tanh(0.5 * t) + 0.5

    y = _mm(h_ref[...], wb_ref[...])
    if act == "sigmoid":
        y = sigmoid(y)
    if act == "silu_tail":
        last = pl.program_id(0) == pl.num_programs(0) - 1
        tail = y[:, -GLA_W:]
        o_ref[:, :-GLA_W] = y[:, :-GLA_W].astype(o_ref.dtype)
        o_ref[:, -GLA_W:] = jnp.where(last, tail * sigmoid(tail), tail).astype(o_ref.dtype)
    else:
        o_ref[...] = y.astype(o_ref.dtype)


def _proj(h, w_t, layer, pieces, n_tiles, out_dtype, name, act=None):
    tn = sum(n for _, n in pieces)

    def w_spec(c0, n):
        return pl.BlockSpec((pl.Squeezed(), pl.Element(n), pl.Element(D_MODEL)),
                            lambda j, i: (layer, pl.multiple_of(c0 + j * tn, 8), 0))

    return pl.pallas_call(
        functools.partial(_proj_kernel, act=act),
        grid=(n_tiles, T_ALL // TM),
        in_specs=[pl.BlockSpec((TM, D_MODEL), lambda j, i: (i, 0))] + [w_spec(c0, n) for c0, n in pieces],
        out_specs=pl.BlockSpec((TM, tn), lambda j, i: (i, j)),
        out_shape=jax.ShapeDtypeStruct((T_ALL, n_tiles * tn), out_dtype),
        scratch_shapes=[pltpu.VMEM((D_MODEL, tn), BF16)],
        compiler_params=_cparams("parallel", "arbitrary"),
        name=name,
    )(h, *([w_t] * len(pieces)))


GLA_GROUP = 256
GLA_CTX_SEQS = 2


def _log_sigmoid(z):
    return jnp.minimum(z, 0.0) - jnp.log(1.0 + jnp.exp(-jnp.abs(z)))


def _split3(x):
    hi = x.astype(BF16)
    r1 = x - hi.astype(F32)
    mid = r1.astype(BF16)
    lo = (r1 - mid.astype(F32)).astype(BF16)
    return hi, mid, lo


def _gla_keep(d):
    ri = lax.broadcasted_iota(jnp.int32, (GLA_GROUP, GLA_GROUP), 0)
    ci = lax.broadcasted_iota(jnp.int32, (GLA_GROUP, GLA_GROUP), 1)
    if d == 0:
        return (ci <= ri) & (ci >= (ri & ~(GLA_CHUNK - 1)))
    return (ci >= ri) & (ci <= (ri | (GLA_CHUNK - 1)))


def _gla_group(load, lr, wg_hi, wg_lo, bg, states, d, keep):
    C = GLA_CHUNK
    nc = GLA_GROUP // C
    lr_hi = lr.astype(BF16)
    lr_lo = (lr - lr_hi.astype(F32)).astype(BF16)
    z = _mm(lr_hi, wg_hi) + _mm(lr_lo, wg_hi) + _mm(lr_hi, wg_lo) + bg
    la = _log_sigmoid(z) / GLA_TAU
    tri = jnp.where(keep, 1.0, 0.0).astype(BF16)
    la_hi, la_mid, la_lo = _split3(la)
    b_all = _mm(tri, la_hi) + _mm(tri, la_mid) + _mm(tri, la_lo)
    edge = C - 1 if d == 0 else 0
    order = range(nc) if d == 0 else range(nc - 1, -1, -1)
    row_chunk = lax.broadcasted_iota(jnp.int32, (GLA_GROUP, GLA_DK), 0) // C
    in_chunk = [row_chunk == c for c in range(nc)]
    outs, new_states = [], []
    for h, st in enumerate(states):
        b = b_all[:, h * GLA_DK:(h + 1) * GLA_DK]
        b_last = [b[c * C + edge:c * C + edge + 1, :] for c in range(nc)]
        bl = jnp.concatenate([jnp.broadcast_to(t, (C, GLA_DK)) for t in b_last], axis=0)
        k = load(1, h)
        q_in = load(0, h) * (GLA_DK ** -0.5) * jnp.exp(b)
        k_in = (k * jnp.exp(-b)).astype(BF16)
        k_end = k * jnp.exp(bl - b)
        vb = load(2, h).astype(BF16)
        att = jnp.where(keep, _dot_t(q_in.astype(BF16), k_in), 0.0)
        o = _mm(att.astype(BF16), vb)
        k_spread = jnp.concatenate([jnp.where(in_chunk[c], k_end, 0.0) for c in range(nc)], axis=-1)
        q_spread = jnp.concatenate([jnp.where(in_chunk[c], q_in, 0.0) for c in range(nc)], axis=-1)
        ds_t = _dot_ta(vb, k_spread.astype(BF16))
        starts = [None] * nc
        for c in order:
            starts[c] = st
            st = st * jnp.exp(b_last[c]) + ds_t[:, c * GLA_DK:(c + 1) * GLA_DK]
        o = o + _dot_t(q_spread.astype(BF16), jnp.concatenate(starts, axis=-1).astype(BF16))
        outs.append(o)
        new_states.append(st)
    return outs, new_states


def _gla_loader(qkv_refs, rows):
    def load(i, h):
        return qkv_refs[i][rows, h * GLA_DK:(h + 1) * GLA_DK].astype(F32)
    return load


def _gla_finish(o, r_act, ng):
    o = o * lax.rsqrt(jnp.mean(o * o, axis=-1, keepdims=True) + EPS)
    return o * ng * r_act


def _gla_ctx_kernel(q_ref, k_ref, v_ref, r_ref, lr_ref, wgh_ref, wgl_ref, bg_ref, ng_ref, *rest):
    y_ref, s_ref = rest[-2:]
    zero = jnp.zeros((GLA_DV, GLA_DK), F32)
    keep = [_gla_keep(0), _gla_keep(1)]
    for s in range(GLA_CTX_SEQS):
        rows = slice(s * SEQ, (s + 1) * SEQ)
        load = _gla_loader((q_ref, k_ref, v_ref), rows)
        lr = lr_ref[rows, :]
        o_dir = []
        for d in range(2):
            outs, sts = _gla_group(load, lr, wgh_ref[d], wgl_ref[d], bg_ref[d], [zero] * GLA_HEADS, d,
                                   keep[d])
            o_dir.append(outs)
            for h in range(GLA_HEADS):
                s_ref[s, d, h] = sts[h].T
        for h in range(GLA_HEADS):
            cs = slice(h * GLA_DV, (h + 1) * GLA_DV)
            y = _gla_finish(o_dir[0][h] + o_dir[1][h], r_ref[rows, cs].astype(F32), ng_ref[:, cs])
            y_ref[rows, cs] = y.astype(y_ref.dtype)


def _gla_lat_kernel(q_ref, k_ref, v_ref, r_ref, lr_ref, wgh_ref, wgl_ref, bg_ref, ng_ref, s0_ref, yin_ref,
                    y_ref, of_ref, ob_ref):
    del yin_ref
    n_groups = DEC_SEQ // GLA_GROUP
    keep_f, keep_b = _gla_keep(0), _gla_keep(1)

    def body(g, carry):
        st_f, st_b = carry
        rf = pl.ds(pl.multiple_of(g * GLA_GROUP, GLA_GROUP), GLA_GROUP)
        rb = pl.ds(pl.multiple_of((n_groups - 1 - g) * GLA_GROUP, GLA_GROUP), GLA_GROUP)
        qkv = (q_ref, k_ref, v_ref)
        outs, st_f = _gla_group(_gla_loader(qkv, rf), lr_ref[rf, :], wgh_ref[0], wgl_ref[0], bg_ref[0],
                                list(st_f), 0, keep_f)
        of_ref[rf, :] = jnp.concatenate(outs, axis=-1)
        outs, st_b = _gla_group(_gla_loader(qkv, rb), lr_ref[rb, :], wgh_ref[1], wgl_ref[1], bg_ref[1],
                                list(st_b), 1, keep_b)
        ob_ref[rb, :] = jnp.concatenate(outs, axis=-1)
        return tuple(st_f), tuple(st_b)

    init = tuple(tuple(s0_ref[d, h].T for h in range(GLA_HEADS)) for d in range(2))
    lax.fori_loop(0, n_groups, body, init)
    for h in range(GLA_HEADS):
        cs = slice(h * GLA_DV, (h + 1) * GLA_DV)
        y = _gla_finish(of_ref[:, cs] + ob_ref[:, cs], r_ref[:, cs].astype(F32), ng_ref[:, cs])
        y_ref[:, cs] = y.astype(y_ref.dtype)


def _gla_ctx(pa, ps, wg_hi, wg_lo, bg, ng, states_prev, layer):
    cb = PA_GLA // GLA_W
    rows = GLA_CTX_SEQS * SEQ
    const = lambda shape: pl.BlockSpec(shape, lambda s: (0,) * len(shape))
    in_specs = [pl.BlockSpec((rows, GLA_W), lambda s, j=j: (s, cb + j)) for j in range(4)] + [
        pl.BlockSpec((rows, LANES), lambda s: (s, PS_LR // LANES)),
        const((2, LANES, GLA_W)), const((2, LANES, GLA_W)), const((2, 1, GLA_W)), const((1, GLA_W))]
    args = [pa, pa, pa, pa, ps, wg_hi, wg_lo, bg, ng]
    aliases = {}
    if states_prev is not None:
        in_specs.append(pl.BlockSpec(memory_space=pl.ANY))
        args.append(states_prev)
        aliases = {len(args) - 1: 1}
    return pl.pallas_call(
        _gla_ctx_kernel,
        grid=(BATCH // GLA_CTX_SEQS,),
        in_specs=in_specs,
        out_specs=[pl.BlockSpec((rows, GLA_W), lambda s: (s, 0)),
                   pl.BlockSpec((GLA_CTX_SEQS, None, 2, GLA_HEADS, GLA_DK, GLA_DV),
                                lambda s: (s, layer, 0, 0, 0, 0))],
        out_shape=[jax.ShapeDtypeStruct((T_ALL, GLA_W), BF16),
                   jax.ShapeDtypeStruct((BATCH, DEPTH, 2, GLA_HEADS, GLA_DK, GLA_DV), F32)],
        input_output_aliases=aliases,
        compiler_params=_cparams("parallel"),
        name="gla_ctx",
    )(*args)


def _gla_lat(pa, ps, wg_hi, wg_lo, bg, ng, s0, y):
    rb0 = T_CTX // DEC_SEQ
    cb = PA_GLA // GLA_W
    const = lambda shape: pl.BlockSpec(shape, lambda s: (0,) * len(shape))
    return pl.pallas_call(
        _gla_lat_kernel,
        grid=(DEC_BATCH,),
        in_specs=[pl.BlockSpec((DEC_SEQ, GLA_W), lambda s, j=j: (rb0 + s, cb + j)) for j in range(4)] + [
            pl.BlockSpec((DEC_SEQ, LANES), lambda s: (rb0 + s, PS_LR // LANES)),
            const((2, LANES, GLA_W)), const((2, LANES, GLA_W)), const((2, 1, GLA_W)), const((1, GLA_W)),
            pl.BlockSpec((None, 2, GLA_HEADS, GLA_DK, GLA_DV), lambda s: (s, 0, 0, 0, 0)),
            pl.BlockSpec(memory_space=pl.ANY)],
        out_specs=pl.BlockSpec((DEC_SEQ, GLA_W), lambda s: (rb0 + s, 0)),
        out_shape=jax.ShapeDtypeStruct((T_ALL, GLA_W), BF16),
        scratch_shapes=[pltpu.VMEM((DEC_SEQ, GLA_W), F32), pltpu.VMEM((DEC_SEQ, GLA_W), F32)],
        input_output_aliases={10: 0},
        compiler_params=_cparams("parallel"),
        name="gla_lat",
    )(pa, pa, pa, pa, ps, wg_hi, wg_lo, bg, ng, s0, y)


LOG2E = 1.4426950408889634
Q_SCALE = HEAD_DIM ** -0.5 * LOG2E


def _attend(q, sink, k_parts, v_parts, masks):
    scores = []
    for k, mask in zip(k_parts, masks):
        s = _dot_t(q, k)
        if mask is not None:
            s = jnp.where(mask, s, NEG_INF)
        scores.append(s)
    sink2 = sink * LOG2E
    m = jnp.broadcast_to(sink2, (q.shape[0], 1)).astype(F32)
    for s in scores:
        m = jnp.maximum(m, jnp.max(s, axis=-1, keepdims=True))
    den = jnp.exp2(sink2 - m)
    o = None
    for s, v in zip(scores, v_parts):
        e = jnp.exp2(s - m)
        den = den + jnp.sum(e, axis=-1, keepdims=True)
        pv = _mm(e.astype(BF16), v)
        o = pv if o is None else o + pv
    return o / den


def _head(a, j):
    return a[:, j * HEAD_DIM:(j + 1) * HEAD_DIM]


ATT_ROWS_CTX = 128


def _attn_ctx_kernel(sink_ref, q_ref, kv_ref, *rest):
    o_ref = rest[-1]
    q = q_ref[...] * Q_SCALE
    kv_all = kv_ref[...]
    ks = [_head(kv_all, kv).astype(BF16) for kv in range(ATT_KV_HEADS)]
    vs = [_head(kv_all, ATT_KV_HEADS + kv).astype(BF16) for kv in range(ATT_KV_HEADS)]
    outs = []
    for h in range(ATT_HEADS):
        kv = h // ATT_GROUP
        qh = _head(q, h).astype(BF16)
        blocks = [_attend(qh[r:r + ATT_ROWS_CTX], sink_ref[h], [ks[kv]], [vs[kv]], [None])
                  for r in range(0, SEQ, ATT_ROWS_CTX)]
        outs.append(jnp.concatenate(blocks, axis=0))
    o_ref[...] = jnp.concatenate(outs, axis=-1).astype(o_ref.dtype)


def _attn_ctx(ps, sink):
    return pl.pallas_call(
        _attn_ctx_kernel,
        grid=(BATCH,),
        in_specs=[
            pl.BlockSpec(memory_space=pltpu.SMEM),
            pl.BlockSpec((SEQ, ATT_W), lambda s: (s, 0)),
            pl.BlockSpec((SEQ, 2 * LANES), lambda s: (s, PS_KV // (2 * LANES))),
        ],
        out_specs=pl.BlockSpec((SEQ, ATT_W), lambda s: (s, 0)),
        out_shape=jax.ShapeDtypeStruct((T_ALL, ATT_W), BF16),
        compiler_params=_cparams("parallel"),
        name="attn_ctx",
    )(sink, ps, ps)


def _rope(x, cos, sin_signed):
    lane = lax.broadcasted_iota(jnp.int32, x.shape, 1)
    partner = jnp.where((lane & 31) < 16, pltpu.roll(x, LANES - 16, 1), pltpu.roll(x, 16, 1))
    return x * cos + partner * sin_signed


def _attn_lat_kernel(sink_ref, q_ref, kvp_ref, kvc_ref, kvn_ref, kctx_ref, vctx_ref, cos_ref, sin_ref, yin_ref,
                     o_ref):
    del yin_ref
    n = pl.program_id(1)
    n_blk = DEC_SEQ // BLOCK
    start = n * BLOCK

    def table(ref, blk):
        blk = jnp.clip(blk, 0, n_blk - 1)
        return ref[pl.ds(pl.multiple_of(blk * BLOCK, BLOCK), BLOCK), :]

    k_loc, v_loc = [], []
    for off, ref in ((-1, kvp_ref), (0, kvc_ref), (1, kvn_ref)):
        k_loc.append(_rope(ref[:, :LANES], table(cos_ref, n + off), table(sin_ref, n + off)))
        v_loc.append(ref[:, LANES:])
    k_loc = jnp.concatenate(k_loc, axis=0)
    v_loc = jnp.concatenate(v_loc, axis=0)
    k_ctx = kctx_ref[...]
    v_ctx = vctx_ref[...]

    span = 3 * BLOCK
    stacked = (ATT_GROUP * BLOCK, span)
    qpos = start + (lax.broadcasted_iota(jnp.int32, stacked, 0) & (BLOCK - 1))
    kpos = start - WINDOW + lax.broadcasted_iota(jnp.int32, stacked, 1)
    valid = (jnp.abs(qpos - kpos) <= WINDOW) & (kpos >= 0) & (kpos < DEC_SEQ)

    cos_q = table(cos_ref, n)
    sin_q = table(sin_ref, n)
    q_heads = []
    for pair in range(ATT_HEADS // 2):
        qr = _rope(q_ref[:, pair * LANES:(pair + 1) * LANES], cos_q, sin_q) * Q_SCALE
        q_heads += [_head(qr, 0).astype(BF16), _head(qr, 1).astype(BF16)]
    outs = []
    for kv in range(ATT_KV_HEADS):
        heads = range(kv * ATT_GROUP, (kv + 1) * ATT_GROUP)
        q = jnp.concatenate([q_heads[h] for h in heads], axis=0)
        sink = jnp.concatenate([jnp.full((BLOCK, 1), sink_ref[h], F32) for h in heads], axis=0)
        o = _attend(q, sink, [_head(k_ctx, kv).astype(BF16), _head(k_loc, kv).astype(BF16)],
                    [_head(v_ctx, kv).astype(BF16), _head(v_loc, kv).astype(BF16)], [None, valid])
        outs += [o[g * BLOCK:(g + 1) * BLOCK] for g in range(ATT_GROUP)]
    o_ref[...] = jnp.concatenate(outs, axis=-1).astype(o_ref.dtype)


def _attn_lat(ps, sink, k_ctx, v_ctx, cos_t, sin_t, y):
    n_blk = DEC_SEQ // BLOCK
    rb0 = T_CTX // BLOCK

    def kv_spec(off):
        return pl.BlockSpec(
            (BLOCK, 2 * LANES),
            lambda b, n: (rb0 + b * n_blk + jnp.clip(n + off, 0, n_blk - 1), PS_KV // (2 * LANES)))

    return pl.pallas_call(
        _attn_lat_kernel,
        grid=(DEC_BATCH, n_blk),
        in_specs=[
            pl.BlockSpec(memory_space=pltpu.SMEM),
            pl.BlockSpec((BLOCK, ATT_W), lambda b, n: (rb0 + b * n_blk + n, 0)),
            kv_spec(-1), kv_spec(0), kv_spec(1),
            pl.BlockSpec((None, PAST_LEN, LANES), lambda b, n: (b, 0, 0)),
            pl.BlockSpec((None, PAST_LEN, LANES), lambda b, n: (b, 0, 0)),
            pl.BlockSpec((DEC_SEQ, LANES), lambda b, n: (0, 0)),
            pl.BlockSpec((DEC_SEQ, LANES), lambda b, n: (0, 0)),
            pl.BlockSpec(memory_space=pl.ANY),
        ],
        out_specs=pl.BlockSpec((BLOCK, ATT_W), lambda b, n: (rb0 + b * n_blk + n, 0)),
        out_shape=jax.ShapeDtypeStruct((T_ALL, ATT_W), BF16),
        input_output_aliases={9: 0},
        compiler_params=_cparams("parallel", "parallel"),
        name="attn_lat",
    )(sink, ps, ps, ps, ps, k_ctx, v_ctx, cos_t, sin_t, y)


def _rope_tables():
    pos = np.arange(DEC_SEQ)
    n_freq = HEAD_DIM // 4
    inv = jnp.asarray(ROPE_THETA, F32) ** (-jnp.arange(n_freq, dtype=F32) / n_freq)
    row = jnp.asarray(pos // GRID_W, F32)
    colp = jnp.asarray(pos % GRID_W, F32)
    ang_r = row[:, None] * inv[None, :]
    ang_c = colp[:, None] * inv[None, :]
    cos = jnp.concatenate([jnp.cos(ang_r)] * 2 + [jnp.cos(ang_c)] * 2, axis=-1)
    sin = jnp.concatenate([-jnp.sin(ang_r), jnp.sin(ang_r), -jnp.sin(ang_c), jnp.sin(ang_c)], axis=-1)
    return jnp.tile(cos, (1, 2)), jnp.tile(sin, (1, 2))


def _split_dot(a, w_hi, w_lo):
    a_hi = a.astype(BF16)
    a_lo = (a - a_hi.astype(F32)).astype(BF16)
    return _mm(a_hi, w_hi) + _mm(a_lo, w_hi) + _mm(a_hi, w_lo)


def _route(logits):
    lane_i = lax.broadcasted_iota(jnp.int32, logits.shape, 1)
    lane = lane_i.astype(F32)
    big = jnp.float32(1 << 20)
    is_g = lane_i < N_EXPERT_GROUPS
    lg = jnp.where(is_g, logits, -jnp.inf)
    m_g = jnp.max(lg, axis=-1, keepdims=True)
    grp = jnp.min(jnp.where(lg == m_g, lane, big), axis=-1, keepdims=True)
    z_g = jnp.sum(jnp.where(is_g, jnp.exp(lg - m_g), 0.0), axis=-1, keepdims=True)
    p_grp = 1.0 / z_g

    e_idx = lane_i - ROUTE_E0
    e_grp = (e_idx >> 2).astype(F32)
    sel = (e_idx >= 0) & (e_idx < N_EXPERTS) & (e_grp == grp)
    le = jnp.where(sel, logits, -jnp.inf)
    m_e = jnp.max(le, axis=-1, keepdims=True)
    ex = jnp.where(sel, jnp.exp(le - m_e), 0.0)
    pe = ex / jnp.sum(ex, axis=-1, keepdims=True)
    pe = jnp.where(sel, pe, -1.0)
    v1 = jnp.max(pe, axis=-1, keepdims=True)
    i1 = jnp.min(jnp.where(pe == v1, lane, big), axis=-1, keepdims=True)
    pe2 = jnp.where(lane == i1, -1.0, pe)
    v2 = jnp.max(pe2, axis=-1, keepdims=True)
    i2 = jnp.min(jnp.where(pe2 == v2, lane, big), axis=-1, keepdims=True)
    tot = v1 + v2
    return (jnp.where(lane == i1, p_grp * (v1 / tot), 0.0)
            + jnp.where(lane == i2, p_grp * (v2 / tot), 0.0)
            + jnp.where(lane_i == LANES - 1, grp, 0.0))


HALO = 16


def _merge_kernel(gate_ref, conv_ref, cprev_ref, cnext_ref, yb_ref, yc_ref, *rest, x_split):
    x_refs, rest = rest[:2 if x_split else 1], rest[2 if x_split else 1:]
    (mod_ref, cw_ref, wb_ref, wo_ref, g2_ref, wr_hi_ref, wr_lo_ref, br_ref,
     x1_ref, h2_ref, rg_ref, wbb_ref, wob_ref) = rest
    tm = TM_MERGE
    i = pl.program_id(0)

    def x_rows(rows):
        if not x_split:
            return x_refs[0][rows, :]
        return jnp.where(i < x_split, x_refs[0][rows, :], x_refs[1][rows, :])

    @pl.when(i == 0)
    def _():
        wbb_ref[...] = wb_ref[...].astype(BF16)
        wob_ref[...] = wo_ref[...].astype(BF16)

    def gated(ref):
        return ref[:, CONV_W:2 * CONV_W].astype(F32) * ref[:, 2 * CONV_W:3 * CONV_W].astype(F32)

    a_b = conv_ref[:, 0:CONV_W].astype(F32)
    u = gated(conv_ref)
    u_before = gated(cprev_ref)[HALO - 1:HALO]
    u_after = gated(cnext_ref)[0:1]
    r = lax.broadcasted_iota(jnp.int32, (tm, 1), 0)
    g_row = i * tm + r
    seq_mask = jnp.where(g_row < T_CTX, SEQ - 1, DEC_SEQ - 1)
    first = (g_row & seq_mask) == 0
    last = ((g_row + 1) & seq_mask) == 0
    u_prev = jnp.where(r == 0, u_before, pltpu.roll(u, 1, 0))
    u_next = jnp.where(r == tm - 1, u_after, pltpu.roll(u, tm - 1, 0))
    u_prev = jnp.where(first, 0.0, u_prev)
    u_next = jnp.where(last, 0.0, u_next)
    y_a = (a_b * (u_prev * cw_ref[0:1, :] + u * cw_ref[1:2, :] + u_next * cw_ref[2:3, :])).astype(BF16)

    blocks = [slice(r0, r0 + MERGE_ROWS) for r0 in range(0, tm, MERGE_ROWS)]
    ys = [(y_a[rows], yb_ref[rows, :], yc_ref[rows, :]) for rows in blocks]
    branches = [[_mm(y, wbb_ref[j]) for j, y in enumerate(y3)] for y3 in ys]
    zs = []
    for rows, br3 in zip(blocks, branches):
        z = sum(gate_ref[rows, j * D_MODEL:(j + 1) * D_MODEL].astype(F32) * br3[j] for j in range(3))
        zs.append(z.astype(BF16))
    outs = [_mm(z, wob_ref[...]) for z in zs]
    h2s = []
    for rows, o in zip(blocks, outs):
        x1 = x_rows(rows) + mod_ref[2:3, :] * o
        x1_ref[rows, :] = x1
        h2 = _mod_norm(x1, g2_ref[...], mod_ref[...], 3)
        h2_ref[rows, :] = h2.astype(BF16)
        h2s.append(h2)
    logits = [_split_dot(h2, wr_hi_ref[...], wr_lo_ref[...]) + br_ref[...] for h2 in h2s]
    for rows, lg in zip(blocks, logits):
        rg_ref[rows, :] = _route(lg)


def _merge(pg, pa, y_b, y_c, x, mod_l, conv_w, layer, wb, wo, g2, wr_hi, wr_lo, br):
    tm = TM_MERGE
    n_tiles = T_ALL // tm
    hb = tm // HALO
    const = lambda shape: pl.BlockSpec(shape, lambda i: (0,) * len(shape))
    if isinstance(x, tuple):
        x_specs, x_split = _split_x_specs(tm)
    else:
        x, x_specs, x_split = (x,), [pl.BlockSpec((tm, D_MODEL), lambda i: (i, 0))], 0
    return pl.pallas_call(
        functools.partial(_merge_kernel, x_split=x_split),
        grid=(n_tiles,),
        in_specs=[
            pl.BlockSpec((tm, PG_W), lambda i: (i, 0)),
            pl.BlockSpec((tm, 3 * CONV_W), lambda i: (i, 0)),
            pl.BlockSpec((HALO, 3 * CONV_W), lambda i: (jnp.maximum(i * hb - 1, 0), 0)),
            pl.BlockSpec((HALO, 3 * CONV_W), lambda i: (jnp.minimum((i + 1) * hb, n_tiles * hb - 1), 0)),
            pl.BlockSpec((tm, GLA_W), lambda i: (i, 0)),
            pl.BlockSpec((tm, ATT_W), lambda i: (i, 0)),
            *x_specs,
            pl.BlockSpec((None, 6, D_MODEL), lambda i: (_cond_row(i, tm), 0, 0)),
            const((CONV_K, CONV_W)),
            pl.BlockSpec((None, 3, 512, D_MODEL), lambda i: (layer, 0, 0, 0)),
            pl.BlockSpec((None, D_MODEL, D_MODEL), lambda i: (layer, 0, 0)),
            const((1, D_MODEL)),
            const((D_MODEL, LANES)),
            const((D_MODEL, LANES)),
            const((1, LANES)),
        ],
        out_specs=[
            pl.BlockSpec((tm, D_MODEL), lambda i: (i, 0)),
            pl.BlockSpec((tm, D_MODEL), lambda i: (i, 0)),
            pl.BlockSpec((tm, LANES), lambda i: (i, 0)),
        ],
        out_shape=[
            jax.ShapeDtypeStruct((T_ALL, D_MODEL), F32),
            jax.ShapeDtypeStruct((T_ALL, D_MODEL), BF16),
            jax.ShapeDtypeStruct((T_ALL, LANES), F32),
        ],
        scratch_shapes=[pltpu.VMEM((3, 512, D_MODEL), BF16), pltpu.VMEM((D_MODEL, D_MODEL), BF16)],
        compiler_params=_cparams("arbitrary"),
        name="merge",
    )(pg, pa, pa, pa, y_b, y_c, *x, mod_l, conv_w, wb, wo, g2, wr_hi, wr_lo, br)


SUB = 128
MOE_ROWS = 256


def _dispatch_tables(rg):
    n_tiles = T_ALL // TM
    grp = rg[:, LANES - 1].astype(jnp.int32).reshape(n_tiles, TM)
    hot = grp[..., None] == jnp.arange(N_EXPERT_GROUPS, dtype=jnp.int32)
    onehot = hot.astype(jnp.int32)
    cnt = onehot.sum(axis=1)
    start = jnp.cumsum(cnt, axis=1) - cnt
    first = start // SUB
    nblk = jnp.where(cnt > 0, (start + cnt + SUB - 1) // SUB - first, 0)
    before = jnp.tril(jnp.ones((TM, TM), BF16), -1)
    rank = jnp.einsum("ts,nsg->ntg", before, hot.astype(BF16), preferred_element_type=F32).astype(jnp.int32)
    slot = ((start[:, None, :] + rank) * onehot).sum(axis=-1)
    return (first.reshape(-1), nblk.reshape(-1), slot.reshape(n_tiles, 1, TM), slot.reshape(T_ALL, 1))


def _moe_kernel(bstart_ref, nblk_ref, h_ref, rg_ref, srow_ref, scol_ref, x1_ref, mod_ref, wg_ref, wu_ref,
                wd_ref, g_ref, *rest, final, tile0):
    if final:
        y_ref, hs_ref, gs_ref, os_ref = rest
    else:
        modn_ref, x2_ref, hn_ref, hs_ref, gs_ref, os_ref = rest
    i = pl.program_id(0)
    grp = pl.program_id(1)

    @pl.when(grp == 0)
    def _():
        slot_of_token = srow_ref[...]
        payload = jnp.concatenate([h_ref[...], *_split3(rg_ref[...])], axis=-1)
        for r0 in range(0, TM, MOE_ROWS):
            rows = slice(r0, r0 + MOE_ROWS)
            slot = r0 + lax.broadcasted_iota(jnp.int32, (MOE_ROWS, TM), 0)
            pm = jnp.where(slot == slot_of_token, 1.0, 0.0).astype(BF16)
            moved = _mm(pm, payload)
            hs_ref[rows, :] = moved[:, :D_MODEL].astype(BF16)
            gs_ref[rows, :] = sum(moved[:, D_MODEL + j * LANES:D_MODEL + (j + 1) * LANES] for j in range(3))
        os_ref[...] = jnp.zeros_like(os_ref)

    k = (tile0 + i) * N_EXPERT_GROUPS + grp
    first = bstart_ref[k]
    n_blocks = nblk_ref[k]

    def experts(block0, n_sub):
        n_rows = n_sub * SUB
        rows = pl.ds(pl.multiple_of(block0 * SUB, SUB), n_rows)
        lane = lax.broadcasted_iota(jnp.int32, (n_rows, LANES), 1)
        x = hs_ref[rows, :]
        gates = gs_ref[rows, :]
        acc = jnp.zeros((n_rows, D_MODEL), F32)
        for e in range(EXPERTS_PER_GROUP):
            ge = _mm(x, wg_ref[e])
            ue = _mm(x, wu_ref[e])
            w = jnp.sum(jnp.where(lane == ROUTE_E0 + grp * EXPERTS_PER_GROUP + e, gates, 0.0),
                        axis=-1, keepdims=True)
            hid = (ge * jax.nn.sigmoid(ge)) * ue * w
            acc = acc + _mm(hid.astype(BF16), wd_ref[e])
        os_ref[rows, :] = (os_ref[rows, :].astype(F32) + acc).astype(BF16)

    odd = n_blocks % 2 == 1
    n_pairs = jnp.where(odd & (n_blocks >= 3), (n_blocks - 3) // 2, n_blocks // 2)

    def pair(p, carry):
        experts(first + 2 * p, 2)
        return carry

    lax.fori_loop(0, n_pairs, pair, 0)

    @pl.when(odd & (n_blocks >= 3))
    def _():
        experts(first + n_blocks - 3, 3)

    @pl.when(n_blocks == 1)
    def _():
        experts(first, 1)

    @pl.when(grp == N_EXPERT_GROUPS - 1)
    def _():
        sorted_out = os_ref[...]
        lane_slot = lax.broadcasted_iota(jnp.int32, (MOE_ROWS, TM), 1)
        for r0 in range(0, TM, MOE_ROWS):
            rows = slice(r0, r0 + MOE_ROWS)
            pt = jnp.where(lane_slot == scol_ref[rows, :], 1.0, 0.0).astype(BF16)
            x2 = x1_ref[rows, :] + mod_ref[5:6, :] * _mm(pt, sorted_out)
            if final:
                y_ref[rows, :] = _rms(x2) * g_ref[...]
            else:
                x2_ref[rows, :] = x2
                hn_ref[rows, :] = _mod_norm(x2, g_ref[...], modn_ref[...], 0).astype(BF16)


def _moe(tables, h2, rg, x1, mod_l, layer, w_eg, w_eu, w_ed, g, mod_next, tile0, n_tiles):
    bstart, nblk, slot_row, slot_col = tables
    final = mod_next is None
    row = lambda i, e, *_: (tile0 + i, 0)
    mod_spec = lambda: pl.BlockSpec((None, 6, D_MODEL), lambda i, e, *_: (_cond_row(tile0 + i, TM), 0, 0))
    group = lambda i, e, *_: (layer, e, 0, 0)
    in_specs = [
        pl.BlockSpec((TM, D_MODEL), row),
        pl.BlockSpec((TM, LANES), row),
        pl.BlockSpec((None, 1, TM), lambda i, e, *_: (tile0 + i, 0, 0)),
        pl.BlockSpec((TM, 1), row),
        pl.BlockSpec((TM, D_MODEL), row),
        mod_spec(),
        pl.BlockSpec((None, EXPERTS_PER_GROUP, D_MODEL, D_EXPERT), group),
        pl.BlockSpec((None, EXPERTS_PER_GROUP, D_MODEL, D_EXPERT), group),
        pl.BlockSpec((None, EXPERTS_PER_GROUP, D_EXPERT, D_MODEL), group),
        pl.BlockSpec((1, D_MODEL), lambda i, e, *_: (0, 0)),
    ]
    args = [h2, rg, slot_row, slot_col, x1, mod_l, w_eg, w_eu, w_ed, g]
    if final:
        out_specs = pl.BlockSpec((TM, D_MODEL), lambda i, e, *_: (i, 0))
        out_shape = jax.ShapeDtypeStruct((n_tiles * TM, D_MODEL), F32)
    else:
        in_specs.append(mod_spec())
        args.append(mod_next)
        out_specs = [pl.BlockSpec((TM, D_MODEL), row)] * 2
        out_shape = [jax.ShapeDtypeStruct((T_ALL, D_MODEL), F32), jax.ShapeDtypeStruct((T_ALL, D_MODEL), BF16)]
    return pl.pallas_call(
        functools.partial(_moe_kernel, final=final, tile0=tile0),
        grid_spec=pltpu.PrefetchScalarGridSpec(
            num_scalar_prefetch=2,
            grid=(n_tiles, N_EXPERT_GROUPS),
            in_specs=in_specs,
            out_specs=out_specs,
            scratch_shapes=[pltpu.VMEM((TM, D_MODEL), BF16), pltpu.VMEM((TM, LANES), F32),
                            pltpu.VMEM((TM, D_MODEL), BF16)],
        ),
        out_shape=out_shape,
        compiler_params=_cparams("parallel", "arbitrary"),
        name="moe_final" if final else "moe",
    )(bstart, nblk, *args)


def _cache_kernel(*refs):
    ps_refs, (k_ref, v_ref) = refs[:DEPTH], refs[DEPTH:]
    n_seq = TM // SEQ
    for l, ps_ref in enumerate(ps_refs):
        k_ref[:, l] = ps_ref[:, :LANES].reshape(n_seq, SEQ, LANES)
        v_ref[:, l] = ps_ref[:, LANES:].reshape(n_seq, SEQ, LANES)


def _cache(ps_layers):
    n_seq = TM // SEQ
    out = pl.BlockSpec((n_seq, DEPTH, SEQ, LANES), lambda i: (i, 0, 0, 0))
    return pl.pallas_call(
        _cache_kernel,
        grid=(T_CTX // TM,),
        in_specs=[pl.BlockSpec((TM, 2 * LANES), lambda i: (i, PS_KV // (2 * LANES)))] * DEPTH,
        out_specs=[out, out],
        out_shape=[jax.ShapeDtypeStruct((BATCH, DEPTH, SEQ, LANES), F32)] * 2,
        compiler_params=_cparams("parallel"),
        name="cache",
    )(*ps_layers)


def _hi_lo(w):
    hi = w.astype(BF16)
    return hi, (w - hi.astype(F32)).astype(BF16)


def _prep_gla_gate(w_gate):
    pads = [((0, 0), (d * GLA_RANK, LANES - (d + 1) * GLA_RANK), (0, 0)) for d in range(2)]
    return _hi_lo(jnp.stack([jnp.pad(w_gate[:, d], pads[d]) for d in range(2)], axis=1))


def _prep_router(w_rg, b_rg, w_re, b_re):
    unused = LANES - N_EXPERT_GROUPS - N_EXPERTS
    w = jnp.concatenate([w_rg, w_re, jnp.zeros((DEPTH, D_MODEL, unused), F32)], axis=-1)
    b = jnp.concatenate([b_rg, b_re, jnp.zeros((DEPTH, unused), F32)], axis=-1)
    return (*_hi_lo(w), b.reshape(DEPTH, 1, LANES))


def kernel(x_prompt, x_sample, state_gla, cache_k, cache_v, c, c_ctx, w_ada, b_ada, norm1_g, norm2_g, w_in,
           conv_w, gla_w_gate, gla_b_gate, gla_norm_g, attn_sink, w_branch, w_out, w_route_group,
           b_route_group, w_route_expert, b_route_expert, w_exp_gate, w_exp_up, w_exp_down, final_norm_g):
    cond = jnp.zeros((N_COND, D_MODEL), F32).at[0].set(c_ctx).at[1:1 + DEC_BATCH].set(c)
    mod = _modulation(cond, w_ada, b_ada).reshape(DEPTH, N_COND, 6, D_MODEL)
    cos_t, sin_t = _rope_tables()
    row = lambda v: v.reshape(1, -1)

    x = (x_prompt.reshape(T_CTX, D_MODEL), x_sample.reshape(T_LAT, D_MODEL))
    h = _prenorm(*x, mod[0], row(norm1_g[0]))
    wg_hi_all, wg_lo_all = _prep_gla_gate(gla_w_gate)
    wr_hi_all, wr_lo_all, br_all = _prep_router(w_route_group, b_route_group, w_route_expert, b_route_expert)
    states = None
    ps_layers = []
    y_prompt = y_sample = None
    w_t = jnp.swapaxes(w_in, 1, 2)
    experts = (w_exp_gate.astype(BF16), w_exp_up.astype(BF16), w_exp_down.astype(BF16))
    for l in range(DEPTH):
        pa = _proj(h, w_t, l, [(0, PA_W // 2)], 2, BF16, "proj_a", act="silu_tail")
        ps = _proj(h, w_t, l, [(W_IN_TQ, W_IN_GATE - W_IN_TQ), (W_IN_LR, LANES)], 1, F32, "proj_s")
        pg = _proj(h, w_t, l, [(W_IN_GATE, PG_W // 2)], 2, BF16, "proj_g", act="sigmoid")
        ps_layers.append(ps)

        wg_hi, wg_lo = wg_hi_all[l], wg_lo_all[l]
        bg = gla_b_gate[l].reshape(2, 1, GLA_W)
        ng = row(gla_norm_g[l])
        y_b, states = _gla_ctx(pa, ps, wg_hi, wg_lo, bg, ng, states, l)
        y_b = _gla_lat(pa, ps, wg_hi, wg_lo, bg, ng, state_gla[:, l], y_b)
        y_c = _attn_ctx(ps, attn_sink[l])
        y_c = _attn_lat(ps, attn_sink[l], cache_k[:, l].reshape(DEC_BATCH, PAST_LEN, LANES),
                        cache_v[:, l].reshape(DEC_BATCH, PAST_LEN, LANES), cos_t, sin_t, y_c)

        x1, h2, rg = _merge(pg, pa, y_b, y_c, x, mod[l], conv_w[l], l, w_branch, w_out, row(norm2_g[l]),
                            wr_hi_all[l], wr_lo_all[l], br_all[l])
        moe_in = (_dispatch_tables(rg), h2, rg, x1, mod[l], l, *experts)
        if l + 1 < DEPTH:
            x, h = _moe(*moe_in, row(norm1_g[l + 1]), mod[l + 1], 0, T_ALL // TM)
        else:
            gf = row(final_norm_g)
            y_prompt = _moe(*moe_in, gf, None, 0, T_CTX // TM)
            y_sample = _moe(*moe_in, gf, None, T_CTX // TM, T_LAT // TM)

    new_k, new_v = _cache(ps_layers)
    kv_shape = (BATCH, DEPTH, SEQ, ATT_KV_HEADS, HEAD_DIM)
    return (y_prompt.reshape(BATCH, SEQ, D_MODEL), y_sample.reshape(DEC_BATCH, DEC_SEQ, D_MODEL), states,
            new_k.reshape(kv_shape), new_v.reshape(kv_shape))
```

```python
import functools

import jax
import jax.numpy as jnp
import numpy as np
from jax import lax
from jax.experimental import pallas as pl
from jax.experimental.pallas import tpu as pltpu

F32 = jnp.float32
BF16 = jnp.bfloat16

D_MODEL = 1024
BATCH = 32
SEQ = 256
DEPTH = 2
DEC_BATCH = 2
DEC_SEQ = 2048
PAST_LEN = 512
GRID_W = 64
EPS = 1e-6
CONV_W = 512
CONV_K = 3
GLA_HEADS = 4
GLA_DK = 128
GLA_DV = 128
GLA_W = GLA_HEADS * GLA_DV
GLA_RANK = 16
GLA_TAU = 16.0
GLA_CHUNK = 64
ATT_HEADS = 8
ATT_KV_HEADS = 2
ATT_GROUP = ATT_HEADS // ATT_KV_HEADS
HEAD_DIM = 64
ATT_W = ATT_HEADS * HEAD_DIM
WINDOW = 128
BLOCK = 128
ROPE_THETA = 10000.0
N_EXPERT_GROUPS = 4
EXPERTS_PER_GROUP = 4
N_EXPERTS = 16
D_EXPERT = 256
NEG_INF = -1e30

T_CTX = BATCH * SEQ
T_LAT = DEC_BATCH * DEC_SEQ
T_ALL = T_CTX + T_LAT
N_COND = 8
LANES = 128

W_IN_LR = 3584
W_IN_TQ = 3616
W_IN_GATE = 4384
N_IN = 7456
PA_W = 3584
PA_GLA = 1536
PS_W = 896
PS_KV = 512
PS_LR = 768
PG_W = 3 * D_MODEL
ROUTE_E0 = N_EXPERT_GROUPS

TM = 1024
TM_MERGE = 512
MERGE_ROWS = 256
VMEM_LIMIT = 56 * 1024 * 1024


def _cparams(*sem):
    return pltpu.CompilerParams(dimension_semantics=sem, vmem_limit_bytes=VMEM_LIMIT)


def _cond_row(i, tm):
    n_ctx = T_CTX // tm
    per = DEC_SEQ // tm
    return jnp.where(i < n_ctx, 0, 1 + (i - n_ctx) // per)


def _mm(a, b):
    return jnp.dot(a, b, preferred_element_type=F32)


def _dot_t(a, b):
    return lax.dot_general(a, b, (((1,), (1,)), ((), ())), preferred_element_type=F32)


def _dot_ta(a, b):
    return lax.dot_general(a, b, (((0,), (0,)), ((), ())), preferred_element_type=F32)


def _rms(x):
    return x * lax.rsqrt(jnp.mean(x * x, axis=-1, keepdims=True) + EPS)


def _mod_norm(x, g, mod, shift_row):
    return _rms(x) * g * (1.0 + mod[shift_row + 1:shift_row + 2, :]) + mod[shift_row:shift_row + 1, :]


def _mod_kernel(c_ref, w_ref, b_ref, o_ref):
    c = c_ref[...]
    s = (c * jax.nn.sigmoid(c)).astype(BF16)
    o_ref[...] = _mm(s, w_ref[...].astype(BF16)) + b_ref[...]


def _modulation(cond, w_ada, b_ada):
    tn = 1536
    return pl.pallas_call(
        _mod_kernel,
        grid=(DEPTH, 6 * D_MODEL // tn),
        in_specs=[
            pl.BlockSpec((N_COND, D_MODEL), lambda l, j: (0, 0)),
            pl.BlockSpec((None, D_MODEL, tn), lambda l, j: (l, 0, j)),
            pl.BlockSpec((None, 1, tn), lambda l, j: (l, 0, j)),
        ],
        out_specs=pl.BlockSpec((None, N_COND, tn), lambda l, j: (l, 0, j)),
        out_shape=jax.ShapeDtypeStruct((DEPTH, N_COND, 6 * D_MODEL), F32),
        compiler_params=_cparams("parallel", "parallel"),
        name="modulation",
    )(cond, w_ada, b_ada.reshape(DEPTH, 1, 6 * D_MODEL))


def _split_x_specs(tm):
    n_ctx = T_CTX // tm
    return [pl.BlockSpec((tm, D_MODEL), lambda i, *_: (jnp.minimum(i, n_ctx - 1), 0)),
            pl.BlockSpec((tm, D_MODEL), lambda i, *_: (jnp.maximum(i - n_ctx, 0), 0))], n_ctx


def _prenorm_kernel(xp_ref, xs_ref, mod_ref, g_ref, h_ref):
    x = jnp.where(pl.program_id(0) < T_CTX // TM, xp_ref[...], xs_ref[...])
    h_ref[...] = _mod_norm(x, g_ref[...], mod_ref[...], 0).astype(BF16)


def _prenorm(xp, xs, mod_l, g):
    x_specs, _ = _split_x_specs(TM)
    return pl.pallas_call(
        _prenorm_kernel,
        grid=(T_ALL // TM,),
        in_specs=x_specs + [
            pl.BlockSpec((None, 6, D_MODEL), lambda i: (_cond_row(i, TM), 0, 0)),
            pl.BlockSpec((1, D_MODEL), lambda i: (0, 0)),
        ],
        out_specs=pl.BlockSpec((TM, D_MODEL), lambda i: (i, 0)),
        out_shape=jax.ShapeDtypeStruct((T_ALL, D_MODEL), BF16),
        compiler_params=_cparams("parallel"),
        name="prenorm",
    )(xp, xs, mod_l, g)


def _proj_kernel(h_ref, *refs, act, n_w):
    w_refs, refs = refs[:n_w], refs[n_w:]
    if len(refs) == 4:
        cast_in_ref, o_ref, cast_out_ref, wb_ref = refs
        cast_out_ref[...] = cast_in_ref[...].astype(BF16)
    else:
        o_ref, wb_ref = refs

    @pl.when(pl.program_id(1) == 0)
    def _():
        col = 0
        for w_ref in w_refs:
            n = w_ref.shape[0]
            wb_ref[:, col:col + n] = w_ref[...].T.astype(BF16)
            col += n

    def sigmoid(t):
        return 0.5 * jnp.tanh(0.5 * t) + 0.5

    y = _mm(h_ref[...], wb_ref[...])
    if act == "sigmoid":
        y = sigmoid(y)
    if act == "silu_tail":
        last = pl.program_id(0) == pl.num_programs(0) - 1
        tail = y[:, -GLA_W:]
        o_ref[:, :-GLA_W] = y[:, :-GLA_W].astype(o_ref.dtype)
        o_ref[:, -GLA_W:] = jnp.where(last, tail * sigmoid(tail), tail).astype(o_ref.dtype)
    else:
        o_ref[...] = y.astype(o_ref.dtype)


def _proj(h, w_t, layer, pieces, n_tiles, out_dtype, name, act=None, cast=None):
    tn = sum(n for _, n in pieces)
    n_rows = T_ALL // TM

    def w_spec(c0, n):
        return pl.BlockSpec((pl.Squeezed(), pl.Element(n), pl.Element(D_MODEL)),
                            lambda j, i: (layer, pl.multiple_of(c0 + j * tn, 8), 0))

    in_specs = [pl.BlockSpec((TM, D_MODEL), lambda j, i: (i, 0))] + [w_spec(c0, n) for c0, n in pieces]
    out_specs = [pl.BlockSpec((TM, tn), lambda j, i: (i, j))]
    out_shape = [jax.ShapeDtypeStruct((T_ALL, n_tiles * tn), out_dtype)]
    args = [h] + [w_t] * len(pieces)
    if cast is not None:
        src, per_block = cast
        per_layer = N_EXPERTS // per_block
        n_blocks = DEPTH * per_layer
        assert n_blocks <= n_tiles * n_rows

        def block(j, i):
            b = jnp.minimum(j * n_rows + i, n_blocks - 1)
            return (b // per_layer, b % per_layer, 0, 0)

        spec = pl.BlockSpec((None, per_block) + src.shape[2:], block)
        in_specs.append(spec)
        out_specs.append(spec)
        out_shape.append(jax.ShapeDtypeStruct(src.shape, BF16))
        args.append(src)
    out = pl.pallas_call(
        functools.partial(_proj_kernel, act=act, n_w=len(pieces)),
        grid=(n_tiles, n_rows),
        in_specs=in_specs,
        out_specs=out_specs,
        out_shape=out_shape,
        scratch_shapes=[pltpu.VMEM((D_MODEL, tn), BF16)],
        compiler_params=_cparams("arbitrary", "arbitrary"),
        name=name,
    )(*args)
    return out[0] if cast is None else out


GLA_GROUP = 256
GLA_CTX_SEQS = 2


def _log_sigmoid(z):
    return jnp.minimum(z, 0.0) - jnp.log(1.0 + jnp.exp(-jnp.abs(z)))


def _split3(x):
    hi = x.astype(BF16)
    r1 = x - hi.astype(F32)
    mid = r1.astype(BF16)
    lo = (r1 - mid.astype(F32)).astype(BF16)
    return hi, mid, lo


def _gla_keep(d):
    ri = lax.broadcasted_iota(jnp.int32, (GLA_GROUP, GLA_GROUP), 0)
    ci = lax.broadcasted_iota(jnp.int32, (GLA_GROUP, GLA_GROUP), 1)
    if d == 0:
        return (ci <= ri) & (ci >= (ri & ~(GLA_CHUNK - 1)))
    return (ci >= ri) & (ci <= (ri | (GLA_CHUNK - 1)))


def _gla_group(load, lr, wg_hi, wg_lo, bg, states, d, keep):
    C = GLA_CHUNK
    nc = GLA_GROUP // C
    lr = jnp.where(lax.broadcasted_iota(jnp.int32, lr.shape, 1) < 2 * GLA_RANK, lr, 0.0)
    lr_hi = lr.astype(BF16)
    lr_lo = (lr - lr_hi.astype(F32)).astype(BF16)
    z = _mm(lr_hi, wg_hi) + _mm(lr_lo, wg_hi) + _mm(lr_hi, wg_lo) + bg
    la = _log_sigmoid(z) / GLA_TAU
    tri = jnp.where(keep, 1.0, 0.0).astype(BF16)
    la_hi, la_mid, la_lo = _split3(la)
    b_all = _mm(tri, la_hi) + _mm(tri, la_mid) + _mm(tri, la_lo)
    edge = C - 1 if d == 0 else 0
    order = range(nc) if d == 0 else range(nc - 1, -1, -1)
    row_chunk = lax.broadcasted_iota(jnp.int32, (GLA_GROUP, GLA_DK), 0) // C
    in_chunk = [row_chunk == c for c in range(nc)]
    outs, new_states = [], []
    for h, st in enumerate(states):
        b = b_all[:, h * GLA_DK:(h + 1) * GLA_DK]
        b_last = [b[c * C + edge:c * C + edge + 1, :] for c in range(nc)]
        bl = jnp.concatenate([jnp.broadcast_to(t, (C, GLA_DK)) for t in b_last], axis=0)
        k = load(1, h)
        q_in = load(0, h) * (GLA_DK ** -0.5) * jnp.exp(b)
        k_in = (k * jnp.exp(-b)).astype(BF16)
        k_end = k * jnp.exp(bl - b)
        vb = load(2, h).astype(BF16)
        att = jnp.where(keep, _dot_t(q_in.astype(BF16), k_in), 0.0)
        o = _mm(att.astype(BF16), vb)
        k_spread = jnp.concatenate([jnp.where(in_chunk[c], k_end, 0.0) for c in range(nc)], axis=-1)
        q_spread = jnp.concatenate([jnp.where(in_chunk[c], q_in, 0.0) for c in range(nc)], axis=-1)
        ds_t = _dot_ta(vb, k_spread.astype(BF16))
        starts = [None] * nc
        for c in order:
            starts[c] = st
            st = st * jnp.exp(b_last[c]) + ds_t[:, c * GLA_DK:(c + 1) * GLA_DK]
        o = o + _dot_t(q_spread.astype(BF16), jnp.concatenate(starts, axis=-1).astype(BF16))
        outs.append(o)
        new_states.append(st)
    return outs, new_states


def _gla_loader(qkv_refs, rows):
    def load(i, h):
        return qkv_refs[i][rows, h * GLA_DK:(h + 1) * GLA_DK].astype(F32)
    return load


def _gla_finish(o, r_act, ng):
    o = o * lax.rsqrt(jnp.mean(o * o, axis=-1, keepdims=True) + EPS)
    return o * ng * r_act


def _gla_ctx_kernel(q_ref, k_ref, v_ref, r_ref, lr_ref, wgh_ref, wgl_ref, bg_ref, ng_ref, *rest):
    y_ref, s_ref = rest[-2:]
    zero = jnp.zeros((GLA_DV, GLA_DK), F32)
    keep = [_gla_keep(0), _gla_keep(1)]
    for s in range(GLA_CTX_SEQS):
        rows = slice(s * SEQ, (s + 1) * SEQ)
        load = _gla_loader((q_ref, k_ref, v_ref), rows)
        lr = lr_ref[rows, :]
        o_dir = []
        for d in range(2):
            outs, sts = _gla_group(load, lr, wgh_ref[d], wgl_ref[d], bg_ref[d], [zero] * GLA_HEADS, d,
                                   keep[d])
            o_dir.append(outs)
            for h in range(GLA_HEADS):
                s_ref[s, d, h] = sts[h].T
        for h in range(GLA_HEADS):
            cs = slice(h * GLA_DV, (h + 1) * GLA_DV)
            y = _gla_finish(o_dir[0][h] + o_dir[1][h], r_ref[rows, cs].astype(F32), ng_ref[:, cs])
            y_ref[rows, cs] = y.astype(y_ref.dtype)


def _gla_lat_kernel(q_ref, k_ref, v_ref, r_ref, lr_ref, wgh_ref, wgl_ref, bg_ref, ng_ref, s0_ref, yin_ref,
                    y_ref, of_ref, ob_ref):
    del yin_ref
    n_groups = DEC_SEQ // GLA_GROUP
    keep_f, keep_b = _gla_keep(0), _gla_keep(1)

    def body(g, carry):
        st_f, st_b = carry
        rf = pl.ds(pl.multiple_of(g * GLA_GROUP, GLA_GROUP), GLA_GROUP)
        rb = pl.ds(pl.multiple_of((n_groups - 1 - g) * GLA_GROUP, GLA_GROUP), GLA_GROUP)
        qkv = (q_ref, k_ref, v_ref)
        outs, st_f = _gla_group(_gla_loader(qkv, rf), lr_ref[rf, :], wgh_ref[0], wgl_ref[0], bg_ref[0],
                                list(st_f), 0, keep_f)
        of_ref[rf, :] = jnp.concatenate(outs, axis=-1)
        outs, st_b = _gla_group(_gla_loader(qkv, rb), lr_ref[rb, :], wgh_ref[1], wgl_ref[1], bg_ref[1],
                                list(st_b), 1, keep_b)
        ob_ref[rb, :] = jnp.concatenate(outs, axis=-1)
        return tuple(st_f), tuple(st_b)

    init = tuple(tuple(s0_ref[d, h].T for h in range(GLA_HEADS)) for d in range(2))
    lax.fori_loop(0, n_groups, body, init)
    for h in range(GLA_HEADS):
        cs = slice(h * GLA_DV, (h + 1) * GLA_DV)
        y = _gla_finish(of_ref[:, cs] + ob_ref[:, cs], r_ref[:, cs].astype(F32), ng_ref[:, cs])
        y_ref[:, cs] = y.astype(y_ref.dtype)


def _gla_ctx(pa, ps, wg_hi, wg_lo, bg, ng, states_prev, layer):
    cb = PA_GLA // GLA_W
    rows = GLA_CTX_SEQS * SEQ
    const = lambda shape: pl.BlockSpec(shape, lambda s: (0,) * len(shape))
    in_specs = [pl.BlockSpec((rows, GLA_W), lambda s, j=j: (s, cb + j)) for j in range(4)] + [
        pl.BlockSpec((rows, LANES), lambda s: (s, PS_LR // LANES)),
        const((2, LANES, GLA_W)), const((2, LANES, GLA_W)), const((2, 1, GLA_W)), const((1, GLA_W))]
    args = [pa, pa, pa, pa, ps, wg_hi, wg_lo, bg, ng]
    aliases = {}
    if states_prev is not None:
        in_specs.append(pl.BlockSpec(memory_space=pl.ANY))
        args.append(states_prev)
        aliases = {len(args) - 1: 1}
    return pl.pallas_call(
        _gla_ctx_kernel,
        grid=(BATCH // GLA_CTX_SEQS,),
        in_specs=in_specs,
        out_specs=[pl.BlockSpec((rows, GLA_W), lambda s: (s, 0)),
                   pl.BlockSpec((GLA_CTX_SEQS, None, 2, GLA_HEADS, GLA_DK, GLA_DV),
                                lambda s: (s, layer, 0, 0, 0, 0))],
        out_shape=[jax.ShapeDtypeStruct((T_ALL, GLA_W), BF16),
                   jax.ShapeDtypeStruct((BATCH, DEPTH, 2, GLA_HEADS, GLA_DK, GLA_DV), F32)],
        input_output_aliases=aliases,
        compiler_params=_cparams("parallel"),
        name="gla_ctx",
    )(*args)


def _gla_lat(pa, ps, wg_hi, wg_lo, bg, ng, s0, y):
    rb0 = T_CTX // DEC_SEQ
    cb = PA_GLA // GLA_W
    const = lambda shape: pl.BlockSpec(shape, lambda s: (0,) * len(shape))
    return pl.pallas_call(
        _gla_lat_kernel,
        grid=(DEC_BATCH,),
        in_specs=[pl.BlockSpec((DEC_SEQ, GLA_W), lambda s, j=j: (rb0 + s, cb + j)) for j in range(4)] + [
            pl.BlockSpec((DEC_SEQ, LANES), lambda s: (rb0 + s, PS_LR // LANES)),
            const((2, LANES, GLA_W)), const((2, LANES, GLA_W)), const((2, 1, GLA_W)), const((1, GLA_W)),
            pl.BlockSpec((None, 2, GLA_HEADS, GLA_DK, GLA_DV), lambda s: (s, 0, 0, 0, 0)),
            pl.BlockSpec(memory_space=pl.ANY)],
        out_specs=pl.BlockSpec((DEC_SEQ, GLA_W), lambda s: (rb0 + s, 0)),
        out_shape=jax.ShapeDtypeStruct((T_ALL, GLA_W), BF16),
        scratch_shapes=[pltpu.VMEM((DEC_SEQ, GLA_W), F32), pltpu.VMEM((DEC_SEQ, GLA_W), F32)],
        input_output_aliases={10: 0},
        compiler_params=_cparams("parallel"),
        name="gla_lat",
    )(pa, pa, pa, pa, ps, wg_hi, wg_lo, bg, ng, s0, y)


LOG2E = 1.4426950408889634
Q_SCALE = HEAD_DIM ** -0.5 * LOG2E


def _attend(q, sink, k_parts, v_parts, masks):
    scores = []
    for k, mask in zip(k_parts, masks):
        s = _dot_t(q, k)
        if mask is not None:
            s = jnp.where(mask, s, NEG_INF)
        scores.append(s)
    sink2 = sink * LOG2E
    m = jnp.broadcast_to(sink2, (q.shape[0], 1)).astype(F32)
    for s in scores:
        m = jnp.maximum(m, jnp.max(s, axis=-1, keepdims=True))
    den = jnp.exp2(sink2 - m)
    o = None
    for s, v in zip(scores, v_parts):
        e = jnp.exp2(s - m)
        den = den + jnp.sum(e, axis=-1, keepdims=True)
        pv = _mm(e.astype(BF16), v)
        o = pv if o is None else o + pv
    return o / den


def _head(a, j):
    return a[:, j * HEAD_DIM:(j + 1) * HEAD_DIM]


ATT_ROWS_CTX = 128


def _attn_ctx_kernel(sink_ref, q_ref, kv_ref, *rest):
    o_ref = rest[-1]
    q = q_ref[...] * Q_SCALE
    kv_all = kv_ref[...]
    ks = [_head(kv_all, kv).astype(BF16) for kv in range(ATT_KV_HEADS)]
    vs = [_head(kv_all, ATT_KV_HEADS + kv).astype(BF16) for kv in range(ATT_KV_HEADS)]
    outs = []
    for h in range(ATT_HEADS):
        kv = h // ATT_GROUP
        qh = _head(q, h).astype(BF16)
        blocks = [_attend(qh[r:r + ATT_ROWS_CTX], sink_ref[h], [ks[kv]], [vs[kv]], [None])
                  for r in range(0, SEQ, ATT_ROWS_CTX)]
        outs.append(jnp.concatenate(blocks, axis=0))
    o_ref[...] = jnp.concatenate(outs, axis=-1).astype(o_ref.dtype)


def _attn_ctx(ps, sink):
    return pl.pallas_call(
        _attn_ctx_kernel,
        grid=(BATCH,),
        in_specs=[
            pl.BlockSpec(memory_space=pltpu.SMEM),
            pl.BlockSpec((SEQ, ATT_W), lambda s: (s, 0)),
            pl.BlockSpec((SEQ, 2 * LANES), lambda s: (s, PS_KV // (2 * LANES))),
        ],
        out_specs=pl.BlockSpec((SEQ, ATT_W), lambda s: (s, 0)),
        out_shape=jax.ShapeDtypeStruct((T_ALL, ATT_W), BF16),
        compiler_params=_cparams("parallel"),
        name="attn_ctx",
    )(sink, ps, ps)


def _rope(x, cos, sin_signed):
    lane = lax.broadcasted_iota(jnp.int32, x.shape, 1)
    partner = jnp.where((lane & 31) < 16, pltpu.roll(x, LANES - 16, 1), pltpu.roll(x, 16, 1))
    return x * cos + partner * sin_signed


def _attn_lat_kernel(sink_ref, q_ref, kvp_ref, kvc_ref, kvn_ref, kctx_ref, vctx_ref, cos_ref, sin_ref, yin_ref,
                     o_ref):
    del yin_ref
    n = pl.program_id(1)
    n_blk = DEC_SEQ // BLOCK
    start = n * BLOCK

    def table(ref, blk):
        blk = jnp.clip(blk, 0, n_blk - 1)
        return ref[pl.ds(pl.multiple_of(blk * BLOCK, BLOCK), BLOCK), :]

    k_loc, v_loc = [], []
    for off, ref in ((-1, kvp_ref), (0, kvc_ref), (1, kvn_ref)):
        k_loc.append(_rope(ref[:, :LANES], table(cos_ref, n + off), table(sin_ref, n + off)))
        v_loc.append(ref[:, LANES:])
    k_loc = jnp.concatenate(k_loc, axis=0)
    v_loc = jnp.concatenate(v_loc, axis=0)
    k_ctx = kctx_ref[...]
    v_ctx = vctx_ref[...]

    span = 3 * BLOCK
    stacked = (ATT_GROUP * BLOCK, span)
    qpos = start + (lax.broadcasted_iota(jnp.int32, stacked, 0) & (BLOCK - 1))
    kpos = start - WINDOW + lax.broadcasted_iota(jnp.int32, stacked, 1)
    valid = (jnp.abs(qpos - kpos) <= WINDOW) & (kpos >= 0) & (kpos < DEC_SEQ)

    cos_q = table(cos_ref, n)
    sin_q = table(sin_ref, n)
    q_heads = []
    for pair in range(ATT_HEADS // 2):
        qr = _rope(q_ref[:, pair * LANES:(pair + 1) * LANES], cos_q, sin_q) * Q_SCALE
        q_heads += [_head(qr, 0).astype(BF16), _head(qr, 1).astype(BF16)]
    outs = []
    for kv in range(ATT_KV_HEADS):
        heads = range(kv * ATT_GROUP, (kv + 1) * ATT_GROUP)
        q = jnp.concatenate([q_heads[h] for h in heads], axis=0)
        sink = jnp.concatenate([jnp.full((BLOCK, 1), sink_ref[h], F32) for h in heads], axis=0)
        o = _attend(q, sink, [_head(k_ctx, kv).astype(BF16), _head(k_loc, kv).astype(BF16)],
                    [_head(v_ctx, kv).astype(BF16), _head(v_loc, kv).astype(BF16)], [None, valid])
        outs += [o[g * BLOCK:(g + 1) * BLOCK] for g in range(ATT_GROUP)]
    o_ref[...] = jnp.concatenate(outs, axis=-1).astype(o_ref.dtype)


def _attn_lat(ps, sink, k_ctx, v_ctx, cos_t, sin_t, y):
    n_blk = DEC_SEQ // BLOCK
    rb0 = T_CTX // BLOCK

    def kv_spec(off):
        return pl.BlockSpec(
            (BLOCK, 2 * LANES),
            lambda b, n: (rb0 + b * n_blk + jnp.clip(n + off, 0, n_blk - 1), PS_KV // (2 * LANES)))

    return pl.pallas_call(
        _attn_lat_kernel,
        grid=(DEC_BATCH, n_blk),
        in_specs=[
            pl.BlockSpec(memory_space=pltpu.SMEM),
            pl.BlockSpec((BLOCK, ATT_W), lambda b, n: (rb0 + b * n_blk + n, 0)),
            kv_spec(-1), kv_spec(0), kv_spec(1),
            pl.BlockSpec((None, PAST_LEN, LANES), lambda b, n: (b, 0, 0)),
            pl.BlockSpec((None, PAST_LEN, LANES), lambda b, n: (b, 0, 0)),
            pl.BlockSpec((DEC_SEQ, LANES), lambda b, n: (0, 0)),
            pl.BlockSpec((DEC_SEQ, LANES), lambda b, n: (0, 0)),
            pl.BlockSpec(memory_space=pl.ANY),
        ],
        out_specs=pl.BlockSpec((BLOCK, ATT_W), lambda b, n: (rb0 + b * n_blk + n, 0)),
        out_shape=jax.ShapeDtypeStruct((T_ALL, ATT_W), BF16),
        input_output_aliases={9: 0},
        compiler_params=_cparams("parallel", "parallel"),
        name="attn_lat",
    )(sink, ps, ps, ps, ps, k_ctx, v_ctx, cos_t, sin_t, y)


def _rope_tables():
    pos = np.arange(DEC_SEQ)
    n_freq = HEAD_DIM // 4
    inv = jnp.asarray(ROPE_THETA, F32) ** (-jnp.arange(n_freq, dtype=F32) / n_freq)
    row = jnp.asarray(pos // GRID_W, F32)
    colp = jnp.asarray(pos % GRID_W, F32)
    ang_r = row[:, None] * inv[None, :]
    ang_c = colp[:, None] * inv[None, :]
    cos = jnp.concatenate([jnp.cos(ang_r)] * 2 + [jnp.cos(ang_c)] * 2, axis=-1)
    sin = jnp.concatenate([-jnp.sin(ang_r), jnp.sin(ang_r), -jnp.sin(ang_c), jnp.sin(ang_c)], axis=-1)
    return jnp.tile(cos, (1, 2)), jnp.tile(sin, (1, 2))


def _split_dot(a, w_hi, w_lo):
    a_hi = a.astype(BF16)
    a_lo = (a - a_hi.astype(F32)).astype(BF16)
    return _mm(a_hi, w_hi) + _mm(a_lo, w_hi) + _mm(a_hi, w_lo)


def _route(logits):
    lane_i = lax.broadcasted_iota(jnp.int32, logits.shape, 1)
    lane = lane_i.astype(F32)
    big = jnp.float32(1 << 20)
    is_g = lane_i < N_EXPERT_GROUPS
    lg = jnp.where(is_g, logits, -jnp.inf)
    m_g = jnp.max(lg, axis=-1, keepdims=True)
    grp = jnp.min(jnp.where(lg == m_g, lane, big), axis=-1, keepdims=True)
    z_g = jnp.sum(jnp.where(is_g, jnp.exp(lg - m_g), 0.0), axis=-1, keepdims=True)
    p_grp = 1.0 / z_g

    e_idx = lane_i - ROUTE_E0
    e_grp = (e_idx >> 2).astype(F32)
    sel = (e_idx >= 0) & (e_idx < N_EXPERTS) & (e_grp == grp)
    le = jnp.where(sel, logits, -jnp.inf)
    m_e = jnp.max(le, axis=-1, keepdims=True)
    ex = jnp.where(sel, jnp.exp(le - m_e), 0.0)
    pe = ex / jnp.sum(ex, axis=-1, keepdims=True)
    pe = jnp.where(sel, pe, -1.0)
    v1 = jnp.max(pe, axis=-1, keepdims=True)
    i1 = jnp.min(jnp.where(pe == v1, lane, big), axis=-1, keepdims=True)
    pe2 = jnp.where(lane == i1, -1.0, pe)
    v2 = jnp.max(pe2, axis=-1, keepdims=True)
    i2 = jnp.min(jnp.where(pe2 == v2, lane, big), axis=-1, keepdims=True)
    tot = v1 + v2
    return (jnp.where(lane == i1, p_grp * (v1 / tot), 0.0)
            + jnp.where(lane == i2, p_grp * (v2 / tot), 0.0)
            + jnp.where(lane_i == LANES - 1, grp, 0.0))


HALO = 16


def _merge_kernel(gate_ref, conv_ref, cprev_ref, cnext_ref, yb_ref, yc_ref, *rest, x_split):
    x_refs, rest = rest[:2 if x_split else 1], rest[2 if x_split else 1:]
    (mod_ref, cw_ref, wb_ref, wo_ref, g2_ref, wr_hi_ref, wr_lo_ref, br_ref,
     x1_ref, h2_ref, rg_ref, wbb_ref, wob_ref) = rest
    tm = TM_MERGE
    i = pl.program_id(0)

    def x_rows(rows):
        if not x_split:
            return x_refs[0][rows, :]
        return jnp.where(i < x_split, x_refs[0][rows, :], x_refs[1][rows, :])

    @pl.when(i == 0)
    def _():
        wbb_ref[...] = wb_ref[...].astype(BF16)
        wob_ref[...] = wo_ref[...].astype(BF16)

    def gated(ref):
        return ref[:, CONV_W:2 * CONV_W].astype(F32) * ref[:, 2 * CONV_W:3 * CONV_W].astype(F32)

    a_b = conv_ref[:, 0:CONV_W].astype(F32)
    u = gated(conv_ref)
    u_before = gated(cprev_ref)[HALO - 1:HALO]
    u_after = gated(cnext_ref)[0:1]
    r = lax.broadcasted_iota(jnp.int32, (tm, 1), 0)
    g_row = i * tm + r
    seq_mask = jnp.where(g_row < T_CTX, SEQ - 1, DEC_SEQ - 1)
    first = (g_row & seq_mask) == 0
    last = ((g_row + 1) & seq_mask) == 0
    u_prev = jnp.where(r == 0, u_before, pltpu.roll(u, 1, 0))
    u_next = jnp.where(r == tm - 1, u_after, pltpu.roll(u, tm - 1, 0))
    u_prev = jnp.where(first, 0.0, u_prev)
    u_next = jnp.where(last, 0.0, u_next)
    y_a = (a_b * (u_prev * cw_ref[0:1, :] + u * cw_ref[1:2, :] + u_next * cw_ref[2:3, :])).astype(BF16)

    blocks = [slice(r0, r0 + MERGE_ROWS) for r0 in range(0, tm, MERGE_ROWS)]
    ys = [(y_a[rows], yb_ref[rows, :], yc_ref[rows, :]) for rows in blocks]
    branches = [[_mm(y, wbb_ref[j]) for j, y in enumerate(y3)] for y3 in ys]
    zs = []
    for rows, br3 in zip(blocks, branches):
        z = sum(gate_ref[rows, j * D_MODEL:(j + 1) * D_MODEL].astype(F32) * br3[j] for j in range(3))
        zs.append(z.astype(BF16))
    outs = [_mm(z, wob_ref[...]) for z in zs]
    h2s = []
    for rows, o in zip(blocks, outs):
        x1 = x_rows(rows) + mod_ref[2:3, :] * o
        x1_ref[rows, :] = x1
        h2 = _mod_norm(x1, g2_ref[...], mod_ref[...], 3)
        h2_ref[rows, :] = h2.astype(BF16)
        h2s.append(h2)
    logits = [_split_dot(h2, wr_hi_ref[...], wr_lo_ref[...]) + br_ref[...] for h2 in h2s]
    for rows, lg in zip(blocks, logits):
        rg_ref[rows, :] = _route(lg)


def _merge(pg, pa, y_b, y_c, x, mod_l, conv_w, layer, wb, wo, g2, wr_hi, wr_lo, br):
    tm = TM_MERGE
    n_tiles = T_ALL // tm
    hb = tm // HALO
    const = lambda shape: pl.BlockSpec(shape, lambda i: (0,) * len(shape))
    if isinstance(x, tuple):
        x_specs, x_split = _split_x_specs(tm)
    else:
        x, x_specs, x_split = (x,), [pl.BlockSpec((tm, D_MODEL), lambda i: (i, 0))], 0
    return pl.pallas_call(
        functools.partial(_merge_kernel, x_split=x_split),
        grid=(n_tiles,),
        in_specs=[
            pl.BlockSpec((tm, PG_W), lambda i: (i, 0)),
            pl.BlockSpec((tm, 3 * CONV_W), lambda i: (i, 0)),
            pl.BlockSpec((HALO, 3 * CONV_W), lambda i: (jnp.maximum(i * hb - 1, 0), 0)),
            pl.BlockSpec((HALO, 3 * CONV_W), lambda i: (jnp.minimum((i + 1) * hb, n_tiles * hb - 1), 0)),
            pl.BlockSpec((tm, GLA_W), lambda i: (i, 0)),
            pl.BlockSpec((tm, ATT_W), lambda i: (i, 0)),
            *x_specs,
            pl.BlockSpec((None, 6, D_MODEL), lambda i: (_cond_row(i, tm), 0, 0)),
            const((CONV_K, CONV_W)),
            pl.BlockSpec((None, 3, 512, D_MODEL), lambda i: (layer, 0, 0, 0)),
            pl.BlockSpec((None, D_MODEL, D_MODEL), lambda i: (layer, 0, 0)),
            const((1, D_MODEL)),
            const((D_MODEL, LANES)),
            const((D_MODEL, LANES)),
            const((1, LANES)),
        ],
        out_specs=[
            pl.BlockSpec((tm, D_MODEL), lambda i: (i, 0)),
            pl.BlockSpec((tm, D_MODEL), lambda i: (i, 0)),
            pl.BlockSpec((tm, LANES), lambda i: (i, 0)),
        ],
        out_shape=[
            jax.ShapeDtypeStruct((T_ALL, D_MODEL), F32),
            jax.ShapeDtypeStruct((T_ALL, D_MODEL), BF16),
            jax.ShapeDtypeStruct((T_ALL, LANES), F32),
        ],
        scratch_shapes=[pltpu.VMEM((3, 512, D_MODEL), BF16), pltpu.VMEM((D_MODEL, D_MODEL), BF16)],
        compiler_params=_cparams("arbitrary"),
        name="merge",
    )(pg, pa, pa, pa, y_b, y_c, *x, mod_l, conv_w, wb, wo, g2, wr_hi, wr_lo, br)


SUB = 128
MOE_ROWS = 256


def _dispatch_tables(rg):
    n_tiles = T_ALL // TM
    grp = rg[:, LANES - 1].astype(jnp.int32).reshape(n_tiles, TM)
    hot = grp[..., None] == jnp.arange(N_EXPERT_GROUPS, dtype=jnp.int32)
    onehot = hot.astype(jnp.int32)
    cnt = onehot.sum(axis=1)
    start = jnp.cumsum(cnt, axis=1) - cnt
    first = start // SUB
    nblk = jnp.where(cnt > 0, (start + cnt + SUB - 1) // SUB - first, 0)
    before = jnp.tril(jnp.ones((TM, TM), BF16), -1)
    rank = jnp.einsum("ts,nsg->ntg", before, hot.astype(BF16), preferred_element_type=F32).astype(jnp.int32)
    slot = ((start[:, None, :] + rank) * onehot).sum(axis=-1)
    return (first.reshape(-1), nblk.reshape(-1), slot.reshape(n_tiles, 1, TM), slot.reshape(T_ALL, 1))


def _moe_kernel(bstart_ref, nblk_ref, h_ref, rg_ref, srow_ref, scol_ref, x1_ref, mod_ref, wg_ref, wu_ref,
                wd_ref, g_ref, *rest, final, tile0):
    if final:
        y_ref, hs_ref, gs_ref, os_ref = rest
    else:
        modn_ref, x2_ref, hn_ref, hs_ref, gs_ref, os_ref = rest
    i = pl.program_id(0)
    grp = pl.program_id(1)

    @pl.when(grp == 0)
    def _():
        slot_of_token = srow_ref[...]
        payload = jnp.concatenate([h_ref[...], *_split3(rg_ref[...])], axis=-1)
        for r0 in range(0, TM, MOE_ROWS):
            rows = slice(r0, r0 + MOE_ROWS)
            slot = r0 + lax.broadcasted_iota(jnp.int32, (MOE_ROWS, TM), 0)
            pm = jnp.where(slot == slot_of_token, 1.0, 0.0).astype(BF16)
            moved = _mm(pm, payload)
            hs_ref[rows, :] = moved[:, :D_MODEL].astype(BF16)
            gs_ref[rows, :] = sum(moved[:, D_MODEL + j * LANES:D_MODEL + (j + 1) * LANES] for j in range(3))
        os_ref[...] = jnp.zeros_like(os_ref)

    k = (tile0 + i) * N_EXPERT_GROUPS + grp
    first = bstart_ref[k]
    n_blocks = nblk_ref[k]

    def experts(block0, n_sub):
        n_rows = n_sub * SUB
        rows = pl.ds(pl.multiple_of(block0 * SUB, SUB), n_rows)
        lane = lax.broadcasted_iota(jnp.int32, (n_rows, LANES), 1)
        x = hs_ref[rows, :]
        gates = gs_ref[rows, :]
        acc = jnp.zeros((n_rows, D_MODEL), F32)
        for e in range(EXPERTS_PER_GROUP):
            ge = _mm(x, wg_ref[e])
            ue = _mm(x, wu_ref[e])
            w = jnp.sum(jnp.where(lane == ROUTE_E0 + grp * EXPERTS_PER_GROUP + e, gates, 0.0),
                        axis=-1, keepdims=True)
            hid = (ge * jax.nn.sigmoid(ge)) * ue * w
            acc = acc + _mm(hid.astype(BF16), wd_ref[e])
        os_ref[rows, :] = (os_ref[rows, :].astype(F32) + acc).astype(BF16)

    odd = n_blocks % 2 == 1
    n_pairs = jnp.where(odd & (n_blocks >= 3), (n_blocks - 3) // 2, n_blocks // 2)

    def pair(p, carry):
        experts(first + 2 * p, 2)
        return carry

    lax.fori_loop(0, n_pairs, pair, 0)

    @pl.when(odd & (n_blocks >= 3))
    def _():
        experts(first + n_blocks - 3, 3)

    @pl.when(n_blocks == 1)
    def _():
        experts(first, 1)

    @pl.when(grp == N_EXPERT_GROUPS - 1)
    def _():
        sorted_out = os_ref[...]
        lane_slot = lax.broadcasted_iota(jnp.int32, (MOE_ROWS, TM), 1)
        for r0 in range(0, TM, MOE_ROWS):
            rows = slice(r0, r0 + MOE_ROWS)
            pt = jnp.where(lane_slot == scol_ref[rows, :], 1.0, 0.0).astype(BF16)
            x2 = x1_ref[rows, :] + mod_ref[5:6, :] * _mm(pt, sorted_out)
            if final:
                y_ref[rows, :] = _rms(x2) * g_ref[...]
            else:
                x2_ref[rows, :] = x2
                hn_ref[rows, :] = _mod_norm(x2, g_ref[...], modn_ref[...], 0).astype(BF16)


def _moe(tables, h2, rg, x1, mod_l, layer, w_eg, w_eu, w_ed, g, mod_next, tile0, n_tiles):
    bstart, nblk, slot_row, slot_col = tables
    final = mod_next is None
    row = lambda i, e, *_: (tile0 + i, 0)
    mod_spec = lambda: pl.BlockSpec((None, 6, D_MODEL), lambda i, e, *_: (_cond_row(tile0 + i, TM), 0, 0))
    group = lambda i, e, *_: (layer, e, 0, 0)
    in_specs = [
        pl.BlockSpec((TM, D_MODEL), row),
        pl.BlockSpec((TM, LANES), row),
        pl.BlockSpec((None, 1, TM), lambda i, e, *_: (tile0 + i, 0, 0)),
        pl.BlockSpec((TM, 1), row),
        pl.BlockSpec((TM, D_MODEL), row),
        mod_spec(),
        pl.BlockSpec((None, EXPERTS_PER_GROUP, D_MODEL, D_EXPERT), group),
        pl.BlockSpec((None, EXPERTS_PER_GROUP, D_MODEL, D_EXPERT), group),
        pl.BlockSpec((None, EXPERTS_PER_GROUP, D_EXPERT, D_MODEL), group),
        pl.BlockSpec((1, D_MODEL), lambda i, e, *_: (0, 0)),
    ]
    args = [h2, rg, slot_row, slot_col, x1, mod_l, w_eg, w_eu, w_ed, g]
    if final:
        out_specs = pl.BlockSpec((TM, D_MODEL), lambda i, e, *_: (i, 0))
        out_shape = jax.ShapeDtypeStruct((n_tiles * TM, D_MODEL), F32)
    else:
        in_specs.append(mod_spec())
        args.append(mod_next)
        out_specs = [pl.BlockSpec((TM, D_MODEL), row)] * 2
        out_shape = [jax.ShapeDtypeStruct((T_ALL, D_MODEL), F32), jax.ShapeDtypeStruct((T_ALL, D_MODEL), BF16)]
    return pl.pallas_call(
        functools.partial(_moe_kernel, final=final, tile0=tile0),
        grid_spec=pltpu.PrefetchScalarGridSpec(
            num_scalar_prefetch=2,
            grid=(n_tiles, N_EXPERT_GROUPS),
            in_specs=in_specs,
            out_specs=out_specs,
            scratch_shapes=[pltpu.VMEM((TM, D_MODEL), BF16), pltpu.VMEM((TM, LANES), F32),
                            pltpu.VMEM((TM, D_MODEL), BF16)],
        ),
        out_shape=out_shape,
        compiler_params=_cparams("parallel", "arbitrary"),
        name="moe_final" if final else "moe",
    )(bstart, nblk, *args)


def _cache_kernel(*refs):
    ps_refs, (k_ref, v_ref) = refs[:DEPTH], refs[DEPTH:]
    n_seq = TM // SEQ
    for l, ps_ref in enumerate(ps_refs):
        k_ref[:, l] = ps_ref[:, :LANES].reshape(n_seq, SEQ, LANES)
        v_ref[:, l] = ps_ref[:, LANES:].reshape(n_seq, SEQ, LANES)


def _cache(ps_layers):
    n_seq = TM // SEQ
    out = pl.BlockSpec((n_seq, DEPTH, SEQ, LANES), lambda i: (i, 0, 0, 0))
    return pl.pallas_call(
        _cache_kernel,
        grid=(T_CTX // TM,),
        in_specs=[pl.BlockSpec((TM, 2 * LANES), lambda i: (i, PS_KV // (2 * LANES)))] * DEPTH,
        out_specs=[out, out],
        out_shape=[jax.ShapeDtypeStruct((BATCH, DEPTH, SEQ, LANES), F32)] * 2,
        compiler_params=_cparams("parallel"),
        name="cache",
    )(*ps_layers)


def _hi_lo(w):
    hi = w.astype(BF16)
    return hi, (w - hi.astype(F32)).astype(BF16)


def _prep_gla_gate(w_gate):
    pads = [((0, 0), (d * GLA_RANK, LANES - (d + 1) * GLA_RANK), (0, 0)) for d in range(2)]
    return _hi_lo(jnp.stack([jnp.pad(w_gate[:, d], pads[d]) for d in range(2)], axis=1))


def _prep_router(w_rg, b_rg, w_re, b_re):
    unused = LANES - N_EXPERT_GROUPS - N_EXPERTS
    w = jnp.concatenate([w_rg, w_re, jnp.zeros((DEPTH, D_MODEL, unused), F32)], axis=-1)
    b = jnp.concatenate([b_rg, b_re, jnp.zeros((DEPTH, unused), F32)], axis=-1)
    return (*_hi_lo(w), b.reshape(DEPTH, 1, LANES))


def kernel(x_prompt, x_sample, state_gla, cache_k, cache_v, c, c_ctx, w_ada, b_ada, norm1_g, norm2_g, w_in,
           conv_w, gla_w_gate, gla_b_gate, gla_norm_g, attn_sink, w_branch, w_out, w_route_group,
           b_route_group, w_route_expert, b_route_expert, w_exp_gate, w_exp_up, w_exp_down, final_norm_g):
    cond = jnp.zeros((N_COND, D_MODEL), F32).at[0].set(c_ctx).at[1:1 + DEC_BATCH].set(c)
    mod = _modulation(cond, w_ada, b_ada).reshape(DEPTH, N_COND, 6, D_MODEL)
    cos_t, sin_t = _rope_tables()
    row = lambda v: v.reshape(1, -1)

    x = (x_prompt.reshape(T_CTX, D_MODEL), x_sample.reshape(T_LAT, D_MODEL))
    h = _prenorm(*x, mod[0], row(norm1_g[0]))
    wg_hi_all, wg_lo_all = _prep_gla_gate(gla_w_gate)
    wr_hi_all, wr_lo_all, br_all = _prep_router(w_route_group, b_route_group, w_route_expert, b_route_expert)
    states = None
    ps_layers = []
    y_prompt = y_sample = None
    w_t = jnp.swapaxes(w_in, 1, 2)
    proj_a = functools.partial(_proj, pieces=[(0, PA_W // 2)], n_tiles=2, out_dtype=BF16, name="proj_a",
                               act="silu_tail")
    proj_s = functools.partial(_proj, pieces=[(W_IN_TQ, W_IN_GATE - W_IN_TQ), (W_IN_LR, LANES)], n_tiles=1,
                               out_dtype=F32, name="proj_s")
    proj_g = functools.partial(_proj, pieces=[(W_IN_GATE, PG_W // 2)], n_tiles=2, out_dtype=BF16,
                               name="proj_g", act="sigmoid")
    experts = None
    for l in range(DEPTH):
        if l == 0:
            pa, w_eg = proj_a(h, w_t, l, cast=(w_exp_gate, 2))
            ps, w_ed = proj_s(h, w_t, l, cast=(w_exp_down, 4))
            pg, w_eu = proj_g(h, w_t, l, cast=(w_exp_up, 2))
            experts = (w_eg, w_eu, w_ed)
        else:
            pa, ps, pg = proj_a(h, w_t, l), proj_s(h, w_t, l), proj_g(h, w_t, l)
        ps_layers.append(ps)

        wg_hi, wg_lo = wg_hi_all[l], wg_lo_all[l]
        bg = gla_b_gate[l].reshape(2, 1, GLA_W)
        ng = row(gla_norm_g[l])
        y_b, states = _gla_ctx(pa, ps, wg_hi, wg_lo, bg, ng, states, l)
        y_b = _gla_lat(pa, ps, wg_hi, wg_lo, bg, ng, state_gla[:, l], y_b)
        y_c = _attn_ctx(ps, attn_sink[l])
        y_c = _attn_lat(ps, attn_sink[l], cache_k[:, l].reshape(DEC_BATCH, PAST_LEN, LANES),
                        cache_v[:, l].reshape(DEC_BATCH, PAST_LEN, LANES), cos_t, sin_t, y_c)

        x1, h2, rg = _merge(pg, pa, y_b, y_c, x, mod[l], conv_w[l], l, w_branch, w_out, row(norm2_g[l]),
                            wr_hi_all[l], wr_lo_all[l], br_all[l])
        moe_in = (_dispatch_tables(rg), h2, rg, x1, mod[l], l, *experts)
        if l + 1 < DEPTH:
            x, h = _moe(*moe_in, row(norm1_g[l + 1]), mod[l + 1], 0, T_ALL // TM)
        else:
            gf = row(final_norm_g)
            y_prompt = _moe(*moe_in, gf, None, 0, T_CTX // TM)
            y_sample = _moe(*moe_in, gf, None, T_CTX // TM, T_LAT // TM)

    new_k, new_v = _cache(ps_layers)
    kv_shape = (BATCH, DEPTH, SEQ, ATT_KV_HEADS, HEAD_DIM)
    return (y_prompt.reshape(BATCH, SEQ, D_MODEL), y_sample.reshape(DEC_BATCH, DEC_SEQ, D_MODEL), states,
            new_k.reshape(kv_shape), new_v.reshape(kv_shape))
```

```python
import functools

import jax
import jax.numpy as jnp
import numpy as np
from jax import lax
from jax.experimental import pallas as pl
from jax.experimental.pallas import tpu as pltpu

F32 = jnp.float32
BF16 = jnp.bfloat16

D_MODEL = 1024
BATCH = 32
SEQ = 256
DEPTH = 2
DEC_BATCH = 2
DEC_SEQ = 2048
PAST_LEN = 512
GRID_W = 64
EPS = 1e-6
CONV_W = 512
CONV_K = 3
GLA_HEADS = 4
GLA_DK = 128
GLA_DV = 128
GLA_W = GLA_HEADS * GLA_DV
GLA_RANK = 16
GLA_TAU = 16.0
GLA_CHUNK = 64
ATT_HEADS = 8
ATT_KV_HEADS = 2
ATT_GROUP = ATT_HEADS // ATT_KV_HEADS
HEAD_DIM = 64
ATT_W = ATT_HEADS * HEAD_DIM
WINDOW = 128
BLOCK = 128
ROPE_THETA = 10000.0
N_EXPERT_GROUPS = 4
EXPERTS_PER_GROUP = 4
N_EXPERTS = 16
D_EXPERT = 256
NEG_INF = -1e30

T_CTX = BATCH * SEQ
T_LAT = DEC_BATCH * DEC_SEQ
T_ALL = T_CTX + T_LAT
N_COND = 8
LANES = 128

W_IN_LR = 3584
W_IN_TQ = 3616
W_IN_GATE = 4384
N_IN = 7456
PA_W = 3584
PA_GLA = 1536
PS_W = 896
PS_KV = 512
PS_LR = 768
PG_W = 3 * D_MODEL
ROUTE_E0 = N_EXPERT_GROUPS

TM = 1024
TM_MERGE = 512
MERGE_ROWS = 256
VMEM_LIMIT = 56 * 1024 * 1024


def _cparams(*sem):
    return pltpu.CompilerParams(dimension_semantics=sem, vmem_limit_bytes=VMEM_LIMIT)


def _cond_row(i, tm):
    n_ctx = T_CTX // tm
    per = DEC_SEQ // tm
    return jnp.where(i < n_ctx, 0, 1 + (i - n_ctx) // per)


def _mm(a, b):
    return jnp.dot(a, b, preferred_element_type=F32)


def _dot_t(a, b):
    return lax.dot_general(a, b, (((1,), (1,)), ((), ())), preferred_element_type=F32)


def _dot_ta(a, b):
    return lax.dot_general(a, b, (((0,), (0,)), ((), ())), preferred_element_type=F32)


def _hi_lo(w):
    hi = w.astype(BF16)
    return hi, (w - hi.astype(F32)).astype(BF16)


def _pack_split(w):
    hi, lo = _hi_lo(w)
    return jnp.concatenate([jnp.concatenate([hi, lo], axis=-1),
                            jnp.concatenate([hi, jnp.zeros_like(lo)], axis=-1)], axis=-2)


def _split_dot(a, w_pack):
    a_hi, a_lo = _hi_lo(a)
    both = _mm(jnp.concatenate([a_hi, a_lo], axis=-1), w_pack)
    n = both.shape[-1] // 2
    return both[:, :n] + both[:, n:]


def _rms(x):
    return x * lax.rsqrt(jnp.mean(x * x, axis=-1, keepdims=True) + EPS)


def _mod_norm(x, g, mod, shift_row):
    return _rms(x) * g * (1.0 + mod[shift_row + 1:shift_row + 2, :]) + mod[shift_row:shift_row + 1, :]


def _mod_kernel(c_ref, w_ref, b_ref, o_ref):
    c = c_ref[...]
    s = (c * jax.nn.sigmoid(c)).astype(BF16)
    o_ref[...] = _mm(s, w_ref[...].astype(BF16)) + b_ref[...]


def _modulation(cond, w_ada, b_ada):
    tn = 1536
    return pl.pallas_call(
        _mod_kernel,
        grid=(DEPTH, 6 * D_MODEL // tn),
        in_specs=[
            pl.BlockSpec((N_COND, D_MODEL), lambda l, j: (0, 0)),
            pl.BlockSpec((None, D_MODEL, tn), lambda l, j: (l, 0, j)),
            pl.BlockSpec((None, 1, tn), lambda l, j: (l, 0, j)),
        ],
        out_specs=pl.BlockSpec((None, N_COND, tn), lambda l, j: (l, 0, j)),
        out_shape=jax.ShapeDtypeStruct((DEPTH, N_COND, 6 * D_MODEL), F32),
        compiler_params=_cparams("parallel", "parallel"),
        name="modulation",
    )(cond, w_ada, b_ada.reshape(DEPTH, 1, 6 * D_MODEL))


def _split_x_specs(tm):
    n_ctx = T_CTX // tm
    return [pl.BlockSpec((tm, D_MODEL), lambda i, *_: (jnp.minimum(i, n_ctx - 1), 0)),
            pl.BlockSpec((tm, D_MODEL), lambda i, *_: (jnp.maximum(i - n_ctx, 0), 0))], n_ctx


def _prenorm_kernel(xp_ref, xs_ref, mod_ref, g_ref, h_ref):
    x = jnp.where(pl.program_id(0) < T_CTX // TM, xp_ref[...], xs_ref[...])
    h_ref[...] = _mod_norm(x, g_ref[...], mod_ref[...], 0).astype(BF16)


def _prenorm(xp, xs, mod_l, g):
    x_specs, _ = _split_x_specs(TM)
    return pl.pallas_call(
        _prenorm_kernel,
        grid=(T_ALL // TM,),
        in_specs=x_specs + [
            pl.BlockSpec((None, 6, D_MODEL), lambda i: (_cond_row(i, TM), 0, 0)),
            pl.BlockSpec((1, D_MODEL), lambda i: (0, 0)),
        ],
        out_specs=pl.BlockSpec((TM, D_MODEL), lambda i: (i, 0)),
        out_shape=jax.ShapeDtypeStruct((T_ALL, D_MODEL), BF16),
        compiler_params=_cparams("parallel"),
        name="prenorm",
    )(xp, xs, mod_l, g)


def _proj_kernel(h_ref, *refs, act, n_w):
    w_refs, refs = refs[:n_w], refs[n_w:]
    if len(refs) == 4:
        cast_in_ref, o_ref, cast_out_ref, wb_ref = refs
        cast_out_ref[...] = cast_in_ref[...].astype(BF16)
    else:
        o_ref, wb_ref = refs

    @pl.when(pl.program_id(1) == 0)
    def _():
        col = 0
        for w_ref in w_refs:
            n = w_ref.shape[0]
            wb_ref[:, col:col + n] = w_ref[...].T.astype(BF16)
            col += n

    def sigmoid(t):
        return 0.5 * jnp.tanh(0.5 * t) + 0.5

    y = _mm(h_ref[...], wb_ref[...])
    if act == "sigmoid":
        y = sigmoid(y)
    if act == "silu_tail":
        last = pl.program_id(0) == pl.num_programs(0) - 1
        tail = y[:, -GLA_W:]
        o_ref[:, :-GLA_W] = y[:, :-GLA_W].astype(o_ref.dtype)
        o_ref[:, -GLA_W:] = jnp.where(last, tail * sigmoid(tail), tail).astype(o_ref.dtype)
    else:
        o_ref[...] = y.astype(o_ref.dtype)


def _proj(h, w_t, layer, pieces, n_tiles, out_dtype, name, act=None, cast=None):
    tn = sum(n for _, n in pieces)
    n_rows = T_ALL // TM

    def w_spec(c0, n):
        return pl.BlockSpec((pl.Squeezed(), pl.Element(n), pl.Element(D_MODEL)),
                            lambda j, i: (layer, pl.multiple_of(c0 + j * tn, 8), 0))

    in_specs = [pl.BlockSpec((TM, D_MODEL), lambda j, i: (i, 0))] + [w_spec(c0, n) for c0, n in pieces]
    out_specs = [pl.BlockSpec((TM, tn), lambda j, i: (i, j))]
    out_shape = [jax.ShapeDtypeStruct((T_ALL, n_tiles * tn), out_dtype)]
    args = [h] + [w_t] * len(pieces)
    if cast is not None:
        src, per_block = cast
        per_layer = N_EXPERTS // per_block
        n_blocks = DEPTH * per_layer
        assert n_blocks <= n_tiles * n_rows

        def block(j, i):
            b = jnp.minimum(j * n_rows + i, n_blocks - 1)
            return (b // per_layer, b % per_layer, 0, 0)

        spec = pl.BlockSpec((None, per_block) + src.shape[2:], block)
        in_specs.append(spec)
        out_specs.append(spec)
        out_shape.append(jax.ShapeDtypeStruct(src.shape, BF16))
        args.append(src)
    out = pl.pallas_call(
        functools.partial(_proj_kernel, act=act, n_w=len(pieces)),
        grid=(n_tiles, n_rows),
        in_specs=in_specs,
        out_specs=out_specs,
        out_shape=out_shape,
        scratch_shapes=[pltpu.VMEM((D_MODEL, tn), BF16)],
        compiler_params=_cparams("arbitrary", "arbitrary"),
        name=name,
    )(*args)
    return out[0] if cast is None else out


GLA_GROUP = 256
GLA_CTX_SEQS = 2


def _log_sigmoid(z):
    return jnp.minimum(z, 0.0) - jnp.log(1.0 + jnp.exp(-jnp.abs(z)))


def _split3(x):
    hi = x.astype(BF16)
    r1 = x - hi.astype(F32)
    mid = r1.astype(BF16)
    lo = (r1 - mid.astype(F32)).astype(BF16)
    return hi, mid, lo


def _gla_keep(d):
    ri = lax.broadcasted_iota(jnp.int32, (GLA_GROUP, GLA_GROUP), 0)
    ci = lax.broadcasted_iota(jnp.int32, (GLA_GROUP, GLA_GROUP), 1)
    if d == 0:
        return (ci <= ri) & (ci >= (ri & ~(GLA_CHUNK - 1)))
    return (ci >= ri) & (ci <= (ri | (GLA_CHUNK - 1)))


def _gla_group(load, lr, wg_pack, bg, states, d, keep):
    C = GLA_CHUNK
    nc = GLA_GROUP // C
    width = len(states) * GLA_DK
    lr = jnp.where(lax.broadcasted_iota(jnp.int32, lr.shape, 1) < 2 * GLA_RANK, lr, 0.0)
    z = _split_dot(lr, wg_pack) + bg
    la = _log_sigmoid(z) / GLA_TAU
    tri = jnp.where(keep, 1.0, 0.0).astype(BF16)
    parts = _mm(tri, jnp.concatenate(_split3(la), axis=-1))
    b_all = parts[:, :width] + parts[:, width:2 * width] + parts[:, 2 * width:]
    edge = C - 1 if d == 0 else 0
    order = range(nc) if d == 0 else range(nc - 1, -1, -1)
    row_chunk = lax.broadcasted_iota(jnp.int32, (GLA_GROUP, GLA_DK), 0) // C
    in_chunk = [row_chunk == c for c in range(nc)]
    outs, new_states = [], []
    for h, st in enumerate(states):
        b = b_all[:, h * GLA_DK:(h + 1) * GLA_DK]
        b_last = [b[c * C + edge:c * C + edge + 1, :] for c in range(nc)]
        bl = jnp.concatenate([jnp.broadcast_to(t, (C, GLA_DK)) for t in b_last], axis=0)
        k = load(1, h)
        q_in = load(0, h) * (GLA_DK ** -0.5) * jnp.exp(b)
        k_in = (k * jnp.exp(-b)).astype(BF16)
        k_end = k * jnp.exp(bl - b)
        vb = load(2, h).astype(BF16)
        att = jnp.where(keep, _dot_t(q_in.astype(BF16), k_in), 0.0)
        o = _mm(att.astype(BF16), vb)
        k_spread = jnp.concatenate([jnp.where(in_chunk[c], k_end, 0.0) for c in range(nc)], axis=-1)
        q_spread = jnp.concatenate([jnp.where(in_chunk[c], q_in, 0.0) for c in range(nc)], axis=-1)
        ds_t = _dot_ta(vb, k_spread.astype(BF16))
        starts = [None] * nc
        for c in order:
            starts[c] = st
            st = st * jnp.exp(b_last[c]) + ds_t[:, c * GLA_DK:(c + 1) * GLA_DK]
        o = o + _dot_t(q_spread.astype(BF16), jnp.concatenate(starts, axis=-1).astype(BF16))
        outs.append(o)
        new_states.append(st)
    return outs, new_states


def _gla_loader(qkv_refs, rows):
    def load(i, h):
        return qkv_refs[i][rows, h * GLA_DK:(h + 1) * GLA_DK].astype(F32)
    return load


def _gla_finish(o, r_act, ng):
    o = o * lax.rsqrt(jnp.mean(o * o, axis=-1, keepdims=True) + EPS)
    return o * ng * r_act


def _gla_ctx_kernel(q_ref, k_ref, v_ref, r_ref, lr_ref, wg_ref, bg_ref, ng_ref, *rest):
    y_ref, s_ref = rest[-2:]
    zero = jnp.zeros((GLA_DV, GLA_DK), F32)
    keep = [_gla_keep(0), _gla_keep(1)]
    for s in range(GLA_CTX_SEQS):
        rows = slice(s * SEQ, (s + 1) * SEQ)
        load = _gla_loader((q_ref, k_ref, v_ref), rows)
        lr = lr_ref[rows, :]
        o_dir = []
        for d in range(2):
            outs, sts = _gla_group(load, lr, wg_ref[d], bg_ref[d], [zero] * GLA_HEADS, d,
                                   keep[d])
            o_dir.append(outs)
            for h in range(GLA_HEADS):
                s_ref[s, d, h] = sts[h].T
        for h in range(GLA_HEADS):
            cs = slice(h * GLA_DV, (h + 1) * GLA_DV)
            y = _gla_finish(o_dir[0][h] + o_dir[1][h], r_ref[rows, cs].astype(F32), ng_ref[:, cs])
            y_ref[rows, cs] = y.astype(y_ref.dtype)


def _gla_lat_kernel(q_ref, k_ref, v_ref, r_ref, lr_ref, wg_ref, bg_ref, ng_ref, s0_ref, yin_ref,
                    y_ref, of_ref, ob_ref):
    del yin_ref
    n_groups = DEC_SEQ // GLA_GROUP
    keep_f, keep_b = _gla_keep(0), _gla_keep(1)

    def body(g, carry):
        st_f, st_b = carry
        rf = pl.ds(pl.multiple_of(g * GLA_GROUP, GLA_GROUP), GLA_GROUP)
        rb = pl.ds(pl.multiple_of((n_groups - 1 - g) * GLA_GROUP, GLA_GROUP), GLA_GROUP)
        qkv = (q_ref, k_ref, v_ref)
        outs, st_f = _gla_group(_gla_loader(qkv, rf), lr_ref[rf, :], wg_ref[0], bg_ref[0],
                                list(st_f), 0, keep_f)
        of_ref[rf, :] = jnp.concatenate(outs, axis=-1)
        outs, st_b = _gla_group(_gla_loader(qkv, rb), lr_ref[rb, :], wg_ref[1], bg_ref[1],
                                list(st_b), 1, keep_b)
        ob_ref[rb, :] = jnp.concatenate(outs, axis=-1)
        return tuple(st_f), tuple(st_b)

    init = tuple(tuple(s0_ref[d, h].T for h in range(GLA_HEADS)) for d in range(2))
    lax.fori_loop(0, n_groups, body, init)
    for h in range(GLA_HEADS):
        cs = slice(h * GLA_DV, (h + 1) * GLA_DV)
        y = _gla_finish(of_ref[:, cs] + ob_ref[:, cs], r_ref[:, cs].astype(F32), ng_ref[:, cs])
        y_ref[:, cs] = y.astype(y_ref.dtype)


def _gla_ctx(pa, ps, wg, bg, ng, states_prev, layer):
    cb = PA_GLA // GLA_W
    rows = GLA_CTX_SEQS * SEQ
    const = lambda shape: pl.BlockSpec(shape, lambda s: (0,) * len(shape))
    in_specs = [pl.BlockSpec((rows, GLA_W), lambda s, j=j: (s, cb + j)) for j in range(4)] + [
        pl.BlockSpec((rows, LANES), lambda s: (s, PS_LR // LANES)),
        const((2, 2 * LANES, 2 * GLA_W)), const((2, 1, GLA_W)), const((1, GLA_W))]
    args = [pa, pa, pa, pa, ps, wg, bg, ng]
    aliases = {}
    if states_prev is not None:
        in_specs.append(pl.BlockSpec(memory_space=pl.ANY))
        args.append(states_prev)
        aliases = {len(args) - 1: 1}
    return pl.pallas_call(
        _gla_ctx_kernel,
        grid=(BATCH // GLA_CTX_SEQS,),
        in_specs=in_specs,
        out_specs=[pl.BlockSpec((rows, GLA_W), lambda s: (s, 0)),
                   pl.BlockSpec((GLA_CTX_SEQS, None, 2, GLA_HEADS, GLA_DK, GLA_DV),
                                lambda s: (s, layer, 0, 0, 0, 0))],
        out_shape=[jax.ShapeDtypeStruct((T_ALL, GLA_W), BF16),
                   jax.ShapeDtypeStruct((BATCH, DEPTH, 2, GLA_HEADS, GLA_DK, GLA_DV), F32)],
        input_output_aliases=aliases,
        compiler_params=_cparams("parallel"),
        name="gla_ctx",
    )(*args)


def _gla_lat(pa, ps, wg, bg, ng, s0, y):
    rb0 = T_CTX // DEC_SEQ
    cb = PA_GLA // GLA_W
    const = lambda shape: pl.BlockSpec(shape, lambda s: (0,) * len(shape))
    return pl.pallas_call(
        _gla_lat_kernel,
        grid=(DEC_BATCH,),
        in_specs=[pl.BlockSpec((DEC_SEQ, GLA_W), lambda s, j=j: (rb0 + s, cb + j)) for j in range(4)] + [
            pl.BlockSpec((DEC_SEQ, LANES), lambda s: (rb0 + s, PS_LR // LANES)),
            const((2, 2 * LANES, 2 * GLA_W)), const((2, 1, GLA_W)), const((1, GLA_W)),
            pl.BlockSpec((None, 2, GLA_HEADS, GLA_DK, GLA_DV), lambda s: (s, 0, 0, 0, 0)),
            pl.BlockSpec(memory_space=pl.ANY)],
        out_specs=pl.BlockSpec((DEC_SEQ, GLA_W), lambda s: (rb0 + s, 0)),
        out_shape=jax.ShapeDtypeStruct((T_ALL, GLA_W), BF16),
        scratch_shapes=[pltpu.VMEM((DEC_SEQ, GLA_W), F32), pltpu.VMEM((DEC_SEQ, GLA_W), F32)],
        input_output_aliases={9: 0},
        compiler_params=_cparams("parallel"),
        name="gla_lat",
    )(pa, pa, pa, pa, ps, wg, bg, ng, s0, y)


LOG2E = 1.4426950408889634
Q_SCALE = HEAD_DIM ** -0.5 * LOG2E


def _attend(q, sink, k_parts, v_parts, masks):
    scores = []
    for k, mask in zip(k_parts, masks):
        s = _dot_t(q, k)
        if mask is not None:
            s = jnp.where(mask, s, NEG_INF)
        scores.append(s)
    sink2 = sink * LOG2E
    m = jnp.broadcast_to(sink2, (q.shape[0], 1)).astype(F32)
    for s in scores:
        m = jnp.maximum(m, jnp.max(s, axis=-1, keepdims=True))
    den = jnp.exp2(sink2 - m)
    o = None
    for s, v in zip(scores, v_parts):
        e = jnp.exp2(s - m)
        den = den + jnp.sum(e, axis=-1, keepdims=True)
        pv = _mm(e.astype(BF16), v)
        o = pv if o is None else o + pv
    return o / den


def _head(a, j):
    return a[:, j * HEAD_DIM:(j + 1) * HEAD_DIM]


ATT_ROWS_CTX = 128


def _attn_ctx_kernel(sink_ref, q_ref, kv_ref, *rest):
    o_ref = rest[-1]
    q = q_ref[...] * Q_SCALE
    kv_all = kv_ref[...]
    ks = [_head(kv_all, kv).astype(BF16) for kv in range(ATT_KV_HEADS)]
    vs = [_head(kv_all, ATT_KV_HEADS + kv).astype(BF16) for kv in range(ATT_KV_HEADS)]
    outs = []
    for h in range(ATT_HEADS):
        kv = h // ATT_GROUP
        qh = _head(q, h).astype(BF16)
        blocks = [_attend(qh[r:r + ATT_ROWS_CTX], sink_ref[h], [ks[kv]], [vs[kv]], [None])
                  for r in range(0, SEQ, ATT_ROWS_CTX)]
        outs.append(jnp.concatenate(blocks, axis=0))
    o_ref[...] = jnp.concatenate(outs, axis=-1).astype(o_ref.dtype)


def _attn_ctx(ps, sink):
    return pl.pallas_call(
        _attn_ctx_kernel,
        grid=(BATCH,),
        in_specs=[
            pl.BlockSpec(memory_space=pltpu.SMEM),
            pl.BlockSpec((SEQ, ATT_W), lambda s: (s, 0)),
            pl.BlockSpec((SEQ, 2 * LANES), lambda s: (s, PS_KV // (2 * LANES))),
        ],
        out_specs=pl.BlockSpec((SEQ, ATT_W), lambda s: (s, 0)),
        out_shape=jax.ShapeDtypeStruct((T_ALL, ATT_W), BF16),
        compiler_params=_cparams("parallel"),
        name="attn_ctx",
    )(sink, ps, ps)


def _rope(x, cos, sin_signed):
    lane = lax.broadcasted_iota(jnp.int32, x.shape, 1)
    partner = jnp.where((lane & 31) < 16, pltpu.roll(x, LANES - 16, 1), pltpu.roll(x, 16, 1))
    return x * cos + partner * sin_signed


def _attn_lat_kernel(sink_ref, q_ref, kvp_ref, kvc_ref, kvn_ref, kctx_ref, vctx_ref, cos_ref, sin_ref, yin_ref,
                     o_ref):
    del yin_ref
    n = pl.program_id(1)
    n_blk = DEC_SEQ // BLOCK
    start = n * BLOCK

    def table(ref, blk):
        blk = jnp.clip(blk, 0, n_blk - 1)
        return ref[pl.ds(pl.multiple_of(blk * BLOCK, BLOCK), BLOCK), :]

    k_loc, v_loc = [], []
    for off, ref in ((-1, kvp_ref), (0, kvc_ref), (1, kvn_ref)):
        k_loc.append(_rope(ref[:, :LANES], table(cos_ref, n + off), table(sin_ref, n + off)))
        v_loc.append(ref[:, LANES:])
    k_loc = jnp.concatenate(k_loc, axis=0)
    v_loc = jnp.concatenate(v_loc, axis=0)
    k_ctx = kctx_ref[...]
    v_ctx = vctx_ref[...]

    span = 3 * BLOCK
    stacked = (ATT_GROUP * BLOCK, span)
    qpos = start + (lax.broadcasted_iota(jnp.int32, stacked, 0) & (BLOCK - 1))
    kpos = start - WINDOW + lax.broadcasted_iota(jnp.int32, stacked, 1)
    valid = (jnp.abs(qpos - kpos) <= WINDOW) & (kpos >= 0) & (kpos < DEC_SEQ)

    cos_q = table(cos_ref, n)
    sin_q = table(sin_ref, n)
    q_heads = []
    for pair in range(ATT_HEADS // 2):
        qr = _rope(q_ref[:, pair * LANES:(pair + 1) * LANES], cos_q, sin_q) * Q_SCALE
        q_heads += [_head(qr, 0).astype(BF16), _head(qr, 1).astype(BF16)]
    outs = []
    for kv in range(ATT_KV_HEADS):
        heads = range(kv * ATT_GROUP, (kv + 1) * ATT_GROUP)
        q = jnp.concatenate([q_heads[h] for h in heads], axis=0)
        sink = jnp.concatenate([jnp.full((BLOCK, 1), sink_ref[h], F32) for h in heads], axis=0)
        o = _attend(q, sink, [_head(k_ctx, kv).astype(BF16), _head(k_loc, kv).astype(BF16)],
                    [_head(v_ctx, kv).astype(BF16), _head(v_loc, kv).astype(BF16)], [None, valid])
        outs += [o[g * BLOCK:(g + 1) * BLOCK] for g in range(ATT_GROUP)]
    o_ref[...] = jnp.concatenate(outs, axis=-1).astype(o_ref.dtype)


def _attn_lat(ps, sink, k_ctx, v_ctx, cos_t, sin_t, y):
    n_blk = DEC_SEQ // BLOCK
    rb0 = T_CTX // BLOCK

    def kv_spec(off):
        return pl.BlockSpec(
            (BLOCK, 2 * LANES),
            lambda b, n: (rb0 + b * n_blk + jnp.clip(n + off, 0, n_blk - 1), PS_KV // (2 * LANES)))

    return pl.pallas_call(
        _attn_lat_kernel,
        grid=(DEC_BATCH, n_blk),
        in_specs=[
            pl.BlockSpec(memory_space=pltpu.SMEM),
            pl.BlockSpec((BLOCK, ATT_W), lambda b, n: (rb0 + b * n_blk + n, 0)),
            kv_spec(-1), kv_spec(0), kv_spec(1),
            pl.BlockSpec((None, PAST_LEN, LANES), lambda b, n: (b, 0, 0)),
            pl.BlockSpec((None, PAST_LEN, LANES), lambda b, n: (b, 0, 0)),
            pl.BlockSpec((DEC_SEQ, LANES), lambda b, n: (0, 0)),
            pl.BlockSpec((DEC_SEQ, LANES), lambda b, n: (0, 0)),
            pl.BlockSpec(memory_space=pl.ANY),
        ],
        out_specs=pl.BlockSpec((BLOCK, ATT_W), lambda b, n: (rb0 + b * n_blk + n, 0)),
        out_shape=jax.ShapeDtypeStruct((T_ALL, ATT_W), BF16),
        input_output_aliases={9: 0},
        compiler_params=_cparams("parallel", "parallel"),
        name="attn_lat",
    )(sink, ps, ps, ps, ps, k_ctx, v_ctx, cos_t, sin_t, y)


def _rope_tables():
    pos = np.arange(DEC_SEQ)
    n_freq = HEAD_DIM // 4
    inv = jnp.asarray(ROPE_THETA, F32) ** (-jnp.arange(n_freq, dtype=F32) / n_freq)
    row = jnp.asarray(pos // GRID_W, F32)
    colp = jnp.asarray(pos % GRID_W, F32)
    ang_r = row[:, None] * inv[None, :]
    ang_c = colp[:, None] * inv[None, :]
    cos = jnp.concatenate([jnp.cos(ang_r)] * 2 + [jnp.cos(ang_c)] * 2, axis=-1)
    sin = jnp.concatenate([-jnp.sin(ang_r), jnp.sin(ang_r), -jnp.sin(ang_c), jnp.sin(ang_c)], axis=-1)
    return jnp.tile(cos, (1, 2)), jnp.tile(sin, (1, 2))


def _route(logits):
    lane_i = lax.broadcasted_iota(jnp.int32, logits.shape, 1)
    lane = lane_i.astype(F32)
    big = jnp.float32(1 << 20)
    is_g = lane_i < N_EXPERT_GROUPS
    lg = jnp.where(is_g, logits, -jnp.inf)
    m_g = jnp.max(lg, axis=-1, keepdims=True)
    grp = jnp.min(jnp.where(lg == m_g, lane, big), axis=-1, keepdims=True)
    z_g = jnp.sum(jnp.where(is_g, jnp.exp(lg - m_g), 0.0), axis=-1, keepdims=True)
    p_grp = 1.0 / z_g

    e_idx = lane_i - ROUTE_E0
    e_grp = (e_idx >> 2).astype(F32)
    sel = (e_idx >= 0) & (e_idx < N_EXPERTS) & (e_grp == grp)
    le = jnp.where(sel, logits, -jnp.inf)
    m_e = jnp.max(le, axis=-1, keepdims=True)
    ex = jnp.where(sel, jnp.exp(le - m_e), 0.0)
    pe = ex / jnp.sum(ex, axis=-1, keepdims=True)
    pe = jnp.where(sel, pe, -1.0)
    v1 = jnp.max(pe, axis=-1, keepdims=True)
    i1 = jnp.min(jnp.where(pe == v1, lane, big), axis=-1, keepdims=True)
    pe2 = jnp.where(lane == i1, -1.0, pe)
    v2 = jnp.max(pe2, axis=-1, keepdims=True)
    i2 = jnp.min(jnp.where(pe2 == v2, lane, big), axis=-1, keepdims=True)
    tot = v1 + v2
    return (jnp.where(lane == i1, p_grp * (v1 / tot), 0.0)
            + jnp.where(lane == i2, p_grp * (v2 / tot), 0.0)
            + jnp.where(lane_i == LANES - 1, grp, 0.0))


HALO = 16


def _merge_kernel(gate_ref, conv_ref, cprev_ref, cnext_ref, yb_ref, yc_ref, *rest, x_split):
    x_refs, rest = rest[:2 if x_split else 1], rest[2 if x_split else 1:]
    (mod_ref, cw_ref, wb_ref, wo_ref, g2_ref, wr_ref, br_ref,
     x1_ref, h2_ref, rg_ref, wbb_ref, wob_ref) = rest
    tm = TM_MERGE
    i = pl.program_id(0)

    def x_rows(rows):
        if not x_split:
            return x_refs[0][rows, :]
        return jnp.where(i < x_split, x_refs[0][rows, :], x_refs[1][rows, :])

    @pl.when(i == 0)
    def _():
        wbb_ref[...] = wb_ref[...].astype(BF16)
        wob_ref[...] = wo_ref[...].astype(BF16)

    def gated(ref):
        return ref[:, CONV_W:2 * CONV_W].astype(F32) * ref[:, 2 * CONV_W:3 * CONV_W].astype(F32)

    a_b = conv_ref[:, 0:CONV_W].astype(F32)
    u = gated(conv_ref)
    u_before = gated(cprev_ref)[HALO - 1:HALO]
    u_after = gated(cnext_ref)[0:1]
    r = lax.broadcasted_iota(jnp.int32, (tm, 1), 0)
    g_row = i * tm + r
    seq_mask = jnp.where(g_row < T_CTX, SEQ - 1, DEC_SEQ - 1)
    first = (g_row & seq_mask) == 0
    last = ((g_row + 1) & seq_mask) == 0
    u_prev = jnp.where(r == 0, u_before, pltpu.roll(u, 1, 0))
    u_next = jnp.where(r == tm - 1, u_after, pltpu.roll(u, tm - 1, 0))
    u_prev = jnp.where(first, 0.0, u_prev)
    u_next = jnp.where(last, 0.0, u_next)
    y_a = (a_b * (u_prev * cw_ref[0:1, :] + u * cw_ref[1:2, :] + u_next * cw_ref[2:3, :])).astype(BF16)

    blocks = [slice(r0, r0 + MERGE_ROWS) for r0 in range(0, tm, MERGE_ROWS)]
    ys = [(y_a[rows], yb_ref[rows, :], yc_ref[rows, :]) for rows in blocks]
    branches = [[_mm(y, wbb_ref[j]) for j, y in enumerate(y3)] for y3 in ys]
    zs = []
    for rows, br3 in zip(blocks, branches):
        z = sum(gate_ref[rows, j * D_MODEL:(j + 1) * D_MODEL].astype(F32) * br3[j] for j in range(3))
        zs.append(z.astype(BF16))
    outs = [_mm(z, wob_ref[...]) for z in zs]
    h2s = []
    for rows, o in zip(blocks, outs):
        x1 = x_rows(rows) + mod_ref[2:3, :] * o
        x1_ref[rows, :] = x1
        h2 = _mod_norm(x1, g2_ref[...], mod_ref[...], 3)
        h2_ref[rows, :] = h2.astype(BF16)
        h2s.append(h2)
    logits = [_split_dot(h2, wr_ref[...]) + br_ref[...] for h2 in h2s]
    for rows, lg in zip(blocks, logits):
        rg_ref[rows, :] = _route(lg)


def _merge(pg, pa, y_b, y_c, x, mod_l, conv_w, layer, wb, wo, g2, wr, br):
    tm = TM_MERGE
    n_tiles = T_ALL // tm
    hb = tm // HALO
    const = lambda shape: pl.BlockSpec(shape, lambda i: (0,) * len(shape))
    if isinstance(x, tuple):
        x_specs, x_split = _split_x_specs(tm)
    else:
        x, x_specs, x_split = (x,), [pl.BlockSpec((tm, D_MODEL), lambda i: (i, 0))], 0
    return pl.pallas_call(
        functools.partial(_merge_kernel, x_split=x_split),
        grid=(n_tiles,),
        in_specs=[
            pl.BlockSpec((tm, PG_W), lambda i: (i, 0)),
            pl.BlockSpec((tm, 3 * CONV_W), lambda i: (i, 0)),
            pl.BlockSpec((HALO, 3 * CONV_W), lambda i: (jnp.maximum(i * hb - 1, 0), 0)),
            pl.BlockSpec((HALO, 3 * CONV_W), lambda i: (jnp.minimum((i + 1) * hb, n_tiles * hb - 1), 0)),
            pl.BlockSpec((tm, GLA_W), lambda i: (i, 0)),
            pl.BlockSpec((tm, ATT_W), lambda i: (i, 0)),
            *x_specs,
            pl.BlockSpec((None, 6, D_MODEL), lambda i: (_cond_row(i, tm), 0, 0)),
            const((CONV_K, CONV_W)),
            pl.BlockSpec((None, 3, 512, D_MODEL), lambda i: (layer, 0, 0, 0)),
            pl.BlockSpec((None, D_MODEL, D_MODEL), lambda i: (layer, 0, 0)),
            const((1, D_MODEL)),
            const((2 * D_MODEL, 2 * LANES)),
            const((1, LANES)),
        ],
        out_specs=[
            pl.BlockSpec((tm, D_MODEL), lambda i: (i, 0)),
            pl.BlockSpec((tm, D_MODEL), lambda i: (i, 0)),
            pl.BlockSpec((tm, LANES), lambda i: (i, 0)),
        ],
        out_shape=[
            jax.ShapeDtypeStruct((T_ALL, D_MODEL), F32),
            jax.ShapeDtypeStruct((T_ALL, D_MODEL), BF16),
            jax.ShapeDtypeStruct((T_ALL, LANES), F32),
        ],
        scratch_shapes=[pltpu.VMEM((3, 512, D_MODEL), BF16), pltpu.VMEM((D_MODEL, D_MODEL), BF16)],
        compiler_params=_cparams("arbitrary"),
        name="merge",
    )(pg, pa, pa, pa, y_b, y_c, *x, mod_l, conv_w, wb, wo, g2, wr, br)


SUB = 128
MOE_ROWS = 256


def _dispatch_tables(rg):
    n_tiles = T_ALL // TM
    grp = rg[:, LANES - 1].astype(jnp.int32).reshape(n_tiles, TM)
    hot = grp[..., None] == jnp.arange(N_EXPERT_GROUPS, dtype=jnp.int32)
    onehot = hot.astype(jnp.int32)
    cnt = onehot.sum(axis=1)
    start = jnp.cumsum(cnt, axis=1) - cnt
    first = start // SUB
    nblk = jnp.where(cnt > 0, (start + cnt + SUB - 1) // SUB - first, 0)
    before = jnp.tril(jnp.ones((TM, TM), BF16), -1)
    rank = jnp.einsum("ts,nsg->ntg", before, hot.astype(BF16), preferred_element_type=F32).astype(jnp.int32)
    slot = ((start[:, None, :] + rank) * onehot).sum(axis=-1)
    return (first.reshape(-1), nblk.reshape(-1), slot.reshape(n_tiles, 1, TM), slot.reshape(T_ALL, 1))


def _moe_kernel(bstart_ref, nblk_ref, h_ref, rg_ref, srow_ref, scol_ref, x1_ref, mod_ref, wg_ref, wu_ref,
                wd_ref, g_ref, *rest, final, tile0):
    if final:
        y_ref, hs_ref, gs_ref, os_ref = rest
    else:
        modn_ref, x2_ref, hn_ref, hs_ref, gs_ref, os_ref = rest
    i = pl.program_id(0)
    grp = pl.program_id(1)

    @pl.when(grp == 0)
    def _():
        slot_of_token = srow_ref[...]
        payload = jnp.concatenate([h_ref[...], *_split3(rg_ref[...])], axis=-1)
        for r0 in range(0, TM, MOE_ROWS):
            rows = slice(r0, r0 + MOE_ROWS)
            slot = r0 + lax.broadcasted_iota(jnp.int32, (MOE_ROWS, TM), 0)
            pm = jnp.where(slot == slot_of_token, 1.0, 0.0).astype(BF16)
            moved = _mm(pm, payload)
            hs_ref[rows, :] = moved[:, :D_MODEL].astype(BF16)
            gs_ref[rows, :] = sum(moved[:, D_MODEL + j * LANES:D_MODEL + (j + 1) * LANES] for j in range(3))
        os_ref[...] = jnp.zeros_like(os_ref)

    k = (tile0 + i) * N_EXPERT_GROUPS + grp
    first = bstart_ref[k]
    n_blocks = nblk_ref[k]

    def experts(block0, n_sub):
        n_rows = n_sub * SUB
        rows = pl.ds(pl.multiple_of(block0 * SUB, SUB), n_rows)
        lane = lax.broadcasted_iota(jnp.int32, (n_rows, LANES), 1)
        x = hs_ref[rows, :]
        gates = gs_ref[rows, :]
        hidden = []
        for e in range(EXPERTS_PER_GROUP):
            ge = _mm(x, wg_ref[e])
            ue = _mm(x, wu_ref[e])
            w = jnp.sum(jnp.where(lane == ROUTE_E0 + grp * EXPERTS_PER_GROUP + e, gates, 0.0),
                        axis=-1, keepdims=True)
            hidden.append(((ge * jax.nn.sigmoid(ge)) * ue * w).astype(BF16))
        acc = _mm(jnp.concatenate(hidden, axis=-1), wd_ref[...].reshape(EXPERTS_PER_GROUP * D_EXPERT, D_MODEL))
        os_ref[rows, :] = (os_ref[rows, :].astype(F32) + acc).astype(BF16)

    odd = n_blocks % 2 == 1
    n_pairs = jnp.where(odd & (n_blocks >= 3), (n_blocks - 3) // 2, n_blocks // 2)

    def pair(p, carry):
        experts(first + 2 * p, 2)
        return carry

    lax.fori_loop(0, n_pairs, pair, 0)

    @pl.when(odd & (n_blocks >= 3))
    def _():
        experts(first + n_blocks - 3, 3)

    @pl.when(n_blocks == 1)
    def _():
        experts(first, 1)

    @pl.when(grp == N_EXPERT_GROUPS - 1)
    def _():
        sorted_out = os_ref[...]
        lane_slot = lax.broadcasted_iota(jnp.int32, (MOE_ROWS, TM), 1)
        for r0 in range(0, TM, MOE_ROWS):
            rows = slice(r0, r0 + MOE_ROWS)
            pt = jnp.where(lane_slot == scol_ref[rows, :], 1.0, 0.0).astype(BF16)
            x2 = x1_ref[rows, :] + mod_ref[5:6, :] * _mm(pt, sorted_out)
            if final:
                y_ref[rows, :] = _rms(x2) * g_ref[...]
            else:
                x2_ref[rows, :] = x2
                hn_ref[rows, :] = _mod_norm(x2, g_ref[...], modn_ref[...], 0).astype(BF16)


def _moe(tables, h2, rg, x1, mod_l, layer, w_eg, w_eu, w_ed, g, mod_next, tile0, n_tiles):
    bstart, nblk, slot_row, slot_col = tables
    final = mod_next is None
    row = lambda i, e, *_: (tile0 + i, 0)
    mod_spec = lambda: pl.BlockSpec((None, 6, D_MODEL), lambda i, e, *_: (_cond_row(tile0 + i, TM), 0, 0))
    group = lambda i, e, *_: (layer, e, 0, 0)
    in_specs = [
        pl.BlockSpec((TM, D_MODEL), row),
        pl.BlockSpec((TM, LANES), row),
        pl.BlockSpec((None, 1, TM), lambda i, e, *_: (tile0 + i, 0, 0)),
        pl.BlockSpec((TM, 1), row),
        pl.BlockSpec((TM, D_MODEL), row),
        mod_spec(),
        pl.BlockSpec((None, EXPERTS_PER_GROUP, D_MODEL, D_EXPERT), group),
        pl.BlockSpec((None, EXPERTS_PER_GROUP, D_MODEL, D_EXPERT), group),
        pl.BlockSpec((None, EXPERTS_PER_GROUP, D_EXPERT, D_MODEL), group),
        pl.BlockSpec((1, D_MODEL), lambda i, e, *_: (0, 0)),
    ]
    args = [h2, rg, slot_row, slot_col, x1, mod_l, w_eg, w_eu, w_ed, g]
    if final:
        out_specs = pl.BlockSpec((TM, D_MODEL), lambda i, e, *_: (i, 0))
        out_shape = jax.ShapeDtypeStruct((n_tiles * TM, D_MODEL), F32)
    else:
        in_specs.append(mod_spec())
        args.append(mod_next)
        out_specs = [pl.BlockSpec((TM, D_MODEL), row)] * 2
        out_shape = [jax.ShapeDtypeStruct((T_ALL, D_MODEL), F32), jax.ShapeDtypeStruct((T_ALL, D_MODEL), BF16)]
    return pl.pallas_call(
        functools.partial(_moe_kernel, final=final, tile0=tile0),
        grid_spec=pltpu.PrefetchScalarGridSpec(
            num_scalar_prefetch=2,
            grid=(n_tiles, N_EXPERT_GROUPS),
            in_specs=in_specs,
            out_specs=out_specs,
            scratch_shapes=[pltpu.VMEM((TM, D_MODEL), BF16), pltpu.VMEM((TM, LANES), F32),
                            pltpu.VMEM((TM, D_MODEL), BF16)],
        ),
        out_shape=out_shape,
        compiler_params=_cparams("parallel", "arbitrary"),
        name="moe_final" if final else "moe",
    )(bstart, nblk, *args)


def _cache_kernel(*refs):
    ps_refs, (k_ref, v_ref) = refs[:DEPTH], refs[DEPTH:]
    n_seq = TM // SEQ
    for l, ps_ref in enumerate(ps_refs):
        k_ref[:, l] = ps_ref[:, :LANES].reshape(n_seq, SEQ, LANES)
        v_ref[:, l] = ps_ref[:, LANES:].reshape(n_seq, SEQ, LANES)


def _cache(ps_layers):
    n_seq = TM // SEQ
    out = pl.BlockSpec((n_seq, DEPTH, SEQ, LANES), lambda i: (i, 0, 0, 0))
    return pl.pallas_call(
        _cache_kernel,
        grid=(T_CTX // TM,),
        in_specs=[pl.BlockSpec((TM, 2 * LANES), lambda i: (i, PS_KV // (2 * LANES)))] * DEPTH,
        out_specs=[out, out],
        out_shape=[jax.ShapeDtypeStruct((BATCH, DEPTH, SEQ, LANES), F32)] * 2,
        compiler_params=_cparams("parallel"),
        name="cache",
    )(*ps_layers)


def _prep_gla_gate(w_gate):
    pads = [((0, 0), (d * GLA_RANK, LANES - (d + 1) * GLA_RANK), (0, 0)) for d in range(2)]
    return _pack_split(jnp.stack([jnp.pad(w_gate[:, d], pads[d]) for d in range(2)], axis=1))


def _prep_router(w_rg, b_rg, w_re, b_re):
    unused = LANES - N_EXPERT_GROUPS - N_EXPERTS
    w = jnp.concatenate([w_rg, w_re, jnp.zeros((DEPTH, D_MODEL, unused), F32)], axis=-1)
    b = jnp.concatenate([b_rg, b_re, jnp.zeros((DEPTH, unused), F32)], axis=-1)
    return _pack_split(w), b.reshape(DEPTH, 1, LANES)


def kernel(x_prompt, x_sample, state_gla, cache_k, cache_v, c, c_ctx, w_ada, b_ada, norm1_g, norm2_g, w_in,
           conv_w, gla_w_gate, gla_b_gate, gla_norm_g, attn_sink, w_branch, w_out, w_route_group,
           b_route_group, w_route_expert, b_route_expert, w_exp_gate, w_exp_up, w_exp_down, final_norm_g):
    cond = jnp.zeros((N_COND, D_MODEL), F32).at[0].set(c_ctx).at[1:1 + DEC_BATCH].set(c)
    mod = _modulation(cond, w_ada, b_ada).reshape(DEPTH, N_COND, 6, D_MODEL)
    cos_t, sin_t = _rope_tables()
    row = lambda v: v.reshape(1, -1)

    x = (x_prompt.reshape(T_CTX, D_MODEL), x_sample.reshape(T_LAT, D_MODEL))
    h = _prenorm(*x, mod[0], row(norm1_g[0]))
    wg_all = _prep_gla_gate(gla_w_gate)
    wr_all, br_all = _prep_router(w_route_group, b_route_group, w_route_expert, b_route_expert)
    states = None
    ps_layers = []
    y_prompt = y_sample = None
    w_t = jnp.swapaxes(w_in, 1, 2)
    proj_a = functools.partial(_proj, pieces=[(0, PA_W // 2)], n_tiles=2, out_dtype=BF16, name="proj_a",
                               act="silu_tail")
    proj_s = functools.partial(_proj, pieces=[(W_IN_TQ, W_IN_GATE - W_IN_TQ), (W_IN_LR, LANES)], n_tiles=1,
                               out_dtype=F32, name="proj_s")
    proj_g = functools.partial(_proj, pieces=[(W_IN_GATE, PG_W // 2)], n_tiles=2, out_dtype=BF16,
                               name="proj_g", act="sigmoid")
    experts = None
    for l in range(DEPTH):
        if l == 0:
            pa, w_eg = proj_a(h, w_t, l, cast=(w_exp_gate, 2))
            ps, w_ed = proj_s(h, w_t, l, cast=(w_exp_down, 4))
            pg, w_eu = proj_g(h, w_t, l, cast=(w_exp_up, 2))
            experts = (w_eg, w_eu, w_ed)
        else:
            pa, ps, pg = proj_a(h, w_t, l), proj_s(h, w_t, l), proj_g(h, w_t, l)
        ps_layers.append(ps)

        wg = wg_all[l]
        bg = gla_b_gate[l].reshape(2, 1, GLA_W)
        ng = row(gla_norm_g[l])
        y_b, states = _gla_ctx(pa, ps, wg, bg, ng, states, l)
        y_b = _gla_lat(pa, ps, wg, bg, ng, state_gla[:, l], y_b)
        y_c = _attn_ctx(ps, attn_sink[l])
        y_c = _attn_lat(ps, attn_sink[l], cache_k[:, l].reshape(DEC_BATCH, PAST_LEN, LANES),
                        cache_v[:, l].reshape(DEC_BATCH, PAST_LEN, LANES), cos_t, sin_t, y_c)

        x1, h2, rg = _merge(pg, pa, y_b, y_c, x, mod[l], conv_w[l], l, w_branch, w_out, row(norm2_g[l]),
                            wr_all[l], br_all[l])
        moe_in = (_dispatch_tables(rg), h2, rg, x1, mod[l], l, *experts)
        if l + 1 < DEPTH:
            x, h = _moe(*moe_in, row(norm1_g[l + 1]), mod[l + 1], 0, T_ALL // TM)
        else:
            gf = row(final_norm_g)
            y_prompt = _moe(*moe_in, gf, None, 0, T_CTX // TM)
            y_sample = _moe(*moe_in, gf, None, T_CTX // TM, T_LAT // TM)

    new_k, new_v = _cache(ps_layers)
    kv_shape = (BATCH, DEPTH, SEQ, ATT_KV_HEADS, HEAD_DIM)
    return (y_prompt.reshape(BATCH, SEQ, D_MODEL), y_sample.reshape(DEC_BATCH, DEC_SEQ, D_MODEL), states,
            new_k.reshape(kv_shape), new_v.reshape(kv_shape))
```

```python
import functools

import jax
import jax.numpy as jnp
import numpy as np
from jax import lax
from jax.experimental import pallas as pl
from jax.experimental.pallas import tpu as pltpu

F32 = jnp.float32
BF16 = jnp.bfloat16

D_MODEL = 1024
BATCH = 32
SEQ = 256
DEPTH = 2
DEC_BATCH = 2
DEC_SEQ = 2048
PAST_LEN = 512
GRID_W = 64
EPS = 1e-6
CONV_W = 512
CONV_K = 3
GLA_HEADS = 4
GLA_DK = 128
GLA_DV = 128
GLA_W = GLA_HEADS * GLA_DV
GLA_RANK = 16
GLA_TAU = 16.0
GLA_CHUNK = 64
ATT_HEADS = 8
ATT_KV_HEADS = 2
ATT_GROUP = ATT_HEADS // ATT_KV_HEADS
HEAD_DIM = 64
ATT_W = ATT_HEADS * HEAD_DIM
WINDOW = 128
BLOCK = 128
ROPE_THETA = 10000.0
N_EXPERT_GROUPS = 4
EXPERTS_PER_GROUP = 4
N_EXPERTS = 16
D_EXPERT = 256
NEG_INF = -1e30

T_CTX = BATCH * SEQ
T_LAT = DEC_BATCH * DEC_SEQ
T_ALL = T_CTX + T_LAT
N_COND = 8
LANES = 128

W_IN_LR = 3584
W_IN_TQ = 3616
W_IN_GATE = 4384
N_IN = 7456
PA_W = 3584
PA_GLA = 1536
PS_W = 896
PS_KV = 512
PS_LR = 768
PG_W = 3 * D_MODEL
ROUTE_E0 = N_EXPERT_GROUPS

TM = 1024
TM_MERGE = 512
MERGE_ROWS = 256
VMEM_LIMIT = 56 * 1024 * 1024


def _cparams(*sem):
    return pltpu.CompilerParams(dimension_semantics=sem, vmem_limit_bytes=VMEM_LIMIT)


def _cond_row(i, tm):
    n_ctx = T_CTX // tm
    per = DEC_SEQ // tm
    return jnp.where(i < n_ctx, 0, 1 + (i - n_ctx) // per)


def _mm(a, b):
    return jnp.dot(a, b, preferred_element_type=F32)


def _dot_t(a, b):
    return lax.dot_general(a, b, (((1,), (1,)), ((), ())), preferred_element_type=F32)


def _dot_ta(a, b):
    return lax.dot_general(a, b, (((0,), (0,)), ((), ())), preferred_element_type=F32)


def _hi_lo(w):
    hi = w.astype(BF16)
    return hi, (w - hi.astype(F32)).astype(BF16)


def _pack_split(w):
    hi, lo = _hi_lo(w)
    return jnp.concatenate([jnp.concatenate([hi, lo], axis=-1),
                            jnp.concatenate([hi, jnp.zeros_like(lo)], axis=-1)], axis=-2)


def _split_dot(a, w_pack):
    a_hi, a_lo = _hi_lo(a)
    both = _mm(jnp.concatenate([a_hi, a_lo], axis=-1), w_pack)
    n = both.shape[-1] // 2
    return both[:, :n] + both[:, n:]


def _rms(x):
    return x * lax.rsqrt(jnp.mean(x * x, axis=-1, keepdims=True) + EPS)


def _mod_norm(x, g, mod, shift_row):
    return _rms(x) * g * (1.0 + mod[shift_row + 1:shift_row + 2, :]) + mod[shift_row:shift_row + 1, :]


def _mod_kernel(c_ref, w_ref, b_ref, o_ref):
    c = c_ref[...]
    s = (c * jax.nn.sigmoid(c)).astype(BF16)
    o_ref[...] = _mm(s, w_ref[...].astype(BF16)) + b_ref[...]


def _modulation(cond, w_ada, b_ada):
    tn = 1536
    return pl.pallas_call(
        _mod_kernel,
        grid=(DEPTH, 6 * D_MODEL // tn),
        in_specs=[
            pl.BlockSpec((N_COND, D_MODEL), lambda l, j: (0, 0)),
            pl.BlockSpec((None, D_MODEL, tn), lambda l, j: (l, 0, j)),
            pl.BlockSpec((None, 1, tn), lambda l, j: (l, 0, j)),
        ],
        out_specs=pl.BlockSpec((None, N_COND, tn), lambda l, j: (l, 0, j)),
        out_shape=jax.ShapeDtypeStruct((DEPTH, N_COND, 6 * D_MODEL), F32),
        compiler_params=_cparams("parallel", "parallel"),
        name="modulation",
    )(cond, w_ada, b_ada.reshape(DEPTH, 1, 6 * D_MODEL))


def _split_x_specs(tm):
    n_ctx = T_CTX // tm
    return [pl.BlockSpec((tm, D_MODEL), lambda i, *_: (jnp.minimum(i, n_ctx - 1), 0)),
            pl.BlockSpec((tm, D_MODEL), lambda i, *_: (jnp.maximum(i - n_ctx, 0), 0))], n_ctx


def _prenorm_kernel(xp_ref, xs_ref, mod_ref, g_ref, h_ref):
    x = jnp.where(pl.program_id(0) < T_CTX // TM, xp_ref[...], xs_ref[...])
    h_ref[...] = _mod_norm(x, g_ref[...], mod_ref[...], 0).astype(BF16)


def _prenorm(xp, xs, mod_l, g):
    x_specs, _ = _split_x_specs(TM)
    return pl.pallas_call(
        _prenorm_kernel,
        grid=(T_ALL // TM,),
        in_specs=x_specs + [
            pl.BlockSpec((None, 6, D_MODEL), lambda i: (_cond_row(i, TM), 0, 0)),
            pl.BlockSpec((1, D_MODEL), lambda i: (0, 0)),
        ],
        out_specs=pl.BlockSpec((TM, D_MODEL), lambda i: (i, 0)),
        out_shape=jax.ShapeDtypeStruct((T_ALL, D_MODEL), BF16),
        compiler_params=_cparams("parallel"),
        name="prenorm",
    )(xp, xs, mod_l, g)


def _proj_kernel(h_ref, *refs, act, n_w):
    w_refs, refs = refs[:n_w], refs[n_w:]
    if len(refs) == 4:
        cast_in_ref, o_ref, cast_out_ref, wb_ref = refs
        cast_out_ref[...] = cast_in_ref[...].astype(BF16)
    else:
        o_ref, wb_ref = refs

    @pl.when(pl.program_id(1) == 0)
    def _():
        col = 0
        for w_ref in w_refs:
            n = w_ref.shape[0]
            wb_ref[:, col:col + n] = w_ref[...].T.astype(BF16)
            col += n

    def sigmoid(t):
        return 0.5 * jnp.tanh(0.5 * t) + 0.5

    y = _mm(h_ref[...], wb_ref[...])
    if act == "sigmoid":
        y = sigmoid(y)
    if act == "silu_tail":
        last = pl.program_id(0) == pl.num_programs(0) - 1
        tail = y[:, -GLA_W:]
        o_ref[:, :-GLA_W] = y[:, :-GLA_W].astype(o_ref.dtype)
        o_ref[:, -GLA_W:] = jnp.where(last, tail * sigmoid(tail), tail).astype(o_ref.dtype)
    else:
        o_ref[...] = y.astype(o_ref.dtype)


def _proj(h, w_t, layer, pieces, n_tiles, out_dtype, name, act=None, cast=None):
    tn = sum(n for _, n in pieces)
    n_rows = T_ALL // TM

    def w_spec(c0, n):
        return pl.BlockSpec((pl.Squeezed(), pl.Element(n), pl.Element(D_MODEL)),
                            lambda j, i: (layer, pl.multiple_of(c0 + j * tn, 8), 0))

    in_specs = [pl.BlockSpec((TM, D_MODEL), lambda j, i: (i, 0))] + [w_spec(c0, n) for c0, n in pieces]
    out_specs = [pl.BlockSpec((TM, tn), lambda j, i: (i, j))]
    out_shape = [jax.ShapeDtypeStruct((T_ALL, n_tiles * tn), out_dtype)]
    args = [h] + [w_t] * len(pieces)
    if cast is not None:
        src, per_block = cast
        per_layer = N_EXPERTS // per_block
        n_blocks = DEPTH * per_layer
        assert n_blocks <= n_tiles * n_rows

        def block(j, i):
            b = jnp.minimum(j * n_rows + i, n_blocks - 1)
            return (b // per_layer, b % per_layer, 0, 0)

        spec = pl.BlockSpec((None, per_block) + src.shape[2:], block)
        in_specs.append(spec)
        out_specs.append(spec)
        out_shape.append(jax.ShapeDtypeStruct(src.shape, BF16))
        args.append(src)
    out = pl.pallas_call(
        functools.partial(_proj_kernel, act=act, n_w=len(pieces)),
        grid=(n_tiles, n_rows),
        in_specs=in_specs,
        out_specs=out_specs,
        out_shape=out_shape,
        scratch_shapes=[pltpu.VMEM((D_MODEL, tn), BF16)],
        compiler_params=_cparams("arbitrary", "arbitrary"),
        name=name,
    )(*args)
    return out[0] if cast is None else out


GLA_GROUP = 256
GLA_CTX_SEQS = 2


def _log_sigmoid(z):
    return jnp.minimum(z, 0.0) - jnp.log(1.0 + jnp.exp(-jnp.abs(z)))


def _split3(x):
    hi = x.astype(BF16)
    r1 = x - hi.astype(F32)
    mid = r1.astype(BF16)
    lo = (r1 - mid.astype(F32)).astype(BF16)
    return hi, mid, lo


def _gla_keep(d):
    ri = lax.broadcasted_iota(jnp.int32, (GLA_GROUP, GLA_GROUP), 0)
    ci = lax.broadcasted_iota(jnp.int32, (GLA_GROUP, GLA_GROUP), 1)
    if d == 0:
        return (ci <= ri) & (ci >= (ri & ~(GLA_CHUNK - 1)))
    return (ci >= ri) & (ci <= (ri | (GLA_CHUNK - 1)))


def _gla_group(load, lr, wg_pack, bg, states, d, keep, keep_t):
    C = GLA_CHUNK
    nc = GLA_GROUP // C
    width = len(states) * GLA_DK
    lr = jnp.where(lax.broadcasted_iota(jnp.int32, lr.shape, 1) < 2 * GLA_RANK, lr, 0.0)
    z = _split_dot(lr, wg_pack) + bg
    la = _log_sigmoid(z) / GLA_TAU
    tri = jnp.where(keep, 1.0, 0.0).astype(BF16)
    parts = _mm(tri, jnp.concatenate(_split3(la), axis=-1))
    b_all = parts[:, :width] + parts[:, width:2 * width] + parts[:, 2 * width:]
    edge = C - 1 if d == 0 else 0
    order = range(nc) if d == 0 else range(nc - 1, -1, -1)
    row_chunk = lax.broadcasted_iota(jnp.int32, (GLA_GROUP, GLA_DK), 0) // C
    in_chunk = [row_chunk == c for c in range(nc)]
    outs, new_states = [], []
    for h, st in enumerate(states):
        b = b_all[:, h * GLA_DK:(h + 1) * GLA_DK]
        b_last = [b[c * C + edge:c * C + edge + 1, :] for c in range(nc)]
        bl = jnp.concatenate([jnp.broadcast_to(t, (C, GLA_DK)) for t in b_last], axis=0)
        k = load(1, h)
        q_in = load(0, h) * (GLA_DK ** -0.5) * jnp.exp(b)
        k_in = (k * jnp.exp(-b)).astype(BF16)
        k_end = k * jnp.exp(bl - b)
        vb = load(2, h).astype(BF16)
        att_t = jnp.where(keep_t, _dot_t(k_in, q_in.astype(BF16)), 0.0)
        o_t = _dot_ta(vb, att_t.astype(BF16))
        k_spread = jnp.concatenate([jnp.where(in_chunk[c], k_end, 0.0) for c in range(nc)], axis=-1)
        q_spread = jnp.concatenate([jnp.where(in_chunk[c], q_in, 0.0) for c in range(nc)], axis=-1)
        ds_t = _dot_ta(vb, k_spread.astype(BF16))
        starts = [None] * nc
        for c in order:
            starts[c] = st
            st = st * jnp.exp(b_last[c]) + ds_t[:, c * GLA_DK:(c + 1) * GLA_DK]
        o_t = o_t + _dot_t(jnp.concatenate(starts, axis=-1).astype(BF16), q_spread.astype(BF16))
        outs.append(o_t)
        new_states.append(st)
    return outs, new_states


def _gla_loader(qkv_refs, rows):
    def load(i, h):
        return qkv_refs[i][rows, h * GLA_DK:(h + 1) * GLA_DK].astype(F32)
    return load


def _gla_finish(o, r_act, ng):
    o = o * lax.rsqrt(jnp.mean(o * o, axis=-1, keepdims=True) + EPS)
    return o * ng * r_act


def _gla_ctx_kernel(q_ref, k_ref, v_ref, r_ref, lr_ref, wg_ref, bg_ref, ng_ref, *rest):
    y_ref, s_ref = rest[-2:]
    zero = jnp.zeros((GLA_DV, GLA_DK), F32)
    keep = [_gla_keep(0), _gla_keep(1)]
    for s in range(GLA_CTX_SEQS):
        rows = slice(s * SEQ, (s + 1) * SEQ)
        load = _gla_loader((q_ref, k_ref, v_ref), rows)
        lr = lr_ref[rows, :]
        o_dir = []
        for d in range(2):
            outs, sts = _gla_group(load, lr, wg_ref[d], bg_ref[d], [zero] * GLA_HEADS, d,
                                   keep[d], keep[1 - d])
            o_dir.append(outs)
            for h in range(GLA_HEADS):
                s_ref[s, d, h] = sts[h].T
        for h in range(GLA_HEADS):
            cs = slice(h * GLA_DV, (h + 1) * GLA_DV)
            y = _gla_finish((o_dir[0][h] + o_dir[1][h]).T, r_ref[rows, cs].astype(F32), ng_ref[:, cs])
            y_ref[rows, cs] = y.astype(y_ref.dtype)


def _gla_lat_kernel(q_ref, k_ref, v_ref, r_ref, lr_ref, wg_ref, bg_ref, ng_ref, s0_ref, yin_ref,
                    y_ref, of_ref, ob_ref):
    del yin_ref
    n_groups = DEC_SEQ // GLA_GROUP
    keep_f, keep_b = _gla_keep(0), _gla_keep(1)

    def body(g, carry):
        st_f, st_b = carry
        rf = pl.ds(pl.multiple_of(g * GLA_GROUP, GLA_GROUP), GLA_GROUP)
        rb = pl.ds(pl.multiple_of((n_groups - 1 - g) * GLA_GROUP, GLA_GROUP), GLA_GROUP)
        qkv = (q_ref, k_ref, v_ref)
        outs, st_f = _gla_group(_gla_loader(qkv, rf), lr_ref[rf, :], wg_ref[0], bg_ref[0],
                                list(st_f), 0, keep_f, keep_b)
        of_ref[rf, :] = jnp.concatenate([o.T for o in outs], axis=-1)
        outs, st_b = _gla_group(_gla_loader(qkv, rb), lr_ref[rb, :], wg_ref[1], bg_ref[1],
                                list(st_b), 1, keep_b, keep_f)
        ob_ref[rb, :] = jnp.concatenate([o.T for o in outs], axis=-1)
        return tuple(st_f), tuple(st_b)

    init = tuple(tuple(s0_ref[d, h].T for h in range(GLA_HEADS)) for d in range(2))
    lax.fori_loop(0, n_groups, body, init)
    for h in range(GLA_HEADS):
        cs = slice(h * GLA_DV, (h + 1) * GLA_DV)
        y = _gla_finish(of_ref[:, cs] + ob_ref[:, cs], r_ref[:, cs].astype(F32), ng_ref[:, cs])
        y_ref[:, cs] = y.astype(y_ref.dtype)


def _gla_ctx(pa, ps, wg, bg, ng, states_prev, layer):
    cb = PA_GLA // GLA_W
    rows = GLA_CTX_SEQS * SEQ
    const = lambda shape: pl.BlockSpec(shape, lambda s: (0,) * len(shape))
    in_specs = [pl.BlockSpec((rows, GLA_W), lambda s, j=j: (s, cb + j)) for j in range(4)] + [
        pl.BlockSpec((rows, LANES), lambda s: (s, PS_LR // LANES)),
        const((2, 2 * LANES, 2 * GLA_W)), const((2, 1, GLA_W)), const((1, GLA_W))]
    args = [pa, pa, pa, pa, ps, wg, bg, ng]
    aliases = {}
    if states_prev is not None:
        in_specs.append(pl.BlockSpec(memory_space=pl.ANY))
        args.append(states_prev)
        aliases = {len(args) - 1: 1}
    return pl.pallas_call(
        _gla_ctx_kernel,
        grid=(BATCH // GLA_CTX_SEQS,),
        in_specs=in_specs,
        out_specs=[pl.BlockSpec((rows, GLA_W), lambda s: (s, 0)),
                   pl.BlockSpec((GLA_CTX_SEQS, None, 2, GLA_HEADS, GLA_DK, GLA_DV),
                                lambda s: (s, layer, 0, 0, 0, 0))],
        out_shape=[jax.ShapeDtypeStruct((T_ALL, GLA_W), BF16),
                   jax.ShapeDtypeStruct((BATCH, DEPTH, 2, GLA_HEADS, GLA_DK, GLA_DV), F32)],
        input_output_aliases=aliases,
        compiler_params=_cparams("parallel"),
        name="gla_ctx",
    )(*args)


def _gla_lat(pa, ps, wg, bg, ng, s0, y):
    rb0 = T_CTX // DEC_SEQ
    cb = PA_GLA // GLA_W
    const = lambda shape: pl.BlockSpec(shape, lambda s: (0,) * len(shape))
    return pl.pallas_call(
        _gla_lat_kernel,
        grid=(DEC_BATCH,),
        in_specs=[pl.BlockSpec((DEC_SEQ, GLA_W), lambda s, j=j: (rb0 + s, cb + j)) for j in range(4)] + [
            pl.BlockSpec((DEC_SEQ, LANES), lambda s: (rb0 + s, PS_LR // LANES)),
            const((2, 2 * LANES, 2 * GLA_W)), const((2, 1, GLA_W)), const((1, GLA_W)),
            pl.BlockSpec((None, 2, GLA_HEADS, GLA_DK, GLA_DV), lambda s: (s, 0, 0, 0, 0)),
            pl.BlockSpec(memory_space=pl.ANY)],
        out_specs=pl.BlockSpec((DEC_SEQ, GLA_W), lambda s: (rb0 + s, 0)),
        out_shape=jax.ShapeDtypeStruct((T_ALL, GLA_W), BF16),
        scratch_shapes=[pltpu.VMEM((DEC_SEQ, GLA_W), F32), pltpu.VMEM((DEC_SEQ, GLA_W), F32)],
        input_output_aliases={9: 0},
        compiler_params=_cparams("parallel"),
        name="gla_lat",
    )(pa, pa, pa, pa, ps, wg, bg, ng, s0, y)


LOG2E = 1.4426950408889634
Q_SCALE = HEAD_DIM ** -0.5 * LOG2E


def _attend_t(problems):
    scores = []
    for q, _, k_parts, _, masks in problems:
        parts = [_dot_t(k, q) for k in k_parts]
        scores.append([s if mask is None else jnp.where(mask, s, NEG_INF) for s, mask in zip(parts, masks)])
    exps = []
    for (q, sink, *_), parts in zip(problems, scores):
        sink2 = sink * LOG2E
        m = jnp.broadcast_to(sink2, (1, q.shape[0])).astype(F32)
        for s in parts:
            m = jnp.maximum(m, jnp.max(s, axis=0, keepdims=True))
        es = [jnp.exp2(s - m) for s in parts]
        den = jnp.exp2(sink2 - m)
        for e in es:
            den = den + jnp.sum(e, axis=0, keepdims=True)
        exps.append(([e.astype(BF16) for e in es], den))
    outs = []
    for (_, _, _, vt_parts, _), (es, den) in zip(problems, exps):
        o = None
        for e, vt in zip(es, vt_parts):
            pv = _mm(vt, e)
            o = pv if o is None else o + pv
        outs.append(o / den)
    return outs


def _lane_halves(block):
    low = lax.broadcasted_iota(jnp.int32, block.shape, 1) < HEAD_DIM
    swapped = pltpu.roll(block, HEAD_DIM, 1)
    zero = jnp.zeros_like(block)
    pick = lambda cond, a: jnp.where(cond, a, zero).astype(BF16)
    return [[pick(low, block), pick(~low, swapped)], [pick(low, swapped), pick(~low, block)]]


def _attn_ctx_kernel(sink_ref, q_ref, kv_ref, *rest):
    o_ref = rest[-1]
    k_halves = _lane_halves(kv_ref[:, :LANES])
    vt = kv_ref[:, LANES:].T.astype(BF16)
    problems = []
    for pair in range(ATT_HEADS // 2):
        kv = 2 * pair // ATT_GROUP
        q_pair = (q_ref[:, pair * LANES:(pair + 1) * LANES] * Q_SCALE).astype(BF16)
        problems += [(q_pair, sink_ref[2 * pair + j], [k_halves[kv][j]],
                      [vt[kv * HEAD_DIM:(kv + 1) * HEAD_DIM]], [None]) for j in range(2)]
    o_t = _attend_t(problems)
    outs = [jnp.concatenate(o_t[2 * pair:2 * pair + 2], axis=0).T for pair in range(ATT_HEADS // 2)]
    o_ref[...] = jnp.concatenate(outs, axis=-1).astype(o_ref.dtype)


def _attn_ctx(ps, sink):
    return pl.pallas_call(
        _attn_ctx_kernel,
        grid=(BATCH,),
        in_specs=[
            pl.BlockSpec(memory_space=pltpu.SMEM),
            pl.BlockSpec((SEQ, ATT_W), lambda s: (s, 0)),
            pl.BlockSpec((SEQ, 2 * LANES), lambda s: (s, PS_KV // (2 * LANES))),
        ],
        out_specs=pl.BlockSpec((SEQ, ATT_W), lambda s: (s, 0)),
        out_shape=jax.ShapeDtypeStruct((T_ALL, ATT_W), BF16),
        compiler_params=_cparams("parallel"),
        name="attn_ctx",
    )(sink, ps, ps)


def _rope(x, cos, sin_signed):
    lane = lax.broadcasted_iota(jnp.int32, x.shape, 1)
    partner = jnp.where((lane & 31) < 16, pltpu.roll(x, LANES - 16, 1), pltpu.roll(x, 16, 1))
    return x * cos + partner * sin_signed


def _attn_lat_kernel(sink_ref, q_ref, kvp_ref, kvc_ref, kvn_ref, kctx_ref, vctx_ref, cos_ref, sin_ref, yin_ref,
                     o_ref):
    del yin_ref
    n = pl.program_id(1)
    n_blk = DEC_SEQ // BLOCK
    start = n * BLOCK

    def table(ref, blk):
        blk = jnp.clip(blk, 0, n_blk - 1)
        return ref[pl.ds(pl.multiple_of(blk * BLOCK, BLOCK), BLOCK), :]

    k_loc, v_loc = [], []
    for off, ref in ((-1, kvp_ref), (0, kvc_ref), (1, kvn_ref)):
        k_loc.append(_rope(ref[:, :LANES], table(cos_ref, n + off), table(sin_ref, n + off)))
        v_loc.append(ref[:, LANES:])
    kl_halves = _lane_halves(jnp.concatenate(k_loc, axis=0))
    kc_halves = _lane_halves(kctx_ref[...])
    vt_loc = jnp.concatenate(v_loc, axis=0).T.astype(BF16)
    vt_ctx = vctx_ref[...].T.astype(BF16)

    pairs_per_kv = ATT_GROUP // 2
    span = 3 * BLOCK
    shape_t = (span, pairs_per_kv * BLOCK)
    kpos = start - WINDOW + lax.broadcasted_iota(jnp.int32, shape_t, 0)
    qpos = start + (lax.broadcasted_iota(jnp.int32, shape_t, 1) & (BLOCK - 1))
    valid_t = (jnp.abs(qpos - kpos) <= WINDOW) & (kpos >= 0) & (kpos < DEC_SEQ)

    cos_q = table(cos_ref, n)
    sin_q = table(sin_ref, n)
    q_pairs = [(_rope(q_ref[:, p * LANES:(p + 1) * LANES], cos_q, sin_q) * Q_SCALE).astype(BF16)
               for p in range(ATT_HEADS // 2)]
    problems = []
    for kv in range(ATT_KV_HEADS):
        pairs = range(kv * pairs_per_kv, (kv + 1) * pairs_per_kv)
        q = jnp.concatenate([q_pairs[p] for p in pairs], axis=0)
        rows = slice(kv * HEAD_DIM, (kv + 1) * HEAD_DIM)
        for j in range(2):
            sink = jnp.concatenate([jnp.full((1, BLOCK), sink_ref[2 * p + j], F32) for p in pairs], axis=1)
            problems.append((q, sink, [kc_halves[kv][j], kl_halves[kv][j]], [vt_ctx[rows], vt_loc[rows]],
                             [None, valid_t]))
    o_t = _attend_t(problems)
    outs = []
    for p in range(ATT_HEADS // 2):
        kv, i = divmod(p, pairs_per_kv)
        cols = slice(i * BLOCK, (i + 1) * BLOCK)
        outs.append(jnp.concatenate([o_t[2 * kv + j][:, cols] for j in range(2)], axis=0).T)
    o_ref[...] = jnp.concatenate(outs, axis=-1).astype(o_ref.dtype)


def _attn_lat(ps, sink, k_ctx, v_ctx, cos_t, sin_t, y):
    n_blk = DEC_SEQ // BLOCK
    rb0 = T_CTX // BLOCK

    def kv_spec(off):
        return pl.BlockSpec(
            (BLOCK, 2 * LANES),
            lambda b, n: (rb0 + b * n_blk + jnp.clip(n + off, 0, n_blk - 1), PS_KV // (2 * LANES)))

    return pl.pallas_call(
        _attn_lat_kernel,
        grid=(DEC_BATCH, n_blk),
        in_specs=[
            pl.BlockSpec(memory_space=pltpu.SMEM),
            pl.BlockSpec((BLOCK, ATT_W), lambda b, n: (rb0 + b * n_blk + n, 0)),
            kv_spec(-1), kv_spec(0), kv_spec(1),
            pl.BlockSpec((None, PAST_LEN, LANES), lambda b, n: (b, 0, 0)),
            pl.BlockSpec((None, PAST_LEN, LANES), lambda b, n: (b, 0, 0)),
            pl.BlockSpec((DEC_SEQ, LANES), lambda b, n: (0, 0)),
            pl.BlockSpec((DEC_SEQ, LANES), lambda b, n: (0, 0)),
            pl.BlockSpec(memory_space=pl.ANY),
        ],
        out_specs=pl.BlockSpec((BLOCK, ATT_W), lambda b, n: (rb0 + b * n_blk + n, 0)),
        out_shape=jax.ShapeDtypeStruct((T_ALL, ATT_W), BF16),
        input_output_aliases={9: 0},
        compiler_params=_cparams("parallel", "parallel"),
        name="attn_lat",
    )(sink, ps, ps, ps, ps, k_ctx, v_ctx, cos_t, sin_t, y)


def _rope_tables():
    pos = np.arange(DEC_SEQ)
    n_freq = HEAD_DIM // 4
    inv = jnp.asarray(ROPE_THETA, F32) ** (-jnp.arange(n_freq, dtype=F32) / n_freq)
    row = jnp.asarray(pos // GRID_W, F32)
    colp = jnp.asarray(pos % GRID_W, F32)
    ang_r = row[:, None] * inv[None, :]
    ang_c = colp[:, None] * inv[None, :]
    cos = jnp.concatenate([jnp.cos(ang_r)] * 2 + [jnp.cos(ang_c)] * 2, axis=-1)
    sin = jnp.concatenate([-jnp.sin(ang_r), jnp.sin(ang_r), -jnp.sin(ang_c), jnp.sin(ang_c)], axis=-1)
    return jnp.tile(cos, (1, 2)), jnp.tile(sin, (1, 2))


def _route(logits):
    lane_i = lax.broadcasted_iota(jnp.int32, logits.shape, 1)
    lane = lane_i.astype(F32)
    big = jnp.float32(1 << 20)
    is_g = lane_i < N_EXPERT_GROUPS
    lg = jnp.where(is_g, logits, -jnp.inf)
    m_g = jnp.max(lg, axis=-1, keepdims=True)
    grp = jnp.min(jnp.where(lg == m_g, lane, big), axis=-1, keepdims=True)
    z_g = jnp.sum(jnp.where(is_g, jnp.exp(lg - m_g), 0.0), axis=-1, keepdims=True)
    p_grp = 1.0 / z_g

    e_idx = lane_i - ROUTE_E0
    e_grp = (e_idx >> 2).astype(F32)
    sel = (e_idx >= 0) & (e_idx < N_EXPERTS) & (e_grp == grp)
    le = jnp.where(sel, logits, -jnp.inf)
    m_e = jnp.max(le, axis=-1, keepdims=True)
    ex = jnp.where(sel, jnp.exp(le - m_e), 0.0)
    pe = ex / jnp.sum(ex, axis=-1, keepdims=True)
    pe = jnp.where(sel, pe, -1.0)
    v1 = jnp.max(pe, axis=-1, keepdims=True)
    i1 = jnp.min(jnp.where(pe == v1, lane, big), axis=-1, keepdims=True)
    pe2 = jnp.where(lane == i1, -1.0, pe)
    v2 = jnp.max(pe2, axis=-1, keepdims=True)
    i2 = jnp.min(jnp.where(pe2 == v2, lane, big), axis=-1, keepdims=True)
    tot = v1 + v2
    return (jnp.where(lane == i1, p_grp * (v1 / tot), 0.0)
            + jnp.where(lane == i2, p_grp * (v2 / tot), 0.0)
            + jnp.where(lane_i == LANES - 1, grp, 0.0))


HALO = 16


def _merge_kernel(gate_ref, conv_ref, cprev_ref, cnext_ref, yb_ref, yc_ref, *rest, x_split):
    x_refs, rest = rest[:2 if x_split else 1], rest[2 if x_split else 1:]
    (mod_ref, cw_ref, wb_ref, wo_ref, g2_ref, wr_ref, br_ref,
     x1_ref, h2_ref, rg_ref, wbb_ref, wob_ref) = rest
    tm = TM_MERGE
    i = pl.program_id(0)

    def x_rows(rows):
        if not x_split:
            return x_refs[0][rows, :]
        return jnp.where(i < x_split, x_refs[0][rows, :], x_refs[1][rows, :])

    @pl.when(i == 0)
    def _():
        wbb_ref[...] = wb_ref[...].astype(BF16)
        wob_ref[...] = wo_ref[...].astype(BF16)

    def gated(ref):
        return ref[:, CONV_W:2 * CONV_W].astype(F32) * ref[:, 2 * CONV_W:3 * CONV_W].astype(F32)

    a_b = conv_ref[:, 0:CONV_W].astype(F32)
    u = gated(conv_ref)
    u_before = gated(cprev_ref)[HALO - 1:HALO]
    u_after = gated(cnext_ref)[0:1]
    r = lax.broadcasted_iota(jnp.int32, (tm, 1), 0)
    g_row = i * tm + r
    seq_mask = jnp.where(g_row < T_CTX, SEQ - 1, DEC_SEQ - 1)
    first = (g_row & seq_mask) == 0
    last = ((g_row + 1) & seq_mask) == 0
    u_prev = jnp.where(r == 0, u_before, pltpu.roll(u, 1, 0))
    u_next = jnp.where(r == tm - 1, u_after, pltpu.roll(u, tm - 1, 0))
    u_prev = jnp.where(first, 0.0, u_prev)
    u_next = jnp.where(last, 0.0, u_next)
    y_a = (a_b * (u_prev * cw_ref[0:1, :] + u * cw_ref[1:2, :] + u_next * cw_ref[2:3, :])).astype(BF16)

    blocks = [slice(r0, r0 + MERGE_ROWS) for r0 in range(0, tm, MERGE_ROWS)]
    ys = [(y_a[rows], yb_ref[rows, :], yc_ref[rows, :]) for rows in blocks]
    branches = [[_mm(y, wbb_ref[j]) for j, y in enumerate(y3)] for y3 in ys]
    zs = []
    for rows, br3 in zip(blocks, branches):
        z = sum(gate_ref[rows, j * D_MODEL:(j + 1) * D_MODEL].astype(F32) * br3[j] for j in range(3))
        zs.append(z.astype(BF16))
    outs = [_mm(z, wob_ref[...]) for z in zs]
    h2s = []
    for rows, o in zip(blocks, outs):
        x1 = x_rows(rows) + mod_ref[2:3, :] * o
        x1_ref[rows, :] = x1
        h2 = _mod_norm(x1, g2_ref[...], mod_ref[...], 3)
        h2_ref[rows, :] = h2.astype(BF16)
        h2s.append(h2)
    logits = [_split_dot(h2, wr_ref[...]) + br_ref[...] for h2 in h2s]
    for rows, lg in zip(blocks, logits):
        rg_ref[rows, :] = _route(lg)


def _merge(pg, pa, y_b, y_c, x, mod_l, conv_w, layer, wb, wo, g2, wr, br):
    tm = TM_MERGE
    n_tiles = T_ALL // tm
    hb = tm // HALO
    const = lambda shape: pl.BlockSpec(shape, lambda i: (0,) * len(shape))
    if isinstance(x, tuple):
        x_specs, x_split = _split_x_specs(tm)
    else:
        x, x_specs, x_split = (x,), [pl.BlockSpec((tm, D_MODEL), lambda i: (i, 0))], 0
    return pl.pallas_call(
        functools.partial(_merge_kernel, x_split=x_split),
        grid=(n_tiles,),
        in_specs=[
            pl.BlockSpec((tm, PG_W), lambda i: (i, 0)),
            pl.BlockSpec((tm, 3 * CONV_W), lambda i: (i, 0)),
            pl.BlockSpec((HALO, 3 * CONV_W), lambda i: (jnp.maximum(i * hb - 1, 0), 0)),
            pl.BlockSpec((HALO, 3 * CONV_W), lambda i: (jnp.minimum((i + 1) * hb, n_tiles * hb - 1), 0)),
            pl.BlockSpec((tm, GLA_W), lambda i: (i, 0)),
            pl.BlockSpec((tm, ATT_W), lambda i: (i, 0)),
            *x_specs,
            pl.BlockSpec((None, 6, D_MODEL), lambda i: (_cond_row(i, tm), 0, 0)),
            const((CONV_K, CONV_W)),
            pl.BlockSpec((None, 3, 512, D_MODEL), lambda i: (layer, 0, 0, 0)),
            pl.BlockSpec((None, D_MODEL, D_MODEL), lambda i: (layer, 0, 0)),
            const((1, D_MODEL)),
            const((2 * D_MODEL, 2 * LANES)),
            const((1, LANES)),
        ],
        out_specs=[
            pl.BlockSpec((tm, D_MODEL), lambda i: (i, 0)),
            pl.BlockSpec((tm, D_MODEL), lambda i: (i, 0)),
            pl.BlockSpec((tm, LANES), lambda i: (i, 0)),
        ],
        out_shape=[
            jax.ShapeDtypeStruct((T_ALL, D_MODEL), F32),
            jax.ShapeDtypeStruct((T_ALL, D_MODEL), BF16),
            jax.ShapeDtypeStruct((T_ALL, LANES), F32),
        ],
        scratch_shapes=[pltpu.VMEM((3, 512, D_MODEL), BF16), pltpu.VMEM((D_MODEL, D_MODEL), BF16)],
        compiler_params=_cparams("arbitrary"),
        name="merge",
    )(pg, pa, pa, pa, y_b, y_c, *x, mod_l, conv_w, wb, wo, g2, wr, br)


SUB = 128
MOE_ROWS = 256


def _dispatch_tables(rg):
    n_tiles = T_ALL // TM
    grp = rg[:, LANES - 1].astype(jnp.int32).reshape(n_tiles, TM)
    hot = grp[..., None] == jnp.arange(N_EXPERT_GROUPS, dtype=jnp.int32)
    onehot = hot.astype(jnp.int32)
    cnt = onehot.sum(axis=1)
    start = jnp.cumsum(cnt, axis=1) - cnt
    first = start // SUB
    nblk = jnp.where(cnt > 0, (start + cnt + SUB - 1) // SUB - first, 0)
    before = jnp.tril(jnp.ones((TM, TM), BF16), -1)
    rank = jnp.einsum("ts,nsg->ntg", before, hot.astype(BF16), preferred_element_type=F32).astype(jnp.int32)
    slot = ((start[:, None, :] + rank) * onehot).sum(axis=-1)
    return (first.reshape(-1), nblk.reshape(-1), slot.reshape(n_tiles, 1, TM), slot.reshape(T_ALL, 1))


def _moe_kernel(bstart_ref, nblk_ref, h_ref, rg_ref, srow_ref, scol_ref, x1_ref, mod_ref, wg_ref, wu_ref,
                wd_ref, g_ref, *rest, final, tile0):
    if final:
        y_ref, hs_ref, gs_ref, os_ref = rest
    else:
        modn_ref, x2_ref, hn_ref, hs_ref, gs_ref, os_ref = rest
    i = pl.program_id(0)
    grp = pl.program_id(1)

    @pl.when(grp == 0)
    def _():
        slot_of_token = srow_ref[...]
        payload = jnp.concatenate([h_ref[...], *_split3(rg_ref[...])], axis=-1)
        for r0 in range(0, TM, MOE_ROWS):
            rows = slice(r0, r0 + MOE_ROWS)
            slot = r0 + lax.broadcasted_iota(jnp.int32, (MOE_ROWS, TM), 0)
            pm = jnp.where(slot == slot_of_token, 1.0, 0.0).astype(BF16)
            moved = _mm(pm, payload)
            hs_ref[rows, :] = moved[:, :D_MODEL].astype(BF16)
            gs_ref[rows, :] = sum(moved[:, D_MODEL + j * LANES:D_MODEL + (j + 1) * LANES] for j in range(3))
        os_ref[...] = jnp.zeros_like(os_ref)

    k = (tile0 + i) * N_EXPERT_GROUPS + grp
    first = bstart_ref[k]
    n_blocks = nblk_ref[k]

    def experts(block0, n_sub):
        n_rows = n_sub * SUB
        rows = pl.ds(pl.multiple_of(block0 * SUB, SUB), n_rows)
        lane = lax.broadcasted_iota(jnp.int32, (n_rows, LANES), 1)
        x = hs_ref[rows, :]
        gates = gs_ref[rows, :]
        hidden = []
        for e in range(EXPERTS_PER_GROUP):
            ge = _mm(x, wg_ref[e])
            ue = _mm(x, wu_ref[e])
            w = jnp.sum(jnp.where(lane == ROUTE_E0 + grp * EXPERTS_PER_GROUP + e, gates, 0.0),
                        axis=-1, keepdims=True)
            hidden.append(((ge * jax.nn.sigmoid(ge)) * ue * w).astype(BF16))
        acc = _mm(jnp.concatenate(hidden, axis=-1), wd_ref[...].reshape(EXPERTS_PER_GROUP * D_EXPERT, D_MODEL))
        os_ref[rows, :] = (os_ref[rows, :].astype(F32) + acc).astype(BF16)

    odd = n_blocks % 2 == 1
    n_pairs = jnp.where(odd & (n_blocks >= 3), (n_blocks - 3) // 2, n_blocks // 2)

    def pair(p, carry):
        experts(first + 2 * p, 2)
        return carry

    lax.fori_loop(0, n_pairs, pair, 0)

    @pl.when(odd & (n_blocks >= 3))
    def _():
        experts(first + n_blocks - 3, 3)

    @pl.when(n_blocks == 1)
    def _():
        experts(first, 1)

    @pl.when(grp == N_EXPERT_GROUPS - 1)
    def _():
        sorted_out = os_ref[...]
        lane_slot = lax.broadcasted_iota(jnp.int32, (MOE_ROWS, TM), 1)
        for r0 in range(0, TM, MOE_ROWS):
            rows = slice(r0, r0 + MOE_ROWS)
            pt = jnp.where(lane_slot == scol_ref[rows, :], 1.0, 0.0).astype(BF16)
            x2 = x1_ref[rows, :] + mod_ref[5:6, :] * _mm(pt, sorted_out)
            if final:
                y_ref[rows, :] = _rms(x2) * g_ref[...]
            else:
                x2_ref[rows, :] = x2
                hn_ref[rows, :] = _mod_norm(x2, g_ref[...], modn_ref[...], 0).astype(BF16)


def _moe(tables, h2, rg, x1, mod_l, layer, w_eg, w_eu, w_ed, g, mod_next, tile0, n_tiles):
    bstart, nblk, slot_row, slot_col = tables
    final = mod_next is None
    row = lambda i, e, *_: (tile0 + i, 0)
    mod_spec = lambda: pl.BlockSpec((None, 6, D_MODEL), lambda i, e, *_: (_cond_row(tile0 + i, TM), 0, 0))
    group = lambda i, e, *_: (layer, e, 0, 0)
    in_specs = [
        pl.BlockSpec((TM, D_MODEL), row),
        pl.BlockSpec((TM, LANES), row),
        pl.BlockSpec((None, 1, TM), lambda i, e, *_: (tile0 + i, 0, 0)),
        pl.BlockSpec((TM, 1), row),
        pl.BlockSpec((TM, D_MODEL), row),
        mod_spec(),
        pl.BlockSpec((None, EXPERTS_PER_GROUP, D_MODEL, D_EXPERT), group),
        pl.BlockSpec((None, EXPERTS_PER_GROUP, D_MODEL, D_EXPERT), group),
        pl.BlockSpec((None, EXPERTS_PER_GROUP, D_EXPERT, D_MODEL), group),
        pl.BlockSpec((1, D_MODEL), lambda i, e, *_: (0, 0)),
    ]
    args = [h2, rg, slot_row, slot_col, x1, mod_l, w_eg, w_eu, w_ed, g]
    if final:
        out_specs = pl.BlockSpec((TM, D_MODEL), lambda i, e, *_: (i, 0))
        out_shape = jax.ShapeDtypeStruct((n_tiles * TM, D_MODEL), F32)
    else:
        in_specs.append(mod_spec())
        args.append(mod_next)
        out_specs = [pl.BlockSpec((TM, D_MODEL), row)] * 2
        out_shape = [jax.ShapeDtypeStruct((T_ALL, D_MODEL), F32), jax.ShapeDtypeStruct((T_ALL, D_MODEL), BF16)]
    return pl.pallas_call(
        functools.partial(_moe_kernel, final=final, tile0=tile0),
        grid_spec=pltpu.PrefetchScalarGridSpec(
            num_scalar_prefetch=2,
            grid=(n_tiles, N_EXPERT_GROUPS),
            in_specs=in_specs,
            out_specs=out_specs,
            scratch_shapes=[pltpu.VMEM((TM, D_MODEL), BF16), pltpu.VMEM((TM, LANES), F32),
                            pltpu.VMEM((TM, D_MODEL), BF16)],
        ),
        out_shape=out_shape,
        compiler_params=_cparams("parallel", "arbitrary"),
        name="moe_final" if final else "moe",
    )(bstart, nblk, *args)


def _cache_kernel(*refs):
    ps_refs, (k_ref, v_ref) = refs[:DEPTH], refs[DEPTH:]
    n_seq = TM // SEQ
    for l, ps_ref in enumerate(ps_refs):
        k_ref[:, l] = ps_ref[:, :LANES].reshape(n_seq, SEQ, LANES)
        v_ref[:, l] = ps_ref[:, LANES:].reshape(n_seq, SEQ, LANES)


def _cache(ps_layers):
    n_seq = TM // SEQ
    out = pl.BlockSpec((n_seq, DEPTH, SEQ, LANES), lambda i: (i, 0, 0, 0))
    return pl.pallas_call(
        _cache_kernel,
        grid=(T_CTX // TM,),
        in_specs=[pl.BlockSpec((TM, 2 * LANES), lambda i: (i, PS_KV // (2 * LANES)))] * DEPTH,
        out_specs=[out, out],
        out_shape=[jax.ShapeDtypeStruct((BATCH, DEPTH, SEQ, LANES), F32)] * 2,
        compiler_params=_cparams("parallel"),
        name="cache",
    )(*ps_layers)


def _prep_gla_gate(w_gate):
    pads = [((0, 0), (d * GLA_RANK, LANES - (d + 1) * GLA_RANK), (0, 0)) for d in range(2)]
    return _pack_split(jnp.stack([jnp.pad(w_gate[:, d], pads[d]) for d in range(2)], axis=1))


def _prep_router(w_rg, b_rg, w_re, b_re):
    unused = LANES - N_EXPERT_GROUPS - N_EXPERTS
    w = jnp.concatenate([w_rg, w_re, jnp.zeros((DEPTH, D_MODEL, unused), F32)], axis=-1)
    b = jnp.concatenate([b_rg, b_re, jnp.zeros((DEPTH, unused), F32)], axis=-1)
    return _pack_split(w), b.reshape(DEPTH, 1, LANES)


def kernel(x_prompt, x_sample, state_gla, cache_k, cache_v, c, c_ctx, w_ada, b_ada, norm1_g, norm2_g, w_in,
           conv_w, gla_w_gate, gla_b_gate, gla_norm_g, attn_sink, w_branch, w_out, w_route_group,
           b_route_group, w_route_expert, b_route_expert, w_exp_gate, w_exp_up, w_exp_down, final_norm_g):
    cond = jnp.zeros((N_COND, D_MODEL), F32).at[0].set(c_ctx).at[1:1 + DEC_BATCH].set(c)
    mod = _modulation(cond, w_ada, b_ada).reshape(DEPTH, N_COND, 6, D_MODEL)
    cos_t, sin_t = _rope_tables()
    row = lambda v: v.reshape(1, -1)

    x = (x_prompt.reshape(T_CTX, D_MODEL), x_sample.reshape(T_LAT, D_MODEL))
    h = _prenorm(*x, mod[0], row(norm1_g[0]))
    wg_all = _prep_gla_gate(gla_w_gate)
    wr_all, br_all = _prep_router(w_route_group, b_route_group, w_route_expert, b_route_expert)
    states = None
    ps_layers = []
    y_prompt = y_sample = None
    w_t = jnp.swapaxes(w_in, 1, 2)
    proj_a = functools.partial(_proj, pieces=[(0, PA_W // 2)], n_tiles=2, out_dtype=BF16, name="proj_a",
                               act="silu_tail")
    proj_s = functools.partial(_proj, pieces=[(W_IN_TQ, W_IN_GATE - W_IN_TQ), (W_IN_LR, LANES)], n_tiles=1,
                               out_dtype=F32, name="proj_s")
    proj_g = functools.partial(_proj, pieces=[(W_IN_GATE, PG_W // 2)], n_tiles=2, out_dtype=BF16,
                               name="proj_g", act="sigmoid")
    experts = None
    for l in range(DEPTH):
        if l == 0:
            pa, w_eg = proj_a(h, w_t, l, cast=(w_exp_gate, 2))
            ps, w_ed = proj_s(h, w_t, l, cast=(w_exp_down, 4))
            pg, w_eu = proj_g(h, w_t, l, cast=(w_exp_up, 2))
            experts = (w_eg, w_eu, w_ed)
        else:
            pa, ps, pg = proj_a(h, w_t, l), proj_s(h, w_t, l), proj_g(h, w_t, l)
        ps_layers.append(ps)

        wg = wg_all[l]
        bg = gla_b_gate[l].reshape(2, 1, GLA_W)
        ng = row(gla_norm_g[l])
        y_b, states = _gla_ctx(pa, ps, wg, bg, ng, states, l)
        y_b = _gla_lat(pa, ps, wg, bg, ng, state_gla[:, l], y_b)
        y_c = _attn_ctx(ps, attn_sink[l])
        y_c = _attn_lat(ps, attn_sink[l], cache_k[:, l].reshape(DEC_BATCH, PAST_LEN, LANES),
                        cache_v[:, l].reshape(DEC_BATCH, PAST_LEN, LANES), cos_t, sin_t, y_c)

        x1, h2, rg = _merge(pg, pa, y_b, y_c, x, mod[l], conv_w[l], l, w_branch, w_out, row(norm2_g[l]),
                            wr_all[l], br_all[l])
        moe_in = (_dispatch_tables(rg), h2, rg, x1, mod[l], l, *experts)
        if l + 1 < DEPTH:
            x, h = _moe(*moe_in, row(norm1_g[l + 1]), mod[l + 1], 0, T_ALL // TM)
        else:
            gf = row(final_norm_g)
            y_prompt = _moe(*moe_in, gf, None, 0, T_CTX // TM)
            y_sample = _moe(*moe_in, gf, None, T_CTX // TM, T_LAT // TM)

    new_k, new_v = _cache(ps_layers)
    kv_shape = (BATCH, DEPTH, SEQ, ATT_KV_HEADS, HEAD_DIM)
    return (y_prompt.reshape(BATCH, SEQ, D_MODEL), y_sample.reshape(DEC_BATCH, DEC_SEQ, D_MODEL), states,
            new_k.reshape(kv_shape), new_v.reshape(kv_shape))
```

```python
import functools

import jax
import jax.numpy as jnp
import numpy as np
from jax import lax
from jax.experimental import pallas as pl
from jax.experimental.pallas import tpu as pltpu

F32 = jnp.float32
BF16 = jnp.bfloat16

D_MODEL = 1024
BATCH = 32
SEQ = 256
DEPTH = 2
DEC_BATCH = 2
DEC_SEQ = 2048
PAST_LEN = 512
GRID_W = 64
EPS = 1e-6
CONV_W = 512
CONV_K = 3
GLA_HEADS = 4
GLA_DK = 128
GLA_DV = 128
GLA_W = GLA_HEADS * GLA_DV
GLA_RANK = 16
GLA_TAU = 16.0
GLA_CHUNK = 64
ATT_HEADS = 8
ATT_KV_HEADS = 2
ATT_GROUP = ATT_HEADS // ATT_KV_HEADS
HEAD_DIM = 64
ATT_W = ATT_HEADS * HEAD_DIM
WINDOW = 128
BLOCK = 128
ROPE_THETA = 10000.0
N_EXPERT_GROUPS = 4
EXPERTS_PER_GROUP = 4
N_EXPERTS = 16
D_EXPERT = 256
NEG_INF = -1e30

T_CTX = BATCH * SEQ
T_LAT = DEC_BATCH * DEC_SEQ
T_ALL = T_CTX + T_LAT
N_COND = 8
LANES = 128

W_IN_LR = 3584
W_IN_TQ = 3616
W_IN_GATE = 4384
N_IN = 7456
PA_W = 3584
PA_GLA = 1536
PS_W = 896
PS_KV = 512
PS_LR = 768
PG_W = 3 * D_MODEL
ROUTE_E0 = N_EXPERT_GROUPS

TM = 1024
TM_MERGE = 512
MERGE_ROWS = 256
VMEM_LIMIT = 56 * 1024 * 1024


def _cparams(*sem):
    return pltpu.CompilerParams(dimension_semantics=sem, vmem_limit_bytes=VMEM_LIMIT)


def _cond_row(i, tm):
    n_ctx = T_CTX // tm
    per = DEC_SEQ // tm
    return jnp.where(i < n_ctx, 0, 1 + (i - n_ctx) // per)


def _mm(a, b):
    return jnp.dot(a, b, preferred_element_type=F32)


def _dot_t(a, b):
    return lax.dot_general(a, b, (((1,), (1,)), ((), ())), preferred_element_type=F32)


def _dot_ta(a, b):
    return lax.dot_general(a, b, (((0,), (0,)), ((), ())), preferred_element_type=F32)


def _hi_lo(w):
    hi = w.astype(BF16)
    return hi, (w - hi.astype(F32)).astype(BF16)


def _pack_split(w):
    hi, lo = _hi_lo(w)
    return jnp.concatenate([jnp.concatenate([hi, lo], axis=-1),
                            jnp.concatenate([hi, jnp.zeros_like(lo)], axis=-1)], axis=-2)


def _split_dot(a, w_pack):
    a_hi, a_lo = _hi_lo(a)
    both = _mm(jnp.concatenate([a_hi, a_lo], axis=-1), w_pack)
    n = both.shape[-1] // 2
    return both[:, :n] + both[:, n:]


def _rms(x):
    return x * lax.rsqrt(jnp.mean(x * x, axis=-1, keepdims=True) + EPS)


def _mod_norm(x, g, mod, shift_row):
    return _rms(x) * g * (1.0 + mod[shift_row + 1:shift_row + 2, :]) + mod[shift_row:shift_row + 1, :]


def _mod_kernel(c_ref, w_ref, b_ref, o_ref):
    c = c_ref[...]
    s = (c * jax.nn.sigmoid(c)).astype(BF16)
    o_ref[...] = _mm(s, w_ref[...].astype(BF16)) + b_ref[...]


def _modulation(cond, w_ada, b_ada):
    tn = 1536
    return pl.pallas_call(
        _mod_kernel,
        grid=(DEPTH, 6 * D_MODEL // tn),
        in_specs=[
            pl.BlockSpec((N_COND, D_MODEL), lambda l, j: (0, 0)),
            pl.BlockSpec((None, D_MODEL, tn), lambda l, j: (l, 0, j)),
            pl.BlockSpec((None, 1, tn), lambda l, j: (l, 0, j)),
        ],
        out_specs=pl.BlockSpec((None, N_COND, tn), lambda l, j: (l, 0, j)),
        out_shape=jax.ShapeDtypeStruct((DEPTH, N_COND, 6 * D_MODEL), F32),
        compiler_params=_cparams("parallel", "parallel"),
        name="modulation",
    )(cond, w_ada, b_ada.reshape(DEPTH, 1, 6 * D_MODEL))


def _split_x_specs(tm):
    n_ctx = T_CTX // tm
    return [pl.BlockSpec((tm, D_MODEL), lambda i, *_: (jnp.minimum(i, n_ctx - 1), 0)),
            pl.BlockSpec((tm, D_MODEL), lambda i, *_: (jnp.maximum(i - n_ctx, 0), 0))], n_ctx


def _prenorm_kernel(xp_ref, xs_ref, mod_ref, g_ref, h_ref):
    x = jnp.where(pl.program_id(0) < T_CTX // TM, xp_ref[...], xs_ref[...])
    h_ref[...] = _mod_norm(x, g_ref[...], mod_ref[...], 0).astype(BF16)


def _prenorm(xp, xs, mod_l, g):
    x_specs, _ = _split_x_specs(TM)
    return pl.pallas_call(
        _prenorm_kernel,
        grid=(T_ALL // TM,),
        in_specs=x_specs + [
            pl.BlockSpec((None, 6, D_MODEL), lambda i: (_cond_row(i, TM), 0, 0)),
            pl.BlockSpec((1, D_MODEL), lambda i: (0, 0)),
        ],
        out_specs=pl.BlockSpec((TM, D_MODEL), lambda i: (i, 0)),
        out_shape=jax.ShapeDtypeStruct((T_ALL, D_MODEL), BF16),
        compiler_params=_cparams("parallel"),
        name="prenorm",
    )(xp, xs, mod_l, g)


def _proj_kernel(h_ref, *refs, act, n_w):
    w_refs, refs = refs[:n_w], refs[n_w:]
    if len(refs) == 4:
        cast_in_ref, o_ref, cast_out_ref, wb_ref = refs
        cast_out_ref[...] = cast_in_ref[...].astype(BF16)
    else:
        o_ref, wb_ref = refs

    @pl.when(pl.program_id(1) == 0)
    def _():
        col = 0
        for w_ref in w_refs:
            n = w_ref.shape[0]
            wb_ref[:, col:col + n] = w_ref[...].T.astype(BF16)
            col += n

    def sigmoid(t):
        return 0.5 * jnp.tanh(0.5 * t) + 0.5

    y = _mm(h_ref[...], wb_ref[...])
    if act == "sigmoid":
        y = sigmoid(y)
    if act == "silu_tail":
        last = pl.program_id(0) == pl.num_programs(0) - 1
        tail = y[:, -GLA_W:]
        o_ref[:, :-GLA_W] = y[:, :-GLA_W].astype(o_ref.dtype)
        o_ref[:, -GLA_W:] = jnp.where(last, tail * sigmoid(tail), tail).astype(o_ref.dtype)
    else:
        o_ref[...] = y.astype(o_ref.dtype)


def _proj(h, w_t, layer, pieces, n_tiles, out_dtype, name, act=None, cast=None):
    tn = sum(n for _, n in pieces)
    n_rows = T_ALL // TM

    def w_spec(c0, n):
        return pl.BlockSpec((pl.Squeezed(), pl.Element(n), pl.Element(D_MODEL)),
                            lambda j, i: (layer, pl.multiple_of(c0 + j * tn, 8), 0))

    in_specs = [pl.BlockSpec((TM, D_MODEL), lambda j, i: (i, 0))] + [w_spec(c0, n) for c0, n in pieces]
    out_specs = [pl.BlockSpec((TM, tn), lambda j, i: (i, j))]
    out_shape = [jax.ShapeDtypeStruct((T_ALL, n_tiles * tn), out_dtype)]
    args = [h] + [w_t] * len(pieces)
    if cast is not None:
        src, per_block = cast
        per_layer = N_EXPERTS // per_block
        n_blocks = DEPTH * per_layer
        assert n_blocks <= n_tiles * n_rows

        def block(j, i):
            b = jnp.minimum(j * n_rows + i, n_blocks - 1)
            return (b // per_layer, b % per_layer, 0, 0)

        spec = pl.BlockSpec((None, per_block) + src.shape[2:], block)
        in_specs.append(spec)
        out_specs.append(spec)
        out_shape.append(jax.ShapeDtypeStruct(src.shape, BF16))
        args.append(src)
    out = pl.pallas_call(
        functools.partial(_proj_kernel, act=act, n_w=len(pieces)),
        grid=(n_tiles, n_rows),
        in_specs=in_specs,
        out_specs=out_specs,
        out_shape=out_shape,
        scratch_shapes=[pltpu.VMEM((D_MODEL, tn), BF16)],
        compiler_params=_cparams("arbitrary", "arbitrary"),
        name=name,
    )(*args)
    return out[0] if cast is None else out


GLA_GROUP = 256
GLA_CTX_SEQS = 4


def _log_sigmoid(z):
    return jnp.minimum(z, 0.0) - jnp.log(1.0 + jnp.exp(-jnp.abs(z)))


def _split3(x):
    hi = x.astype(BF16)
    r1 = x - hi.astype(F32)
    mid = r1.astype(BF16)
    lo = (r1 - mid.astype(F32)).astype(BF16)
    return hi, mid, lo


def _gla_keep(d):
    ri = lax.broadcasted_iota(jnp.int32, (GLA_GROUP, GLA_GROUP), 0)
    ci = lax.broadcasted_iota(jnp.int32, (GLA_GROUP, GLA_GROUP), 1)
    if d == 0:
        return (ci <= ri) & (ci >= (ri & ~(GLA_CHUNK - 1)))
    return (ci >= ri) & (ci <= (ri | (GLA_CHUNK - 1)))


def _gla_groups(items, keeps):
    C = GLA_CHUNK
    nc = GLA_GROUP // C
    row_chunk = lax.broadcasted_iota(jnp.int32, (GLA_GROUP, GLA_DK), 0) // C
    in_chunk = [row_chunk == c for c in range(nc)]
    tris = [jnp.where(keep, 1.0, 0.0).astype(BF16) for keep in keeps]

    las = []
    for _, lr, wg_pack, bg, _, _ in items:
        lr = jnp.where(lax.broadcasted_iota(jnp.int32, lr.shape, 1) < 2 * GLA_RANK, lr, 0.0)
        las.append(_log_sigmoid(_split_dot(lr, wg_pack) + bg) / GLA_TAU)
    cums = [_mm(tris[item[5]], jnp.concatenate(_split3(la), axis=-1)) for item, la in zip(items, las)]

    ops = []
    for (load, _, _, _, states, d), parts in zip(items, cums):
        width = len(states) * GLA_DK
        b_all = parts[:, :width] + parts[:, width:2 * width] + parts[:, 2 * width:]
        edge = C - 1 if d == 0 else 0
        for h in range(len(states)):
            b = b_all[:, h * GLA_DK:(h + 1) * GLA_DK]
            b_last = [b[c * C + edge:c * C + edge + 1, :] for c in range(nc)]
            bl = jnp.concatenate([jnp.broadcast_to(t, (C, GLA_DK)) for t in b_last], axis=0)
            k = load(1, h)
            q_in = load(0, h) * (GLA_DK ** -0.5) * jnp.exp(b)
            k_in = (k * jnp.exp(-b)).astype(BF16)
            k_end = k * jnp.exp(bl - b)
            k_spread = jnp.concatenate([jnp.where(in_chunk[c], k_end, 0.0) for c in range(nc)], axis=-1)
            q_spread = jnp.concatenate([jnp.where(in_chunk[c], q_in, 0.0) for c in range(nc)], axis=-1)
            ops.append((q_in.astype(BF16), k_in, load(2, h).astype(BF16), k_spread.astype(BF16),
                        q_spread.astype(BF16), [jnp.exp(t) for t in b_last], d))
    att_t = [jnp.where(keeps[1 - op[6]], _dot_t(op[1], op[0]), 0.0).astype(BF16) for op in ops]
    ds_t = [_dot_ta(op[2], op[3]) for op in ops]
    o_t = [_dot_ta(op[2], a) for op, a in zip(ops, att_t)]
    all_states = [st for item in items for st in item[4]]
    new_states, stacked = [], []
    for st, ds, op in zip(all_states, ds_t, ops):
        order = range(nc) if op[6] == 0 else range(nc - 1, -1, -1)
        starts = [None] * nc
        for c in order:
            starts[c] = st
            st = st * op[5][c] + ds[:, c * GLA_DK:(c + 1) * GLA_DK]
        new_states.append(st)
        stacked.append(jnp.concatenate(starts, axis=-1).astype(BF16))
    outs = [o + _dot_t(s, op[4]) for o, s, op in zip(o_t, stacked, ops)]
    results, pos = [], 0
    for item in items:
        n = len(item[4])
        results.append((outs[pos:pos + n], new_states[pos:pos + n]))
        pos += n
    return results


def _gla_loader(qkv_refs, rows):
    def load(i, h):
        return qkv_refs[i][rows, h * GLA_DK:(h + 1) * GLA_DK].astype(F32)
    return load


def _gla_finish(o, r_act, ng):
    o = o * lax.rsqrt(jnp.mean(o * o, axis=-1, keepdims=True) + EPS)
    return o * ng * r_act


def _gla_ctx_kernel(q_ref, k_ref, v_ref, r_ref, lr_ref, wg_ref, bg_ref, ng_ref, *rest):
    y_ref, s_ref = rest[-2:]
    zero = jnp.zeros((GLA_DV, GLA_DK), F32)
    items = []
    for s in range(GLA_CTX_SEQS):
        rows = slice(s * SEQ, (s + 1) * SEQ)
        load = _gla_loader((q_ref, k_ref, v_ref), rows)
        items += [(load, lr_ref[rows, :], wg_ref[d], bg_ref[d], [zero] * GLA_HEADS, d) for d in range(2)]
    results = _gla_groups(items, [_gla_keep(0), _gla_keep(1)])
    for s in range(GLA_CTX_SEQS):
        rows = slice(s * SEQ, (s + 1) * SEQ)
        (o_f, st_f), (o_b, st_b) = results[2 * s], results[2 * s + 1]
        for h in range(GLA_HEADS):
            s_ref[s, 0, h] = st_f[h].T
            s_ref[s, 1, h] = st_b[h].T
            cs = slice(h * GLA_DV, (h + 1) * GLA_DV)
            y = _gla_finish((o_f[h] + o_b[h]).T, r_ref[rows, cs].astype(F32), ng_ref[:, cs])
            y_ref[rows, cs] = y.astype(y_ref.dtype)


def _gla_lat_kernel(q_ref, k_ref, v_ref, r_ref, lr_ref, wg_ref, bg_ref, ng_ref, s0_ref, yin_ref,
                    y_ref, of_ref, ob_ref):
    del yin_ref
    n_groups = DEC_SEQ // GLA_GROUP
    keeps = [_gla_keep(0), _gla_keep(1)]

    def body(g, carry):
        st_f, st_b = carry
        rf = pl.ds(pl.multiple_of(g * GLA_GROUP, GLA_GROUP), GLA_GROUP)
        rb = pl.ds(pl.multiple_of((n_groups - 1 - g) * GLA_GROUP, GLA_GROUP), GLA_GROUP)
        qkv = (q_ref, k_ref, v_ref)
        (o_f, st_f), (o_b, st_b) = _gla_groups(
            [(_gla_loader(qkv, rf), lr_ref[rf, :], wg_ref[0], bg_ref[0], list(st_f), 0),
             (_gla_loader(qkv, rb), lr_ref[rb, :], wg_ref[1], bg_ref[1], list(st_b), 1)], keeps)
        of_ref[rf, :] = jnp.concatenate([o.T for o in o_f], axis=-1)
        ob_ref[rb, :] = jnp.concatenate([o.T for o in o_b], axis=-1)
        return tuple(st_f), tuple(st_b)

    init = tuple(tuple(s0_ref[d, h].T for h in range(GLA_HEADS)) for d in range(2))
    lax.fori_loop(0, n_groups, body, init)
    for h in range(GLA_HEADS):
        cs = slice(h * GLA_DV, (h + 1) * GLA_DV)
        y = _gla_finish(of_ref[:, cs] + ob_ref[:, cs], r_ref[:, cs].astype(F32), ng_ref[:, cs])
        y_ref[:, cs] = y.astype(y_ref.dtype)


def _gla_ctx(pa, ps, wg, bg, ng, states_prev, layer):
    cb = PA_GLA // GLA_W
    rows = GLA_CTX_SEQS * SEQ
    const = lambda shape: pl.BlockSpec(shape, lambda s: (0,) * len(shape))
    in_specs = [pl.BlockSpec((rows, GLA_W), lambda s, j=j: (s, cb + j)) for j in range(4)] + [
        pl.BlockSpec((rows, LANES), lambda s: (s, PS_LR // LANES)),
        const((2, 2 * LANES, 2 * GLA_W)), const((2, 1, GLA_W)), const((1, GLA_W))]
    args = [pa, pa, pa, pa, ps, wg, bg, ng]
    aliases = {}
    if states_prev is not None:
        in_specs.append(pl.BlockSpec(memory_space=pl.ANY))
        args.append(states_prev)
        aliases = {len(args) - 1: 1}
    return pl.pallas_call(
        _gla_ctx_kernel,
        grid=(BATCH // GLA_CTX_SEQS,),
        in_specs=in_specs,
        out_specs=[pl.BlockSpec((rows, GLA_W), lambda s: (s, 0)),
                   pl.BlockSpec((GLA_CTX_SEQS, None, 2, GLA_HEADS, GLA_DK, GLA_DV),
                                lambda s: (s, layer, 0, 0, 0, 0))],
        out_shape=[jax.ShapeDtypeStruct((T_ALL, GLA_W), BF16),
                   jax.ShapeDtypeStruct((BATCH, DEPTH, 2, GLA_HEADS, GLA_DK, GLA_DV), F32)],
        input_output_aliases=aliases,
        compiler_params=_cparams("parallel"),
        name="gla_ctx",
    )(*args)


def _gla_lat(pa, ps, wg, bg, ng, s0, y):
    rb0 = T_CTX // DEC_SEQ
    cb = PA_GLA // GLA_W
    const = lambda shape: pl.BlockSpec(shape, lambda s: (0,) * len(shape))
    return pl.pallas_call(
        _gla_lat_kernel,
        grid=(DEC_BATCH,),
        in_specs=[pl.BlockSpec((DEC_SEQ, GLA_W), lambda s, j=j: (rb0 + s, cb + j)) for j in range(4)] + [
            pl.BlockSpec((DEC_SEQ, LANES), lambda s: (rb0 + s, PS_LR // LANES)),
            const((2, 2 * LANES, 2 * GLA_W)), const((2, 1, GLA_W)), const((1, GLA_W)),
            pl.BlockSpec((None, 2, GLA_HEADS, GLA_DK, GLA_DV), lambda s: (s, 0, 0, 0, 0)),
            pl.BlockSpec(memory_space=pl.ANY)],
        out_specs=pl.BlockSpec((DEC_SEQ, GLA_W), lambda s: (rb0 + s, 0)),
        out_shape=jax.ShapeDtypeStruct((T_ALL, GLA_W), BF16),
        scratch_shapes=[pltpu.VMEM((DEC_SEQ, GLA_W), F32), pltpu.VMEM((DEC_SEQ, GLA_W), F32)],
        input_output_aliases={9: 0},
        compiler_params=_cparams("parallel"),
        name="gla_lat",
    )(pa, pa, pa, pa, ps, wg, bg, ng, s0, y)


LOG2E = 1.4426950408889634
Q_SCALE = HEAD_DIM ** -0.5 * LOG2E


def _attend_t(problems):
    scores = []
    for q, _, k_parts, _, masks in problems:
        parts = [_dot_t(k, q) for k in k_parts]
        scores.append([s if mask is None else jnp.where(mask, s, NEG_INF) for s, mask in zip(parts, masks)])
    exps = []
    for (q, sink, *_), parts in zip(problems, scores):
        sink2 = sink * LOG2E
        m = jnp.broadcast_to(sink2, (1, q.shape[0])).astype(F32)
        for s in parts:
            m = jnp.maximum(m, jnp.max(s, axis=0, keepdims=True))
        es = [jnp.exp2(s - m) for s in parts]
        den = jnp.exp2(sink2 - m)
        for e in es:
            den = den + jnp.sum(e, axis=0, keepdims=True)
        exps.append(([e.astype(BF16) for e in es], den))
    outs = []
    for (_, _, _, vt_parts, _), (es, den) in zip(problems, exps):
        o = None
        for e, vt in zip(es, vt_parts):
            pv = _mm(vt, e)
            o = pv if o is None else o + pv
        outs.append(o / den)
    return outs


def _lane_halves(block):
    low = lax.broadcasted_iota(jnp.int32, block.shape, 1) < HEAD_DIM
    swapped = pltpu.roll(block, HEAD_DIM, 1)
    zero = jnp.zeros_like(block)
    pick = lambda cond, a: jnp.where(cond, a, zero).astype(BF16)
    return [[pick(low, block), pick(~low, swapped)], [pick(low, swapped), pick(~low, block)]]


ATT_CTX_SEQS = 4


def _attn_ctx_kernel(sink_ref, q_ref, kv_ref, *rest):
    o_ref = rest[-1]
    problems = []
    for s in range(ATT_CTX_SEQS):
        rows = slice(s * SEQ, (s + 1) * SEQ)
        k_halves = _lane_halves(kv_ref[rows, :LANES])
        vt = kv_ref[rows, LANES:].T.astype(BF16)
        for pair in range(ATT_HEADS // 2):
            kv = 2 * pair // ATT_GROUP
            q_pair = (q_ref[rows, pair * LANES:(pair + 1) * LANES] * Q_SCALE).astype(BF16)
            problems += [(q_pair, sink_ref[2 * pair + j], [k_halves[kv][j]],
                          [vt[kv * HEAD_DIM:(kv + 1) * HEAD_DIM]], [None]) for j in range(2)]
    o_t = _attend_t(problems)
    for s in range(ATT_CTX_SEQS):
        outs = [jnp.concatenate(o_t[ATT_HEADS * s + 2 * pair:ATT_HEADS * s + 2 * pair + 2], axis=0).T
                for pair in range(ATT_HEADS // 2)]
        o_ref[s * SEQ:(s + 1) * SEQ, :] = jnp.concatenate(outs, axis=-1).astype(o_ref.dtype)


def _attn_ctx(ps, sink):
    return pl.pallas_call(
        _attn_ctx_kernel,
        grid=(BATCH // ATT_CTX_SEQS,),
        in_specs=[
            pl.BlockSpec(memory_space=pltpu.SMEM),
            pl.BlockSpec((ATT_CTX_SEQS * SEQ, ATT_W), lambda s: (s, 0)),
            pl.BlockSpec((ATT_CTX_SEQS * SEQ, 2 * LANES), lambda s: (s, PS_KV // (2 * LANES))),
        ],
        out_specs=pl.BlockSpec((ATT_CTX_SEQS * SEQ, ATT_W), lambda s: (s, 0)),
        out_shape=jax.ShapeDtypeStruct((T_ALL, ATT_W), BF16),
        compiler_params=_cparams("parallel"),
        name="attn_ctx",
    )(sink, ps, ps)


def _rope(x, cos, sin_signed):
    lane = lax.broadcasted_iota(jnp.int32, x.shape, 1)
    partner = jnp.where((lane & 31) < 16, pltpu.roll(x, LANES - 16, 1), pltpu.roll(x, 16, 1))
    return x * cos + partner * sin_signed


def _attn_lat_kernel(sink_ref, q_ref, kvp_ref, kvc_ref, kvn_ref, kctx_ref, vctx_ref, cos_ref, sin_ref, yin_ref,
                     o_ref):
    del yin_ref
    n = pl.program_id(1)
    n_blk = DEC_SEQ // BLOCK
    start = n * BLOCK

    def table(ref, blk):
        blk = jnp.clip(blk, 0, n_blk - 1)
        return ref[pl.ds(pl.multiple_of(blk * BLOCK, BLOCK), BLOCK), :]

    k_loc, v_loc = [], []
    for off, ref in ((-1, kvp_ref), (0, kvc_ref), (1, kvn_ref)):
        k_loc.append(_rope(ref[:, :LANES], table(cos_ref, n + off), table(sin_ref, n + off)))
        v_loc.append(ref[:, LANES:])
    kl_halves = _lane_halves(jnp.concatenate(k_loc, axis=0))
    kc_halves = _lane_halves(kctx_ref[...])
    vt_loc = jnp.concatenate(v_loc, axis=0).T.astype(BF16)
    vt_ctx = vctx_ref[...].T.astype(BF16)

    pairs_per_kv = ATT_GROUP // 2
    span = 3 * BLOCK
    shape_t = (span, pairs_per_kv * BLOCK)
    kpos = start - WINDOW + lax.broadcasted_iota(jnp.int32, shape_t, 0)
    qpos = start + (lax.broadcasted_iota(jnp.int32, shape_t, 1) & (BLOCK - 1))
    valid_t = (jnp.abs(qpos - kpos) <= WINDOW) & (kpos >= 0) & (kpos < DEC_SEQ)

    cos_q = table(cos_ref, n)
    sin_q = table(sin_ref, n)
    q_pairs = [(_rope(q_ref[:, p * LANES:(p + 1) * LANES], cos_q, sin_q) * Q_SCALE).astype(BF16)
               for p in range(ATT_HEADS // 2)]
    problems = []
    for kv in range(ATT_KV_HEADS):
        pairs = range(kv * pairs_per_kv, (kv + 1) * pairs_per_kv)
        q = jnp.concatenate([q_pairs[p] for p in pairs], axis=0)
        rows = slice(kv * HEAD_DIM, (kv + 1) * HEAD_DIM)
        for j in range(2):
            sink = jnp.concatenate([jnp.full((1, BLOCK), sink_ref[2 * p + j], F32) for p in pairs], axis=1)
            problems.append((q, sink, [kc_halves[kv][j], kl_halves[kv][j]], [vt_ctx[rows], vt_loc[rows]],
                             [None, valid_t]))
    o_t = _attend_t(problems)
    outs = []
    for p in range(ATT_HEADS // 2):
        kv, i = divmod(p, pairs_per_kv)
        cols = slice(i * BLOCK, (i + 1) * BLOCK)
        outs.append(jnp.concatenate([o_t[2 * kv + j][:, cols] for j in range(2)], axis=0).T)
    o_ref[...] = jnp.concatenate(outs, axis=-1).astype(o_ref.dtype)


def _attn_lat(ps, sink, k_ctx, v_ctx, cos_t, sin_t, y):
    n_blk = DEC_SEQ // BLOCK
    rb0 = T_CTX // BLOCK

    def kv_spec(off):
        return pl.BlockSpec(
            (BLOCK, 2 * LANES),
            lambda b, n: (rb0 + b * n_blk + jnp.clip(n + off, 0, n_blk - 1), PS_KV // (2 * LANES)))

    return pl.pallas_call(
        _attn_lat_kernel,
        grid=(DEC_BATCH, n_blk),
        in_specs=[
            pl.BlockSpec(memory_space=pltpu.SMEM),
            pl.BlockSpec((BLOCK, ATT_W), lambda b, n: (rb0 + b * n_blk + n, 0)),
            kv_spec(-1), kv_spec(0), kv_spec(1),
            pl.BlockSpec((None, PAST_LEN, LANES), lambda b, n: (b, 0, 0)),
            pl.BlockSpec((None, PAST_LEN, LANES), lambda b, n: (b, 0, 0)),
            pl.BlockSpec((DEC_SEQ, LANES), lambda b, n: (0, 0)),
            pl.BlockSpec((DEC_SEQ, LANES), lambda b, n: (0, 0)),
            pl.BlockSpec(memory_space=pl.ANY),
        ],
        out_specs=pl.BlockSpec((BLOCK, ATT_W), lambda b, n: (rb0 + b * n_blk + n, 0)),
        out_shape=jax.ShapeDtypeStruct((T_ALL, ATT_W), BF16),
        input_output_aliases={9: 0},
        compiler_params=_cparams("parallel", "parallel"),
        name="attn_lat",
    )(sink, ps, ps, ps, ps, k_ctx, v_ctx, cos_t, sin_t, y)


def _rope_tables():
    pos = np.arange(DEC_SEQ)
    n_freq = HEAD_DIM // 4
    inv = jnp.asarray(ROPE_THETA, F32) ** (-jnp.arange(n_freq, dtype=F32) / n_freq)
    row = jnp.asarray(pos // GRID_W, F32)
    colp = jnp.asarray(pos % GRID_W, F32)
    ang_r = row[:, None] * inv[None, :]
    ang_c = colp[:, None] * inv[None, :]
    cos = jnp.concatenate([jnp.cos(ang_r)] * 2 + [jnp.cos(ang_c)] * 2, axis=-1)
    sin = jnp.concatenate([-jnp.sin(ang_r), jnp.sin(ang_r), -jnp.sin(ang_c), jnp.sin(ang_c)], axis=-1)
    return jnp.tile(cos, (1, 2)), jnp.tile(sin, (1, 2))


def _route(logits):
    lane_i = lax.broadcasted_iota(jnp.int32, logits.shape, 1)
    lane = lane_i.astype(F32)
    big = jnp.float32(1 << 20)
    is_g = lane_i < N_EXPERT_GROUPS
    lg = jnp.where(is_g, logits, -jnp.inf)
    m_g = jnp.max(lg, axis=-1, keepdims=True)
    grp = jnp.min(jnp.where(lg == m_g, lane, big), axis=-1, keepdims=True)
    z_g = jnp.sum(jnp.where(is_g, jnp.exp(lg - m_g), 0.0), axis=-1, keepdims=True)
    p_grp = 1.0 / z_g

    e_idx = lane_i - ROUTE_E0
    e_grp = (e_idx >> 2).astype(F32)
    sel = (e_idx >= 0) & (e_idx < N_EXPERTS) & (e_grp == grp)
    le = jnp.where(sel, logits, -jnp.inf)
    m_e = jnp.max(le, axis=-1, keepdims=True)
    ex = jnp.where(sel, jnp.exp(le - m_e), 0.0)
    pe = ex / jnp.sum(ex, axis=-1, keepdims=True)
    pe = jnp.where(sel, pe, -1.0)
    v1 = jnp.max(pe, axis=-1, keepdims=True)
    i1 = jnp.min(jnp.where(pe == v1, lane, big), axis=-1, keepdims=True)
    pe2 = jnp.where(lane == i1, -1.0, pe)
    v2 = jnp.max(pe2, axis=-1, keepdims=True)
    i2 = jnp.min(jnp.where(pe2 == v2, lane, big), axis=-1, keepdims=True)
    tot = v1 + v2
    return (jnp.where(lane == i1, p_grp * (v1 / tot), 0.0)
            + jnp.where(lane == i2, p_grp * (v2 / tot), 0.0)
            + jnp.where(lane_i == LANES - 1, grp, 0.0))


HALO = 16


def _merge_kernel(gate_ref, conv_ref, cprev_ref, cnext_ref, yb_ref, yc_ref, *rest, x_split):
    x_refs, rest = rest[:2 if x_split else 1], rest[2 if x_split else 1:]
    (mod_ref, cw_ref, wb_ref, wo_ref, g2_ref, wr_ref, br_ref,
     x1_ref, h2_ref, rg_ref, wbb_ref, wob_ref) = rest
    tm = TM_MERGE
    i = pl.program_id(0)

    def x_rows(rows):
        if not x_split:
            return x_refs[0][rows, :]
        return jnp.where(i < x_split, x_refs[0][rows, :], x_refs[1][rows, :])

    @pl.when(i == 0)
    def _():
        wbb_ref[...] = wb_ref[...].astype(BF16)
        wob_ref[...] = wo_ref[...].astype(BF16)

    def gated(ref):
        return ref[:, CONV_W:2 * CONV_W].astype(F32) * ref[:, 2 * CONV_W:3 * CONV_W].astype(F32)

    a_b = conv_ref[:, 0:CONV_W].astype(F32)
    u = gated(conv_ref)
    u_before = gated(cprev_ref)[HALO - 1:HALO]
    u_after = gated(cnext_ref)[0:1]
    r = lax.broadcasted_iota(jnp.int32, (tm, 1), 0)
    g_row = i * tm + r
    seq_mask = jnp.where(g_row < T_CTX, SEQ - 1, DEC_SEQ - 1)
    first = (g_row & seq_mask) == 0
    last = ((g_row + 1) & seq_mask) == 0
    u_prev = jnp.where(r == 0, u_before, pltpu.roll(u, 1, 0))
    u_next = jnp.where(r == tm - 1, u_after, pltpu.roll(u, tm - 1, 0))
    u_prev = jnp.where(first, 0.0, u_prev)
    u_next = jnp.where(last, 0.0, u_next)
    y_a = (a_b * (u_prev * cw_ref[0:1, :] + u * cw_ref[1:2, :] + u_next * cw_ref[2:3, :])).astype(BF16)

    blocks = [slice(r0, r0 + MERGE_ROWS) for r0 in range(0, tm, MERGE_ROWS)]
    ys = [(y_a[rows], yb_ref[rows, :], yc_ref[rows, :]) for rows in blocks]
    branches = [[_mm(y, wbb_ref[j]) for j, y in enumerate(y3)] for y3 in ys]
    zs = []
    for rows, br3 in zip(blocks, branches):
        z = sum(gate_ref[rows, j * D_MODEL:(j + 1) * D_MODEL].astype(F32) * br3[j] for j in range(3))
        zs.append(z.astype(BF16))
    outs = [_mm(z, wob_ref[...]) for z in zs]
    h2s = []
    for rows, o in zip(blocks, outs):
        x1 = x_rows(rows) + mod_ref[2:3, :] * o
        x1_ref[rows, :] = x1
        h2 = _mod_norm(x1, g2_ref[...], mod_ref[...], 3)
        h2_ref[rows, :] = h2.astype(BF16)
        h2s.append(h2)
    logits = [_split_dot(h2, wr_ref[...]) + br_ref[...] for h2 in h2s]
    for rows, lg in zip(blocks, logits):
        rg_ref[rows, :] = _route(lg)


def _merge(pg, pa, y_b, y_c, x, mod_l, conv_w, layer, wb, wo, g2, wr, br):
    tm = TM_MERGE
    n_tiles = T_ALL // tm
    hb = tm // HALO
    const = lambda shape: pl.BlockSpec(shape, lambda i: (0,) * len(shape))
    if isinstance(x, tuple):
        x_specs, x_split = _split_x_specs(tm)
    else:
        x, x_specs, x_split = (x,), [pl.BlockSpec((tm, D_MODEL), lambda i: (i, 0))], 0
    return pl.pallas_call(
        functools.partial(_merge_kernel, x_split=x_split),
        grid=(n_tiles,),
        in_specs=[
            pl.BlockSpec((tm, PG_W), lambda i: (i, 0)),
            pl.BlockSpec((tm, 3 * CONV_W), lambda i: (i, 0)),
            pl.BlockSpec((HALO, 3 * CONV_W), lambda i: (jnp.maximum(i * hb - 1, 0), 0)),
            pl.BlockSpec((HALO, 3 * CONV_W), lambda i: (jnp.minimum((i + 1) * hb, n_tiles * hb - 1), 0)),
            pl.BlockSpec((tm, GLA_W), lambda i: (i, 0)),
            pl.BlockSpec((tm, ATT_W), lambda i: (i, 0)),
            *x_specs,
            pl.BlockSpec((None, 6, D_MODEL), lambda i: (_cond_row(i, tm), 0, 0)),
            const((CONV_K, CONV_W)),
            pl.BlockSpec((None, 3, 512, D_MODEL), lambda i: (layer, 0, 0, 0)),
            pl.BlockSpec((None, D_MODEL, D_MODEL), lambda i: (layer, 0, 0)),
            const((1, D_MODEL)),
            const((2 * D_MODEL, 2 * LANES)),
            const((1, LANES)),
        ],
        out_specs=[
            pl.BlockSpec((tm, D_MODEL), lambda i: (i, 0)),
            pl.BlockSpec((tm, D_MODEL), lambda i: (i, 0)),
            pl.BlockSpec((tm, LANES), lambda i: (i, 0)),
        ],
        out_shape=[
            jax.ShapeDtypeStruct((T_ALL, D_MODEL), F32),
            jax.ShapeDtypeStruct((T_ALL, D_MODEL), BF16),
            jax.ShapeDtypeStruct((T_ALL, LANES), F32),
        ],
        scratch_shapes=[pltpu.VMEM((3, 512, D_MODEL), BF16), pltpu.VMEM((D_MODEL, D_MODEL), BF16)],
        compiler_params=_cparams("arbitrary"),
        name="merge",
    )(pg, pa, pa, pa, y_b, y_c, *x, mod_l, conv_w, wb, wo, g2, wr, br)


SUB = 128
MOE_ROWS = 256


def _dispatch_tables(rg):
    n_tiles = T_ALL // TM
    grp = rg[:, LANES - 1].astype(jnp.int32).reshape(n_tiles, TM)
    hot = grp[..., None] == jnp.arange(N_EXPERT_GROUPS, dtype=jnp.int32)
    onehot = hot.astype(jnp.int32)
    cnt = onehot.sum(axis=1)
    start = jnp.cumsum(cnt, axis=1) - cnt
    first = start // SUB
    nblk = jnp.where(cnt > 0, (start + cnt + SUB - 1) // SUB - first, 0)
    before = jnp.tril(jnp.ones((TM, TM), BF16), -1)
    rank = jnp.einsum("ts,nsg->ntg", before, hot.astype(BF16), preferred_element_type=F32).astype(jnp.int32)
    slot = ((start[:, None, :] + rank) * onehot).sum(axis=-1)
    return (first.reshape(-1), nblk.reshape(-1), slot.reshape(n_tiles, 1, TM), slot.reshape(T_ALL, 1))


def _moe_kernel(bstart_ref, nblk_ref, h_ref, rg_ref, srow_ref, scol_ref, x1_ref, mod_ref, wg_ref, wu_ref,
                wd_ref, g_ref, *rest, final, tile0):
    if final:
        y_ref, hs_ref, gs_ref, os_ref = rest
    else:
        modn_ref, x2_ref, hn_ref, hs_ref, gs_ref, os_ref = rest
    i = pl.program_id(0)
    grp = pl.program_id(1)

    @pl.when(grp == 0)
    def _():
        slot_of_token = srow_ref[...]
        payload = jnp.concatenate([h_ref[...], *_split3(rg_ref[...])], axis=-1)
        for r0 in range(0, TM, MOE_ROWS):
            rows = slice(r0, r0 + MOE_ROWS)
            slot = r0 + lax.broadcasted_iota(jnp.int32, (MOE_ROWS, TM), 0)
            pm = jnp.where(slot == slot_of_token, 1.0, 0.0).astype(BF16)
            moved = _mm(pm, payload)
            hs_ref[rows, :] = moved[:, :D_MODEL].astype(BF16)
            gs_ref[rows, :] = sum(moved[:, D_MODEL + j * LANES:D_MODEL + (j + 1) * LANES] for j in range(3))
        os_ref[...] = jnp.zeros_like(os_ref)

    k = (tile0 + i) * N_EXPERT_GROUPS + grp
    first = bstart_ref[k]
    n_blocks = nblk_ref[k]

    def experts(block0, n_sub):
        n_rows = n_sub * SUB
        rows = pl.ds(pl.multiple_of(block0 * SUB, SUB), n_rows)
        lane = lax.broadcasted_iota(jnp.int32, (n_rows, LANES), 1)
        x = hs_ref[rows, :]
        gates = gs_ref[rows, :]
        hidden = []
        for e in range(EXPERTS_PER_GROUP):
            ge = _mm(x, wg_ref[e])
            ue = _mm(x, wu_ref[e])
            w = jnp.sum(jnp.where(lane == ROUTE_E0 + grp * EXPERTS_PER_GROUP + e, gates, 0.0),
                        axis=-1, keepdims=True)
            hidden.append(((ge * jax.nn.sigmoid(ge)) * ue * w).astype(BF16))
        acc = _mm(jnp.concatenate(hidden, axis=-1), wd_ref[...].reshape(EXPERTS_PER_GROUP * D_EXPERT, D_MODEL))
        os_ref[rows, :] = (os_ref[rows, :].astype(F32) + acc).astype(BF16)

    odd = n_blocks % 2 == 1
    n_pairs = jnp.where(odd & (n_blocks >= 3), (n_blocks - 3) // 2, n_blocks // 2)

    def pair(p, carry):
        experts(first + 2 * p, 2)
        return carry

    lax.fori_loop(0, n_pairs, pair, 0)

    @pl.when(odd & (n_blocks >= 3))
    def _():
        experts(first + n_blocks - 3, 3)

    @pl.when(n_blocks == 1)
    def _():
        experts(first, 1)

    @pl.when(grp == N_EXPERT_GROUPS - 1)
    def _():
        sorted_out = os_ref[...]
        lane_slot = lax.broadcasted_iota(jnp.int32, (MOE_ROWS, TM), 1)
        for r0 in range(0, TM, MOE_ROWS):
            rows = slice(r0, r0 + MOE_ROWS)
            pt = jnp.where(lane_slot == scol_ref[rows, :], 1.0, 0.0).astype(BF16)
            x2 = x1_ref[rows, :] + mod_ref[5:6, :] * _mm(pt, sorted_out)
            if final:
                y_ref[rows, :] = _rms(x2) * g_ref[...]
            else:
                x2_ref[rows, :] = x2
                hn_ref[rows, :] = _mod_norm(x2, g_ref[...], modn_ref[...], 0).astype(BF16)


def _moe(tables, h2, rg, x1, mod_l, layer, w_eg, w_eu, w_ed, g, mod_next, tile0, n_tiles):
    bstart, nblk, slot_row, slot_col = tables
    final = mod_next is None
    row = lambda i, e, *_: (tile0 + i, 0)
    mod_spec = lambda: pl.BlockSpec((None, 6, D_MODEL), lambda i, e, *_: (_cond_row(tile0 + i, TM), 0, 0))
    group = lambda i, e, *_: (layer, e, 0, 0)
    in_specs = [
        pl.BlockSpec((TM, D_MODEL), row),
        pl.BlockSpec((TM, LANES), row),
        pl.BlockSpec((None, 1, TM), lambda i, e, *_: (tile0 + i, 0, 0)),
        pl.BlockSpec((TM, 1), row),
        pl.BlockSpec((TM, D_MODEL), row),
        mod_spec(),
        pl.BlockSpec((None, EXPERTS_PER_GROUP, D_MODEL, D_EXPERT), group),
        pl.BlockSpec((None, EXPERTS_PER_GROUP, D_MODEL, D_EXPERT), group),
        pl.BlockSpec((None, EXPERTS_PER_GROUP, D_EXPERT, D_MODEL), group),
        pl.BlockSpec((1, D_MODEL), lambda i, e, *_: (0, 0)),
    ]
    args = [h2, rg, slot_row, slot_col, x1, mod_l, w_eg, w_eu, w_ed, g]
    if final:
        out_specs = pl.BlockSpec((TM, D_MODEL), lambda i, e, *_: (i, 0))
        out_shape = jax.ShapeDtypeStruct((n_tiles * TM, D_MODEL), F32)
    else:
        in_specs.append(mod_spec())
        args.append(mod_next)
        out_specs = [pl.BlockSpec((TM, D_MODEL), row)] * 2
        out_shape = [jax.ShapeDtypeStruct((T_ALL, D_MODEL), F32), jax.ShapeDtypeStruct((T_ALL, D_MODEL), BF16)]
    return pl.pallas_call(
        functools.partial(_moe_kernel, final=final, tile0=tile0),
        grid_spec=pltpu.PrefetchScalarGridSpec(
            num_scalar_prefetch=2,
            grid=(n_tiles, N_EXPERT_GROUPS),
            in_specs=in_specs,
            out_specs=out_specs,
            scratch_shapes=[pltpu.VMEM((TM, D_MODEL), BF16), pltpu.VMEM((TM, LANES), F32),
                            pltpu.VMEM((TM, D_MODEL), BF16)],
        ),
        out_shape=out_shape,
        compiler_params=_cparams("parallel", "arbitrary"),
        name="moe_final" if final else "moe",
    )(bstart, nblk, *args)


def _cache_kernel(*refs):
    ps_refs, (k_ref, v_ref) = refs[:DEPTH], refs[DEPTH:]
    n_seq = TM // SEQ
    for l, ps_ref in enumerate(ps_refs):
        k_ref[:, l] = ps_ref[:, :LANES].reshape(n_seq, SEQ, LANES)
        v_ref[:, l] = ps_ref[:, LANES:].reshape(n_seq, SEQ, LANES)


def _cache(ps_layers):
    n_seq = TM // SEQ
    out = pl.BlockSpec((n_seq, DEPTH, SEQ, LANES), lambda i: (i, 0, 0, 0))
    return pl.pallas_call(
        _cache_kernel,
        grid=(T_CTX // TM,),
        in_specs=[pl.BlockSpec((TM, 2 * LANES), lambda i: (i, PS_KV // (2 * LANES)))] * DEPTH,
        out_specs=[out, out],
        out_shape=[jax.ShapeDtypeStruct((BATCH, DEPTH, SEQ, LANES), F32)] * 2,
        compiler_params=_cparams("parallel"),
        name="cache",
    )(*ps_layers)


def _prep_gla_gate(w_gate):
    pads = [((0, 0), (d * GLA_RANK, LANES - (d + 1) * GLA_RANK), (0, 0)) for d in range(2)]
    return _pack_split(jnp.stack([jnp.pad(w_gate[:, d], pads[d]) for d in range(2)], axis=1))


def _prep_router(w_rg, b_rg, w_re, b_re):
    unused = LANES - N_EXPERT_GROUPS - N_EXPERTS
    w = jnp.concatenate([w_rg, w_re, jnp.zeros((DEPTH, D_MODEL, unused), F32)], axis=-1)
    b = jnp.concatenate([b_rg, b_re, jnp.zeros((DEPTH, unused), F32)], axis=-1)
    return _pack_split(w), b.reshape(DEPTH, 1, LANES)


def kernel(x_prompt, x_sample, state_gla, cache_k, cache_v, c, c_ctx, w_ada, b_ada, norm1_g, norm2_g, w_in,
           conv_w, gla_w_gate, gla_b_gate, gla_norm_g, attn_sink, w_branch, w_out, w_route_group,
           b_route_group, w_route_expert, b_route_expert, w_exp_gate, w_exp_up, w_exp_down, final_norm_g):
    cond = jnp.zeros((N_COND, D_MODEL), F32).at[0].set(c_ctx).at[1:1 + DEC_BATCH].set(c)
    mod = _modulation(cond, w_ada, b_ada).reshape(DEPTH, N_COND, 6, D_MODEL)
    cos_t, sin_t = _rope_tables()
    row = lambda v: v.reshape(1, -1)

    x = (x_prompt.reshape(T_CTX, D_MODEL), x_sample.reshape(T_LAT, D_MODEL))
    h = _prenorm(*x, mod[0], row(norm1_g[0]))
    wg_all = _prep_gla_gate(gla_w_gate)
    wr_all, br_all = _prep_router(w_route_group, b_route_group, w_route_expert, b_route_expert)
    states = None
    ps_layers = []
    y_prompt = y_sample = None
    w_t = jnp.swapaxes(w_in, 1, 2)
    proj_a = functools.partial(_proj, pieces=[(0, PA_W // 2)], n_tiles=2, out_dtype=BF16, name="proj_a",
                               act="silu_tail")
    proj_s = functools.partial(_proj, pieces=[(W_IN_TQ, W_IN_GATE - W_IN_TQ), (W_IN_LR, LANES)], n_tiles=1,
                               out_dtype=F32, name="proj_s")
    proj_g = functools.partial(_proj, pieces=[(W_IN_GATE, PG_W // 2)], n_tiles=2, out_dtype=BF16,
                               name="proj_g", act="sigmoid")
    experts = None
    for l in range(DEPTH):
        if l == 0:
            pa, w_eg = proj_a(h, w_t, l, cast=(w_exp_gate, 2))
            ps, w_ed = proj_s(h, w_t, l, cast=(w_exp_down, 4))
            pg, w_eu = proj_g(h, w_t, l, cast=(w_exp_up, 2))
            experts = (w_eg, w_eu, w_ed)
        else:
            pa, ps, pg = proj_a(h, w_t, l), proj_s(h, w_t, l), proj_g(h, w_t, l)
        ps_layers.append(ps)

        wg = wg_all[l]
        bg = gla_b_gate[l].reshape(2, 1, GLA_W)
        ng = row(gla_norm_g[l])
        y_b, states = _gla_ctx(pa, ps, wg, bg, ng, states, l)
        y_b = _gla_lat(pa, ps, wg, bg, ng, state_gla[:, l], y_b)
        y_c = _attn_ctx(ps, attn_sink[l])
        y_c = _attn_lat(ps, attn_sink[l], cache_k[:, l].reshape(DEC_BATCH, PAST_LEN, LANES),
                        cache_v[:, l].reshape(DEC_BATCH, PAST_LEN, LANES), cos_t, sin_t, y_c)

        x1, h2, rg = _merge(pg, pa, y_b, y_c, x, mod[l], conv_w[l], l, w_branch, w_out, row(norm2_g[l]),
                            wr_all[l], br_all[l])
        moe_in = (_dispatch_tables(rg), h2, rg, x1, mod[l], l, *experts)
        if l + 1 < DEPTH:
            x, h = _moe(*moe_in, row(norm1_g[l + 1]), mod[l + 1], 0, T_ALL // TM)
        else:
            gf = row(final_norm_g)
            y_prompt = _moe(*moe_in, gf, None, 0, T_CTX // TM)
            y_sample = _moe(*moe_in, gf, None, T_CTX // TM, T_LAT // TM)

    new_k, new_v = _cache(ps_layers)
    kv_shape = (BATCH, DEPTH, SEQ, ATT_KV_HEADS, HEAD_DIM)
    return (y_prompt.reshape(BATCH, SEQ, D_MODEL), y_sample.reshape(DEC_BATCH, DEC_SEQ, D_MODEL), states,
            new_k.reshape(kv_shape), new_v.reshape(kv_shape))
```

```python
import functools

import jax
import jax.numpy as jnp
import numpy as np
from jax import lax
from jax.experimental import pallas as pl
from jax.experimental.pallas import tpu as pltpu

F32 = jnp.float32
BF16 = jnp.bfloat16

D_MODEL = 1024
BATCH = 32
SEQ = 256
DEPTH = 2
DEC_BATCH = 2
DEC_SEQ = 2048
PAST_LEN = 512
GRID_W = 64
EPS = 1e-6
CONV_W = 512
CONV_K = 3
GLA_HEADS = 4
GLA_DK = 128
GLA_DV = 128
GLA_W = GLA_HEADS * GLA_DV
GLA_RANK = 16
GLA_TAU = 16.0
GLA_CHUNK = 64
ATT_HEADS = 8
ATT_KV_HEADS = 2
ATT_GROUP = ATT_HEADS // ATT_KV_HEADS
HEAD_DIM = 64
ATT_W = ATT_HEADS * HEAD_DIM
WINDOW = 128
BLOCK = 128
ROPE_THETA = 10000.0
N_EXPERT_GROUPS = 4
EXPERTS_PER_GROUP = 4
N_EXPERTS = 16
D_EXPERT = 256
NEG_INF = -1e30

T_CTX = BATCH * SEQ
T_LAT = DEC_BATCH * DEC_SEQ
T_ALL = T_CTX + T_LAT
N_COND = 8
LANES = 128

W_IN_LR = 3584
W_IN_TQ = 3616
W_IN_GATE = 4384
N_IN = 7456
PA_W = 3584
PA_GLA = 1536
PS_W = 896
PS_KV = 512
PS_LR = 768
PG_W = 3 * D_MODEL
ROUTE_E0 = N_EXPERT_GROUPS

TM = 1024
TM_MERGE = 512
MERGE_ROWS = 256
VMEM_LIMIT = 56 * 1024 * 1024


def _cparams(*sem):
    return pltpu.CompilerParams(dimension_semantics=sem, vmem_limit_bytes=VMEM_LIMIT)


def _cond_row(i, tm):
    n_ctx = T_CTX // tm
    per = DEC_SEQ // tm
    return jnp.where(i < n_ctx, 0, 1 + (i - n_ctx) // per)


def _mm(a, b):
    return jnp.dot(a, b, preferred_element_type=F32)


def _dot_t(a, b):
    return lax.dot_general(a, b, (((1,), (1,)), ((), ())), preferred_element_type=F32)


def _dot_ta(a, b):
    return lax.dot_general(a, b, (((0,), (0,)), ((), ())), preferred_element_type=F32)


def _hi_lo(w):
    hi = w.astype(BF16)
    return hi, (w - hi.astype(F32)).astype(BF16)


def _pack_split(w):
    hi, lo = _hi_lo(w)
    return jnp.concatenate([jnp.concatenate([hi, lo], axis=-1),
                            jnp.concatenate([hi, jnp.zeros_like(lo)], axis=-1)], axis=-2)


def _split_dot(a, w_pack):
    a_hi, a_lo = _hi_lo(a)
    both = _mm(jnp.concatenate([a_hi, a_lo], axis=-1), w_pack)
    n = both.shape[-1] // 2
    return both[:, :n] + both[:, n:]


def _rms(x):
    return x * lax.rsqrt(jnp.mean(x * x, axis=-1, keepdims=True) + EPS)


def _mod_norm(x, g, mod, shift_row):
    return _rms(x) * g * (1.0 + mod[shift_row + 1:shift_row + 2, :]) + mod[shift_row:shift_row + 1, :]


def _mod_kernel(c_ref, w_ref, b_ref, o_ref):
    c = c_ref[...]
    s = (c * jax.nn.sigmoid(c)).astype(BF16)
    o_ref[...] = _mm(s, w_ref[...].astype(BF16)) + b_ref[...]


def _modulation(cond, w_ada, b_ada):
    tn = 1536
    return pl.pallas_call(
        _mod_kernel,
        grid=(DEPTH, 6 * D_MODEL // tn),
        in_specs=[
            pl.BlockSpec((N_COND, D_MODEL), lambda l, j: (0, 0)),
            pl.BlockSpec((None, D_MODEL, tn), lambda l, j: (l, 0, j)),
            pl.BlockSpec((None, 1, tn), lambda l, j: (l, 0, j)),
        ],
        out_specs=pl.BlockSpec((None, N_COND, tn), lambda l, j: (l, 0, j)),
        out_shape=jax.ShapeDtypeStruct((DEPTH, N_COND, 6 * D_MODEL), F32),
        compiler_params=_cparams("parallel", "parallel"),
        name="modulation",
    )(cond, w_ada, b_ada.reshape(DEPTH, 1, 6 * D_MODEL))


def _split_x_specs(tm):
    n_ctx = T_CTX // tm
    return [pl.BlockSpec((tm, D_MODEL), lambda i, *_: (jnp.minimum(i, n_ctx - 1), 0)),
            pl.BlockSpec((tm, D_MODEL), lambda i, *_: (jnp.maximum(i - n_ctx, 0), 0))], n_ctx


def _prenorm_kernel(xp_ref, xs_ref, mod_ref, g_ref, h_ref):
    x = jnp.where(pl.program_id(0) < T_CTX // TM, xp_ref[...], xs_ref[...])
    h_ref[...] = _mod_norm(x, g_ref[...], mod_ref[...], 0).astype(BF16)


def _prenorm(xp, xs, mod_l, g):
    x_specs, _ = _split_x_specs(TM)
    return pl.pallas_call(
        _prenorm_kernel,
        grid=(T_ALL // TM,),
        in_specs=x_specs + [
            pl.BlockSpec((None, 6, D_MODEL), lambda i: (_cond_row(i, TM), 0, 0)),
            pl.BlockSpec((1, D_MODEL), lambda i: (0, 0)),
        ],
        out_specs=pl.BlockSpec((TM, D_MODEL), lambda i: (i, 0)),
        out_shape=jax.ShapeDtypeStruct((T_ALL, D_MODEL), BF16),
        compiler_params=_cparams("parallel"),
        name="prenorm",
    )(xp, xs, mod_l, g)


def _proj_kernel(h_ref, *refs, act, n_w):
    w_refs, refs = refs[:n_w], refs[n_w:]
    if len(refs) == 4:
        cast_in_ref, o_ref, cast_out_ref, wb_ref = refs
        cast_out_ref[...] = cast_in_ref[...].astype(BF16)
    else:
        o_ref, wb_ref = refs

    @pl.when(pl.program_id(1) == 0)
    def _():
        col = 0
        for w_ref in w_refs:
            n = w_ref.shape[0]
            wb_ref[:, col:col + n] = w_ref[...].T.astype(BF16)
            col += n

    def sigmoid(t):
        return 0.5 * jnp.tanh(0.5 * t) + 0.5

    y = _mm(h_ref[...], wb_ref[...])
    if act == "sigmoid":
        y = sigmoid(y)
    if act == "silu_tail":
        last = pl.program_id(0) == pl.num_programs(0) - 1
        tail = y[:, -GLA_W:]
        o_ref[:, :-GLA_W] = y[:, :-GLA_W].astype(o_ref.dtype)
        o_ref[:, -GLA_W:] = jnp.where(last, tail * sigmoid(tail), tail).astype(o_ref.dtype)
    else:
        o_ref[...] = y.astype(o_ref.dtype)


def _proj(h, w_t, layer, pieces, n_tiles, out_dtype, name, act=None, cast=None):
    tn = sum(n for _, n in pieces)
    n_rows = T_ALL // TM

    def w_spec(c0, n):
        return pl.BlockSpec((pl.Squeezed(), pl.Element(n), pl.Element(D_MODEL)),
                            lambda j, i: (layer, pl.multiple_of(c0 + j * tn, 8), 0))

    in_specs = [pl.BlockSpec((TM, D_MODEL), lambda j, i: (i, 0))] + [w_spec(c0, n) for c0, n in pieces]
    out_specs = [pl.BlockSpec((TM, tn), lambda j, i: (i, j))]
    out_shape = [jax.ShapeDtypeStruct((T_ALL, n_tiles * tn), out_dtype)]
    args = [h] + [w_t] * len(pieces)
    if cast is not None:
        src, per_block = cast
        per_layer = N_EXPERTS // per_block
        n_blocks = DEPTH * per_layer
        assert n_blocks <= n_tiles * n_rows

        def block(j, i):
            b = jnp.minimum(j * n_rows + i, n_blocks - 1)
            return (b // per_layer, b % per_layer, 0, 0)

        spec = pl.BlockSpec((None, per_block) + src.shape[2:], block)
        in_specs.append(spec)
        out_specs.append(spec)
        out_shape.append(jax.ShapeDtypeStruct(src.shape, BF16))
        args.append(src)
    out = pl.pallas_call(
        functools.partial(_proj_kernel, act=act, n_w=len(pieces)),
        grid=(n_tiles, n_rows),
        in_specs=in_specs,
        out_specs=out_specs,
        out_shape=out_shape,
        scratch_shapes=[pltpu.VMEM((D_MODEL, tn), BF16)],
        compiler_params=_cparams("arbitrary", "arbitrary"),
        name=name,
    )(*args)
    return out[0] if cast is None else out


GLA_GROUP = 256
GLA_CTX_SEQS = 4


def _log_sigmoid(z):
    return jnp.minimum(z, 0.0) - jnp.log(1.0 + jnp.exp(-jnp.abs(z)))


def _split3(x):
    hi = x.astype(BF16)
    r1 = x - hi.astype(F32)
    mid = r1.astype(BF16)
    lo = (r1 - mid.astype(F32)).astype(BF16)
    return hi, mid, lo


def _gla_keep(d):
    ri = lax.broadcasted_iota(jnp.int32, (GLA_GROUP, GLA_GROUP), 0)
    ci = lax.broadcasted_iota(jnp.int32, (GLA_GROUP, GLA_GROUP), 1)
    if d == 0:
        return (ci <= ri) & (ci >= (ri & ~(GLA_CHUNK - 1)))
    return (ci >= ri) & (ci <= (ri | (GLA_CHUNK - 1)))


def _gla_groups(items, keeps):
    C = GLA_CHUNK
    nc = GLA_GROUP // C
    row_chunk = lax.broadcasted_iota(jnp.int32, (GLA_GROUP, GLA_DK), 0) // C
    in_chunk = [row_chunk == c for c in range(nc)]
    tris = [jnp.where(keep, 1.0, 0.0).astype(BF16) for keep in keeps]

    las = []
    for _, lr, wg_pack, bg, _, _ in items:
        lr = jnp.where(lax.broadcasted_iota(jnp.int32, lr.shape, 1) < 2 * GLA_RANK, lr, 0.0)
        las.append(_log_sigmoid(_split_dot(lr, wg_pack) + bg) / GLA_TAU)
    cums = [_mm(tris[item[5]], jnp.concatenate(_split3(la), axis=-1)) for item, la in zip(items, las)]

    ops = []
    for (load, _, _, _, states, d), parts in zip(items, cums):
        width = len(states) * GLA_DK
        b_all = parts[:, :width] + parts[:, width:2 * width] + parts[:, 2 * width:]
        edge = C - 1 if d == 0 else 0
        for h in range(len(states)):
            b = b_all[:, h * GLA_DK:(h + 1) * GLA_DK]
            b_last = [b[c * C + edge:c * C + edge + 1, :] for c in range(nc)]
            bl = jnp.concatenate([jnp.broadcast_to(t, (C, GLA_DK)) for t in b_last], axis=0)
            k = load(1, h)
            q_in = load(0, h) * (GLA_DK ** -0.5) * jnp.exp(b)
            k_in = (k * jnp.exp(-b)).astype(BF16)
            k_end = k * jnp.exp(bl - b)
            k_spread = jnp.concatenate([jnp.where(in_chunk[c], k_end, 0.0) for c in range(nc)], axis=-1)
            q_spread = jnp.concatenate([jnp.where(in_chunk[c], q_in, 0.0) for c in range(nc)], axis=-1)
            ops.append((q_in.astype(BF16), k_in, load(2, h).astype(BF16), k_spread.astype(BF16),
                        q_spread.astype(BF16), [jnp.exp(t) for t in b_last], d))
    att_t = [jnp.where(keeps[1 - op[6]], _dot_t(op[1], op[0]), 0.0).astype(BF16) for op in ops]
    ds_t = [_dot_ta(op[2], op[3]) for op in ops]
    o_t = [_dot_ta(op[2], a) for op, a in zip(ops, att_t)]
    all_states = [st for item in items for st in item[4]]
    new_states, stacked = [], []
    for st, ds, op in zip(all_states, ds_t, ops):
        order = range(nc) if op[6] == 0 else range(nc - 1, -1, -1)
        starts = [None] * nc
        for c in order:
            starts[c] = st
            st = st * op[5][c] + ds[:, c * GLA_DK:(c + 1) * GLA_DK]
        new_states.append(st)
        stacked.append(jnp.concatenate(starts, axis=-1).astype(BF16))
    outs = [o + _dot_t(s, op[4]) for o, s, op in zip(o_t, stacked, ops)]
    results, pos = [], 0
    for item in items:
        n = len(item[4])
        results.append((outs[pos:pos + n], new_states[pos:pos + n]))
        pos += n
    return results


def _gla_loader(qkv_refs, rows):
    def load(i, h):
        return qkv_refs[i][rows, h * GLA_DK:(h + 1) * GLA_DK].astype(F32)
    return load


def _gla_finish(o, r_act, ng):
    o = o * lax.rsqrt(jnp.mean(o * o, axis=-1, keepdims=True) + EPS)
    return o * ng * r_act


def _gla_ctx_kernel(q_ref, k_ref, v_ref, r_ref, lr_ref, wg_ref, bg_ref, ng_ref, *rest):
    y_ref, s_ref = rest[-2:]
    zero = jnp.zeros((GLA_DV, GLA_DK), F32)
    items = []
    for s in range(GLA_CTX_SEQS):
        rows = slice(s * SEQ, (s + 1) * SEQ)
        load = _gla_loader((q_ref, k_ref, v_ref), rows)
        items += [(load, lr_ref[rows, :], wg_ref[d], bg_ref[d], [zero] * GLA_HEADS, d) for d in range(2)]
    results = _gla_groups(items, [_gla_keep(0), _gla_keep(1)])
    for s in range(GLA_CTX_SEQS):
        rows = slice(s * SEQ, (s + 1) * SEQ)
        (o_f, st_f), (o_b, st_b) = results[2 * s], results[2 * s + 1]
        for h in range(GLA_HEADS):
            s_ref[s, 0, h] = st_f[h].T
            s_ref[s, 1, h] = st_b[h].T
            cs = slice(h * GLA_DV, (h + 1) * GLA_DV)
            y = _gla_finish((o_f[h] + o_b[h]).T, r_ref[rows, cs].astype(F32), ng_ref[:, cs])
            y_ref[rows, cs] = y.astype(y_ref.dtype)


def _gla_lat_kernel(q_ref, k_ref, v_ref, r_ref, lr_ref, wg_ref, bg_ref, ng_ref, s0_ref, yin_ref,
                    y_ref, of_ref, ob_ref):
    del yin_ref
    n_groups = DEC_SEQ // GLA_GROUP
    keeps = [_gla_keep(0), _gla_keep(1)]

    def body(g, carry):
        st_f, st_b = carry
        rf = pl.ds(pl.multiple_of(g * GLA_GROUP, GLA_GROUP), GLA_GROUP)
        rb = pl.ds(pl.multiple_of((n_groups - 1 - g) * GLA_GROUP, GLA_GROUP), GLA_GROUP)
        qkv = (q_ref, k_ref, v_ref)
        (o_f, st_f), (o_b, st_b) = _gla_groups(
            [(_gla_loader(qkv, rf), lr_ref[rf, :], wg_ref[0], bg_ref[0], list(st_f), 0),
             (_gla_loader(qkv, rb), lr_ref[rb, :], wg_ref[1], bg_ref[1], list(st_b), 1)], keeps)
        of_ref[rf, :] = jnp.concatenate([o.T for o in o_f], axis=-1)
        ob_ref[rb, :] = jnp.concatenate([o.T for o in o_b], axis=-1)
        return tuple(st_f), tuple(st_b)

    init = tuple(tuple(s0_ref[d, h].T for h in range(GLA_HEADS)) for d in range(2))
    lax.fori_loop(0, n_groups, body, init)
    for h in range(GLA_HEADS):
        cs = slice(h * GLA_DV, (h + 1) * GLA_DV)
        y = _gla_finish(of_ref[:, cs] + ob_ref[:, cs], r_ref[:, cs].astype(F32), ng_ref[:, cs])
        y_ref[:, cs] = y.astype(y_ref.dtype)


def _gla_ctx(pa, ps, wg, bg, ng, states_prev, layer):
    cb = PA_GLA // GLA_W
    rows = GLA_CTX_SEQS * SEQ
    const = lambda shape: pl.BlockSpec(shape, lambda s: (0,) * len(shape))
    in_specs = [pl.BlockSpec((rows, GLA_W), lambda s, j=j: (s, cb + j)) for j in range(4)] + [
        pl.BlockSpec((rows, LANES), lambda s: (s, PS_LR // LANES)),
        const((2, 2 * LANES, 2 * GLA_W)), const((2, 1, GLA_W)), const((1, GLA_W))]
    args = [pa, pa, pa, pa, ps, wg, bg, ng]
    aliases = {}
    if states_prev is not None:
        in_specs.append(pl.BlockSpec(memory_space=pl.ANY))
        args.append(states_prev)
        aliases = {len(args) - 1: 1}
    return pl.pallas_call(
        _gla_ctx_kernel,
        grid=(BATCH // GLA_CTX_SEQS,),
        in_specs=in_specs,
        out_specs=[pl.BlockSpec((rows, GLA_W), lambda s: (s, 0)),
                   pl.BlockSpec((GLA_CTX_SEQS, None, 2, GLA_HEADS, GLA_DK, GLA_DV),
                                lambda s: (s, layer, 0, 0, 0, 0))],
        out_shape=[jax.ShapeDtypeStruct((T_ALL, GLA_W), BF16),
                   jax.ShapeDtypeStruct((BATCH, DEPTH, 2, GLA_HEADS, GLA_DK, GLA_DV), F32)],
        input_output_aliases=aliases,
        compiler_params=_cparams("parallel"),
        name="gla_ctx",
    )(*args)


def _gla_lat(pa, ps, wg, bg, ng, s0, y):
    rb0 = T_CTX // DEC_SEQ
    cb = PA_GLA // GLA_W
    const = lambda shape: pl.BlockSpec(shape, lambda s: (0,) * len(shape))
    return pl.pallas_call(
        _gla_lat_kernel,
        grid=(DEC_BATCH,),
        in_specs=[pl.BlockSpec((DEC_SEQ, GLA_W), lambda s, j=j: (rb0 + s, cb + j)) for j in range(4)] + [
            pl.BlockSpec((DEC_SEQ, LANES), lambda s: (rb0 + s, PS_LR // LANES)),
            const((2, 2 * LANES, 2 * GLA_W)), const((2, 1, GLA_W)), const((1, GLA_W)),
            pl.BlockSpec((None, 2, GLA_HEADS, GLA_DK, GLA_DV), lambda s: (s, 0, 0, 0, 0)),
            pl.BlockSpec(memory_space=pl.ANY)],
        out_specs=pl.BlockSpec((DEC_SEQ, GLA_W), lambda s: (rb0 + s, 0)),
        out_shape=jax.ShapeDtypeStruct((T_ALL, GLA_W), BF16),
        scratch_shapes=[pltpu.VMEM((DEC_SEQ, GLA_W), F32), pltpu.VMEM((DEC_SEQ, GLA_W), F32)],
        input_output_aliases={9: 0},
        compiler_params=_cparams("parallel"),
        name="gla_lat",
    )(pa, pa, pa, pa, ps, wg, bg, ng, s0, y)


LOG2E = 1.4426950408889634
Q_SCALE = HEAD_DIM ** -0.5 * LOG2E


def _attend_t(problems):
    scores = []
    for q, _, k_parts, _, masks in problems:
        parts = [_dot_t(k, q) for k in k_parts]
        scores.append([s if mask is None else jnp.where(mask, s, NEG_INF) for s, mask in zip(parts, masks)])
    exps = []
    for (q, sink, *_), parts in zip(problems, scores):
        sink2 = sink * LOG2E
        m = jnp.broadcast_to(sink2, (1, q.shape[0])).astype(F32)
        for s in parts:
            m = jnp.maximum(m, jnp.max(s, axis=0, keepdims=True))
        es = [jnp.exp2(s - m) for s in parts]
        den = jnp.exp2(sink2 - m)
        for e in es:
            den = den + jnp.sum(e, axis=0, keepdims=True)
        exps.append(([e.astype(BF16) for e in es], den))
    outs = []
    for (_, _, _, vt_parts, _), (es, den) in zip(problems, exps):
        o = None
        for e, vt in zip(es, vt_parts):
            pv = _mm(vt, e)
            o = pv if o is None else o + pv
        outs.append(o / den)
    return outs


def _lane_halves(block):
    low = lax.broadcasted_iota(jnp.int32, block.shape, 1) < HEAD_DIM
    swapped = pltpu.roll(block, HEAD_DIM, 1)
    zero = jnp.zeros_like(block)
    pick = lambda cond, a: jnp.where(cond, a, zero).astype(BF16)
    return [[pick(low, block), pick(~low, swapped)], [pick(low, swapped), pick(~low, block)]]


ATT_CTX_SEQS = 4


def _attn_ctx_kernel(sink_ref, q_ref, kv_ref, *rest):
    o_ref = rest[-1]
    problems = []
    for s in range(ATT_CTX_SEQS):
        rows = slice(s * SEQ, (s + 1) * SEQ)
        k_halves = _lane_halves(kv_ref[rows, :LANES])
        vt = kv_ref[rows, LANES:].T.astype(BF16)
        for pair in range(ATT_HEADS // 2):
            kv = 2 * pair // ATT_GROUP
            q_pair = (q_ref[rows, pair * LANES:(pair + 1) * LANES] * Q_SCALE).astype(BF16)
            problems += [(q_pair, sink_ref[2 * pair + j], [k_halves[kv][j]],
                          [vt[kv * HEAD_DIM:(kv + 1) * HEAD_DIM]], [None]) for j in range(2)]
    o_t = _attend_t(problems)
    for s in range(ATT_CTX_SEQS):
        outs = [jnp.concatenate(o_t[ATT_HEADS * s + 2 * pair:ATT_HEADS * s + 2 * pair + 2], axis=0).T
                for pair in range(ATT_HEADS // 2)]
        o_ref[s * SEQ:(s + 1) * SEQ, :] = jnp.concatenate(outs, axis=-1).astype(o_ref.dtype)


def _attn_ctx(ps, sink):
    return pl.pallas_call(
        _attn_ctx_kernel,
        grid=(BATCH // ATT_CTX_SEQS,),
        in_specs=[
            pl.BlockSpec(memory_space=pltpu.SMEM),
            pl.BlockSpec((ATT_CTX_SEQS * SEQ, ATT_W), lambda s: (s, 0)),
            pl.BlockSpec((ATT_CTX_SEQS * SEQ, 2 * LANES), lambda s: (s, PS_KV // (2 * LANES))),
        ],
        out_specs=pl.BlockSpec((ATT_CTX_SEQS * SEQ, ATT_W), lambda s: (s, 0)),
        out_shape=jax.ShapeDtypeStruct((T_ALL, ATT_W), BF16),
        compiler_params=_cparams("parallel"),
        name="attn_ctx",
    )(sink, ps, ps)


def _rope(x, cos, sin_signed):
    lane = lax.broadcasted_iota(jnp.int32, x.shape, 1)
    partner = jnp.where((lane & 31) < 16, pltpu.roll(x, LANES - 16, 1), pltpu.roll(x, 16, 1))
    return x * cos + partner * sin_signed


LAT_QBLOCKS = 4


def _attn_lat_kernel(sink_ref, q_ref, kvp_ref, kvc_ref, kvn_ref, kctx_ref, vctx_ref, cos_ref, sin_ref, yin_ref,
                     o_ref):
    del yin_ref
    n_blk = DEC_SEQ // BLOCK
    first = pl.program_id(1) * LAT_QBLOCKS

    def table(ref, blk, n_rows=BLOCK):
        blk = jnp.clip(blk, 0, n_blk - 1)
        return ref[pl.ds(pl.multiple_of(blk * BLOCK, BLOCK), n_rows), :]

    cur = LAT_QBLOCKS * BLOCK
    k_loc = jnp.concatenate(
        [_rope(kvp_ref[:, :LANES], table(cos_ref, first - 1), table(sin_ref, first - 1)),
         _rope(kvc_ref[:, :LANES], table(cos_ref, first, cur), table(sin_ref, first, cur)),
         _rope(kvn_ref[:, :LANES], table(cos_ref, first + LAT_QBLOCKS), table(sin_ref, first + LAT_QBLOCKS))],
        axis=0)
    v_loc = jnp.concatenate([kvp_ref[:, LANES:], kvc_ref[:, LANES:], kvn_ref[:, LANES:]], axis=0)
    kl_halves = _lane_halves(k_loc)
    kc_halves = _lane_halves(kctx_ref[...])
    vt_loc = v_loc.T.astype(BF16)
    vt_ctx = vctx_ref[...].T.astype(BF16)

    q_pairs = [(_rope(q_ref[:, p * LANES:(p + 1) * LANES], table(cos_ref, first, cur), table(sin_ref, first, cur))
                * Q_SCALE).astype(BF16) for p in range(ATT_HEADS // 2)]

    pairs_per_kv = ATT_GROUP // 2
    span = 3 * BLOCK
    shape_t = (span, pairs_per_kv * BLOCK)
    problems = []
    for i in range(LAT_QBLOCKS):
        start = (first + i) * BLOCK
        kpos = start - WINDOW + lax.broadcasted_iota(jnp.int32, shape_t, 0)
        qpos = start + (lax.broadcasted_iota(jnp.int32, shape_t, 1) & (BLOCK - 1))
        valid_t = (jnp.abs(qpos - kpos) <= WINDOW) & (kpos >= 0) & (kpos < DEC_SEQ)
        near = slice(i * BLOCK, i * BLOCK + span)
        for kv in range(ATT_KV_HEADS):
            pairs = range(kv * pairs_per_kv, (kv + 1) * pairs_per_kv)
            q = jnp.concatenate([q_pairs[p][i * BLOCK:(i + 1) * BLOCK] for p in pairs], axis=0)
            rows = slice(kv * HEAD_DIM, (kv + 1) * HEAD_DIM)
            for j in range(2):
                sink = jnp.concatenate([jnp.full((1, BLOCK), sink_ref[2 * p + j], F32) for p in pairs], axis=1)
                problems.append((q, sink, [kc_halves[kv][j], kl_halves[kv][j][near]],
                                 [vt_ctx[rows], vt_loc[rows, near]], [None, valid_t]))
    o_t = _attend_t(problems)
    for i in range(LAT_QBLOCKS):
        outs = []
        for p in range(ATT_HEADS // 2):
            kv, c = divmod(p, pairs_per_kv)
            cols = slice(c * BLOCK, (c + 1) * BLOCK)
            halves = [o_t[(i * ATT_KV_HEADS + kv) * 2 + j][:, cols] for j in range(2)]
            outs.append(jnp.concatenate(halves, axis=0).T)
        o_ref[i * BLOCK:(i + 1) * BLOCK, :] = jnp.concatenate(outs, axis=-1).astype(o_ref.dtype)


def _attn_lat(ps, sink, k_ctx, v_ctx, cos_t, sin_t, y):
    n_blk = DEC_SEQ // BLOCK
    n_steps = n_blk // LAT_QBLOCKS
    rb0 = T_CTX // BLOCK
    kv_col = PS_KV // (2 * LANES)
    cur = LAT_QBLOCKS * BLOCK

    def edge_spec(off):
        return pl.BlockSpec(
            (BLOCK, 2 * LANES),
            lambda b, n: (rb0 + b * n_blk + jnp.clip(n * LAT_QBLOCKS + off, 0, n_blk - 1), kv_col))

    step_rows = lambda b, n: rb0 // LAT_QBLOCKS + b * n_steps + n
    return pl.pallas_call(
        _attn_lat_kernel,
        grid=(DEC_BATCH, n_steps),
        in_specs=[
            pl.BlockSpec(memory_space=pltpu.SMEM),
            pl.BlockSpec((cur, ATT_W), lambda b, n: (step_rows(b, n), 0)),
            edge_spec(-1),
            pl.BlockSpec((cur, 2 * LANES), lambda b, n: (step_rows(b, n), kv_col)),
            edge_spec(LAT_QBLOCKS),
            pl.BlockSpec((None, PAST_LEN, LANES), lambda b, n: (b, 0, 0)),
            pl.BlockSpec((None, PAST_LEN, LANES), lambda b, n: (b, 0, 0)),
            pl.BlockSpec((DEC_SEQ, LANES), lambda b, n: (0, 0)),
            pl.BlockSpec((DEC_SEQ, LANES), lambda b, n: (0, 0)),
            pl.BlockSpec(memory_space=pl.ANY),
        ],
        out_specs=pl.BlockSpec((cur, ATT_W), lambda b, n: (step_rows(b, n), 0)),
        out_shape=jax.ShapeDtypeStruct((T_ALL, ATT_W), BF16),
        input_output_aliases={9: 0},
        compiler_params=_cparams("parallel", "parallel"),
        name="attn_lat",
    )(sink, ps, ps, ps, ps, k_ctx, v_ctx, cos_t, sin_t, y)


def _rope_tables():
    pos = np.arange(DEC_SEQ)
    n_freq = HEAD_DIM // 4
    inv = jnp.asarray(ROPE_THETA, F32) ** (-jnp.arange(n_freq, dtype=F32) / n_freq)
    row = jnp.asarray(pos // GRID_W, F32)
    colp = jnp.asarray(pos % GRID_W, F32)
    ang_r = row[:, None] * inv[None, :]
    ang_c = colp[:, None] * inv[None, :]
    cos = jnp.concatenate([jnp.cos(ang_r)] * 2 + [jnp.cos(ang_c)] * 2, axis=-1)
    sin = jnp.concatenate([-jnp.sin(ang_r), jnp.sin(ang_r), -jnp.sin(ang_c), jnp.sin(ang_c)], axis=-1)
    return jnp.tile(cos, (1, 2)), jnp.tile(sin, (1, 2))


def _route(logits):
    lane_i = lax.broadcasted_iota(jnp.int32, logits.shape, 1)
    lane = lane_i.astype(F32)
    big = jnp.float32(1 << 20)
    is_g = lane_i < N_EXPERT_GROUPS
    lg = jnp.where(is_g, logits, -jnp.inf)
    m_g = jnp.max(lg, axis=-1, keepdims=True)
    grp = jnp.min(jnp.where(lg == m_g, lane, big), axis=-1, keepdims=True)
    z_g = jnp.sum(jnp.where(is_g, jnp.exp(lg - m_g), 0.0), axis=-1, keepdims=True)
    p_grp = 1.0 / z_g

    e_idx = lane_i - ROUTE_E0
    e_grp = (e_idx >> 2).astype(F32)
    sel = (e_idx >= 0) & (e_idx < N_EXPERTS) & (e_grp == grp)
    le = jnp.where(sel, logits, -jnp.inf)
    m_e = jnp.max(le, axis=-1, keepdims=True)
    ex = jnp.where(sel, jnp.exp(le - m_e), 0.0)
    pe = ex / jnp.sum(ex, axis=-1, keepdims=True)
    pe = jnp.where(sel, pe, -1.0)
    v1 = jnp.max(pe, axis=-1, keepdims=True)
    i1 = jnp.min(jnp.where(pe == v1, lane, big), axis=-1, keepdims=True)
    pe2 = jnp.where(lane == i1, -1.0, pe)
    v2 = jnp.max(pe2, axis=-1, keepdims=True)
    i2 = jnp.min(jnp.where(pe2 == v2, lane, big), axis=-1, keepdims=True)
    tot = v1 + v2
    return (jnp.where(lane == i1, p_grp * (v1 / tot), 0.0)
            + jnp.where(lane == i2, p_grp * (v2 / tot), 0.0)
            + jnp.where(lane_i == LANES - 1, grp, 0.0))


HALO = 16


def _merge_kernel(gate_ref, conv_ref, cprev_ref, cnext_ref, yb_ref, yc_ref, *rest, x_split):
    x_refs, rest = rest[:2 if x_split else 1], rest[2 if x_split else 1:]
    (mod_ref, cw_ref, wb_ref, wo_ref, g2_ref, wr_ref, br_ref,
     x1_ref, h2_ref, rg_ref, wbb_ref, wob_ref) = rest
    tm = TM_MERGE
    i = pl.program_id(0)

    def x_rows(rows):
        if not x_split:
            return x_refs[0][rows, :]
        return jnp.where(i < x_split, x_refs[0][rows, :], x_refs[1][rows, :])

    @pl.when(i == 0)
    def _():
        wbb_ref[...] = wb_ref[...].astype(BF16)
        wob_ref[...] = wo_ref[...].astype(BF16)

    def gated(ref):
        return ref[:, CONV_W:2 * CONV_W].astype(F32) * ref[:, 2 * CONV_W:3 * CONV_W].astype(F32)

    a_b = conv_ref[:, 0:CONV_W].astype(F32)
    u = gated(conv_ref)
    u_before = gated(cprev_ref)[HALO - 1:HALO]
    u_after = gated(cnext_ref)[0:1]
    r = lax.broadcasted_iota(jnp.int32, (tm, 1), 0)
    g_row = i * tm + r
    seq_mask = jnp.where(g_row < T_CTX, SEQ - 1, DEC_SEQ - 1)
    first = (g_row & seq_mask) == 0
    last = ((g_row + 1) & seq_mask) == 0
    u_prev = jnp.where(r == 0, u_before, pltpu.roll(u, 1, 0))
    u_next = jnp.where(r == tm - 1, u_after, pltpu.roll(u, tm - 1, 0))
    u_prev = jnp.where(first, 0.0, u_prev)
    u_next = jnp.where(last, 0.0, u_next)
    y_a = (a_b * (u_prev * cw_ref[0:1, :] + u * cw_ref[1:2, :] + u_next * cw_ref[2:3, :])).astype(BF16)

    blocks = [slice(r0, r0 + MERGE_ROWS) for r0 in range(0, tm, MERGE_ROWS)]
    ys = [(y_a[rows], yb_ref[rows, :], yc_ref[rows, :]) for rows in blocks]
    branches = [[_mm(y, wbb_ref[j]) for j, y in enumerate(y3)] for y3 in ys]
    zs = []
    for rows, br3 in zip(blocks, branches):
        z = sum(gate_ref[rows, j * D_MODEL:(j + 1) * D_MODEL].astype(F32) * br3[j] for j in range(3))
        zs.append(z.astype(BF16))
    outs = [_mm(z, wob_ref[...]) for z in zs]
    h2s = []
    for rows, o in zip(blocks, outs):
        x1 = x_rows(rows) + mod_ref[2:3, :] * o
        x1_ref[rows, :] = x1
        h2 = _mod_norm(x1, g2_ref[...], mod_ref[...], 3)
        h2_ref[rows, :] = h2.astype(BF16)
        h2s.append(h2)
    logits = [_split_dot(h2, wr_ref[...]) + br_ref[...] for h2 in h2s]
    for rows, lg in zip(blocks, logits):
        rg_ref[rows, :] = _route(lg)


def _merge(pg, pa, y_b, y_c, x, mod_l, conv_w, layer, wb, wo, g2, wr, br):
    tm = TM_MERGE
    n_tiles = T_ALL // tm
    hb = tm // HALO
    const = lambda shape: pl.BlockSpec(shape, lambda i: (0,) * len(shape))
    if isinstance(x, tuple):
        x_specs, x_split = _split_x_specs(tm)
    else:
        x, x_specs, x_split = (x,), [pl.BlockSpec((tm, D_MODEL), lambda i: (i, 0))], 0
    return pl.pallas_call(
        functools.partial(_merge_kernel, x_split=x_split),
        grid=(n_tiles,),
        in_specs=[
            pl.BlockSpec((tm, PG_W), lambda i: (i, 0)),
            pl.BlockSpec((tm, 3 * CONV_W), lambda i: (i, 0)),
            pl.BlockSpec((HALO, 3 * CONV_W), lambda i: (jnp.maximum(i * hb - 1, 0), 0)),
            pl.BlockSpec((HALO, 3 * CONV_W), lambda i: (jnp.minimum((i + 1) * hb, n_tiles * hb - 1), 0)),
            pl.BlockSpec((tm, GLA_W), lambda i: (i, 0)),
            pl.BlockSpec((tm, ATT_W), lambda i: (i, 0)),
            *x_specs,
            pl.BlockSpec((None, 6, D_MODEL), lambda i: (_cond_row(i, tm), 0, 0)),
            const((CONV_K, CONV_W)),
            pl.BlockSpec((None, 3, 512, D_MODEL), lambda i: (layer, 0, 0, 0)),
            pl.BlockSpec((None, D_MODEL, D_MODEL), lambda i: (layer, 0, 0)),
            const((1, D_MODEL)),
            const((2 * D_MODEL, 2 * LANES)),
            const((1, LANES)),
        ],
        out_specs=[
            pl.BlockSpec((tm, D_MODEL), lambda i: (i, 0)),
            pl.BlockSpec((tm, D_MODEL), lambda i: (i, 0)),
            pl.BlockSpec((tm, LANES), lambda i: (i, 0)),
        ],
        out_shape=[
            jax.ShapeDtypeStruct((T_ALL, D_MODEL), F32),
            jax.ShapeDtypeStruct((T_ALL, D_MODEL), BF16),
            jax.ShapeDtypeStruct((T_ALL, LANES), F32),
        ],
        scratch_shapes=[pltpu.VMEM((3, 512, D_MODEL), BF16), pltpu.VMEM((D_MODEL, D_MODEL), BF16)],
        compiler_params=_cparams("arbitrary"),
        name="merge",
    )(pg, pa, pa, pa, y_b, y_c, *x, mod_l, conv_w, wb, wo, g2, wr, br)


SUB = 128
MOE_ROWS = 256


def _dispatch_tables(rg):
    n_tiles = T_ALL // TM
    grp = rg[:, LANES - 1].astype(jnp.int32).reshape(n_tiles, TM)
    hot = grp[..., None] == jnp.arange(N_EXPERT_GROUPS, dtype=jnp.int32)
    onehot = hot.astype(jnp.int32)
    cnt = onehot.sum(axis=1)
    start = jnp.cumsum(cnt, axis=1) - cnt
    first = start // SUB
    nblk = jnp.where(cnt > 0, (start + cnt + SUB - 1) // SUB - first, 0)
    before = jnp.tril(jnp.ones((TM, TM), BF16), -1)
    rank = jnp.einsum("ts,nsg->ntg", before, hot.astype(BF16), preferred_element_type=F32).astype(jnp.int32)
    slot = ((start[:, None, :] + rank) * onehot).sum(axis=-1)
    return (first.reshape(-1), nblk.reshape(-1), slot.reshape(n_tiles, 1, TM), slot.reshape(T_ALL, 1))


def _moe_kernel(bstart_ref, nblk_ref, h_ref, rg_ref, srow_ref, scol_ref, x1_ref, mod_ref, wg_ref, wu_ref,
                wd_ref, g_ref, *rest, final, tile0):
    if final:
        y_ref, hs_ref, gs_ref, os_ref = rest
    else:
        modn_ref, x2_ref, hn_ref, hs_ref, gs_ref, os_ref = rest
    i = pl.program_id(0)
    grp = pl.program_id(1)

    @pl.when(grp == 0)
    def _():
        slot_of_token = srow_ref[...]
        payload = jnp.concatenate([h_ref[...], *_split3(rg_ref[...])], axis=-1)
        for r0 in range(0, TM, MOE_ROWS):
            rows = slice(r0, r0 + MOE_ROWS)
            slot = r0 + lax.broadcasted_iota(jnp.int32, (MOE_ROWS, TM), 0)
            pm = jnp.where(slot == slot_of_token, 1.0, 0.0).astype(BF16)
            moved = _mm(pm, payload)
            hs_ref[rows, :] = moved[:, :D_MODEL].astype(BF16)
            gs_ref[rows, :] = sum(moved[:, D_MODEL + j * LANES:D_MODEL + (j + 1) * LANES] for j in range(3))
        os_ref[...] = jnp.zeros_like(os_ref)

    k = (tile0 + i) * N_EXPERT_GROUPS + grp
    first = bstart_ref[k]
    n_blocks = nblk_ref[k]

    def experts(block0, n_sub):
        n_rows = n_sub * SUB
        rows = pl.ds(pl.multiple_of(block0 * SUB, SUB), n_rows)
        lane = lax.broadcasted_iota(jnp.int32, (n_rows, LANES), 1)
        x = hs_ref[rows, :]
        gates = gs_ref[rows, :]
        hidden = []
        for e in range(EXPERTS_PER_GROUP):
            ge = _mm(x, wg_ref[e])
            ue = _mm(x, wu_ref[e])
            w = jnp.sum(jnp.where(lane == ROUTE_E0 + grp * EXPERTS_PER_GROUP + e, gates, 0.0),
                        axis=-1, keepdims=True)
            hidden.append(((ge * jax.nn.sigmoid(ge)) * ue * w).astype(BF16))
        acc = _mm(jnp.concatenate(hidden, axis=-1), wd_ref[...].reshape(EXPERTS_PER_GROUP * D_EXPERT, D_MODEL))
        os_ref[rows, :] = (os_ref[rows, :].astype(F32) + acc).astype(BF16)

    odd = n_blocks % 2 == 1
    n_pairs = jnp.where(odd & (n_blocks >= 3), (n_blocks - 3) // 2, n_blocks // 2)

    def pair(p, carry):
        experts(first + 2 * p, 2)
        return carry

    lax.fori_loop(0, n_pairs, pair, 0)

    @pl.when(odd & (n_blocks >= 3))
    def _():
        experts(first + n_blocks - 3, 3)

    @pl.when(n_blocks == 1)
    def _():
        experts(first, 1)

    @pl.when(grp == N_EXPERT_GROUPS - 1)
    def _():
        sorted_out = os_ref[...]
        lane_slot = lax.broadcasted_iota(jnp.int32, (MOE_ROWS, TM), 1)
        for r0 in range(0, TM, MOE_ROWS):
            rows = slice(r0, r0 + MOE_ROWS)
            pt = jnp.where(lane_slot == scol_ref[rows, :], 1.0, 0.0).astype(BF16)
            x2 = x1_ref[rows, :] + mod_ref[5:6, :] * _mm(pt, sorted_out)
            if final:
                y_ref[rows, :] = _rms(x2) * g_ref[...]
            else:
                x2_ref[rows, :] = x2
                hn_ref[rows, :] = _mod_norm(x2, g_ref[...], modn_ref[...], 0).astype(BF16)


def _moe(tables, h2, rg, x1, mod_l, layer, w_eg, w_eu, w_ed, g, mod_next, tile0, n_tiles):
    bstart, nblk, slot_row, slot_col = tables
    final = mod_next is None
    row = lambda i, e, *_: (tile0 + i, 0)
    mod_spec = lambda: pl.BlockSpec((None, 6, D_MODEL), lambda i, e, *_: (_cond_row(tile0 + i, TM), 0, 0))
    group = lambda i, e, *_: (layer, e, 0, 0)
    in_specs = [
        pl.BlockSpec((TM, D_MODEL), row),
        pl.BlockSpec((TM, LANES), row),
        pl.BlockSpec((None, 1, TM), lambda i, e, *_: (tile0 + i, 0, 0)),
        pl.BlockSpec((TM, 1), row),
        pl.BlockSpec((TM, D_MODEL), row),
        mod_spec(),
        pl.BlockSpec((None, EXPERTS_PER_GROUP, D_MODEL, D_EXPERT), group),
        pl.BlockSpec((None, EXPERTS_PER_GROUP, D_MODEL, D_EXPERT), group),
        pl.BlockSpec((None, EXPERTS_PER_GROUP, D_EXPERT, D_MODEL), group),
        pl.BlockSpec((1, D_MODEL), lambda i, e, *_: (0, 0)),
    ]
    args = [h2, rg, slot_row, slot_col, x1, mod_l, w_eg, w_eu, w_ed, g]
    if final:
        out_specs = pl.BlockSpec((TM, D_MODEL), lambda i, e, *_: (i, 0))
        out_shape = jax.ShapeDtypeStruct((n_tiles * TM, D_MODEL), F32)
    else:
        in_specs.append(mod_spec())
        args.append(mod_next)
        out_specs = [pl.BlockSpec((TM, D_MODEL), row)] * 2
        out_shape = [jax.ShapeDtypeStruct((T_ALL, D_MODEL), F32), jax.ShapeDtypeStruct((T_ALL, D_MODEL), BF16)]
    return pl.pallas_call(
        functools.partial(_moe_kernel, final=final, tile0=tile0),
        grid_spec=pltpu.PrefetchScalarGridSpec(
            num_scalar_prefetch=2,
            grid=(n_tiles, N_EXPERT_GROUPS),
            in_specs=in_specs,
            out_specs=out_specs,
            scratch_shapes=[pltpu.VMEM((TM, D_MODEL), BF16), pltpu.VMEM((TM, LANES), F32),
                            pltpu.VMEM((TM, D_MODEL), BF16)],
        ),
        out_shape=out_shape,
        compiler_params=_cparams("parallel", "arbitrary"),
        name="moe_final" if final else "moe",
    )(bstart, nblk, *args)


def _cache_kernel(*refs):
    ps_refs, (k_ref, v_ref) = refs[:DEPTH], refs[DEPTH:]
    n_seq = TM // SEQ
    for l, ps_ref in enumerate(ps_refs):
        k_ref[:, l] = ps_ref[:, :LANES].reshape(n_seq, SEQ, LANES)
        v_ref[:, l] = ps_ref[:, LANES:].reshape(n_seq, SEQ, LANES)


def _cache(ps_layers):
    n_seq = TM // SEQ
    out = pl.BlockSpec((n_seq, DEPTH, SEQ, LANES), lambda i: (i, 0, 0, 0))
    return pl.pallas_call(
        _cache_kernel,
        grid=(T_CTX // TM,),
        in_specs=[pl.BlockSpec((TM, 2 * LANES), lambda i: (i, PS_KV // (2 * LANES)))] * DEPTH,
        out_specs=[out, out],
        out_shape=[jax.ShapeDtypeStruct((BATCH, DEPTH, SEQ, LANES), F32)] * 2,
        compiler_params=_cparams("parallel"),
        name="cache",
    )(*ps_layers)


def _prep_gla_gate(w_gate):
    pads = [((0, 0), (d * GLA_RANK, LANES - (d + 1) * GLA_RANK), (0, 0)) for d in range(2)]
    return _pack_split(jnp.stack([jnp.pad(w_gate[:, d], pads[d]) for d in range(2)], axis=1))


def _prep_router(w_rg, b_rg, w_re, b_re):
    unused = LANES - N_EXPERT_GROUPS - N_EXPERTS
    w = jnp.concatenate([w_rg, w_re, jnp.zeros((DEPTH, D_MODEL, unused), F32)], axis=-1)
    b = jnp.concatenate([b_rg, b_re, jnp.zeros((DEPTH, unused), F32)], axis=-1)
    return _pack_split(w), b.reshape(DEPTH, 1, LANES)


def kernel(x_prompt, x_sample, state_gla, cache_k, cache_v, c, c_ctx, w_ada, b_ada, norm1_g, norm2_g, w_in,
           conv_w, gla_w_gate, gla_b_gate, gla_norm_g, attn_sink, w_branch, w_out, w_route_group,
           b_route_group, w_route_expert, b_route_expert, w_exp_gate, w_exp_up, w_exp_down, final_norm_g):
    cond = jnp.zeros((N_COND, D_MODEL), F32).at[0].set(c_ctx).at[1:1 + DEC_BATCH].set(c)
    mod = _modulation(cond, w_ada, b_ada).reshape(DEPTH, N_COND, 6, D_MODEL)
    cos_t, sin_t = _rope_tables()
    row = lambda v: v.reshape(1, -1)

    x = (x_prompt.reshape(T_CTX, D_MODEL), x_sample.reshape(T_LAT, D_MODEL))
    h = _prenorm(*x, mod[0], row(norm1_g[0]))
    wg_all = _prep_gla_gate(gla_w_gate)
    wr_all, br_all = _prep_router(w_route_group, b_route_group, w_route_expert, b_route_expert)
    states = None
    ps_layers = []
    y_prompt = y_sample = None
    w_t = jnp.swapaxes(w_in, 1, 2)
    proj_a = functools.partial(_proj, pieces=[(0, PA_W // 2)], n_tiles=2, out_dtype=BF16, name="proj_a",
                               act="silu_tail")
    proj_s = functools.partial(_proj, pieces=[(W_IN_TQ, W_IN_GATE - W_IN_TQ), (W_IN_LR, LANES)], n_tiles=1,
                               out_dtype=F32, name="proj_s")
    proj_g = functools.partial(_proj, pieces=[(W_IN_GATE, PG_W // 2)], n_tiles=2, out_dtype=BF16,
                               name="proj_g", act="sigmoid")
    experts = None
    for l in range(DEPTH):
        if l == 0:
            pa, w_eg = proj_a(h, w_t, l, cast=(w_exp_gate, 2))
            ps, w_ed = proj_s(h, w_t, l, cast=(w_exp_down, 4))
            pg, w_eu = proj_g(h, w_t, l, cast=(w_exp_up, 2))
            experts = (w_eg, w_eu, w_ed)
        else:
            pa, ps, pg = proj_a(h, w_t, l), proj_s(h, w_t, l), proj_g(h, w_t, l)
        ps_layers.append(ps)

        wg = wg_all[l]
        bg = gla_b_gate[l].reshape(2, 1, GLA_W)
        ng = row(gla_norm_g[l])
        y_b, states = _gla_ctx(pa, ps, wg, bg, ng, states, l)
        y_b = _gla_lat(pa, ps, wg, bg, ng, state_gla[:, l], y_b)
        y_c = _attn_ctx(ps, attn_sink[l])
        y_c = _attn_lat(ps, attn_sink[l], cache_k[:, l].reshape(DEC_BATCH, PAST_LEN, LANES),
                        cache_v[:, l].reshape(DEC_BATCH, PAST_LEN, LANES), cos_t, sin_t, y_c)

        x1, h2, rg = _merge(pg, pa, y_b, y_c, x, mod[l], conv_w[l], l, w_branch, w_out, row(norm2_g[l]),
                            wr_all[l], br_all[l])
        moe_in = (_dispatch_tables(rg), h2, rg, x1, mod[l], l, *experts)
        if l + 1 < DEPTH:
            x, h = _moe(*moe_in, row(norm1_g[l + 1]), mod[l + 1], 0, T_ALL // TM)
        else:
            gf = row(final_norm_g)
            y_prompt = _moe(*moe_in, gf, None, 0, T_CTX // TM)
            y_sample = _moe(*moe_in, gf, None, T_CTX // TM, T_LAT // TM)

    new_k, new_v = _cache(ps_layers)
    kv_shape = (BATCH, DEPTH, SEQ, ATT_KV_HEADS, HEAD_DIM)
    return (y_prompt.reshape(BATCH, SEQ, D_MODEL), y_sample.reshape(DEC_BATCH, DEC_SEQ, D_MODEL), states,
            new_k.reshape(kv_shape), new_v.reshape(kv_shape))
```

```python
import functools

import jax
import jax.numpy as jnp
import numpy as np
from jax import lax
from jax.experimental import pallas as pl
from jax.experimental.pallas import tpu as pltpu

F32 = jnp.float32
BF16 = jnp.bfloat16

D_MODEL = 1024
BATCH = 32
SEQ = 256
DEPTH = 2
DEC_BATCH = 2
DEC_SEQ = 2048
PAST_LEN = 512
GRID_W = 64
EPS = 1e-6
CONV_W = 512
CONV_K = 3
GLA_HEADS = 4
GLA_DK = 128
GLA_DV = 128
GLA_W = GLA_HEADS * GLA_DV
GLA_RANK = 16
GLA_TAU = 16.0
GLA_CHUNK = 64
ATT_HEADS = 8
ATT_KV_HEADS = 2
ATT_GROUP = ATT_HEADS // ATT_KV_HEADS
HEAD_DIM = 64
ATT_W = ATT_HEADS * HEAD_DIM
WINDOW = 128
BLOCK = 128
ROPE_THETA = 10000.0
N_EXPERT_GROUPS = 4
EXPERTS_PER_GROUP = 4
N_EXPERTS = 16
D_EXPERT = 256
NEG_INF = -1e30

T_CTX = BATCH * SEQ
T_LAT = DEC_BATCH * DEC_SEQ
T_ALL = T_CTX + T_LAT
N_COND = 8
LANES = 128

W_IN_LR = 3584
W_IN_TQ = 3616
W_IN_GATE = 4384
N_IN = 7456
PA_W = 3584
PA_GLA = 1536
PS_W = 896
PS_KV = 512
PS_LR = 768
PG_W = 3 * D_MODEL
ROUTE_E0 = N_EXPERT_GROUPS

TM = 1024
TM_MERGE = 512
MERGE_ROWS = 256
VMEM_LIMIT = 56 * 1024 * 1024


def _cparams(*sem):
    return pltpu.CompilerParams(dimension_semantics=sem, vmem_limit_bytes=VMEM_LIMIT)


def _cond_row(i, tm):
    n_ctx = T_CTX // tm
    per = DEC_SEQ // tm
    return jnp.where(i < n_ctx, 0, 1 + (i - n_ctx) // per)


def _mm(a, b):
    return jnp.dot(a, b, preferred_element_type=F32)


def _dot_t(a, b):
    return lax.dot_general(a, b, (((1,), (1,)), ((), ())), preferred_element_type=F32)


def _dot_ta(a, b):
    return lax.dot_general(a, b, (((0,), (0,)), ((), ())), preferred_element_type=F32)


def _hi_lo(w):
    hi = w.astype(BF16)
    return hi, (w - hi.astype(F32)).astype(BF16)


def _pack_split(w):
    hi, lo = _hi_lo(w)
    return jnp.concatenate([jnp.concatenate([hi, lo], axis=-1),
                            jnp.concatenate([hi, jnp.zeros_like(lo)], axis=-1)], axis=-2)


def _split_dot(a, w_pack):
    a_hi, a_lo = _hi_lo(a)
    both = _mm(jnp.concatenate([a_hi, a_lo], axis=-1), w_pack)
    n = both.shape[-1] // 2
    return both[:, :n] + both[:, n:]


def _rms(x):
    return x * lax.rsqrt(jnp.mean(x * x, axis=-1, keepdims=True) + EPS)


def _mod_norm(x, g, mod, shift_row):
    return _rms(x) * g * (1.0 + mod[shift_row + 1:shift_row + 2, :]) + mod[shift_row:shift_row + 1, :]


def _mod_kernel(c_ref, w_ref, b_ref, o_ref):
    c = c_ref[...]
    s = (c * jax.nn.sigmoid(c)).astype(BF16)
    o_ref[...] = _mm(s, w_ref[...].astype(BF16)) + b_ref[...]


def _modulation(cond, w_ada, b_ada):
    tn = 1536
    return pl.pallas_call(
        _mod_kernel,
        grid=(DEPTH, 6 * D_MODEL // tn),
        in_specs=[
            pl.BlockSpec((N_COND, D_MODEL), lambda l, j: (0, 0)),
            pl.BlockSpec((None, D_MODEL, tn), lambda l, j: (l, 0, j)),
            pl.BlockSpec((None, 1, tn), lambda l, j: (l, 0, j)),
        ],
        out_specs=pl.BlockSpec((None, N_COND, tn), lambda l, j: (l, 0, j)),
        out_shape=jax.ShapeDtypeStruct((DEPTH, N_COND, 6 * D_MODEL), F32),
        compiler_params=_cparams("parallel", "parallel"),
        name="modulation",
    )(cond, w_ada, b_ada.reshape(DEPTH, 1, 6 * D_MODEL))


def _split_x_specs(tm):
    n_ctx = T_CTX // tm
    return [pl.BlockSpec((tm, D_MODEL), lambda i, *_: (jnp.minimum(i, n_ctx - 1), 0)),
            pl.BlockSpec((tm, D_MODEL), lambda i, *_: (jnp.maximum(i - n_ctx, 0), 0))], n_ctx


def _prenorm_kernel(xp_ref, xs_ref, mod_ref, g_ref, h_ref):
    x = jnp.where(pl.program_id(0) < T_CTX // TM, xp_ref[...], xs_ref[...])
    h_ref[...] = _mod_norm(x, g_ref[...], mod_ref[...], 0).astype(BF16)


def _prenorm(xp, xs, mod_l, g):
    x_specs, _ = _split_x_specs(TM)
    return pl.pallas_call(
        _prenorm_kernel,
        grid=(T_ALL // TM,),
        in_specs=x_specs + [
            pl.BlockSpec((None, 6, D_MODEL), lambda i: (_cond_row(i, TM), 0, 0)),
            pl.BlockSpec((1, D_MODEL), lambda i: (0, 0)),
        ],
        out_specs=pl.BlockSpec((TM, D_MODEL), lambda i: (i, 0)),
        out_shape=jax.ShapeDtypeStruct((T_ALL, D_MODEL), BF16),
        compiler_params=_cparams("parallel"),
        name="prenorm",
    )(xp, xs, mod_l, g)


def _proj_kernel(h_ref, *refs, act, n_w):
    w_refs, refs = refs[:n_w], refs[n_w:]
    if len(refs) == 4:
        cast_in_ref, o_ref, cast_out_ref, wb_ref = refs
        cast_out_ref[...] = cast_in_ref[...].astype(BF16)
    else:
        o_ref, wb_ref = refs

    @pl.when(pl.program_id(1) == 0)
    def _():
        col = 0
        for w_ref in w_refs:
            n = w_ref.shape[0]
            wb_ref[:, col:col + n] = w_ref[...].T.astype(BF16)
            col += n

    def sigmoid(t):
        return 0.5 * jnp.tanh(0.5 * t) + 0.5

    y = _mm(h_ref[...], wb_ref[...])
    if act == "sigmoid":
        y = sigmoid(y)
    if act == "silu_tail":
        last = pl.program_id(0) == pl.num_programs(0) - 1
        tail = y[:, -GLA_W:]
        o_ref[:, :-GLA_W] = y[:, :-GLA_W].astype(o_ref.dtype)
        o_ref[:, -GLA_W:] = jnp.where(last, tail * sigmoid(tail), tail).astype(o_ref.dtype)
    else:
        o_ref[...] = y.astype(o_ref.dtype)


def _proj(h, w_t, layer, pieces, n_tiles, out_dtype, name, act=None, cast=None):
    tn = sum(n for _, n in pieces)
    n_rows = T_ALL // TM

    def w_spec(c0, n):
        return pl.BlockSpec((pl.Squeezed(), pl.Element(n), pl.Element(D_MODEL)),
                            lambda j, i: (layer, pl.multiple_of(c0 + j * tn, 8), 0))

    in_specs = [pl.BlockSpec((TM, D_MODEL), lambda j, i: (i, 0))] + [w_spec(c0, n) for c0, n in pieces]
    out_specs = [pl.BlockSpec((TM, tn), lambda j, i: (i, j))]
    out_shape = [jax.ShapeDtypeStruct((T_ALL, n_tiles * tn), out_dtype)]
    args = [h] + [w_t] * len(pieces)
    if cast is not None:
        src, per_block = cast
        per_layer = N_EXPERTS // per_block
        n_blocks = DEPTH * per_layer
        assert n_blocks <= n_tiles * n_rows

        def block(j, i):
            b = jnp.minimum(j * n_rows + i, n_blocks - 1)
            return (b // per_layer, b % per_layer, 0, 0)

        spec = pl.BlockSpec((None, per_block) + src.shape[2:], block)
        in_specs.append(spec)
        out_specs.append(spec)
        out_shape.append(jax.ShapeDtypeStruct(src.shape, BF16))
        args.append(src)
    out = pl.pallas_call(
        functools.partial(_proj_kernel, act=act, n_w=len(pieces)),
        grid=(n_tiles, n_rows),
        in_specs=in_specs,
        out_specs=out_specs,
        out_shape=out_shape,
        scratch_shapes=[pltpu.VMEM((D_MODEL, tn), BF16)],
        compiler_params=_cparams("arbitrary", "arbitrary"),
        name=name,
    )(*args)
    return out[0] if cast is None else out


GLA_GROUP = 256
GLA_CTX_SEQS = 4


def _log_sigmoid(z):
    return jnp.minimum(z, 0.0) - jnp.log(1.0 + jnp.exp(-jnp.abs(z)))


def _split3(x):
    hi = x.astype(BF16)
    r1 = x - hi.astype(F32)
    mid = r1.astype(BF16)
    lo = (r1 - mid.astype(F32)).astype(BF16)
    return hi, mid, lo


def _gla_keep(d):
    ri = lax.broadcasted_iota(jnp.int32, (GLA_GROUP, GLA_GROUP), 0)
    ci = lax.broadcasted_iota(jnp.int32, (GLA_GROUP, GLA_GROUP), 1)
    if d == 0:
        return (ci <= ri) & (ci >= (ri & ~(GLA_CHUNK - 1)))
    return (ci >= ri) & (ci <= (ri | (GLA_CHUNK - 1)))


def _gla_groups(items, keeps):
    C = GLA_CHUNK
    nc = GLA_GROUP // C
    row_chunk = lax.broadcasted_iota(jnp.int32, (GLA_GROUP, GLA_DK), 0) // C
    in_chunk = [row_chunk == c for c in range(nc)]
    tris = [jnp.where(keep, 1.0, 0.0).astype(BF16) for keep in keeps]

    las = []
    for _, lr, wg_pack, bg, _, _ in items:
        lr = jnp.where(lax.broadcasted_iota(jnp.int32, lr.shape, 1) < 2 * GLA_RANK, lr, 0.0)
        las.append(_log_sigmoid(_split_dot(lr, wg_pack) + bg) / GLA_TAU)
    cums = [_mm(tris[item[5]], jnp.concatenate(_split3(la), axis=-1)) for item, la in zip(items, las)]

    ops = []
    for (load, _, _, _, states, d), parts in zip(items, cums):
        width = len(states) * GLA_DK
        b_all = parts[:, :width] + parts[:, width:2 * width] + parts[:, 2 * width:]
        edge = C - 1 if d == 0 else 0
        for h in range(len(states)):
            b = b_all[:, h * GLA_DK:(h + 1) * GLA_DK]
            b_last = [b[c * C + edge:c * C + edge + 1, :] for c in range(nc)]
            bl = jnp.concatenate([jnp.broadcast_to(t, (C, GLA_DK)) for t in b_last], axis=0)
            k = load(1, h)
            q_in = load(0, h) * (GLA_DK ** -0.5) * jnp.exp(b)
            k_in = (k * jnp.exp(-b)).astype(BF16)
            k_end = k * jnp.exp(bl - b)
            k_spread = jnp.concatenate([jnp.where(in_chunk[c], k_end, 0.0) for c in range(nc)], axis=-1)
            q_spread = jnp.concatenate([jnp.where(in_chunk[c], q_in, 0.0) for c in range(nc)], axis=-1)
            ops.append((q_in.astype(BF16), k_in, load(2, h).astype(BF16), k_spread.astype(BF16),
                        q_spread.astype(BF16), [jnp.exp(t) for t in b_last], d))
    att_t = [jnp.where(keeps[1 - op[6]], _dot_t(op[1], op[0]), 0.0).astype(BF16) for op in ops]
    ds_t = [_dot_ta(op[2], op[3]) for op in ops]
    o_t = [_dot_ta(op[2], a) for op, a in zip(ops, att_t)]
    all_states = [st for item in items for st in item[4]]
    new_states, stacked = [], []
    for st, ds, op in zip(all_states, ds_t, ops):
        order = range(nc) if op[6] == 0 else range(nc - 1, -1, -1)
        starts = [None] * nc
        for c in order:
            starts[c] = st
            st = st * op[5][c] + ds[:, c * GLA_DK:(c + 1) * GLA_DK]
        new_states.append(st)
        stacked.append(jnp.concatenate(starts, axis=-1).astype(BF16))
    outs = [o + _dot_t(s, op[4]) for o, s, op in zip(o_t, stacked, ops)]
    results, pos = [], 0
    for item in items:
        n = len(item[4])
        results.append((outs[pos:pos + n], new_states[pos:pos + n]))
        pos += n
    return results


def _gla_loader(qkv_refs, rows):
    def load(i, h):
        return qkv_refs[i][rows, h * GLA_DK:(h + 1) * GLA_DK].astype(F32)
    return load


def _gla_finish(o, r_act, ng):
    o = o * lax.rsqrt(jnp.mean(o * o, axis=-1, keepdims=True) + EPS)
    return o * ng * r_act


def _gla_ctx_kernel(q_ref, k_ref, v_ref, r_ref, lr_ref, wg_ref, bg_ref, ng_ref, *rest):
    y_ref, s_ref = rest[-2:]
    zero = jnp.zeros((GLA_DV, GLA_DK), F32)
    items = []
    for s in range(GLA_CTX_SEQS):
        rows = slice(s * SEQ, (s + 1) * SEQ)
        load = _gla_loader((q_ref, k_ref, v_ref), rows)
        items += [(load, lr_ref[rows, :], wg_ref[d], bg_ref[d], [zero] * GLA_HEADS, d) for d in range(2)]
    results = _gla_groups(items, [_gla_keep(0), _gla_keep(1)])
    for s in range(GLA_CTX_SEQS):
        rows = slice(s * SEQ, (s + 1) * SEQ)
        (o_f, st_f), (o_b, st_b) = results[2 * s], results[2 * s + 1]
        for h in range(GLA_HEADS):
            s_ref[s, 0, h] = st_f[h].T
            s_ref[s, 1, h] = st_b[h].T
            cs = slice(h * GLA_DV, (h + 1) * GLA_DV)
            y = _gla_finish((o_f[h] + o_b[h]).T, r_ref[rows, cs].astype(F32), ng_ref[:, cs])
            y_ref[rows, cs] = y.astype(y_ref.dtype)


def _gla_lat_kernel(q_ref, k_ref, v_ref, r_ref, lr_ref, wg_ref, bg_ref, ng_ref, s0_ref, yin_ref,
                    y_ref, of_ref, ob_ref):
    del yin_ref
    n_groups = DEC_SEQ // GLA_GROUP
    keeps = [_gla_keep(0), _gla_keep(1)]

    def body(g, carry):
        st_f, st_b = carry
        rf = pl.ds(pl.multiple_of(g * GLA_GROUP, GLA_GROUP), GLA_GROUP)
        rb = pl.ds(pl.multiple_of((n_groups - 1 - g) * GLA_GROUP, GLA_GROUP), GLA_GROUP)
        qkv = (q_ref, k_ref, v_ref)
        (o_f, st_f), (o_b, st_b) = _gla_groups(
            [(_gla_loader(qkv, rf), lr_ref[rf, :], wg_ref[0], bg_ref[0], list(st_f), 0),
             (_gla_loader(qkv, rb), lr_ref[rb, :], wg_ref[1], bg_ref[1], list(st_b), 1)], keeps)
        of_ref[rf, :] = jnp.concatenate([o.T for o in o_f], axis=-1)
        ob_ref[rb, :] = jnp.concatenate([o.T for o in o_b], axis=-1)
        return tuple(st_f), tuple(st_b)

    init = tuple(tuple(s0_ref[d, h].T for h in range(GLA_HEADS)) for d in range(2))
    lax.fori_loop(0, n_groups, body, init)
    for h in range(GLA_HEADS):
        cs = slice(h * GLA_DV, (h + 1) * GLA_DV)
        y = _gla_finish(of_ref[:, cs] + ob_ref[:, cs], r_ref[:, cs].astype(F32), ng_ref[:, cs])
        y_ref[:, cs] = y.astype(y_ref.dtype)


def _gla_ctx(pa, ps, wg, bg, ng, states_prev, layer):
    cb = PA_GLA // GLA_W
    rows = GLA_CTX_SEQS * SEQ
    const = lambda shape: pl.BlockSpec(shape, lambda s: (0,) * len(shape))
    in_specs = [pl.BlockSpec((rows, GLA_W), lambda s, j=j: (s, cb + j)) for j in range(4)] + [
        pl.BlockSpec((rows, LANES), lambda s: (s, PS_LR // LANES)),
        const((2, 2 * LANES, 2 * GLA_W)), const((2, 1, GLA_W)), const((1, GLA_W))]
    args = [pa, pa, pa, pa, ps, wg, bg, ng]
    aliases = {}
    if states_prev is not None:
        in_specs.append(pl.BlockSpec(memory_space=pl.ANY))
        args.append(states_prev)
        aliases = {len(args) - 1: 1}
    return pl.pallas_call(
        _gla_ctx_kernel,
        grid=(BATCH // GLA_CTX_SEQS,),
        in_specs=in_specs,
        out_specs=[pl.BlockSpec((rows, GLA_W), lambda s: (s, 0)),
                   pl.BlockSpec((GLA_CTX_SEQS, None, 2, GLA_HEADS, GLA_DK, GLA_DV),
                                lambda s: (s, layer, 0, 0, 0, 0))],
        out_shape=[jax.ShapeDtypeStruct((T_ALL, GLA_W), BF16),
                   jax.ShapeDtypeStruct((BATCH, DEPTH, 2, GLA_HEADS, GLA_DK, GLA_DV), F32)],
        input_output_aliases=aliases,
        compiler_params=_cparams("parallel"),
        name="gla_ctx",
    )(*args)


def _gla_lat(pa, ps, wg, bg, ng, s0, y):
    rb0 = T_CTX // DEC_SEQ
    cb = PA_GLA // GLA_W
    const = lambda shape: pl.BlockSpec(shape, lambda s: (0,) * len(shape))
    return pl.pallas_call(
        _gla_lat_kernel,
        grid=(DEC_BATCH,),
        in_specs=[pl.BlockSpec((DEC_SEQ, GLA_W), lambda s, j=j: (rb0 + s, cb + j)) for j in range(4)] + [
            pl.BlockSpec((DEC_SEQ, LANES), lambda s: (rb0 + s, PS_LR // LANES)),
            const((2, 2 * LANES, 2 * GLA_W)), const((2, 1, GLA_W)), const((1, GLA_W)),
            pl.BlockSpec((None, 2, GLA_HEADS, GLA_DK, GLA_DV), lambda s: (s, 0, 0, 0, 0)),
            pl.BlockSpec(memory_space=pl.ANY)],
        out_specs=pl.BlockSpec((DEC_SEQ, GLA_W), lambda s: (rb0 + s, 0)),
        out_shape=jax.ShapeDtypeStruct((T_ALL, GLA_W), BF16),
        scratch_shapes=[pltpu.VMEM((DEC_SEQ, GLA_W), F32), pltpu.VMEM((DEC_SEQ, GLA_W), F32)],
        input_output_aliases={9: 0},
        compiler_params=_cparams("parallel"),
        name="gla_lat",
    )(pa, pa, pa, pa, ps, wg, bg, ng, s0, y)


LOG2E = 1.4426950408889634
Q_SCALE = HEAD_DIM ** -0.5 * LOG2E


def _attend_t(problems):
    scores = []
    for q, _, k_parts, _, masks in problems:
        parts = [_dot_t(k, q) for k in k_parts]
        scores.append([s if mask is None else jnp.where(mask, s, NEG_INF) for s, mask in zip(parts, masks)])
    exps = []
    for (q, sink, *_), parts in zip(problems, scores):
        sink2 = sink * LOG2E
        m = jnp.broadcast_to(sink2, (1, q.shape[0])).astype(F32)
        for s in parts:
            m = jnp.maximum(m, jnp.max(s, axis=0, keepdims=True))
        es = [jnp.exp2(s - m) for s in parts]
        den = jnp.exp2(sink2 - m)
        for e in es:
            den = den + jnp.sum(e, axis=0, keepdims=True)
        exps.append(([e.astype(BF16) for e in es], den))
    outs = []
    for (_, _, _, vt_parts, _), (es, den) in zip(problems, exps):
        o = None
        for e, vt in zip(es, vt_parts):
            pv = _mm(vt, e)
            o = pv if o is None else o + pv
        outs.append(o / den)
    return outs


def _lane_halves(block):
    low = lax.broadcasted_iota(jnp.int32, block.shape, 1) < HEAD_DIM
    swapped = pltpu.roll(block, HEAD_DIM, 1)
    zero = jnp.zeros_like(block)
    pick = lambda cond, a: jnp.where(cond, a, zero).astype(BF16)
    return [[pick(low, block), pick(~low, swapped)], [pick(low, swapped), pick(~low, block)]]


ATT_CTX_SEQS = 4


def _attn_ctx_kernel(sink_ref, q_ref, kv_ref, *rest):
    o_ref = rest[-1]
    problems = []
    for s in range(ATT_CTX_SEQS):
        rows = slice(s * SEQ, (s + 1) * SEQ)
        k_halves = _lane_halves(kv_ref[rows, :LANES])
        vt = kv_ref[rows, LANES:].T.astype(BF16)
        for pair in range(ATT_HEADS // 2):
            kv = 2 * pair // ATT_GROUP
            q_pair = (q_ref[rows, pair * LANES:(pair + 1) * LANES] * Q_SCALE).astype(BF16)
            problems += [(q_pair, sink_ref[2 * pair + j], [k_halves[kv][j]],
                          [vt[kv * HEAD_DIM:(kv + 1) * HEAD_DIM]], [None]) for j in range(2)]
    o_t = _attend_t(problems)
    for s in range(ATT_CTX_SEQS):
        outs = [jnp.concatenate(o_t[ATT_HEADS * s + 2 * pair:ATT_HEADS * s + 2 * pair + 2], axis=0).T
                for pair in range(ATT_HEADS // 2)]
        o_ref[s * SEQ:(s + 1) * SEQ, :] = jnp.concatenate(outs, axis=-1).astype(o_ref.dtype)


def _attn_ctx(ps, sink):
    return pl.pallas_call(
        _attn_ctx_kernel,
        grid=(BATCH // ATT_CTX_SEQS,),
        in_specs=[
            pl.BlockSpec(memory_space=pltpu.SMEM),
            pl.BlockSpec((ATT_CTX_SEQS * SEQ, ATT_W), lambda s: (s, 0)),
            pl.BlockSpec((ATT_CTX_SEQS * SEQ, 2 * LANES), lambda s: (s, PS_KV // (2 * LANES))),
        ],
        out_specs=pl.BlockSpec((ATT_CTX_SEQS * SEQ, ATT_W), lambda s: (s, 0)),
        out_shape=jax.ShapeDtypeStruct((T_ALL, ATT_W), BF16),
        compiler_params=_cparams("parallel"),
        name="attn_ctx",
    )(sink, ps, ps)


def _rope(x, cos, sin_signed):
    lane = lax.broadcasted_iota(jnp.int32, x.shape, 1)
    partner = jnp.where((lane & 31) < 16, pltpu.roll(x, LANES - 16, 1), pltpu.roll(x, 16, 1))
    return x * cos + partner * sin_signed


LAT_QBLOCKS = 4


def _attn_lat_kernel(sink_ref, q_ref, kvp_ref, kvc_ref, kvn_ref, kctx_ref, vctx_ref, cos_ref, sin_ref, yin_ref,
                     o_ref):
    del yin_ref
    n_blk = DEC_SEQ // BLOCK
    first = pl.program_id(1) * LAT_QBLOCKS

    def table(ref, blk, n_rows=BLOCK):
        blk = jnp.clip(blk, 0, n_blk - 1)
        return ref[pl.ds(pl.multiple_of(blk * BLOCK, BLOCK), n_rows), :]

    cur = LAT_QBLOCKS * BLOCK
    k_loc = jnp.concatenate(
        [_rope(kvp_ref[:, :LANES], table(cos_ref, first - 1), table(sin_ref, first - 1)),
         _rope(kvc_ref[:, :LANES], table(cos_ref, first, cur), table(sin_ref, first, cur)),
         _rope(kvn_ref[:, :LANES], table(cos_ref, first + LAT_QBLOCKS), table(sin_ref, first + LAT_QBLOCKS))],
        axis=0)
    v_loc = jnp.concatenate([kvp_ref[:, LANES:], kvc_ref[:, LANES:], kvn_ref[:, LANES:]], axis=0)
    kl_halves = _lane_halves(k_loc)
    kc_halves = _lane_halves(kctx_ref[...])
    vt_loc = v_loc.T.astype(BF16)
    vt_ctx = vctx_ref[...].T.astype(BF16)

    q_pairs = [(_rope(q_ref[:, p * LANES:(p + 1) * LANES], table(cos_ref, first, cur), table(sin_ref, first, cur))
                * Q_SCALE).astype(BF16) for p in range(ATT_HEADS // 2)]

    pairs_per_kv = ATT_GROUP // 2
    span = 3 * BLOCK
    shape_t = (span, pairs_per_kv * BLOCK)
    problems = []
    for i in range(LAT_QBLOCKS):
        start = (first + i) * BLOCK
        kpos = start - WINDOW + lax.broadcasted_iota(jnp.int32, shape_t, 0)
        qpos = start + (lax.broadcasted_iota(jnp.int32, shape_t, 1) & (BLOCK - 1))
        valid_t = (jnp.abs(qpos - kpos) <= WINDOW) & (kpos >= 0) & (kpos < DEC_SEQ)
        near = slice(i * BLOCK, i * BLOCK + span)
        for kv in range(ATT_KV_HEADS):
            pairs = range(kv * pairs_per_kv, (kv + 1) * pairs_per_kv)
            q = jnp.concatenate([q_pairs[p][i * BLOCK:(i + 1) * BLOCK] for p in pairs], axis=0)
            rows = slice(kv * HEAD_DIM, (kv + 1) * HEAD_DIM)
            for j in range(2):
                sink = jnp.concatenate([jnp.full((1, BLOCK), sink_ref[2 * p + j], F32) for p in pairs], axis=1)
                problems.append((q, sink, [kc_halves[kv][j], kl_halves[kv][j][near]],
                                 [vt_ctx[rows], vt_loc[rows, near]], [None, valid_t]))
    o_t = _attend_t(problems)
    for i in range(LAT_QBLOCKS):
        outs = []
        for p in range(ATT_HEADS // 2):
            kv, c = divmod(p, pairs_per_kv)
            cols = slice(c * BLOCK, (c + 1) * BLOCK)
            halves = [o_t[(i * ATT_KV_HEADS + kv) * 2 + j][:, cols] for j in range(2)]
            outs.append(jnp.concatenate(halves, axis=0).T)
        o_ref[i * BLOCK:(i + 1) * BLOCK, :] = jnp.concatenate(outs, axis=-1).astype(o_ref.dtype)


def _attn_lat(ps, sink, k_ctx, v_ctx, cos_t, sin_t, y):
    n_blk = DEC_SEQ // BLOCK
    n_steps = n_blk // LAT_QBLOCKS
    rb0 = T_CTX // BLOCK
    kv_col = PS_KV // (2 * LANES)
    cur = LAT_QBLOCKS * BLOCK

    def edge_spec(off):
        return pl.BlockSpec(
            (BLOCK, 2 * LANES),
            lambda b, n: (rb0 + b * n_blk + jnp.clip(n * LAT_QBLOCKS + off, 0, n_blk - 1), kv_col))

    step_rows = lambda b, n: rb0 // LAT_QBLOCKS + b * n_steps + n
    return pl.pallas_call(
        _attn_lat_kernel,
        grid=(DEC_BATCH, n_steps),
        in_specs=[
            pl.BlockSpec(memory_space=pltpu.SMEM),
            pl.BlockSpec((cur, ATT_W), lambda b, n: (step_rows(b, n), 0)),
            edge_spec(-1),
            pl.BlockSpec((cur, 2 * LANES), lambda b, n: (step_rows(b, n), kv_col)),
            edge_spec(LAT_QBLOCKS),
            pl.BlockSpec((None, PAST_LEN, LANES), lambda b, n: (b, 0, 0)),
            pl.BlockSpec((None, PAST_LEN, LANES), lambda b, n: (b, 0, 0)),
            pl.BlockSpec((DEC_SEQ, LANES), lambda b, n: (0, 0)),
            pl.BlockSpec((DEC_SEQ, LANES), lambda b, n: (0, 0)),
            pl.BlockSpec(memory_space=pl.ANY),
        ],
        out_specs=pl.BlockSpec((cur, ATT_W), lambda b, n: (step_rows(b, n), 0)),
        out_shape=jax.ShapeDtypeStruct((T_ALL, ATT_W), BF16),
        input_output_aliases={9: 0},
        compiler_params=_cparams("parallel", "parallel"),
        name="attn_lat",
    )(sink, ps, ps, ps, ps, k_ctx, v_ctx, cos_t, sin_t, y)


def _rope_tables():
    pos = np.arange(DEC_SEQ)
    n_freq = HEAD_DIM // 4
    inv = jnp.asarray(ROPE_THETA, F32) ** (-jnp.arange(n_freq, dtype=F32) / n_freq)
    row = jnp.asarray(pos // GRID_W, F32)
    colp = jnp.asarray(pos % GRID_W, F32)
    ang_r = row[:, None] * inv[None, :]
    ang_c = colp[:, None] * inv[None, :]
    cos = jnp.concatenate([jnp.cos(ang_r)] * 2 + [jnp.cos(ang_c)] * 2, axis=-1)
    sin = jnp.concatenate([-jnp.sin(ang_r), jnp.sin(ang_r), -jnp.sin(ang_c), jnp.sin(ang_c)], axis=-1)
    return jnp.tile(cos, (1, 2)), jnp.tile(sin, (1, 2))


ROUTE_ROWS = 24


def _route_t(logits_t):
    row_i = lax.broadcasted_iota(jnp.int32, logits_t.shape, 0)
    row = row_i.astype(F32)
    big = jnp.float32(1 << 20)
    is_g = row_i < N_EXPERT_GROUPS
    lg = jnp.where(is_g, logits_t, -jnp.inf)
    m_g = jnp.max(lg, axis=0, keepdims=True)
    grp = jnp.min(jnp.where(lg == m_g, row, big), axis=0, keepdims=True)
    z_g = jnp.sum(jnp.where(is_g, jnp.exp(lg - m_g), 0.0), axis=0, keepdims=True)
    p_grp = 1.0 / z_g

    e_idx = row_i - ROUTE_E0
    e_grp = (e_idx >> 2).astype(F32)
    sel = (e_idx >= 0) & (e_idx < N_EXPERTS) & (e_grp == grp)
    le = jnp.where(sel, logits_t, -jnp.inf)
    m_e = jnp.max(le, axis=0, keepdims=True)
    ex = jnp.where(sel, jnp.exp(le - m_e), 0.0)
    pe = ex / jnp.sum(ex, axis=0, keepdims=True)
    pe = jnp.where(sel, pe, -1.0)
    v1 = jnp.max(pe, axis=0, keepdims=True)
    i1 = jnp.min(jnp.where(pe == v1, row, big), axis=0, keepdims=True)
    pe2 = jnp.where(row == i1, -1.0, pe)
    v2 = jnp.max(pe2, axis=0, keepdims=True)
    i2 = jnp.min(jnp.where(pe2 == v2, row, big), axis=0, keepdims=True)
    tot = v1 + v2
    gates = jnp.where(row == i1, p_grp * (v1 / tot), 0.0) + jnp.where(row == i2, p_grp * (v2 / tot), 0.0)
    return gates, grp


HALO = 16


def _merge_kernel(gate_ref, conv_ref, cprev_ref, cnext_ref, yb_ref, yc_ref, *rest, x_split):
    x_refs, rest = rest[:2 if x_split else 1], rest[2 if x_split else 1:]
    (mod_ref, cw_ref, wb_ref, wo_ref, g2_ref, wr_ref, br_ref,
     x1_ref, h2_ref, rg_ref, wbb_ref, wob_ref) = rest
    tm = TM_MERGE
    i = pl.program_id(0)

    def x_rows(rows):
        if not x_split:
            return x_refs[0][rows, :]
        return jnp.where(i < x_split, x_refs[0][rows, :], x_refs[1][rows, :])

    @pl.when(i == 0)
    def _():
        wbb_ref[...] = wb_ref[...].astype(BF16)
        wob_ref[...] = wo_ref[...].astype(BF16)

    def gated(ref):
        return ref[:, CONV_W:2 * CONV_W].astype(F32) * ref[:, 2 * CONV_W:3 * CONV_W].astype(F32)

    a_b = conv_ref[:, 0:CONV_W].astype(F32)
    u = gated(conv_ref)
    u_before = gated(cprev_ref)[HALO - 1:HALO]
    u_after = gated(cnext_ref)[0:1]
    r = lax.broadcasted_iota(jnp.int32, (tm, 1), 0)
    g_row = i * tm + r
    seq_mask = jnp.where(g_row < T_CTX, SEQ - 1, DEC_SEQ - 1)
    first = (g_row & seq_mask) == 0
    last = ((g_row + 1) & seq_mask) == 0
    u_prev = jnp.where(r == 0, u_before, pltpu.roll(u, 1, 0))
    u_next = jnp.where(r == tm - 1, u_after, pltpu.roll(u, tm - 1, 0))
    u_prev = jnp.where(first, 0.0, u_prev)
    u_next = jnp.where(last, 0.0, u_next)
    y_a = (a_b * (u_prev * cw_ref[0:1, :] + u * cw_ref[1:2, :] + u_next * cw_ref[2:3, :])).astype(BF16)

    blocks = [slice(r0, r0 + MERGE_ROWS) for r0 in range(0, tm, MERGE_ROWS)]
    ys = [(y_a[rows], yb_ref[rows, :], yc_ref[rows, :]) for rows in blocks]
    branches = [[_mm(y, wbb_ref[j]) for j, y in enumerate(y3)] for y3 in ys]
    zs = []
    for rows, br3 in zip(blocks, branches):
        z = sum(gate_ref[rows, j * D_MODEL:(j + 1) * D_MODEL].astype(F32) * br3[j] for j in range(3))
        zs.append(z.astype(BF16))
    outs = [_mm(z, wob_ref[...]) for z in zs]
    h2s = []
    for rows, o in zip(blocks, outs):
        x1 = x_rows(rows) + mod_ref[2:3, :] * o
        x1_ref[rows, :] = x1
        h2 = _mod_norm(x1, g2_ref[...], mod_ref[...], 3)
        h2_ref[rows, :] = h2.astype(BF16)
        h2s.append(h2)
    packed = [_dot_t(wr_ref[...], jnp.concatenate(_hi_lo(h2), axis=-1)) for h2 in h2s]
    for rows, both in zip(blocks, packed):
        logits_t = (both[:LANES] + both[LANES:] + br_ref[...])[:ROUTE_ROWS]
        gates, grp = _route_t(logits_t)
        last = lax.broadcasted_iota(jnp.int32, (8, MERGE_ROWS), 0) == 7
        rg_t = jnp.concatenate([gates, jnp.zeros((LANES - ROUTE_ROWS - 8, MERGE_ROWS), F32),
                                jnp.where(last, grp, 0.0)], axis=0)
        rg_ref[rows, :] = rg_t.T


def _merge(pg, pa, y_b, y_c, x, mod_l, conv_w, layer, wb, wo, g2, wr, br):
    tm = TM_MERGE
    n_tiles = T_ALL // tm
    hb = tm // HALO
    const = lambda shape: pl.BlockSpec(shape, lambda i: (0,) * len(shape))
    if isinstance(x, tuple):
        x_specs, x_split = _split_x_specs(tm)
    else:
        x, x_specs, x_split = (x,), [pl.BlockSpec((tm, D_MODEL), lambda i: (i, 0))], 0
    return pl.pallas_call(
        functools.partial(_merge_kernel, x_split=x_split),
        grid=(n_tiles,),
        in_specs=[
            pl.BlockSpec((tm, PG_W), lambda i: (i, 0)),
            pl.BlockSpec((tm, 3 * CONV_W), lambda i: (i, 0)),
            pl.BlockSpec((HALO, 3 * CONV_W), lambda i: (jnp.maximum(i * hb - 1, 0), 0)),
            pl.BlockSpec((HALO, 3 * CONV_W), lambda i: (jnp.minimum((i + 1) * hb, n_tiles * hb - 1), 0)),
            pl.BlockSpec((tm, GLA_W), lambda i: (i, 0)),
            pl.BlockSpec((tm, ATT_W), lambda i: (i, 0)),
            *x_specs,
            pl.BlockSpec((None, 6, D_MODEL), lambda i: (_cond_row(i, tm), 0, 0)),
            const((CONV_K, CONV_W)),
            pl.BlockSpec((None, 3, 512, D_MODEL), lambda i: (layer, 0, 0, 0)),
            pl.BlockSpec((None, D_MODEL, D_MODEL), lambda i: (layer, 0, 0)),
            const((1, D_MODEL)),
            const((2 * LANES, 2 * D_MODEL)),
            const((LANES, 1)),
        ],
        out_specs=[
            pl.BlockSpec((tm, D_MODEL), lambda i: (i, 0)),
            pl.BlockSpec((tm, D_MODEL), lambda i: (i, 0)),
            pl.BlockSpec((tm, LANES), lambda i: (i, 0)),
        ],
        out_shape=[
            jax.ShapeDtypeStruct((T_ALL, D_MODEL), F32),
            jax.ShapeDtypeStruct((T_ALL, D_MODEL), BF16),
            jax.ShapeDtypeStruct((T_ALL, LANES), F32),
        ],
        scratch_shapes=[pltpu.VMEM((3, 512, D_MODEL), BF16), pltpu.VMEM((D_MODEL, D_MODEL), BF16)],
        compiler_params=_cparams("arbitrary"),
        name="merge",
    )(pg, pa, pa, pa, y_b, y_c, *x, mod_l, conv_w, wb, wo, g2, wr, br)


SUB = 128
MOE_ROWS = 256


def _dispatch_tables(rg):
    n_tiles = T_ALL // TM
    grp = rg[:, LANES - 1].astype(jnp.int32).reshape(n_tiles, TM)
    hot = grp[..., None] == jnp.arange(N_EXPERT_GROUPS, dtype=jnp.int32)
    onehot = hot.astype(jnp.int32)
    cnt = onehot.sum(axis=1)
    start = jnp.cumsum(cnt, axis=1) - cnt
    first = start // SUB
    nblk = jnp.where(cnt > 0, (start + cnt + SUB - 1) // SUB - first, 0)
    before = jnp.tril(jnp.ones((TM, TM), BF16), -1)
    rank = jnp.einsum("ts,nsg->ntg", before, hot.astype(BF16), preferred_element_type=F32).astype(jnp.int32)
    slot = ((start[:, None, :] + rank) * onehot).sum(axis=-1)
    return (first.reshape(-1), nblk.reshape(-1), slot.reshape(n_tiles, 1, TM), slot.reshape(T_ALL, 1))


def _moe_kernel(bstart_ref, nblk_ref, h_ref, rg_ref, srow_ref, scol_ref, x1_ref, mod_ref, wg_ref, wu_ref,
                wd_ref, g_ref, *rest, final, tile0):
    if final:
        y_ref, hs_ref, gs_ref, os_ref = rest
    else:
        modn_ref, x2_ref, hn_ref, hs_ref, gs_ref, os_ref = rest
    i = pl.program_id(0)
    grp = pl.program_id(1)

    @pl.when(grp == 0)
    def _():
        slot_of_token = srow_ref[...]
        payload = jnp.concatenate([h_ref[...], *_split3(rg_ref[...])], axis=-1)
        for r0 in range(0, TM, MOE_ROWS):
            rows = slice(r0, r0 + MOE_ROWS)
            slot = r0 + lax.broadcasted_iota(jnp.int32, (MOE_ROWS, TM), 0)
            pm = jnp.where(slot == slot_of_token, 1.0, 0.0).astype(BF16)
            moved = _mm(pm, payload)
            hs_ref[rows, :] = moved[:, :D_MODEL].astype(BF16)
            gs_ref[rows, :] = sum(moved[:, D_MODEL + j * LANES:D_MODEL + (j + 1) * LANES] for j in range(3))
        os_ref[...] = jnp.zeros_like(os_ref)

    k = (tile0 + i) * N_EXPERT_GROUPS + grp
    first = bstart_ref[k]
    n_blocks = nblk_ref[k]

    def experts(block0, n_sub):
        n_rows = n_sub * SUB
        rows = pl.ds(pl.multiple_of(block0 * SUB, SUB), n_rows)
        lane = lax.broadcasted_iota(jnp.int32, (n_rows, LANES), 1)
        x = hs_ref[rows, :]
        gates = gs_ref[rows, :]
        hidden = []
        for e in range(EXPERTS_PER_GROUP):
            ge = _mm(x, wg_ref[e])
            ue = _mm(x, wu_ref[e])
            w = jnp.sum(jnp.where(lane == ROUTE_E0 + grp * EXPERTS_PER_GROUP + e, gates, 0.0),
                        axis=-1, keepdims=True)
            hidden.append(((ge * jax.nn.sigmoid(ge)) * ue * w).astype(BF16))
        acc = _mm(jnp.concatenate(hidden, axis=-1), wd_ref[...].reshape(EXPERTS_PER_GROUP * D_EXPERT, D_MODEL))
        os_ref[rows, :] = (os_ref[rows, :].astype(F32) + acc).astype(BF16)

    odd = n_blocks % 2 == 1
    n_pairs = jnp.where(odd & (n_blocks >= 3), (n_blocks - 3) // 2, n_blocks // 2)

    def pair(p, carry):
        experts(first + 2 * p, 2)
        return carry

    lax.fori_loop(0, n_pairs, pair, 0)

    @pl.when(odd & (n_blocks >= 3))
    def _():
        experts(first + n_blocks - 3, 3)

    @pl.when(n_blocks == 1)
    def _():
        experts(first, 1)

    @pl.when(grp == N_EXPERT_GROUPS - 1)
    def _():
        sorted_out = os_ref[...]
        lane_slot = lax.broadcasted_iota(jnp.int32, (MOE_ROWS, TM), 1)
        for r0 in range(0, TM, MOE_ROWS):
            rows = slice(r0, r0 + MOE_ROWS)
            pt = jnp.where(lane_slot == scol_ref[rows, :], 1.0, 0.0).astype(BF16)
            x2 = x1_ref[rows, :] + mod_ref[5:6, :] * _mm(pt, sorted_out)
            if final:
                y_ref[rows, :] = _rms(x2) * g_ref[...]
            else:
                x2_ref[rows, :] = x2
                hn_ref[rows, :] = _mod_norm(x2, g_ref[...], modn_ref[...], 0).astype(BF16)


def _moe(tables, h2, rg, x1, mod_l, layer, w_eg, w_eu, w_ed, g, mod_next, tile0, n_tiles):
    bstart, nblk, slot_row, slot_col = tables
    final = mod_next is None
    row = lambda i, e, *_: (tile0 + i, 0)
    mod_spec = lambda: pl.BlockSpec((None, 6, D_MODEL), lambda i, e, *_: (_cond_row(tile0 + i, TM), 0, 0))
    group = lambda i, e, *_: (layer, e, 0, 0)
    in_specs = [
        pl.BlockSpec((TM, D_MODEL), row),
        pl.BlockSpec((TM, LANES), row),
        pl.BlockSpec((None, 1, TM), lambda i, e, *_: (tile0 + i, 0, 0)),
        pl.BlockSpec((TM, 1), row),
        pl.BlockSpec((TM, D_MODEL), row),
        mod_spec(),
        pl.BlockSpec((None, EXPERTS_PER_GROUP, D_MODEL, D_EXPERT), group),
        pl.BlockSpec((None, EXPERTS_PER_GROUP, D_MODEL, D_EXPERT), group),
        pl.BlockSpec((None, EXPERTS_PER_GROUP, D_EXPERT, D_MODEL), group),
        pl.BlockSpec((1, D_MODEL), lambda i, e, *_: (0, 0)),
    ]
    args = [h2, rg, slot_row, slot_col, x1, mod_l, w_eg, w_eu, w_ed, g]
    if final:
        out_specs = pl.BlockSpec((TM, D_MODEL), lambda i, e, *_: (i, 0))
        out_shape = jax.ShapeDtypeStruct((n_tiles * TM, D_MODEL), F32)
    else:
        in_specs.append(mod_spec())
        args.append(mod_next)
        out_specs = [pl.BlockSpec((TM, D_MODEL), row)] * 2
        out_shape = [jax.ShapeDtypeStruct((T_ALL, D_MODEL), F32), jax.ShapeDtypeStruct((T_ALL, D_MODEL), BF16)]
    return pl.pallas_call(
        functools.partial(_moe_kernel, final=final, tile0=tile0),
        grid_spec=pltpu.PrefetchScalarGridSpec(
            num_scalar_prefetch=2,
            grid=(n_tiles, N_EXPERT_GROUPS),
            in_specs=in_specs,
            out_specs=out_specs,
            scratch_shapes=[pltpu.VMEM((TM, D_MODEL), BF16), pltpu.VMEM((TM, LANES), F32),
                            pltpu.VMEM((TM, D_MODEL), BF16)],
        ),
        out_shape=out_shape,
        compiler_params=_cparams("parallel", "arbitrary"),
        name="moe_final" if final else "moe",
    )(bstart, nblk, *args)


def _cache_kernel(*refs):
    ps_refs, (k_ref, v_ref) = refs[:DEPTH], refs[DEPTH:]
    n_seq = TM // SEQ
    for l, ps_ref in enumerate(ps_refs):
        k_ref[:, l] = ps_ref[:, :LANES].reshape(n_seq, SEQ, LANES)
        v_ref[:, l] = ps_ref[:, LANES:].reshape(n_seq, SEQ, LANES)


def _cache(ps_layers):
    n_seq = TM // SEQ
    out = pl.BlockSpec((n_seq, DEPTH, SEQ, LANES), lambda i: (i, 0, 0, 0))
    return pl.pallas_call(
        _cache_kernel,
        grid=(T_CTX // TM,),
        in_specs=[pl.BlockSpec((TM, 2 * LANES), lambda i: (i, PS_KV // (2 * LANES)))] * DEPTH,
        out_specs=[out, out],
        out_shape=[jax.ShapeDtypeStruct((BATCH, DEPTH, SEQ, LANES), F32)] * 2,
        compiler_params=_cparams("parallel"),
        name="cache",
    )(*ps_layers)


def _prep_gla_gate(w_gate):
    pads = [((0, 0), (d * GLA_RANK, LANES - (d + 1) * GLA_RANK), (0, 0)) for d in range(2)]
    return _pack_split(jnp.stack([jnp.pad(w_gate[:, d], pads[d]) for d in range(2)], axis=1))


def _prep_router(w_rg, b_rg, w_re, b_re):
    unused = LANES - N_EXPERT_GROUPS - N_EXPERTS
    w = jnp.concatenate([w_rg, w_re, jnp.zeros((DEPTH, D_MODEL, unused), F32)], axis=-1)
    b = jnp.concatenate([b_rg, b_re, jnp.zeros((DEPTH, unused), F32)], axis=-1)
    return jnp.swapaxes(_pack_split(w), -1, -2), b.reshape(DEPTH, LANES, 1)


def kernel(x_prompt, x_sample, state_gla, cache_k, cache_v, c, c_ctx, w_ada, b_ada, norm1_g, norm2_g, w_in,
           conv_w, gla_w_gate, gla_b_gate, gla_norm_g, attn_sink, w_branch, w_out, w_route_group,
           b_route_group, w_route_expert, b_route_expert, w_exp_gate, w_exp_up, w_exp_down, final_norm_g):
    cond = jnp.zeros((N_COND, D_MODEL), F32).at[0].set(c_ctx).at[1:1 + DEC_BATCH].set(c)
    mod = _modulation(cond, w_ada, b_ada).reshape(DEPTH, N_COND, 6, D_MODEL)
    cos_t, sin_t = _rope_tables()
    row = lambda v: v.reshape(1, -1)

    x = (x_prompt.reshape(T_CTX, D_MODEL), x_sample.reshape(T_LAT, D_MODEL))
    h = _prenorm(*x, mod[0], row(norm1_g[0]))
    wg_all = _prep_gla_gate(gla_w_gate)
    wr_all, br_all = _prep_router(w_route_group, b_route_group, w_route_expert, b_route_expert)
    states = None
    ps_layers = []
    y_prompt = y_sample = None
    w_t = jnp.swapaxes(w_in, 1, 2)
    proj_a = functools.partial(_proj, pieces=[(0, PA_W // 2)], n_tiles=2, out_dtype=BF16, name="proj_a",
                               act="silu_tail")
    proj_s = functools.partial(_proj, pieces=[(W_IN_TQ, W_IN_GATE - W_IN_TQ), (W_IN_LR, LANES)], n_tiles=1,
                               out_dtype=F32, name="proj_s")
    proj_g = functools.partial(_proj, pieces=[(W_IN_GATE, PG_W // 2)], n_tiles=2, out_dtype=BF16,
                               name="proj_g", act="sigmoid")
    experts = None
    for l in range(DEPTH):
        if l == 0:
            pa, w_eg = proj_a(h, w_t, l, cast=(w_exp_gate, 2))
            ps, w_ed = proj_s(h, w_t, l, cast=(w_exp_down, 4))
            pg, w_eu = proj_g(h, w_t, l, cast=(w_exp_up, 2))
            experts = (w_eg, w_eu, w_ed)
        else:
            pa, ps, pg = proj_a(h, w_t, l), proj_s(h, w_t, l), proj_g(h, w_t, l)
        ps_layers.append(ps)

        wg = wg_all[l]
        bg = gla_b_gate[l].reshape(2, 1, GLA_W)
        ng = row(gla_norm_g[l])
        y_b, states = _gla_ctx(pa, ps, wg, bg, ng, states, l)
        y_b = _gla_lat(pa, ps, wg, bg, ng, state_gla[:, l], y_b)
        y_c = _attn_ctx(ps, attn_sink[l])
        y_c = _attn_lat(ps, attn_sink[l], cache_k[:, l].reshape(DEC_BATCH, PAST_LEN, LANES),
                        cache_v[:, l].reshape(DEC_BATCH, PAST_LEN, LANES), cos_t, sin_t, y_c)

        x1, h2, rg = _merge(pg, pa, y_b, y_c, x, mod[l], conv_w[l], l, w_branch, w_out, row(norm2_g[l]),
                            wr_all[l], br_all[l])
        moe_in = (_dispatch_tables(rg), h2, rg, x1, mod[l], l, *experts)
        if l + 1 < DEPTH:
            x, h = _moe(*moe_in, row(norm1_g[l + 1]), mod[l + 1], 0, T_ALL // TM)
        else:
            gf = row(final_norm_g)
            y_prompt = _moe(*moe_in, gf, None, 0, T_CTX // TM)
            y_sample = _moe(*moe_in, gf, None, T_CTX // TM, T_LAT // TM)

    new_k, new_v = _cache(ps_layers)
    kv_shape = (BATCH, DEPTH, SEQ, ATT_KV_HEADS, HEAD_DIM)
    return (y_prompt.reshape(BATCH, SEQ, D_MODEL), y_sample.reshape(DEC_BATCH, DEC_SEQ, D_MODEL), states,
            new_k.reshape(kv_shape), new_v.reshape(kv_shape))
```

```python
import functools

import jax
import jax.numpy as jnp
import numpy as np
from jax import lax
from jax.experimental import pallas as pl
from jax.experimental.pallas import tpu as pltpu

F32 = jnp.float32
BF16 = jnp.bfloat16

D_MODEL = 1024
BATCH = 32
SEQ = 256
DEPTH = 2
DEC_BATCH = 2
DEC_SEQ = 2048
PAST_LEN = 512
GRID_W = 64
EPS = 1e-6
CONV_W = 512
CONV_K = 3
GLA_HEADS = 4
GLA_DK = 128
GLA_DV = 128
GLA_W = GLA_HEADS * GLA_DV
GLA_RANK = 16
GLA_TAU = 16.0
GLA_CHUNK = 64
ATT_HEADS = 8
ATT_KV_HEADS = 2
ATT_GROUP = ATT_HEADS // ATT_KV_HEADS
HEAD_DIM = 64
ATT_W = ATT_HEADS * HEAD_DIM
WINDOW = 128
BLOCK = 128
ROPE_THETA = 10000.0
N_EXPERT_GROUPS = 4
EXPERTS_PER_GROUP = 4
N_EXPERTS = 16
D_EXPERT = 256
NEG_INF = -1e30

T_CTX = BATCH * SEQ
T_LAT = DEC_BATCH * DEC_SEQ
T_ALL = T_CTX + T_LAT
N_COND = 8
LANES = 128

W_IN_LR = 3584
W_IN_TQ = 3616
W_IN_GATE = 4384
N_IN = 7456
PA_W = 3584
PA_GLA = 1536
PS_W = 896
PS_KV = 512
PS_LR = 768
PG_W = 3 * D_MODEL
ROUTE_E0 = N_EXPERT_GROUPS

TM = 1024
TM_MERGE = 512
MERGE_ROWS = 256
VMEM_LIMIT = 56 * 1024 * 1024


def _cparams(*sem):
    return pltpu.CompilerParams(dimension_semantics=sem, vmem_limit_bytes=VMEM_LIMIT)


def _cond_row(i, tm):
    n_ctx = T_CTX // tm
    per = DEC_SEQ // tm
    return jnp.where(i < n_ctx, 0, 1 + (i - n_ctx) // per)


def _mm(a, b):
    return jnp.dot(a, b, preferred_element_type=F32)


def _dot_t(a, b):
    return lax.dot_general(a, b, (((1,), (1,)), ((), ())), preferred_element_type=F32)


def _dot_ta(a, b):
    return lax.dot_general(a, b, (((0,), (0,)), ((), ())), preferred_element_type=F32)


def _hi_lo(w):
    hi = w.astype(BF16)
    return hi, (w - hi.astype(F32)).astype(BF16)


def _pack_split(w):
    hi, lo = _hi_lo(w)
    return jnp.concatenate([jnp.concatenate([hi, lo], axis=-1),
                            jnp.concatenate([hi, jnp.zeros_like(lo)], axis=-1)], axis=-2)


def _split_dot(a, w_pack):
    a_hi, a_lo = _hi_lo(a)
    both = _mm(jnp.concatenate([a_hi, a_lo], axis=-1), w_pack)
    n = both.shape[-1] // 2
    return both[:, :n] + both[:, n:]


def _rms(x):
    return x * lax.rsqrt(jnp.mean(x * x, axis=-1, keepdims=True) + EPS)


def _mod_norm(x, g, mod, shift_row):
    return _rms(x) * g * (1.0 + mod[shift_row + 1:shift_row + 2, :]) + mod[shift_row:shift_row + 1, :]


def _mod_kernel(c_ref, w_ref, b_ref, o_ref):
    c = c_ref[...]
    s = (c * jax.nn.sigmoid(c)).astype(BF16)
    o_ref[...] = _mm(s, w_ref[...].astype(BF16)) + b_ref[...]


def _modulation(cond, w_ada, b_ada):
    tn = 1536
    return pl.pallas_call(
        _mod_kernel,
        grid=(DEPTH, 6 * D_MODEL // tn),
        in_specs=[
            pl.BlockSpec((N_COND, D_MODEL), lambda l, j: (0, 0)),
            pl.BlockSpec((None, D_MODEL, tn), lambda l, j: (l, 0, j)),
            pl.BlockSpec((None, 1, tn), lambda l, j: (l, 0, j)),
        ],
        out_specs=pl.BlockSpec((None, N_COND, tn), lambda l, j: (l, 0, j)),
        out_shape=jax.ShapeDtypeStruct((DEPTH, N_COND, 6 * D_MODEL), F32),
        compiler_params=_cparams("parallel", "parallel"),
        name="modulation",
    )(cond, w_ada, b_ada.reshape(DEPTH, 1, 6 * D_MODEL))


def _split_x_specs(tm):
    n_ctx = T_CTX // tm
    return [pl.BlockSpec((tm, D_MODEL), lambda i, *_: (jnp.minimum(i, n_ctx - 1), 0)),
            pl.BlockSpec((tm, D_MODEL), lambda i, *_: (jnp.maximum(i - n_ctx, 0), 0))], n_ctx


def _prenorm_kernel(xp_ref, xs_ref, mod_ref, g_ref, h_ref):
    x = jnp.where(pl.program_id(0) < T_CTX // TM, xp_ref[...], xs_ref[...])
    h_ref[...] = _mod_norm(x, g_ref[...], mod_ref[...], 0).astype(BF16)


def _prenorm(xp, xs, mod_l, g):
    x_specs, _ = _split_x_specs(TM)
    return pl.pallas_call(
        _prenorm_kernel,
        grid=(T_ALL // TM,),
        in_specs=x_specs + [
            pl.BlockSpec((None, 6, D_MODEL), lambda i: (_cond_row(i, TM), 0, 0)),
            pl.BlockSpec((1, D_MODEL), lambda i: (0, 0)),
        ],
        out_specs=pl.BlockSpec((TM, D_MODEL), lambda i: (i, 0)),
        out_shape=jax.ShapeDtypeStruct((T_ALL, D_MODEL), BF16),
        compiler_params=_cparams("parallel"),
        name="prenorm",
    )(xp, xs, mod_l, g)


def _proj_kernel(h_ref, *refs, act, n_w):
    w_refs, refs = refs[:n_w], refs[n_w:]
    if len(refs) == 4:
        cast_in_ref, o_ref, cast_out_ref, wb_ref = refs
        cast_out_ref[...] = cast_in_ref[...].astype(BF16)
    else:
        o_ref, wb_ref = refs

    @pl.when(pl.program_id(1) == 0)
    def _():
        col = 0
        for w_ref in w_refs:
            n = w_ref.shape[0]
            wb_ref[:, col:col + n] = w_ref[...].T.astype(BF16)
            col += n

    def sigmoid(t):
        return 0.5 * jnp.tanh(0.5 * t) + 0.5

    y = _mm(h_ref[...], wb_ref[...])
    if act == "sigmoid":
        y = sigmoid(y)
    if act == "silu_tail":
        last = pl.program_id(0) == pl.num_programs(0) - 1
        tail = y[:, -GLA_W:]
        o_ref[:, :-GLA_W] = y[:, :-GLA_W].astype(o_ref.dtype)
        o_ref[:, -GLA_W:] = jnp.where(last, tail * sigmoid(tail), tail).astype(o_ref.dtype)
    else:
        o_ref[...] = y.astype(o_ref.dtype)


def _proj(h, w_t, layer, pieces, n_tiles, out_dtype, name, act=None, cast=None):
    tn = sum(n for _, n in pieces)
    n_rows = T_ALL // TM

    def w_spec(c0, n):
        return pl.BlockSpec((pl.Squeezed(), pl.Element(n), pl.Element(D_MODEL)),
                            lambda j, i: (layer, pl.multiple_of(c0 + j * tn, 8), 0))

    in_specs = [pl.BlockSpec((TM, D_MODEL), lambda j, i: (i, 0))] + [w_spec(c0, n) for c0, n in pieces]
    out_specs = [pl.BlockSpec((TM, tn), lambda j, i: (i, j))]
    out_shape = [jax.ShapeDtypeStruct((T_ALL, n_tiles * tn), out_dtype)]
    args = [h] + [w_t] * len(pieces)
    if cast is not None:
        src, per_block = cast
        per_layer = N_EXPERTS // per_block
        n_blocks = DEPTH * per_layer
        assert n_blocks <= n_tiles * n_rows

        def block(j, i):
            b = jnp.minimum(j * n_rows + i, n_blocks - 1)
            return (b // per_layer, b % per_layer, 0, 0)

        spec = pl.BlockSpec((None, per_block) + src.shape[2:], block)
        in_specs.append(spec)
        out_specs.append(spec)
        out_shape.append(jax.ShapeDtypeStruct(src.shape, BF16))
        args.append(src)
    out = pl.pallas_call(
        functools.partial(_proj_kernel, act=act, n_w=len(pieces)),
        grid=(n_tiles, n_rows),
        in_specs=in_specs,
        out_specs=out_specs,
        out_shape=out_shape,
        scratch_shapes=[pltpu.VMEM((D_MODEL, tn), BF16)],
        compiler_params=_cparams("arbitrary", "arbitrary"),
        name=name,
    )(*args)
    return out[0] if cast is None else out


GLA_GROUP = 256
GLA_CTX_SEQS = 4


def _log_sigmoid(z):
    return jnp.minimum(z, 0.0) - jnp.log(1.0 + jnp.exp(-jnp.abs(z)))


def _split3(x):
    hi = x.astype(BF16)
    r1 = x - hi.astype(F32)
    mid = r1.astype(BF16)
    lo = (r1 - mid.astype(F32)).astype(BF16)
    return hi, mid, lo


def _gla_keep(d):
    ri = lax.broadcasted_iota(jnp.int32, (GLA_GROUP, GLA_GROUP), 0)
    ci = lax.broadcasted_iota(jnp.int32, (GLA_GROUP, GLA_GROUP), 1)
    if d == 0:
        return (ci <= ri) & (ci >= (ri & ~(GLA_CHUNK - 1)))
    return (ci >= ri) & (ci <= (ri | (GLA_CHUNK - 1)))


def _gla_groups(items, keeps):
    C = GLA_CHUNK
    nc = GLA_GROUP // C
    row_chunk = lax.broadcasted_iota(jnp.int32, (GLA_GROUP, GLA_DK), 0) // C
    in_chunk = [row_chunk == c for c in range(nc)]
    tris = [jnp.where(keep, 1.0, 0.0).astype(BF16) for keep in keeps]

    las = []
    for _, lr, wg_pack, bg, _, _ in items:
        lr = jnp.where(lax.broadcasted_iota(jnp.int32, lr.shape, 1) < 2 * GLA_RANK, lr, 0.0)
        las.append(_log_sigmoid(_split_dot(lr, wg_pack) + bg) / GLA_TAU)
    cums = [_mm(tris[item[5]], jnp.concatenate(_split3(la), axis=-1)) for item, la in zip(items, las)]

    ops = []
    for (load, _, _, _, states, d), parts in zip(items, cums):
        width = len(states) * GLA_DK
        b_all = parts[:, :width] + parts[:, width:2 * width] + parts[:, 2 * width:]
        edge = C - 1 if d == 0 else 0
        for h in range(len(states)):
            b = b_all[:, h * GLA_DK:(h + 1) * GLA_DK]
            b_last = [b[c * C + edge:c * C + edge + 1, :] for c in range(nc)]
            bl = jnp.concatenate([jnp.broadcast_to(t, (C, GLA_DK)) for t in b_last], axis=0)
            k = load(1, h)
            q_in = load(0, h) * (GLA_DK ** -0.5) * jnp.exp(b)
            k_in = (k * jnp.exp(-b)).astype(BF16)
            k_end = k * jnp.exp(bl - b)
            k_spread = jnp.concatenate([jnp.where(in_chunk[c], k_end, 0.0) for c in range(nc)], axis=-1)
            q_spread = jnp.concatenate([jnp.where(in_chunk[c], q_in, 0.0) for c in range(nc)], axis=-1)
            ops.append((q_in.astype(BF16), k_in, load(2, h).astype(BF16), k_spread.astype(BF16),
                        q_spread.astype(BF16), [jnp.exp(t) for t in b_last], d))
    att_t = [jnp.where(keeps[1 - op[6]], _dot_t(op[1], op[0]), 0.0).astype(BF16) for op in ops]
    ds_t = [_dot_ta(op[2], op[3]) for op in ops]
    o_t = [_dot_ta(op[2], a) for op, a in zip(ops, att_t)]
    all_states = [st for item in items for st in item[4]]
    new_states, stacked = [], []
    for st, ds, op in zip(all_states, ds_t, ops):
        order = range(nc) if op[6] == 0 else range(nc - 1, -1, -1)
        starts = [None] * nc
        for c in order:
            starts[c] = st
            st = st * op[5][c] + ds[:, c * GLA_DK:(c + 1) * GLA_DK]
        new_states.append(st)
        stacked.append(jnp.concatenate(starts, axis=-1).astype(BF16))
    outs = [o + _dot_t(s, op[4]) for o, s, op in zip(o_t, stacked, ops)]
    results, pos = [], 0
    for item in items:
        n = len(item[4])
        results.append((outs[pos:pos + n], new_states[pos:pos + n]))
        pos += n
    return results


def _gla_loader(qkv_refs, rows):
    def load(i, h):
        return qkv_refs[i][rows, h * GLA_DK:(h + 1) * GLA_DK].astype(F32)
    return load


def _gla_finish(o, r_act, ng):
    o = o * lax.rsqrt(jnp.mean(o * o, axis=-1, keepdims=True) + EPS)
    return o * ng * r_act


def _gla_ctx_kernel(q_ref, k_ref, v_ref, r_ref, lr_ref, wg_ref, bg_ref, ng_ref, *rest):
    y_ref, s_ref = rest[-2:]
    zero = jnp.zeros((GLA_DV, GLA_DK), F32)
    items = []
    for s in range(GLA_CTX_SEQS):
        rows = slice(s * SEQ, (s + 1) * SEQ)
        load = _gla_loader((q_ref, k_ref, v_ref), rows)
        items += [(load, lr_ref[rows, :], wg_ref[d], bg_ref[d], [zero] * GLA_HEADS, d) for d in range(2)]
    results = _gla_groups(items, [_gla_keep(0), _gla_keep(1)])
    for s in range(GLA_CTX_SEQS):
        rows = slice(s * SEQ, (s + 1) * SEQ)
        (o_f, st_f), (o_b, st_b) = results[2 * s], results[2 * s + 1]
        for h in range(GLA_HEADS):
            s_ref[s, 0, h] = st_f[h].T
            s_ref[s, 1, h] = st_b[h].T
            cs = slice(h * GLA_DV, (h + 1) * GLA_DV)
            y = _gla_finish((o_f[h] + o_b[h]).T, r_ref[rows, cs].astype(F32), ng_ref[:, cs])
            y_ref[rows, cs] = y.astype(y_ref.dtype)


def _gla_lat_kernel(q_ref, k_ref, v_ref, r_ref, lr_ref, wg_ref, bg_ref, ng_ref, s0_ref, yin_ref,
                    y_ref, of_ref, ob_ref):
    del yin_ref
    n_groups = DEC_SEQ // GLA_GROUP
    keeps = [_gla_keep(0), _gla_keep(1)]

    def body(g, carry):
        st_f, st_b = carry
        rf = pl.ds(pl.multiple_of(g * GLA_GROUP, GLA_GROUP), GLA_GROUP)
        rb = pl.ds(pl.multiple_of((n_groups - 1 - g) * GLA_GROUP, GLA_GROUP), GLA_GROUP)
        qkv = (q_ref, k_ref, v_ref)
        (o_f, st_f), (o_b, st_b) = _gla_groups(
            [(_gla_loader(qkv, rf), lr_ref[rf, :], wg_ref[0], bg_ref[0], list(st_f), 0),
             (_gla_loader(qkv, rb), lr_ref[rb, :], wg_ref[1], bg_ref[1], list(st_b), 1)], keeps)
        of_ref[rf, :] = jnp.concatenate([o.T for o in o_f], axis=-1)
        ob_ref[rb, :] = jnp.concatenate([o.T for o in o_b], axis=-1)
        return tuple(st_f), tuple(st_b)

    init = tuple(tuple(s0_ref[d, h].T for h in range(GLA_HEADS)) for d in range(2))
    lax.fori_loop(0, n_groups, body, init)
    for h in range(GLA_HEADS):
        cs = slice(h * GLA_DV, (h + 1) * GLA_DV)
        y = _gla_finish(of_ref[:, cs] + ob_ref[:, cs], r_ref[:, cs].astype(F32), ng_ref[:, cs])
        y_ref[:, cs] = y.astype(y_ref.dtype)


def _gla_ctx(pa, ps, wg, bg, ng, states_prev, layer):
    cb = PA_GLA // GLA_W
    rows = GLA_CTX_SEQS * SEQ
    const = lambda shape: pl.BlockSpec(shape, lambda s: (0,) * len(shape))
    in_specs = [pl.BlockSpec((rows, GLA_W), lambda s, j=j: (s, cb + j)) for j in range(4)] + [
        pl.BlockSpec((rows, LANES), lambda s: (s, PS_LR // LANES)),
        const((2, 2 * LANES, 2 * GLA_W)), const((2, 1, GLA_W)), const((1, GLA_W))]
    args = [pa, pa, pa, pa, ps, wg, bg, ng]
    aliases = {}
    if states_prev is not None:
        in_specs.append(pl.BlockSpec(memory_space=pl.ANY))
        args.append(states_prev)
        aliases = {len(args) - 1: 1}
    return pl.pallas_call(
        _gla_ctx_kernel,
        grid=(BATCH // GLA_CTX_SEQS,),
        in_specs=in_specs,
        out_specs=[pl.BlockSpec((rows, GLA_W), lambda s: (s, 0)),
                   pl.BlockSpec((GLA_CTX_SEQS, None, 2, GLA_HEADS, GLA_DK, GLA_DV),
                                lambda s: (s, layer, 0, 0, 0, 0))],
        out_shape=[jax.ShapeDtypeStruct((T_ALL, GLA_W), BF16),
                   jax.ShapeDtypeStruct((BATCH, DEPTH, 2, GLA_HEADS, GLA_DK, GLA_DV), F32)],
        input_output_aliases=aliases,
        compiler_params=_cparams("parallel"),
        name="gla_ctx",
    )(*args)


def _gla_lat(pa, ps, wg, bg, ng, s0, y):
    rb0 = T_CTX // DEC_SEQ
    cb = PA_GLA // GLA_W
    const = lambda shape: pl.BlockSpec(shape, lambda s: (0,) * len(shape))
    return pl.pallas_call(
        _gla_lat_kernel,
        grid=(DEC_BATCH,),
        in_specs=[pl.BlockSpec((DEC_SEQ, GLA_W), lambda s, j=j: (rb0 + s, cb + j)) for j in range(4)] + [
            pl.BlockSpec((DEC_SEQ, LANES), lambda s: (rb0 + s, PS_LR // LANES)),
            const((2, 2 * LANES, 2 * GLA_W)), const((2, 1, GLA_W)), const((1, GLA_W)),
            pl.BlockSpec((None, 2, GLA_HEADS, GLA_DK, GLA_DV), lambda s: (s, 0, 0, 0, 0)),
            pl.BlockSpec(memory_space=pl.ANY)],
        out_specs=pl.BlockSpec((DEC_SEQ, GLA_W), lambda s: (rb0 + s, 0)),
        out_shape=jax.ShapeDtypeStruct((T_ALL, GLA_W), BF16),
        scratch_shapes=[pltpu.VMEM((DEC_SEQ, GLA_W), F32), pltpu.VMEM((DEC_SEQ, GLA_W), F32)],
        input_output_aliases={9: 0},
        compiler_params=_cparams("parallel"),
        name="gla_lat",
    )(pa, pa, pa, pa, ps, wg, bg, ng, s0, y)


LOG2E = 1.4426950408889634
Q_SCALE = HEAD_DIM ** -0.5 * LOG2E


def _attend_t(problems):
    scores = []
    for q, _, k_parts, _, masks in problems:
        parts = [_dot_t(k, q) for k in k_parts]
        scores.append([s if mask is None else jnp.where(mask, s, NEG_INF) for s, mask in zip(parts, masks)])
    exps = []
    for (q, sink, *_), parts in zip(problems, scores):
        sink2 = sink * LOG2E
        m = jnp.broadcast_to(sink2, (1, q.shape[0])).astype(F32)
        for s in parts:
            m = jnp.maximum(m, jnp.max(s, axis=0, keepdims=True))
        es = [jnp.exp2(s - m) for s in parts]
        den = jnp.exp2(sink2 - m)
        for e in es:
            den = den + jnp.sum(e, axis=0, keepdims=True)
        exps.append(([e.astype(BF16) for e in es], den))
    outs = []
    for (_, _, _, vt_parts, _), (es, den) in zip(problems, exps):
        o = None
        for e, vt in zip(es, vt_parts):
            pv = _mm(vt, e)
            o = pv if o is None else o + pv
        outs.append(o / den)
    return outs


def _lane_halves(block):
    low = lax.broadcasted_iota(jnp.int32, block.shape, 1) < HEAD_DIM
    swapped = pltpu.roll(block, HEAD_DIM, 1)
    zero = jnp.zeros_like(block)
    pick = lambda cond, a: jnp.where(cond, a, zero).astype(BF16)
    return [[pick(low, block), pick(~low, swapped)], [pick(low, swapped), pick(~low, block)]]


ATT_CTX_SEQS = 4


def _attn_ctx_kernel(sink_ref, q_ref, kv_ref, *rest):
    o_ref = rest[-1]
    problems = []
    for s in range(ATT_CTX_SEQS):
        rows = slice(s * SEQ, (s + 1) * SEQ)
        k_halves = _lane_halves(kv_ref[rows, :LANES])
        vt = kv_ref[rows, LANES:].T.astype(BF16)
        for pair in range(ATT_HEADS // 2):
            kv = 2 * pair // ATT_GROUP
            q_pair = (q_ref[rows, pair * LANES:(pair + 1) * LANES] * Q_SCALE).astype(BF16)
            problems += [(q_pair, sink_ref[2 * pair + j], [k_halves[kv][j]],
                          [vt[kv * HEAD_DIM:(kv + 1) * HEAD_DIM]], [None]) for j in range(2)]
    o_t = _attend_t(problems)
    for s in range(ATT_CTX_SEQS):
        outs = [jnp.concatenate(o_t[ATT_HEADS * s + 2 * pair:ATT_HEADS * s + 2 * pair + 2], axis=0).T
                for pair in range(ATT_HEADS // 2)]
        o_ref[s * SEQ:(s + 1) * SEQ, :] = jnp.concatenate(outs, axis=-1).astype(o_ref.dtype)


def _attn_ctx(ps, sink):
    return pl.pallas_call(
        _attn_ctx_kernel,
        grid=(BATCH // ATT_CTX_SEQS,),
        in_specs=[
            pl.BlockSpec(memory_space=pltpu.SMEM),
            pl.BlockSpec((ATT_CTX_SEQS * SEQ, ATT_W), lambda s: (s, 0)),
            pl.BlockSpec((ATT_CTX_SEQS * SEQ, 2 * LANES), lambda s: (s, PS_KV // (2 * LANES))),
        ],
        out_specs=pl.BlockSpec((ATT_CTX_SEQS * SEQ, ATT_W), lambda s: (s, 0)),
        out_shape=jax.ShapeDtypeStruct((T_ALL, ATT_W), BF16),
        compiler_params=_cparams("parallel"),
        name="attn_ctx",
    )(sink, ps, ps)


def _rope(x, cos, sin_signed):
    lane = lax.broadcasted_iota(jnp.int32, x.shape, 1)
    partner = jnp.where((lane & 31) < 16, pltpu.roll(x, LANES - 16, 1), pltpu.roll(x, 16, 1))
    return x * cos + partner * sin_signed


LAT_QBLOCKS = 4


def _attn_lat_kernel(sink_ref, q_ref, kvp_ref, kvc_ref, kvn_ref, kctx_ref, vctx_ref, cos_ref, sin_ref, yin_ref,
                     o_ref):
    del yin_ref
    n_blk = DEC_SEQ // BLOCK
    first = pl.program_id(1) * LAT_QBLOCKS

    def table(ref, blk, n_rows=BLOCK):
        blk = jnp.clip(blk, 0, n_blk - 1)
        return ref[pl.ds(pl.multiple_of(blk * BLOCK, BLOCK), n_rows), :]

    cur = LAT_QBLOCKS * BLOCK
    k_loc = jnp.concatenate(
        [_rope(kvp_ref[:, :LANES], table(cos_ref, first - 1), table(sin_ref, first - 1)),
         _rope(kvc_ref[:, :LANES], table(cos_ref, first, cur), table(sin_ref, first, cur)),
         _rope(kvn_ref[:, :LANES], table(cos_ref, first + LAT_QBLOCKS), table(sin_ref, first + LAT_QBLOCKS))],
        axis=0)
    v_loc = jnp.concatenate([kvp_ref[:, LANES:], kvc_ref[:, LANES:], kvn_ref[:, LANES:]], axis=0)
    kl_halves = _lane_halves(k_loc)
    kc_halves = _lane_halves(kctx_ref[...])
    vt_loc = v_loc.T.astype(BF16)
    vt_ctx = vctx_ref[...].T.astype(BF16)

    q_pairs = [(_rope(q_ref[:, p * LANES:(p + 1) * LANES], table(cos_ref, first, cur), table(sin_ref, first, cur))
                * Q_SCALE).astype(BF16) for p in range(ATT_HEADS // 2)]

    pairs_per_kv = ATT_GROUP // 2
    span = 3 * BLOCK
    shape_t = (span, pairs_per_kv * BLOCK)
    problems = []
    for i in range(LAT_QBLOCKS):
        start = (first + i) * BLOCK
        kpos = start - WINDOW + lax.broadcasted_iota(jnp.int32, shape_t, 0)
        qpos = start + (lax.broadcasted_iota(jnp.int32, shape_t, 1) & (BLOCK - 1))
        valid_t = (jnp.abs(qpos - kpos) <= WINDOW) & (kpos >= 0) & (kpos < DEC_SEQ)
        near = slice(i * BLOCK, i * BLOCK + span)
        for kv in range(ATT_KV_HEADS):
            pairs = range(kv * pairs_per_kv, (kv + 1) * pairs_per_kv)
            q = jnp.concatenate([q_pairs[p][i * BLOCK:(i + 1) * BLOCK] for p in pairs], axis=0)
            rows = slice(kv * HEAD_DIM, (kv + 1) * HEAD_DIM)
            for j in range(2):
                sink = jnp.concatenate([jnp.full((1, BLOCK), sink_ref[2 * p + j], F32) for p in pairs], axis=1)
                problems.append((q, sink, [kc_halves[kv][j], kl_halves[kv][j][near]],
                                 [vt_ctx[rows], vt_loc[rows, near]], [None, valid_t]))
    o_t = _attend_t(problems)
    for i in range(LAT_QBLOCKS):
        outs = []
        for p in range(ATT_HEADS // 2):
            kv, c = divmod(p, pairs_per_kv)
            cols = slice(c * BLOCK, (c + 1) * BLOCK)
            halves = [o_t[(i * ATT_KV_HEADS + kv) * 2 + j][:, cols] for j in range(2)]
            outs.append(jnp.concatenate(halves, axis=0).T)
        o_ref[i * BLOCK:(i + 1) * BLOCK, :] = jnp.concatenate(outs, axis=-1).astype(o_ref.dtype)


def _attn_lat(ps, sink, k_ctx, v_ctx, cos_t, sin_t, y):
    n_blk = DEC_SEQ // BLOCK
    n_steps = n_blk // LAT_QBLOCKS
    rb0 = T_CTX // BLOCK
    kv_col = PS_KV // (2 * LANES)
    cur = LAT_QBLOCKS * BLOCK

    def edge_spec(off):
        return pl.BlockSpec(
            (BLOCK, 2 * LANES),
            lambda b, n: (rb0 + b * n_blk + jnp.clip(n * LAT_QBLOCKS + off, 0, n_blk - 1), kv_col))

    step_rows = lambda b, n: rb0 // LAT_QBLOCKS + b * n_steps + n
    return pl.pallas_call(
        _attn_lat_kernel,
        grid=(DEC_BATCH, n_steps),
        in_specs=[
            pl.BlockSpec(memory_space=pltpu.SMEM),
            pl.BlockSpec((cur, ATT_W), lambda b, n: (step_rows(b, n), 0)),
            edge_spec(-1),
            pl.BlockSpec((cur, 2 * LANES), lambda b, n: (step_rows(b, n), kv_col)),
            edge_spec(LAT_QBLOCKS),
            pl.BlockSpec((None, PAST_LEN, LANES), lambda b, n: (b, 0, 0)),
            pl.BlockSpec((None, PAST_LEN, LANES), lambda b, n: (b, 0, 0)),
            pl.BlockSpec((DEC_SEQ, LANES), lambda b, n: (0, 0)),
            pl.BlockSpec((DEC_SEQ, LANES), lambda b, n: (0, 0)),
            pl.BlockSpec(memory_space=pl.ANY),
        ],
        out_specs=pl.BlockSpec((cur, ATT_W), lambda b, n: (step_rows(b, n), 0)),
        out_shape=jax.ShapeDtypeStruct((T_ALL, ATT_W), BF16),
        input_output_aliases={9: 0},
        compiler_params=_cparams("parallel", "parallel"),
        name="attn_lat",
    )(sink, ps, ps, ps, ps, k_ctx, v_ctx, cos_t, sin_t, y)


def _rope_tables():
    pos = np.arange(DEC_SEQ)
    n_freq = HEAD_DIM // 4
    inv = jnp.asarray(ROPE_THETA, F32) ** (-jnp.arange(n_freq, dtype=F32) / n_freq)
    row = jnp.asarray(pos // GRID_W, F32)
    colp = jnp.asarray(pos % GRID_W, F32)
    ang_r = row[:, None] * inv[None, :]
    ang_c = colp[:, None] * inv[None, :]
    cos = jnp.concatenate([jnp.cos(ang_r)] * 2 + [jnp.cos(ang_c)] * 2, axis=-1)
    sin = jnp.concatenate([-jnp.sin(ang_r), jnp.sin(ang_r), -jnp.sin(ang_c), jnp.sin(ang_c)], axis=-1)
    return jnp.tile(cos, (1, 2)), jnp.tile(sin, (1, 2))


ROUTE_ROWS = 24


def _route_t(logits_t):
    row_i = lax.broadcasted_iota(jnp.int32, logits_t.shape, 0)
    row = row_i.astype(F32)
    big = jnp.float32(1 << 20)
    is_g = row_i < N_EXPERT_GROUPS
    lg = jnp.where(is_g, logits_t, -jnp.inf)
    m_g = jnp.max(lg, axis=0, keepdims=True)
    grp = jnp.min(jnp.where(lg == m_g, row, big), axis=0, keepdims=True)
    z_g = jnp.sum(jnp.where(is_g, jnp.exp(lg - m_g), 0.0), axis=0, keepdims=True)
    p_grp = 1.0 / z_g

    e_idx = row_i - ROUTE_E0
    e_grp = (e_idx >> 2).astype(F32)
    sel = (e_idx >= 0) & (e_idx < N_EXPERTS) & (e_grp == grp)
    le = jnp.where(sel, logits_t, -jnp.inf)
    m_e = jnp.max(le, axis=0, keepdims=True)
    ex = jnp.where(sel, jnp.exp(le - m_e), 0.0)
    pe = ex / jnp.sum(ex, axis=0, keepdims=True)
    pe = jnp.where(sel, pe, -1.0)
    v1 = jnp.max(pe, axis=0, keepdims=True)
    i1 = jnp.min(jnp.where(pe == v1, row, big), axis=0, keepdims=True)
    pe2 = jnp.where(row == i1, -1.0, pe)
    v2 = jnp.max(pe2, axis=0, keepdims=True)
    i2 = jnp.min(jnp.where(pe2 == v2, row, big), axis=0, keepdims=True)
    tot = v1 + v2
    gates = jnp.where(row == i1, p_grp * (v1 / tot), 0.0) + jnp.where(row == i2, p_grp * (v2 / tot), 0.0)
    return gates, grp


HALO = 16


def _merge_kernel(gate_ref, conv_ref, cprev_ref, cnext_ref, yb_ref, yc_ref, *rest, x_split):
    x_refs, rest = rest[:2 if x_split else 1], rest[2 if x_split else 1:]
    (mod_ref, cw_ref, wb_ref, wo_ref, g2_ref, wr_ref, br_ref,
     x1_ref, h2_ref, rg_ref, wbb_ref, wob_ref) = rest
    tm = TM_MERGE
    i = pl.program_id(0)

    def x_rows(rows):
        if not x_split:
            return x_refs[0][rows, :]
        return jnp.where(i < x_split, x_refs[0][rows, :], x_refs[1][rows, :])

    @pl.when(i == 0)
    def _():
        wbb_ref[...] = wb_ref[...].astype(BF16)
        wob_ref[...] = wo_ref[...].astype(BF16)

    def gated(ref):
        return ref[:, CONV_W:2 * CONV_W].astype(F32) * ref[:, 2 * CONV_W:3 * CONV_W].astype(F32)

    a_b = conv_ref[:, 0:CONV_W].astype(F32)
    u = gated(conv_ref)
    u_before = gated(cprev_ref)[HALO - 1:HALO]
    u_after = gated(cnext_ref)[0:1]
    r = lax.broadcasted_iota(jnp.int32, (tm, 1), 0)
    g_row = i * tm + r
    seq_mask = jnp.where(g_row < T_CTX, SEQ - 1, DEC_SEQ - 1)
    first = (g_row & seq_mask) == 0
    last = ((g_row + 1) & seq_mask) == 0
    u_prev = jnp.where(r == 0, u_before, pltpu.roll(u, 1, 0))
    u_next = jnp.where(r == tm - 1, u_after, pltpu.roll(u, tm - 1, 0))
    u_prev = jnp.where(first, 0.0, u_prev)
    u_next = jnp.where(last, 0.0, u_next)
    y_a = (a_b * (u_prev * cw_ref[0:1, :] + u * cw_ref[1:2, :] + u_next * cw_ref[2:3, :])).astype(BF16)

    blocks = [slice(r0, r0 + MERGE_ROWS) for r0 in range(0, tm, MERGE_ROWS)]
    ys = [(y_a[rows], yb_ref[rows, :], yc_ref[rows, :]) for rows in blocks]
    branches = [[_mm(y, wbb_ref[j]) for j, y in enumerate(y3)] for y3 in ys]
    zs = []
    for rows, br3 in zip(blocks, branches):
        z = sum(gate_ref[rows, j * D_MODEL:(j + 1) * D_MODEL].astype(F32) * br3[j] for j in range(3))
        zs.append(z.astype(BF16))
    outs = [_mm(z, wob_ref[...]) for z in zs]
    h2s = []
    for rows, o in zip(blocks, outs):
        x1 = x_rows(rows) + mod_ref[2:3, :] * o
        x1_ref[rows, :] = x1
        h2 = _mod_norm(x1, g2_ref[...], mod_ref[...], 3)
        h2_ref[rows, :] = h2.astype(BF16)
        h2s.append(h2)
    packed = [_dot_t(wr_ref[...], jnp.concatenate(_hi_lo(h2), axis=-1)) for h2 in h2s]
    for rows, both in zip(blocks, packed):
        logits_t = (both[:LANES] + both[LANES:] + br_ref[...])[:ROUTE_ROWS]
        gates, grp = _route_t(logits_t)
        last = lax.broadcasted_iota(jnp.int32, (8, MERGE_ROWS), 0) == 7
        rg_t = jnp.concatenate([gates, jnp.zeros((LANES - ROUTE_ROWS - 8, MERGE_ROWS), F32),
                                jnp.where(last, grp, 0.0)], axis=0)
        rg_ref[rows, :] = rg_t.T


def _merge(pg, pa, y_b, y_c, x, mod_l, conv_w, layer, wb, wo, g2, wr, br):
    tm = TM_MERGE
    n_tiles = T_ALL // tm
    hb = tm // HALO
    const = lambda shape: pl.BlockSpec(shape, lambda i: (0,) * len(shape))
    if isinstance(x, tuple):
        x_specs, x_split = _split_x_specs(tm)
    else:
        x, x_specs, x_split = (x,), [pl.BlockSpec((tm, D_MODEL), lambda i: (i, 0))], 0
    return pl.pallas_call(
        functools.partial(_merge_kernel, x_split=x_split),
        grid=(n_tiles,),
        in_specs=[
            pl.BlockSpec((tm, PG_W), lambda i: (i, 0)),
            pl.BlockSpec((tm, 3 * CONV_W), lambda i: (i, 0)),
            pl.BlockSpec((HALO, 3 * CONV_W), lambda i: (jnp.maximum(i * hb - 1, 0), 0)),
            pl.BlockSpec((HALO, 3 * CONV_W), lambda i: (jnp.minimum((i + 1) * hb, n_tiles * hb - 1), 0)),
            pl.BlockSpec((tm, GLA_W), lambda i: (i, 0)),
            pl.BlockSpec((tm, ATT_W), lambda i: (i, 0)),
            *x_specs,
            pl.BlockSpec((None, 6, D_MODEL), lambda i: (_cond_row(i, tm), 0, 0)),
            const((CONV_K, CONV_W)),
            pl.BlockSpec((None, 3, 512, D_MODEL), lambda i: (layer, 0, 0, 0)),
            pl.BlockSpec((None, D_MODEL, D_MODEL), lambda i: (layer, 0, 0)),
            const((1, D_MODEL)),
            const((2 * LANES, 2 * D_MODEL)),
            const((LANES, 1)),
        ],
        out_specs=[
            pl.BlockSpec((tm, D_MODEL), lambda i: (i, 0)),
            pl.BlockSpec((tm, D_MODEL), lambda i: (i, 0)),
            pl.BlockSpec((tm, LANES), lambda i: (i, 0)),
        ],
        out_shape=[
            jax.ShapeDtypeStruct((T_ALL, D_MODEL), F32),
            jax.ShapeDtypeStruct((T_ALL, D_MODEL), BF16),
            jax.ShapeDtypeStruct((T_ALL, LANES), F32),
        ],
        scratch_shapes=[pltpu.VMEM((3, 512, D_MODEL), BF16), pltpu.VMEM((D_MODEL, D_MODEL), BF16)],
        compiler_params=_cparams("arbitrary"),
        name="merge",
    )(pg, pa, pa, pa, y_b, y_c, *x, mod_l, conv_w, wb, wo, g2, wr, br)


SUB = 128
MOE_ROWS = 256


def _dispatch_tables(rg):
    n_tiles = T_ALL // TM
    grp = rg[:, LANES - 1].astype(jnp.int32).reshape(n_tiles, TM)
    hot = grp[..., None] == jnp.arange(N_EXPERT_GROUPS, dtype=jnp.int32)
    onehot = hot.astype(jnp.int32)
    cnt = onehot.sum(axis=1)
    start = jnp.cumsum(cnt, axis=1) - cnt
    first = start // SUB
    nblk = jnp.where(cnt > 0, (start + cnt + SUB - 1) // SUB - first, 0)
    before = jnp.tril(jnp.ones((TM, TM), BF16), -1)
    rank = jnp.einsum("ts,nsg->ntg", before, hot.astype(BF16), preferred_element_type=F32).astype(jnp.int32)
    slot = ((start[:, None, :] + rank) * onehot).sum(axis=-1)
    return (first.reshape(-1), nblk.reshape(-1), slot.reshape(n_tiles, 1, TM), slot.reshape(T_ALL, 1))


def _moe_kernel(bstart_ref, nblk_ref, h_ref, rg_ref, srow_ref, scol_ref, x1_ref, mod_ref, wg_ref, wu_ref,
                wd_ref, g_ref, *rest, final, tile0):
    if final:
        y_ref, hs_ref, gs_ref, os_ref = rest
    else:
        modn_ref, x2_ref, hn_ref, hs_ref, gs_ref, os_ref = rest
    i = pl.program_id(0)
    grp = pl.program_id(1)

    @pl.when(grp == 0)
    def _():
        slot_of_token = srow_ref[...]
        lane = lax.broadcasted_iota(jnp.int32, (TM, LANES), 1)
        hi, mid, lo = _split3(rg_ref[...])
        packed = jnp.where(lane < N_EXPERTS, pltpu.roll(hi.astype(F32), LANES - ROUTE_E0, 1),
                           jnp.where(lane < 2 * N_EXPERTS, pltpu.roll(mid.astype(F32), N_EXPERTS - ROUTE_E0, 1),
                                     pltpu.roll(lo.astype(F32), 2 * N_EXPERTS - ROUTE_E0, 1)))
        packed = jnp.where(lane < 3 * N_EXPERTS, packed, 0.0).astype(BF16)
        payload = jnp.concatenate([h_ref[...], packed], axis=-1)
        for r0 in range(0, TM, MOE_ROWS):
            rows = slice(r0, r0 + MOE_ROWS)
            slot = r0 + lax.broadcasted_iota(jnp.int32, (MOE_ROWS, TM), 0)
            pm = jnp.where(slot == slot_of_token, 1.0, 0.0).astype(BF16)
            moved = _mm(pm, payload)
            hs_ref[rows, :] = moved[:, :D_MODEL].astype(BF16)
            parts = moved[:, D_MODEL:]
            gs_ref[rows, :] = (parts + pltpu.roll(parts, LANES - N_EXPERTS, 1)
                               + pltpu.roll(parts, LANES - 2 * N_EXPERTS, 1))
        os_ref[...] = jnp.zeros_like(os_ref)

    k = (tile0 + i) * N_EXPERT_GROUPS + grp
    first = bstart_ref[k]
    n_blocks = nblk_ref[k]

    def experts(block0, n_sub):
        n_rows = n_sub * SUB
        rows = pl.ds(pl.multiple_of(block0 * SUB, SUB), n_rows)
        lane = lax.broadcasted_iota(jnp.int32, (n_rows, LANES), 1)
        x = hs_ref[rows, :]
        gates = gs_ref[rows, :]
        hidden = []
        for e in range(EXPERTS_PER_GROUP):
            ge = _mm(x, wg_ref[e])
            ue = _mm(x, wu_ref[e])
            w = jnp.sum(jnp.where(lane == grp * EXPERTS_PER_GROUP + e, gates, 0.0),
                        axis=-1, keepdims=True)
            hidden.append(((ge * jax.nn.sigmoid(ge)) * ue * w).astype(BF16))
        acc = _mm(jnp.concatenate(hidden, axis=-1), wd_ref[...].reshape(EXPERTS_PER_GROUP * D_EXPERT, D_MODEL))
        os_ref[rows, :] = (os_ref[rows, :].astype(F32) + acc).astype(BF16)

    odd = n_blocks % 2 == 1
    n_pairs = jnp.where(odd & (n_blocks >= 3), (n_blocks - 3) // 2, n_blocks // 2)

    def pair(p, carry):
        experts(first + 2 * p, 2)
        return carry

    lax.fori_loop(0, n_pairs, pair, 0)

    @pl.when(odd & (n_blocks >= 3))
    def _():
        experts(first + n_blocks - 3, 3)

    @pl.when(n_blocks == 1)
    def _():
        experts(first, 1)

    @pl.when(grp == N_EXPERT_GROUPS - 1)
    def _():
        sorted_out = os_ref[...]
        lane_slot = lax.broadcasted_iota(jnp.int32, (MOE_ROWS, TM), 1)
        for r0 in range(0, TM, MOE_ROWS):
            rows = slice(r0, r0 + MOE_ROWS)
            pt = jnp.where(lane_slot == scol_ref[rows, :], 1.0, 0.0).astype(BF16)
            x2 = x1_ref[rows, :] + mod_ref[5:6, :] * _mm(pt, sorted_out)
            if final:
                y_ref[rows, :] = _rms(x2) * g_ref[...]
            else:
                x2_ref[rows, :] = x2
                hn_ref[rows, :] = _mod_norm(x2, g_ref[...], modn_ref[...], 0).astype(BF16)


def _moe(tables, h2, rg, x1, mod_l, layer, w_eg, w_eu, w_ed, g, mod_next, tile0, n_tiles):
    bstart, nblk, slot_row, slot_col = tables
    final = mod_next is None
    row = lambda i, e, *_: (tile0 + i, 0)
    mod_spec = lambda: pl.BlockSpec((None, 6, D_MODEL), lambda i, e, *_: (_cond_row(tile0 + i, TM), 0, 0))
    group = lambda i, e, *_: (layer, e, 0, 0)
    in_specs = [
        pl.BlockSpec((TM, D_MODEL), row),
        pl.BlockSpec((TM, LANES), row),
        pl.BlockSpec((None, 1, TM), lambda i, e, *_: (tile0 + i, 0, 0)),
        pl.BlockSpec((TM, 1), row),
        pl.BlockSpec((TM, D_MODEL), row),
        mod_spec(),
        pl.BlockSpec((None, EXPERTS_PER_GROUP, D_MODEL, D_EXPERT), group),
        pl.BlockSpec((None, EXPERTS_PER_GROUP, D_MODEL, D_EXPERT), group),
        pl.BlockSpec((None, EXPERTS_PER_GROUP, D_EXPERT, D_MODEL), group),
        pl.BlockSpec((1, D_MODEL), lambda i, e, *_: (0, 0)),
    ]
    args = [h2, rg, slot_row, slot_col, x1, mod_l, w_eg, w_eu, w_ed, g]
    if final:
        out_specs = pl.BlockSpec((TM, D_MODEL), lambda i, e, *_: (i, 0))
        out_shape = jax.ShapeDtypeStruct((n_tiles * TM, D_MODEL), F32)
    else:
        in_specs.append(mod_spec())
        args.append(mod_next)
        out_specs = [pl.BlockSpec((TM, D_MODEL), row)] * 2
        out_shape = [jax.ShapeDtypeStruct((T_ALL, D_MODEL), F32), jax.ShapeDtypeStruct((T_ALL, D_MODEL), BF16)]
    return pl.pallas_call(
        functools.partial(_moe_kernel, final=final, tile0=tile0),
        grid_spec=pltpu.PrefetchScalarGridSpec(
            num_scalar_prefetch=2,
            grid=(n_tiles, N_EXPERT_GROUPS),
            in_specs=in_specs,
            out_specs=out_specs,
            scratch_shapes=[pltpu.VMEM((TM, D_MODEL), BF16), pltpu.VMEM((TM, LANES), F32),
                            pltpu.VMEM((TM, D_MODEL), BF16)],
        ),
        out_shape=out_shape,
        compiler_params=_cparams("parallel", "arbitrary"),
        name="moe_final" if final else "moe",
    )(bstart, nblk, *args)


def _cache_kernel(*refs):
    ps_refs, (k_ref, v_ref) = refs[:DEPTH], refs[DEPTH:]
    n_seq = TM // SEQ
    for l, ps_ref in enumerate(ps_refs):
        k_ref[:, l] = ps_ref[:, :LANES].reshape(n_seq, SEQ, LANES)
        v_ref[:, l] = ps_ref[:, LANES:].reshape(n_seq, SEQ, LANES)


def _cache(ps_layers):
    n_seq = TM // SEQ
    out = pl.BlockSpec((n_seq, DEPTH, SEQ, LANES), lambda i: (i, 0, 0, 0))
    return pl.pallas_call(
        _cache_kernel,
        grid=(T_CTX // TM,),
        in_specs=[pl.BlockSpec((TM, 2 * LANES), lambda i: (i, PS_KV // (2 * LANES)))] * DEPTH,
        out_specs=[out, out],
        out_shape=[jax.ShapeDtypeStruct((BATCH, DEPTH, SEQ, LANES), F32)] * 2,
        compiler_params=_cparams("parallel"),
        name="cache",
    )(*ps_layers)


def _prep_gla_gate(w_gate):
    pads = [((0, 0), (d * GLA_RANK, LANES - (d + 1) * GLA_RANK), (0, 0)) for d in range(2)]
    return _pack_split(jnp.stack([jnp.pad(w_gate[:, d], pads[d]) for d in range(2)], axis=1))


def _prep_router(w_rg, b_rg, w_re, b_re):
    unused = LANES - N_EXPERT_GROUPS - N_EXPERTS
    w = jnp.concatenate([w_rg, w_re, jnp.zeros((DEPTH, D_MODEL, unused), F32)], axis=-1)
    b = jnp.concatenate([b_rg, b_re, jnp.zeros((DEPTH, unused), F32)], axis=-1)
    return jnp.swapaxes(_pack_split(w), -1, -2), b.reshape(DEPTH, LANES, 1)


def kernel(x_prompt, x_sample, state_gla, cache_k, cache_v, c, c_ctx, w_ada, b_ada, norm1_g, norm2_g, w_in,
           conv_w, gla_w_gate, gla_b_gate, gla_norm_g, attn_sink, w_branch, w_out, w_route_group,
           b_route_group, w_route_expert, b_route_expert, w_exp_gate, w_exp_up, w_exp_down, final_norm_g):
    cond = jnp.zeros((N_COND, D_MODEL), F32).at[0].set(c_ctx).at[1:1 + DEC_BATCH].set(c)
    mod = _modulation(cond, w_ada, b_ada).reshape(DEPTH, N_COND, 6, D_MODEL)
    cos_t, sin_t = _rope_tables()
    row = lambda v: v.reshape(1, -1)

    x = (x_prompt.reshape(T_CTX, D_MODEL), x_sample.reshape(T_LAT, D_MODEL))
    h = _prenorm(*x, mod[0], row(norm1_g[0]))
    wg_all = _prep_gla_gate(gla_w_gate)
    wr_all, br_all = _prep_router(w_route_group, b_route_group, w_route_expert, b_route_expert)
    states = None
    ps_layers = []
    y_prompt = y_sample = None
    w_t = jnp.swapaxes(w_in, 1, 2)
    proj_a = functools.partial(_proj, pieces=[(0, PA_W // 2)], n_tiles=2, out_dtype=BF16, name="proj_a",
                               act="silu_tail")
    proj_s = functools.partial(_proj, pieces=[(W_IN_TQ, W_IN_GATE - W_IN_TQ), (W_IN_LR, LANES)], n_tiles=1,
                               out_dtype=F32, name="proj_s")
    proj_g = functools.partial(_proj, pieces=[(W_IN_GATE, PG_W // 2)], n_tiles=2, out_dtype=BF16,
                               name="proj_g", act="sigmoid")
    experts = None
    for l in range(DEPTH):
        if l == 0:
            pa, w_eg = proj_a(h, w_t, l, cast=(w_exp_gate, 2))
            ps, w_ed = proj_s(h, w_t, l, cast=(w_exp_down, 4))
            pg, w_eu = proj_g(h, w_t, l, cast=(w_exp_up, 2))
            experts = (w_eg, w_eu, w_ed)
        else:
            pa, ps, pg = proj_a(h, w_t, l), proj_s(h, w_t, l), proj_g(h, w_t, l)
        ps_layers.append(ps)

        wg = wg_all[l]
        bg = gla_b_gate[l].reshape(2, 1, GLA_W)
        ng = row(gla_norm_g[l])
        y_b, states = _gla_ctx(pa, ps, wg, bg, ng, states, l)
        y_b = _gla_lat(pa, ps, wg, bg, ng, state_gla[:, l], y_b)
        y_c = _attn_ctx(ps, attn_sink[l])
        y_c = _attn_lat(ps, attn_sink[l], cache_k[:, l].reshape(DEC_BATCH, PAST_LEN, LANES),
                        cache_v[:, l].reshape(DEC_BATCH, PAST_LEN, LANES), cos_t, sin_t, y_c)

        x1, h2, rg = _merge(pg, pa, y_b, y_c, x, mod[l], conv_w[l], l, w_branch, w_out, row(norm2_g[l]),
                            wr_all[l], br_all[l])
        moe_in = (_dispatch_tables(rg), h2, rg, x1, mod[l], l, *experts)
        if l + 1 < DEPTH:
            x, h = _moe(*moe_in, row(norm1_g[l + 1]), mod[l + 1], 0, T_ALL // TM)
        else:
            gf = row(final_norm_g)
            y_prompt = _moe(*moe_in, gf, None, 0, T_CTX // TM)
            y_sample = _moe(*moe_in, gf, None, T_CTX // TM, T_LAT // TM)

    new_k, new_v = _cache(ps_layers)
    kv_shape = (BATCH, DEPTH, SEQ, ATT_KV_HEADS, HEAD_DIM)
    return (y_prompt.reshape(BATCH, SEQ, D_MODEL), y_sample.reshape(DEC_BATCH, DEC_SEQ, D_MODEL), states,
            new_k.reshape(kv_shape), new_v.reshape(kv_shape))
```
